```python
import math
import jax, jax.numpy as jnp
from jax import lax
import numpy as np

D_MODEL = 1024
BATCH = 16
SEQ = 2048
DEPTH = 1

CHUNK = 64
Q_BLOCK = 128

A_HEADS = 4
A_QK_DIM = 64
A_V_DIM = 128
B_HEADS = 8
B_LAT = 128
B_V_DIM = 64
IDX_HEADS = 4
IDX_DIM = 64
DSA_TOPK_MAX = 256

A_Q = A_HEADS * 2 * A_QK_DIM
A_K = A_HEADS * 2 * A_QK_DIM
A_V = A_HEADS * A_V_DIM
B_Q = B_HEADS * B_LAT
B_KV = B_LAT
I_Q = IDX_HEADS * IDX_DIM
I_K = IDX_DIM
I_W = IDX_HEADS
IN_SIZES = (A_Q, A_K, A_V, B_Q, B_KV, I_Q, I_K, I_W)
IN_COLS = sum(IN_SIZES)
MIX_WIDTH = A_HEADS * A_V_DIM + B_HEADS * B_V_DIM

PEER_HEADS = 8
PEER_NKEYS = 128
PEER_NEXPERTS = PEER_NKEYS * PEER_NKEYS
PEER_QDIM = 128
PEER_TOPK = 16
PEER_TOKEN_BLOCK = 128

EPS = 1e-6
NEG = -1e30

kernel_name = "hymba_diffattn_dsa_peer_adaln_block"


def rms_norm(x, g):
    xf = x.astype(jnp.float32)
    y = xf * lax.rsqrt(jnp.mean(xf * xf, axis=-1, keepdims=True) + EPS)
    return (y * g.astype(jnp.float32)).astype(x.dtype)


def modulate(h, shift, scale):
    return h * (1.0 + scale[:, None, :]) + shift[:, None, :]


def alibi_slopes(n):
    return jnp.asarray([2.0 ** (-8.0 * (i + 1) / n) for i in range(n)], dtype=jnp.float32)


def chunk_end(pos):
    return (pos // CHUNK + 1) * CHUNK


def to_blocks(a):
    b, s = a.shape[0], a.shape[1]
    return jnp.moveaxis(a.reshape((b, s // Q_BLOCK, Q_BLOCK) + a.shape[2:]), 1, 0)


def from_blocks(a):
    a = jnp.moveaxis(a, 0, 1)
    return a.reshape((a.shape[0], a.shape[1] * a.shape[2]) + a.shape[3:])


def diff_attention(q, k, v, lam, slopes):
    s_len = q.shape[1]
    kpos = jnp.arange(s_len)
    scale = A_QK_DIM ** -0.5
    kf = k.astype(jnp.float32)
    vf = v.astype(jnp.float32)

    def block(args):
        qb, i = args
        qpos = i * Q_BLOCK + jnp.arange(Q_BLOCK)
        s = jnp.einsum('bqhmd,bshmd->bhmqs', qb.astype(jnp.float32), kf) * scale
        dist = jnp.abs(qpos[:, None] - kpos[None, :]).astype(jnp.float32)
        bias = -slopes[:, None, None, None] * dist
        allowed = kpos[None, :] < chunk_end(qpos)[:, None]
        s = jnp.where(allowed, s + bias, NEG)
        p = jax.nn.softmax(s, axis=-1)
        a = p[:, :, 0] - lam * p[:, :, 1]
        return jnp.einsum('bhqs,bshd->bqhd', a, vf)

    o = lax.map(block, (to_blocks(q), jnp.arange(s_len // Q_BLOCK)))
    return from_blocks(o)


def dsa_attention(q, kv, q_idx, k_idx, w_idx, w_uv, slopes):
    s_len = q.shape[1]
    topk = min(DSA_TOPK_MAX, s_len // 4)
    kpos = jnp.arange(s_len)
    kvf = kv.astype(jnp.float32)
    kif = k_idx.astype(jnp.float32)
    wuv = w_uv.astype(jnp.float32)
    idx_scale = (IDX_DIM ** -0.5) * (IDX_HEADS ** -0.5)
    att_scale = B_LAT ** -0.5

    def block(args):
        qb, qib, wib, i = args
        qpos = i * Q_BLOCK + jnp.arange(Q_BLOCK)
        cend = chunk_end(qpos)
        allowed = kpos[None, :] < cend[:, None]
        rel = jax.nn.relu(jnp.einsum('bqhd,bsd->bqhs', qib.astype(jnp.float32), kif))
        score = jnp.einsum('bqhs,bqh->bqs', rel, wib.astype(jnp.float32)) * idx_scale
        score = jnp.where(allowed, score, NEG)
        _, sel = lax.top_k(score, topk)
        valid = sel < cend[None, :, None]
        kv_sel = jax.vmap(lambda a, ix: a[ix])(kvf, sel)
        s = jnp.einsum('bqhl,bqkl->bhqk', qb.astype(jnp.float32), kv_sel) * att_scale
        dist = jnp.abs(qpos[None, :, None] - sel).astype(jnp.float32)
        s = s - slopes[None, :, None, None] * dist[:, None]
        s = jnp.where(valid[:, None], s, NEG)
        p = jax.nn.softmax(s, axis=-1)
        o_lat = jnp.einsum('bhqk,bqkl->bqhl', p, kv_sel)
        return jnp.einsum('bqhl,hld->bqhd', o_lat, wuv)

    o = lax.map(block, (to_blocks(q), to_blocks(q_idx), to_blocks(w_idx),
                        jnp.arange(s_len // Q_BLOCK)))
    return from_blocks(o)


def peer_ffn(h, w_q, sub_keys, u_tab, v_tab):
    b, s_len, d = h.shape
    t = b * s_len
    ht = h.reshape(t, d)
    q = (ht @ w_q).astype(jnp.float32).reshape(t, PEER_HEADS, 2, PEER_QDIM // 2)
    sub = jnp.einsum('thcd,cnd->thcn', q, sub_keys.astype(jnp.float32))
    sv, si = lax.top_k(sub, PEER_TOPK)
    cand = (sv[:, :, 0, :, None] + sv[:, :, 1, None, :]).reshape(t, PEER_HEADS, PEER_TOPK * PEER_TOPK)
    top_s, top_c = lax.top_k(cand, PEER_TOPK)
    i1 = jnp.take_along_axis(si[:, :, 0], top_c // PEER_TOPK, axis=-1)
    i2 = jnp.take_along_axis(si[:, :, 1], top_c % PEER_TOPK, axis=-1)
    experts = i1 * PEER_NKEYS + i2
    g = jax.nn.softmax(top_s, axis=-1)
    nblk = t // PEER_TOKEN_BLOCK

    def block(args):
        xb, eb, gb = args
        u = u_tab[eb].astype(jnp.float32)
        a = jnp.einsum('thkd,td->thk', u, xb.astype(jnp.float32))
        a = jax.nn.gelu(a, approximate=False) * gb
        return jnp.einsum('thk,thkd->td', a, v_tab[eb].astype(jnp.float32))

    out = lax.map(block, (ht.reshape(nblk, PEER_TOKEN_BLOCK, d),
                          experts.reshape(nblk, PEER_TOKEN_BLOCK, PEER_HEADS, PEER_TOPK),
                          g.reshape(nblk, PEER_TOKEN_BLOCK, PEER_HEADS, PEER_TOPK)))
    return out.reshape(b, s_len, d).astype(h.dtype)


def setup_inputs(seed: int = 0) -> dict:
    key = jax.random.key(seed)
    ks = jax.random.split(key, 20)
    L, D = DEPTH, D_MODEL

    def nrm(k, shape, scale):
        return jax.random.normal(k, shape, jnp.float32) * scale

    return {
        "x": nrm(ks[0], (BATCH, SEQ, D), 1.0),
        "c": nrm(ks[1], (BATCH, D), 1.0),
        "ada_w": nrm(ks[2], (L, D, 6 * D), 0.5 * D ** -0.5),
        "ada_b": nrm(ks[3], (L, 6 * D), 0.01),
        "norm1_g": 1.0 + nrm(ks[4], (L, D), 0.02),
        "norm2_g": 1.0 + nrm(ks[5], (L, D), 0.02),
        "w_in": nrm(ks[6], (L, D, IN_COLS), D ** -0.5),
        "a_qk_gain": 1.0 + nrm(ks[7], (L, 2, A_QK_DIM), 0.02),
        "a_lambda": nrm(ks[8], (L, 4, A_QK_DIM), 0.1),
        "a_sub_gain": 1.0 + nrm(ks[9], (L, A_V_DIM), 0.02),
        "b_q_gain": 1.0 + nrm(ks[10], (L, B_LAT), 0.02),
        "b_kv_gain": 1.0 + nrm(ks[11], (L, B_LAT), 0.02),
        "b_w_uv": nrm(ks[12], (L, B_HEADS, B_LAT, B_V_DIM), B_LAT ** -0.5),
        "w_out": nrm(ks[13], (L, MIX_WIDTH, D), MIX_WIDTH ** -0.5),
        "peer_wq": nrm(ks[14], (L, D, PEER_HEADS * PEER_QDIM), D ** -0.5),
        "peer_subkeys": nrm(ks[15], (L, 2, PEER_NKEYS, PEER_QDIM // 2), (PEER_QDIM // 2) ** -0.5),
        "peer_u": nrm(ks[16], (L, PEER_NEXPERTS, D), D ** -0.5),
        "peer_v": nrm(ks[17], (L, PEER_NEXPERTS, D), 1.0),
    }


def reference(x, c, ada_w, ada_b, norm1_g, norm2_g, w_in, a_qk_gain, a_lambda, a_sub_gain,
              b_q_gain, b_kv_gain, b_w_uv, w_out, peer_wq, peer_subkeys, peer_u, peer_v):
    b, s_len, _ = x.shape
    slopes_a = alibi_slopes(A_HEADS)
    slopes_b = alibi_slopes(B_HEADS)
    split_at = [int(v) for v in np.cumsum(IN_SIZES)[:-1]]
    for l in range(DEPTH):
        mod = jax.nn.silu(c) @ ada_w[l] + ada_b[l]
        sh1, sc1, g1, sh2, sc2, g2 = jnp.split(mod, 6, axis=-1)

        h = modulate(rms_norm(x, norm1_g[l]), sh1, sc1)
        proj = h @ w_in[l]
        pqa, pka, pva, pqb, pkv, pqi, pki, pwi = jnp.split(proj, split_at, axis=-1)

        qa = rms_norm(pqa.reshape(b, s_len, A_HEADS, 2, A_QK_DIM), a_qk_gain[l, 0])
        ka = rms_norm(pka.reshape(b, s_len, A_HEADS, 2, A_QK_DIM), a_qk_gain[l, 1])
        va = pva.reshape(b, s_len, A_HEADS, A_V_DIM)
        lam_init = 0.8 - 0.6 * math.exp(-0.3 * l)
        lf = a_lambda[l].astype(jnp.float32)
        lam = jnp.exp(jnp.sum(lf[0] * lf[1])) - jnp.exp(jnp.sum(lf[2] * lf[3])) + lam_init
        oa = diff_attention(qa, ka, va, lam, slopes_a)
        oa = rms_norm(oa, a_sub_gain[l]) * (1.0 - lam_init)

        qb = rms_norm(pqb.reshape(b, s_len, B_HEADS, B_LAT), b_q_gain[l])
        kvb = rms_norm(pkv, b_kv_gain[l])
        qi = pqi.reshape(b, s_len, IDX_HEADS, IDX_DIM)
        ob = dsa_attention(qb, kvb, qi, pki, pwi, b_w_uv[l], slopes_b)

        mix = jnp.concatenate([oa.reshape(b, s_len, -1), ob.reshape(b, s_len, -1)], axis=-1)
        x = x + g1[:, None, :] * (mix.astype(x.dtype) @ w_out[l])

        h2 = modulate(rms_norm(x, norm2_g[l]), sh2, sc2)
        x = x + g2[:, None, :] * peer_ffn(h2, peer_wq[l], peer_subkeys[l], peer_u[l], peer_v[l])
    return x
```

```python
import functools
import math

import jax
import jax.numpy as jnp
import numpy as np
from jax import lax
from jax.experimental import pallas as pl
from jax.experimental.pallas import tpu as pltpu

CHUNK = 64
A_HEADS, A_QK_DIM, A_V_DIM = 4, 64, 128
B_HEADS, B_LAT, B_V_DIM = 8, 128, 64
IDX_HEADS, IDX_DIM = 4, 64
DSA_TOPK_MAX = 256
A_Q = A_HEADS * 2 * A_QK_DIM
A_K = A_Q
A_V = A_HEADS * A_V_DIM
B_Q = B_HEADS * B_LAT
B_KV = B_LAT
I_Q = IDX_HEADS * IDX_DIM
I_K = IDX_DIM
I_W = IDX_HEADS
OFF_AQ, OFF_AK, OFF_AV = 0, A_Q, A_Q + A_K
OFF_BQ = OFF_AV + A_V
OFF_KV = OFF_BQ + B_Q
OFF_IQ = OFF_KV + B_KV
OFF_IK = OFF_IQ + I_Q
OFF_IW = OFF_IK + I_K
IN_COLS = OFF_IW + I_W
PEER_HEADS, PEER_NKEYS, PEER_QDIM, PEER_TOPK = 8, 128, 128, 16
EPS = 1e-6
NEG = -1e30
INT_MIN = -(2 ** 31)

SUBLANES = 8
LANES = 128
VMEM_LIMIT = 56 * 1024 * 1024

PROJ_TM = 256
ATT_TQ = 128
PEER_TM = 128
PEER_TB = 8
PEER_NSEL = PEER_HEADS * PEER_TOPK
HALF_ROWS = 4


def _split_bf16(x):
    hi = x.astype(jnp.bfloat16)
    lo = (x - hi.astype(jnp.float32)).astype(jnp.bfloat16)
    return hi, lo


def _dot(a, b):
    return jnp.dot(a, b, preferred_element_type=jnp.float32)


def _dot_nt(a, b):
    return lax.dot_general(a, b, (((1,), (1,)), ((), ())), preferred_element_type=jnp.float32)


def _cparams(sem):
    return pltpu.CompilerParams(dimension_semantics=sem, vmem_limit_bytes=VMEM_LIMIT)


def _adaln_kernel(c_ref, w_ref, b_ref, o_ref):
    cf = c_ref[...]
    a = cf * (1.0 / (1.0 + jnp.exp(-cf)))
    a_hi, a_lo = _split_bf16(a)
    w_hi, w_lo = _split_bf16(w_ref[...])
    o_ref[...] = _dot(a_hi, w_hi) + _dot(a_hi, w_lo) + _dot(a_lo, w_hi) + b_ref[...]


def _adaln(c, w, b):
    bsz, d = c.shape
    n = w.shape[1]
    tn = 1024
    return pl.pallas_call(
        _adaln_kernel,
        grid=(n // tn,),
        in_specs=[pl.BlockSpec((bsz, d), lambda j: (0, 0)),
                  pl.BlockSpec((d, tn), lambda j: (0, j)),
                  pl.BlockSpec((1, tn), lambda j: (0, j))],
        out_specs=pl.BlockSpec((bsz, tn), lambda j: (0, j)),
        out_shape=jax.ShapeDtypeStruct((bsz, n), jnp.float32),
        compiler_params=_cparams(("arbitrary",)),
        name="adaln",
    )(c, w, b.reshape(1, n))


def _rms_rows(x, g):
    return x * lax.rsqrt(jnp.mean(x * x, axis=-1, keepdims=True) + EPS) * g


def _group_norm_block(p, gmat, gain, n):
    hi, lo = _split_bf16(p * p)
    ss = _dot(hi, gmat) + _dot(lo, gmat)
    return p * lax.rsqrt(ss * (1.0 / n) + EPS) * gain


def _inproj_kernel(x_ref, sh_ref, sc_ref, g_ref, w_ref, gq_ref, gk_ref, gbq_ref, gkv_ref,
                   qa_ref, ka_ref, va_ref, qb_ref, kv_ref, qi_ref, ki_ref, wi_ref):
    h = _rms_rows(x_ref[...], g_ref[...]) * (1.0 + sc_ref[0]) + sh_ref[0]
    hb = h.astype(jnp.bfloat16)
    row = lax.broadcasted_iota(jnp.int32, (LANES, LANES), 0)
    col = lax.broadcasted_iota(jnp.int32, (LANES, LANES), 1)
    g64 = jnp.where((row // A_QK_DIM) == (col // A_QK_DIM), 1.0, 0.0).astype(jnp.bfloat16)
    g128 = jnp.ones((LANES, LANES), jnp.bfloat16)

    def proj(off, width):
        return _dot(hb, w_ref[:, off:off + width])

    for blk in range(A_Q // LANES):
        sl = slice(blk * LANES, (blk + 1) * LANES)
        p = proj(OFF_AQ + blk * LANES, LANES)
        qa_ref[:, sl] = (_group_norm_block(p, g64, gq_ref[...], A_QK_DIM) * (A_QK_DIM ** -0.5)).astype(qa_ref.dtype)
        p = proj(OFF_AK + blk * LANES, LANES)
        ka_ref[:, sl] = _group_norm_block(p, g64, gk_ref[...], A_QK_DIM).astype(ka_ref.dtype)
    va_ref[...] = proj(OFF_AV, A_V).astype(va_ref.dtype)
    for blk in range(B_HEADS):
        sl = slice(blk * LANES, (blk + 1) * LANES)
        p = proj(OFF_BQ + blk * LANES, LANES)
        qb_ref[:, sl] = _group_norm_block(p, g128, gbq_ref[...], B_LAT).astype(qb_ref.dtype)
    p = proj(OFF_KV, B_KV)
    kv_ref[...] = _group_norm_block(p, g128, gkv_ref[...], B_LAT).astype(kv_ref.dtype)
    qi_ref[...] = proj(OFF_IQ, I_Q).astype(qi_ref.dtype)
    tail = proj(OFF_IK, 2 * LANES)
    ki_ref[...] = tail[:, :LANES].astype(ki_ref.dtype)
    wi_ref[...] = tail[:, I_K:I_K + LANES]


def _inproj(x2, sh1, sc1, g1n, w_in, a_qk_gain, b_q_gain, b_kv_gain, batch):
    t, d = x2.shape
    per_b = (t // batch) // PROJ_TM
    wpad = OFF_IK + 2 * LANES
    w = jnp.zeros((d, wpad), jnp.bfloat16).at[:, :IN_COLS].set(w_in.astype(jnp.bfloat16))
    gq = jnp.tile(a_qk_gain[0], 2).reshape(1, LANES)
    gk = jnp.tile(a_qk_gain[1], 2).reshape(1, LANES)
    tok = lambda width: pl.BlockSpec((PROJ_TM, width), lambda i: (i, 0))
    vec = lambda width: pl.BlockSpec((1, width), lambda i: (0, 0))
    per_batch = pl.BlockSpec((1, 1, d), lambda i: (i // per_b, 0, 0))
    bf = jnp.bfloat16
    outs = [(A_Q, bf), (A_K, bf), (A_V, bf), (B_Q, bf), (B_KV, bf), (I_Q, bf), (LANES, bf), (LANES, jnp.float32)]
    return pl.pallas_call(
        _inproj_kernel,
        grid=(t // PROJ_TM,),
        in_specs=[tok(d), per_batch, per_batch, vec(d),
                  pl.BlockSpec((d, wpad), lambda i: (0, 0)),
                  vec(LANES), vec(LANES), vec(LANES), vec(LANES)],
        out_specs=[tok(wd) for wd, _ in outs],
        out_shape=[jax.ShapeDtypeStruct((t, wd), dt) for wd, dt in outs],
        compiler_params=_cparams(("arbitrary",)),
        name="inproj",
    )(x2, sh1.reshape(batch, 1, d), sc1.reshape(batch, 1, d), g1n.reshape(1, d), w,
      gq, gk, b_q_gain.reshape(1, LANES), b_kv_gain.reshape(1, LANES))


def _positions(q0, tq, s_len):
    qpos = q0 + lax.broadcasted_iota(jnp.int32, (tq, s_len), 0)
    kpos = lax.broadcasted_iota(jnp.int32, (tq, s_len), 1)
    cend = (qpos // CHUNK + 1) * CHUNK
    return qpos, kpos, cend


def _softmax_rows(s):
    m = jnp.max(s, axis=-1, keepdims=True)
    p = jnp.exp(s - m)
    return p, jnp.sum(p, axis=-1, keepdims=True)


def _diffattn_kernel(q_ref, k_ref, v_ref, lam_ref, gain_ref, o_ref, *, lam_init):
    tq, s_len = q_ref.shape[0], k_ref.shape[0]
    q0 = pl.program_id(1) * tq
    qpos, kpos, cend = _positions(q0, tq, s_len)
    dist = jnp.abs(qpos - kpos).astype(jnp.float32)
    negmask = jnp.where(kpos < cend, 0.0, NEG)
    lf = lam_ref[...]
    lam = (jnp.exp(jnp.sum(lf[0:1] * lf[1:2], axis=-1, keepdims=True))
           - jnp.exp(jnp.sum(lf[2:3] * lf[3:4], axis=-1, keepdims=True)) + lam_init)
    for h in range(A_HEADS):
        slope = 2.0 ** (-8.0 * (h + 1) / A_HEADS)
        bias = negmask - slope * dist
        v = v_ref[:, h * A_V_DIM:(h + 1) * A_V_DIM]
        outs = []
        for m in range(2):
            c0 = (h * 2 + m) * A_QK_DIM
            s = _dot_nt(q_ref[:, c0:c0 + A_QK_DIM], k_ref[:, c0:c0 + A_QK_DIM]) + bias
            p, l = _softmax_rows(s)
            outs.append(_dot(p.astype(jnp.bfloat16), v) / l)
        o = outs[0] - lam * outs[1]
        o = _rms_rows(o, gain_ref[...]) * (1.0 - lam_init)
        o_ref[:, h * A_V_DIM:(h + 1) * A_V_DIM] = o.astype(o_ref.dtype)


def _diffattn(qa, ka, va, a_lambda, a_sub_gain, batch, lam_init):
    t = qa.shape[0]
    s_len = t // batch
    nq = s_len // ATT_TQ
    return pl.pallas_call(
        functools.partial(_diffattn_kernel, lam_init=lam_init),
        grid=(batch, nq),
        in_specs=[pl.BlockSpec((ATT_TQ, A_Q), lambda b, i: (b * nq + i, 0)),
                  pl.BlockSpec((s_len, A_K), lambda b, i: (b, 0)),
                  pl.BlockSpec((s_len, A_V), lambda b, i: (b, 0)),
                  pl.BlockSpec((4, A_QK_DIM), lambda b, i: (0, 0)),
                  pl.BlockSpec((1, A_V_DIM), lambda b, i: (0, 0))],
        out_specs=pl.BlockSpec((ATT_TQ, A_V), lambda b, i: (b * nq + i, 0)),
        out_shape=jax.ShapeDtypeStruct((t, A_V), jnp.bfloat16),
        compiler_params=_cparams(("arbitrary", "arbitrary")),
        name="diffattn",
    )(qa, ka, va, a_lambda, a_sub_gain.reshape(1, A_V_DIM))


def _count(mask):
    return jnp.sum(jnp.where(mask, 1.0, 0.0), axis=-1, keepdims=True)


def _topk_mask(score, allowed, kpos, topk, s_len):
    bits = pltpu.bitcast(score + 0.0, jnp.int32)
    key = jnp.where(bits < 0, bits ^ jnp.int32(0x7FFFFFFF), bits)
    key = jnp.where(allowed, key, jnp.int32(INT_MIN))
    kf = float(topk)
    thr = jnp.where(_count(key >= 0) >= kf, jnp.int32(0), jnp.int32(INT_MIN))

    def value_bit(i, thr):
        cand = thr | (jnp.int32(1) << (30 - i))
        return jnp.where(_count(key >= cand) >= kf, cand, thr)

    thr = lax.fori_loop(0, 31, value_bit, thr)
    above = key > thr
    tie = key == thr
    need = kf - _count(above)
    nbits = int(math.log2(s_len))

    def pos_bit(i, pos):
        cand = pos | (jnp.int32(1) << (nbits - 1 - i))
        return jnp.where(_count(tie & (kpos < cand)) < need, cand, pos)

    pos = lax.fori_loop(0, nbits, pos_bit, jnp.zeros_like(thr))
    return allowed & (above | (tie & (kpos <= pos)))


def _dsa_kernel(qi_ref, wi_ref, ki_ref, qb_ref, kv_ref, wuv_ref, o_ref, *, topk):
    tq, s_len = qb_ref.shape[0], kv_ref.shape[0]
    q0 = pl.program_id(1) * tq
    qpos, kpos, cend = _positions(q0, tq, s_len)
    allowed = kpos < cend
    ki = ki_ref[...]
    wi = wi_ref[...]
    zpad = jnp.zeros((tq, LANES - IDX_DIM), jnp.bfloat16)
    score = jnp.zeros((tq, s_len), jnp.float32)
    for h in range(IDX_HEADS):
        qh = jnp.concatenate([qi_ref[:, h * IDX_DIM:(h + 1) * IDX_DIM], zpad], axis=-1)
        score = score + jnp.maximum(_dot_nt(qh, ki), 0.0) * wi[:, h:h + 1]
    score = score * ((IDX_DIM ** -0.5) * (IDX_HEADS ** -0.5))
    sel = _topk_mask(score, allowed, kpos, topk, s_len)

    dist = jnp.abs(qpos - kpos).astype(jnp.float32)
    negmask = jnp.where(sel, 0.0, NEG)
    kv = kv_ref[...]
    for h in range(B_HEADS):
        slope = 2.0 ** (-8.0 * (h + 1) / B_HEADS)
        s = _dot_nt(qb_ref[:, h * B_LAT:(h + 1) * B_LAT], kv) * (B_LAT ** -0.5) + (negmask - slope * dist)
        p, l = _softmax_rows(s)
        o_lat = _dot(p.astype(jnp.bfloat16), kv) / l
        o = _dot(o_lat.astype(jnp.bfloat16), wuv_ref[h])
        o_ref[:, h * B_V_DIM:(h + 1) * B_V_DIM] = o.astype(o_ref.dtype)


def _dsa(qi, wi, ki, qb, kvb, w_uv, batch):
    t = qb.shape[0]
    s_len = t // batch
    nq = s_len // ATT_TQ
    topk = min(DSA_TOPK_MAX, s_len // 4)
    qblk = lambda width: pl.BlockSpec((ATT_TQ, width), lambda b, i: (b * nq + i, 0))
    kblk = lambda width: pl.BlockSpec((s_len, width), lambda b, i: (b, 0))
    return pl.pallas_call(
        functools.partial(_dsa_kernel, topk=topk),
        grid=(batch, nq),
        in_specs=[qblk(I_Q), qblk(LANES), kblk(LANES), qblk(B_Q), kblk(B_KV),
                  pl.BlockSpec((B_HEADS, B_LAT, B_V_DIM), lambda b, i: (0, 0, 0))],
        out_specs=qblk(B_HEADS * B_V_DIM),
        out_shape=jax.ShapeDtypeStruct((t, B_HEADS * B_V_DIM), jnp.bfloat16),
        compiler_params=_cparams(("arbitrary", "arbitrary")),
        name="dsa",
    )(qi, wi, ki, qb, kvb, w_uv.astype(jnp.bfloat16))


def _outproj_kernel(x_ref, oa_ref, ob_ref, wo_ref, g1_ref, sh_ref, sc_ref, gn_ref, wq_ref,
                    x1_ref, h2_ref, q_ref):
    na = oa_ref.shape[1]
    y = _dot(oa_ref[...], wo_ref[:na, :]) + _dot(ob_ref[...], wo_ref[na:, :])
    x1 = x_ref[...] + g1_ref[0] * y
    x1_ref[...] = x1
    h2 = _rms_rows(x1, gn_ref[...]) * (1.0 + sc_ref[0]) + sh_ref[0]
    h2_ref[...] = h2
    q_ref[...] = _dot(h2.astype(jnp.bfloat16), wq_ref[...]).astype(q_ref.dtype)


def _outproj(x2, oa, ob, w_out, g1, sh2, sc2, g2n, peer_wq, batch):
    t, d = x2.shape
    per_b = (t // batch) // PROJ_TM
    nq = peer_wq.shape[1]
    tok = lambda width: pl.BlockSpec((PROJ_TM, width), lambda i: (i, 0))
    per_batch = pl.BlockSpec((1, 1, d), lambda i: (i // per_b, 0, 0))
    full = lambda a, b: pl.BlockSpec((a, b), lambda i: (0, 0))
    return pl.pallas_call(
        _outproj_kernel,
        grid=(t // PROJ_TM,),
        in_specs=[tok(d), tok(oa.shape[1]), tok(ob.shape[1]), full(w_out.shape[0], d),
                  per_batch, per_batch, per_batch, full(1, d), full(d, nq)],
        out_specs=[tok(d), tok(d), tok(nq)],
        out_shape=[jax.ShapeDtypeStruct((t, d), jnp.float32),
                   jax.ShapeDtypeStruct((t, d), jnp.float32),
                   jax.ShapeDtypeStruct((t, nq), jnp.bfloat16)],
        compiler_params=_cparams(("arbitrary",)),
        name="outproj",
    )(x2, oa, ob, w_out.astype(jnp.bfloat16), g1.reshape(batch, 1, d), sh2.reshape(batch, 1, d),
      sc2.reshape(batch, 1, d), g2n.reshape(1, d), peer_wq.astype(jnp.bfloat16))


def _extract_topk(vals, payload, k):
    n, tm = vals.shape
    rows = lax.broadcasted_iota(jnp.int32, (n, tm), 0).astype(jnp.float32)
    top_v, top_p = [], []
    for _ in range(k):
        m = jnp.max(vals, axis=0, keepdims=True)
        pos = jnp.min(jnp.where(vals == m, rows, float(n)), axis=0, keepdims=True)
        hit = rows == pos
        top_v.append(m)
        top_p.append(pos if payload is None else jnp.max(jnp.where(hit, payload, -1.0), axis=0, keepdims=True))
        vals = jnp.where(hit, -jnp.inf, vals)
    return jnp.concatenate(top_v, axis=0), jnp.concatenate(top_p, axis=0)


def _peer_select_kernel(q_ref, sk_ref, e_ref, g_ref):
    half = PEER_QDIM // 2
    e_rows, g_rows = [], []
    for h in range(PEER_HEADS):
        sub = _dot_nt(sk_ref[...], q_ref[:, h * PEER_QDIM:(h + 1) * PEER_QDIM])
        sv0, si0 = _extract_topk(sub[:PEER_NKEYS], None, PEER_TOPK)
        sv1, si1 = _extract_topk(sub[PEER_NKEYS:], None, PEER_TOPK)
        cand = jnp.concatenate([sv0[a:a + 1] + sv1 for a in range(PEER_TOPK)], axis=0)
        ids = jnp.concatenate([si0[a:a + 1] * PEER_NKEYS + si1 for a in range(PEER_TOPK)], axis=0)
        top_s, top_e = _extract_topk(cand, ids, PEER_TOPK)
        p = jnp.exp(top_s - top_s[0:1])
        e_rows.append(top_e)
        g_rows.append(p / jnp.sum(p, axis=0, keepdims=True))
    e_t = jnp.concatenate(e_rows, axis=0)
    g_t = jnp.concatenate(g_rows, axis=0)
    e_ref[...] = e_t.T.astype(jnp.int32)
    g_ref[...] = g_t.T


def _peer_select(q, sub_keys):
    t, nq = q.shape
    half = PEER_QDIM // 2
    sk = jnp.zeros((2 * PEER_NKEYS, PEER_QDIM), jnp.bfloat16)
    sk = sk.at[:PEER_NKEYS, :half].set(sub_keys[0].astype(jnp.bfloat16))
    sk = sk.at[PEER_NKEYS:, half:].set(sub_keys[1].astype(jnp.bfloat16))
    tok = lambda width: pl.BlockSpec((PEER_TM, width), lambda i: (i, 0))
    return pl.pallas_call(
        _peer_select_kernel,
        grid=(t // PEER_TM,),
        in_specs=[tok(nq), pl.BlockSpec((2 * PEER_NKEYS, PEER_QDIM), lambda i: (0, 0))],
        out_specs=[tok(PEER_NSEL), tok(PEER_NSEL)],
        out_shape=[jax.ShapeDtypeStruct((t, PEER_NSEL), jnp.int32),
                   jax.ShapeDtypeStruct((t, PEER_NSEL), jnp.float32)],
        compiler_params=_cparams(("arbitrary",)),
        name="peer_select",
    )(q, sk)


def _gather_row(tab_ref, e):
    w = tab_ref[pl.ds(pl.multiple_of(e * HALF_ROWS, HALF_ROWS), HALF_ROWS), :]
    hi = pltpu.bitcast(w & jnp.uint32(0xFFFF0000), jnp.float32)
    lo = pltpu.bitcast(w << 16, jnp.float32)
    return hi, lo


def _peer_u_kernel(idx_ref, h_ref, gate_ref, tab_ref, coef_ref, abuf_ref):
    ones8 = jnp.ones((SUBLANES, LANES), jnp.bfloat16)
    rows_per_tok = PEER_NSEL * HALF_ROWS

    def batch(b, carry):
        def token(tb, c2):
            t = b * PEER_TB + tb
            base = pl.multiple_of(t * SUBLANES, SUBLANES)
            xh = h_ref[pl.ds(base, HALF_ROWS), :]
            xl = h_ref[pl.ds(base + HALF_ROWS, HALF_ROWS), :]
            off = pl.multiple_of(tb * rows_per_tok, rows_per_tok)
            for j in range(PEER_NSEL):
                hi, lo = _gather_row(tab_ref, idx_ref[0, t, j])
                abuf_ref[pl.ds(off + j * HALF_ROWS, HALF_ROWS), :] = hi * xh + lo * xl
            return c2

        lax.fori_loop(0, PEER_TB, token, 0)
        rows = []
        for tb in range(PEER_TB):
            ab = abuf_ref[pl.ds(tb * rows_per_tok, PEER_NSEL, stride=HALF_ROWS), :]
            for s in range(1, HALF_ROWS):
                ab = ab + abuf_ref[pl.ds(tb * rows_per_tok + s, PEER_NSEL, stride=HALF_ROWS), :]
            ab_hi, ab_lo = _split_bf16(ab)
            a8 = _dot_nt(ones8, ab_hi) + _dot_nt(ones8, ab_lo)
            rows.append(a8[0:1, :])
        a = jnp.concatenate(rows, axis=0)
        sl = pl.ds(pl.multiple_of(b * PEER_TB, PEER_TB), PEER_TB)
        gelu = 0.5 * a * (1.0 + lax.erf(a * (2.0 ** -0.5)))
        coef_ref[sl, :] = gelu * gate_ref[sl, :]
        return carry

    lax.fori_loop(0, PEER_TM // PEER_TB, batch, 0)


def _peer_v_kernel(idx_ref, coef_ref, x_ref, g2_ref, tab_ref, out_ref, cb_ref):
    g2 = g2_ref[...]
    ones = jnp.ones((LANES, LANES), jnp.bfloat16)
    eye = (lax.broadcasted_iota(jnp.int32, (PEER_NSEL, LANES), 0)
           == lax.broadcasted_iota(jnp.int32, (PEER_NSEL, LANES), 1))
    nacc = 2

    def batch(b, carry):
        for tb in range(PEER_TB):
            row = coef_ref[pl.ds(b * PEER_TB + tb, 1), :]
            diag = jnp.where(eye, jnp.broadcast_to(row, (PEER_NSEL, LANES)), 0.0)
            d0 = diag.astype(jnp.bfloat16)
            r1 = diag - d0.astype(jnp.float32)
            d1 = r1.astype(jnp.bfloat16)
            d2 = (r1 - d1.astype(jnp.float32)).astype(jnp.bfloat16)
            cb_ref[pl.ds(tb * PEER_NSEL, PEER_NSEL), :] = _dot(d0, ones) + _dot(d1, ones) + _dot(d2, ones)

        def token(tb, c2):
            t = b * PEER_TB + tb
            zero = jnp.zeros((HALF_ROWS, LANES), jnp.float32)
            acc_h = [zero] * nacc
            acc_l = [zero] * nacc
            cbase = pl.multiple_of(tb * PEER_NSEL, PEER_NSEL)
            for j in range(PEER_NSEL):
                hi, lo = _gather_row(tab_ref, idx_ref[0, t, j])
                c = jnp.broadcast_to(cb_ref[pl.ds(cbase + j, 1), :], (HALF_ROWS, LANES))
                acc_h[j % nacc] = acc_h[j % nacc] + c * hi
                acc_l[j % nacc] = acc_l[j % nacc] + c * lo
            base = pl.multiple_of(t * SUBLANES, SUBLANES)
            sh = pl.ds(base, HALF_ROWS)
            sl = pl.ds(base + HALF_ROWS, HALF_ROWS)
            out_ref[sh, :] = x_ref[sh, :] + g2[:HALF_ROWS] * (acc_h[0] + acc_h[1])
            out_ref[sl, :] = x_ref[sl, :] + g2[HALF_ROWS:] * (acc_l[0] + acc_l[1])
            return c2

        lax.fori_loop(0, PEER_TB, token, 0)
        return carry

    lax.fori_loop(0, PEER_TM // PEER_TB, batch, 0)


def _pack_table(tab):
    n, d = tab.shape
    bits = lax.bitcast_convert_type(tab.astype(jnp.bfloat16), jnp.uint16).astype(jnp.uint32)
    packed = (bits[:, : d // 2] << 16) | bits[:, d // 2:]
    return packed.reshape(n * HALF_ROWS, LANES)


def _peer_apply(h2, x1, g2, experts, gates, u_tab, v_tab, batch):
    t, d = h2.shape
    nblk = t // PEER_TM
    per_b = nblk // batch
    assert d == SUBLANES * LANES and d // 2 == HALF_ROWS * LANES
    idx3 = experts.reshape(nblk, PEER_TM, PEER_NSEL)
    smem_blk = pl.BlockSpec((1, PEER_TM, PEER_NSEL), lambda i: (i, 0, 0), memory_space=pltpu.SMEM)
    tok_blk = pl.BlockSpec((PEER_TM * SUBLANES, LANES), lambda i: (i, 0))
    sel_blk = pl.BlockSpec((PEER_TM, PEER_NSEL), lambda i: (i, 0))
    tab_spec = pl.BlockSpec(memory_space=pltpu.VMEM)

    coef = pl.pallas_call(
        _peer_u_kernel,
        grid=(nblk,),
        in_specs=[smem_blk, tok_blk, sel_blk, tab_spec],
        out_specs=sel_blk,
        out_shape=jax.ShapeDtypeStruct((t, PEER_NSEL), jnp.float32),
        scratch_shapes=[pltpu.VMEM((PEER_TB * PEER_NSEL * HALF_ROWS, LANES), jnp.float32)],
        compiler_params=_cparams(("arbitrary",)),
        name="peer_u",
    )(idx3, h2.reshape(t * SUBLANES, LANES), gates, _pack_table(u_tab))

    out = pl.pallas_call(
        _peer_v_kernel,
        grid=(nblk,),
        in_specs=[smem_blk, sel_blk, tok_blk,
                  pl.BlockSpec((SUBLANES, LANES), lambda i: (i // per_b, 0)),
                  tab_spec],
        out_specs=tok_blk,
        out_shape=jax.ShapeDtypeStruct((t * SUBLANES, LANES), jnp.float32),
        scratch_shapes=[pltpu.VMEM((PEER_TB * PEER_NSEL, LANES), jnp.float32)],
        compiler_params=_cparams(("arbitrary",)),
        name="peer_v",
    )(idx3, coef, x1.reshape(t * SUBLANES, LANES), g2.reshape(batch * SUBLANES, LANES), _pack_table(v_tab))
    return out.reshape(t, d)


def kernel(x, c, ada_w, ada_b, norm1_g, norm2_g, w_in, a_qk_gain, a_lambda, a_sub_gain, b_q_gain, b_kv_gain, b_w_uv, w_out, peer_wq, peer_subkeys, peer_u, peer_v):
    b, s, d = x.shape
    t = b * s
    x2 = x.reshape(t, d)
    for l in range(ada_w.shape[0]):
        mod = _adaln(c, ada_w[l], ada_b[l])
        sh1, sc1, g1, sh2, sc2, g2 = [mod[:, i * d:(i + 1) * d] for i in range(6)]
        qa, ka, va, qb, kvb, qi, ki, wi = _inproj(x2, sh1, sc1, norm1_g[l], w_in[l], a_qk_gain[l],
                                                  b_q_gain[l], b_kv_gain[l], b)
        lam_init = 0.8 - 0.6 * math.exp(-0.3 * l)
        oa = _diffattn(qa, ka, va, a_lambda[l], a_sub_gain[l], b, lam_init)
        ob = _dsa(qi, wi, ki, qb, kvb, b_w_uv[l], b)
        x1, h2, q = _outproj(x2, oa, ob, w_out[l], g1, sh2, sc2, norm2_g[l], peer_wq[l], b)
        experts, gates = _peer_select(q, peer_subkeys[l])
        x2 = _peer_apply(h2, x1, g2, experts, gates, peer_u[l], peer_v[l], b)
    return x2.reshape(b, s, d)
```

```python
import functools
import math

import jax
import jax.numpy as jnp
import numpy as np
from jax import lax
from jax.experimental import pallas as pl
from jax.experimental.pallas import tpu as pltpu

CHUNK = 64
A_HEADS, A_QK_DIM, A_V_DIM = 4, 64, 128
B_HEADS, B_LAT, B_V_DIM = 8, 128, 64
IDX_HEADS, IDX_DIM = 4, 64
DSA_TOPK_MAX = 256
A_Q = A_HEADS * 2 * A_QK_DIM
A_K = A_Q
A_V = A_HEADS * A_V_DIM
B_Q = B_HEADS * B_LAT
B_KV = B_LAT
I_Q = IDX_HEADS * IDX_DIM
I_K = IDX_DIM
I_W = IDX_HEADS
OFF_AQ, OFF_AK, OFF_AV = 0, A_Q, A_Q + A_K
OFF_BQ = OFF_AV + A_V
OFF_KV = OFF_BQ + B_Q
OFF_IQ = OFF_KV + B_KV
OFF_IK = OFF_IQ + I_Q
OFF_IW = OFF_IK + I_K
IN_COLS = OFF_IW + I_W
PEER_HEADS, PEER_NKEYS, PEER_QDIM, PEER_TOPK = 8, 128, 128, 16
EPS = 1e-6
NEG = -1e30
INT_MIN = -(2 ** 31)

SUBLANES = 8
LANES = 128
VMEM_LIMIT = 56 * 1024 * 1024

PROJ_TM = 256
ATT_TQ = 128
ATT_GROUP_SPAN = 256
PEER_TM = 128
PEER_TB = 8
PEER_NSEL = PEER_HEADS * PEER_TOPK
HALF_ROWS = 4


def _split_bf16(x):
    hi = x.astype(jnp.bfloat16)
    lo = (x - hi.astype(jnp.float32)).astype(jnp.bfloat16)
    return hi, lo


def _dot(a, b):
    return jnp.dot(a, b, preferred_element_type=jnp.float32)


def _dot_nt(a, b):
    return lax.dot_general(a, b, (((1,), (1,)), ((), ())), preferred_element_type=jnp.float32)


def _cparams(sem):
    return pltpu.CompilerParams(dimension_semantics=sem, vmem_limit_bytes=VMEM_LIMIT)


def _adaln_kernel(c_ref, w_ref, b_ref, o_ref):
    cf = c_ref[...]
    a = cf * (1.0 / (1.0 + jnp.exp(-cf)))
    a_hi, a_lo = _split_bf16(a)
    w_hi, w_lo = _split_bf16(w_ref[...])
    o_ref[...] = _dot(a_hi, w_hi) + _dot(a_hi, w_lo) + _dot(a_lo, w_hi) + b_ref[...]


def _adaln(c, w, b):
    bsz, d = c.shape
    n = w.shape[1]
    tn = 1024
    return pl.pallas_call(
        _adaln_kernel,
        grid=(n // tn,),
        in_specs=[pl.BlockSpec((bsz, d), lambda j: (0, 0)),
                  pl.BlockSpec((d, tn), lambda j: (0, j)),
                  pl.BlockSpec((1, tn), lambda j: (0, j))],
        out_specs=pl.BlockSpec((bsz, tn), lambda j: (0, j)),
        out_shape=jax.ShapeDtypeStruct((bsz, n), jnp.float32),
        compiler_params=_cparams(("arbitrary",)),
        name="adaln",
    )(c, w, b.reshape(1, n))


def _rms_rows(x, g):
    return x * lax.rsqrt(jnp.mean(x * x, axis=-1, keepdims=True) + EPS) * g


def _group_norm_block(p, gmat, gain, n):
    hi, lo = _split_bf16(p * p)
    ss = _dot(hi, gmat) + _dot(lo, gmat)
    return p * lax.rsqrt(ss * (1.0 / n) + EPS) * gain


def _inproj_kernel(x_ref, sh_ref, sc_ref, g_ref, w_ref, gq_ref, gk_ref, gbq_ref, gkv_ref,
                   qa_ref, ka_ref, va_ref, qb_ref, kv_ref, qi_ref, ki_ref, wi_ref):
    h = _rms_rows(x_ref[...], g_ref[...]) * (1.0 + sc_ref[0]) + sh_ref[0]
    hb = h.astype(jnp.bfloat16)
    row = lax.broadcasted_iota(jnp.int32, (LANES, LANES), 0)
    col = lax.broadcasted_iota(jnp.int32, (LANES, LANES), 1)
    g64 = jnp.where((row // A_QK_DIM) == (col // A_QK_DIM), 1.0, 0.0).astype(jnp.bfloat16)
    g128 = jnp.ones((LANES, LANES), jnp.bfloat16)

    def proj(off, width):
        return _dot(hb, w_ref[:, off:off + width])

    for blk in range(A_Q // LANES):
        sl = slice(blk * LANES, (blk + 1) * LANES)
        p = proj(OFF_AQ + blk * LANES, LANES)
        qa_ref[:, sl] = (_group_norm_block(p, g64, gq_ref[...], A_QK_DIM) * (A_QK_DIM ** -0.5)).astype(qa_ref.dtype)
        p = proj(OFF_AK + blk * LANES, LANES)
        ka_ref[:, sl] = _group_norm_block(p, g64, gk_ref[...], A_QK_DIM).astype(ka_ref.dtype)
    va_ref[...] = proj(OFF_AV, A_V).astype(va_ref.dtype)
    for blk in range(B_HEADS):
        sl = slice(blk * LANES, (blk + 1) * LANES)
        p = proj(OFF_BQ + blk * LANES, LANES)
        qb_ref[:, sl] = _group_norm_block(p, g128, gbq_ref[...], B_LAT).astype(qb_ref.dtype)
    p = proj(OFF_KV, B_KV)
    kv_ref[...] = _group_norm_block(p, g128, gkv_ref[...], B_LAT).astype(kv_ref.dtype)
    qi_ref[...] = proj(OFF_IQ, I_Q).astype(qi_ref.dtype)
    tail = proj(OFF_IK, 2 * LANES)
    ki_ref[...] = tail[:, :LANES].astype(ki_ref.dtype)
    wi_ref[...] = tail[:, I_K:I_K + LANES]


def _inproj(x2, sh1, sc1, g1n, w_in, a_qk_gain, b_q_gain, b_kv_gain, batch):
    t, d = x2.shape
    per_b = (t // batch) // PROJ_TM
    wpad = OFF_IK + 2 * LANES
    w = jnp.zeros((d, wpad), jnp.bfloat16).at[:, :IN_COLS].set(w_in.astype(jnp.bfloat16))
    gq = jnp.tile(a_qk_gain[0], 2).reshape(1, LANES)
    gk = jnp.tile(a_qk_gain[1], 2).reshape(1, LANES)
    tok = lambda width: pl.BlockSpec((PROJ_TM, width), lambda i: (i, 0))
    vec = lambda width: pl.BlockSpec((1, width), lambda i: (0, 0))
    per_batch = pl.BlockSpec((1, 1, d), lambda i: (i // per_b, 0, 0))
    bf = jnp.bfloat16
    outs = [(A_Q, bf), (A_K, bf), (A_V, bf), (B_Q, bf), (B_KV, bf), (I_Q, bf), (LANES, bf), (LANES, jnp.float32)]
    return pl.pallas_call(
        _inproj_kernel,
        grid=(t // PROJ_TM,),
        in_specs=[tok(d), per_batch, per_batch, vec(d),
                  pl.BlockSpec((d, wpad), lambda i: (0, 0)),
                  vec(LANES), vec(LANES), vec(LANES), vec(LANES)],
        out_specs=[tok(wd) for wd, _ in outs],
        out_shape=[jax.ShapeDtypeStruct((t, wd), dt) for wd, dt in outs],
        compiler_params=_cparams(("arbitrary",)),
        name="inproj",
    )(x2, sh1.reshape(batch, 1, d), sc1.reshape(batch, 1, d), g1n.reshape(1, d), w,
      gq, gk, b_q_gain.reshape(1, LANES), b_kv_gain.reshape(1, LANES))


def _positions(q0, tq, s_len):
    qpos = q0 + lax.broadcasted_iota(jnp.int32, (tq, s_len), 0)
    kpos = lax.broadcasted_iota(jnp.int32, (tq, s_len), 1)
    cend = (qpos // CHUNK + 1) * CHUNK
    return qpos, kpos, cend


def _softmax_rows(s):
    m = jnp.max(s, axis=-1, keepdims=True)
    p = jnp.exp(s - m)
    return p, jnp.sum(p, axis=-1, keepdims=True)


def _diffattn_kernel(q_ref, k_ref, v_ref, lam_ref, gain_ref, o_ref, *, lam_init, q_base):
    tq, s_len = q_ref.shape[0], k_ref.shape[0]
    q0 = q_base + pl.program_id(1) * tq
    qpos, kpos, cend = _positions(q0, tq, s_len)
    dist = jnp.abs(qpos - kpos).astype(jnp.float32)
    negmask = jnp.where(kpos < cend, 0.0, NEG)
    lf = lam_ref[...]
    lam = (jnp.exp(jnp.sum(lf[0:1] * lf[1:2], axis=-1, keepdims=True))
           - jnp.exp(jnp.sum(lf[2:3] * lf[3:4], axis=-1, keepdims=True)) + lam_init)
    for h in range(A_HEADS):
        slope = 2.0 ** (-8.0 * (h + 1) / A_HEADS)
        bias = negmask - slope * dist
        v = v_ref[:, h * A_V_DIM:(h + 1) * A_V_DIM]
        outs = []
        for m in range(2):
            c0 = (h * 2 + m) * A_QK_DIM
            s = _dot_nt(q_ref[:, c0:c0 + A_QK_DIM], k_ref[:, c0:c0 + A_QK_DIM]) + bias
            p, l = _softmax_rows(s)
            outs.append(_dot(p.astype(jnp.bfloat16), v) / l)
        o = outs[0] - lam * outs[1]
        o = _rms_rows(o, gain_ref[...]) * (1.0 - lam_init)
        o_ref[:, h * A_V_DIM:(h + 1) * A_V_DIM] = o.astype(o_ref.dtype)


def _causal_groups(s_len):
    span = min(ATT_GROUP_SPAN, s_len)
    tiles = span // ATT_TQ
    return [(g * tiles, tiles, (g + 1) * span) for g in range(s_len // span)]


def _qblk(first_tile, width):
    return pl.BlockSpec((None, ATT_TQ, width), lambda b, i: (b, first_tile + i, 0))


def _kblk(klen, width):
    return pl.BlockSpec((None, klen, width), lambda b, i: (b, 0, 0))


def _oblk(width):
    return pl.BlockSpec((None, ATT_TQ, width), lambda b, i: (b, i, 0))


def _diffattn(qa, ka, va, a_lambda, a_sub_gain, batch, lam_init):
    t = qa.shape[0]
    s_len = t // batch
    qa3, ka3, va3 = (a.reshape(batch, s_len, a.shape[1]) for a in (qa, ka, va))
    outs = []
    for first, tiles, klen in _causal_groups(s_len):
        outs.append(pl.pallas_call(
            functools.partial(_diffattn_kernel, lam_init=lam_init, q_base=first * ATT_TQ),
            grid=(batch, tiles),
            in_specs=[_qblk(first, A_Q), _kblk(klen, A_K), _kblk(klen, A_V),
                      pl.BlockSpec((4, A_QK_DIM), lambda b, i: (0, 0)),
                      pl.BlockSpec((1, A_V_DIM), lambda b, i: (0, 0))],
            out_specs=_oblk(A_V),
            out_shape=jax.ShapeDtypeStruct((batch, tiles * ATT_TQ, A_V), jnp.bfloat16),
            compiler_params=_cparams(("arbitrary", "arbitrary")),
            name="diffattn",
        )(qa3, ka3, va3, a_lambda, a_sub_gain.reshape(1, A_V_DIM)))
    return jnp.concatenate(outs, axis=1).reshape(t, A_V)


def _count(mask):
    return jnp.sum(jnp.where(mask, 1.0, 0.0), axis=-1, keepdims=True)


def _topk_mask(score, allowed, kpos, topk, s_len):
    bits = pltpu.bitcast(score + 0.0, jnp.int32)
    key = jnp.where(bits < 0, bits ^ jnp.int32(0x7FFFFFFF), bits)
    key = jnp.where(allowed, key, jnp.int32(INT_MIN))
    kf = float(topk)
    thr = jnp.where(_count(key >= 0) >= kf, jnp.int32(0), jnp.int32(INT_MIN))

    def value_bit(i, thr):
        cand = thr | (jnp.int32(1) << (30 - i))
        return jnp.where(_count(key >= cand) >= kf, cand, thr)

    thr = lax.fori_loop(0, 31, value_bit, thr)
    above = key > thr
    tie = key == thr
    need = kf - _count(above)
    nbits = (s_len - 1).bit_length()

    def pos_bit(i, pos):
        cand = pos | (jnp.int32(1) << (nbits - 1 - i))
        return jnp.where(_count(tie & (kpos < cand)) < need, cand, pos)

    pos = lax.fori_loop(0, nbits, pos_bit, jnp.zeros_like(thr))
    return allowed & (above | (tie & (kpos <= pos)))


def _dsa_kernel(qi_ref, wi_ref, ki_ref, qb_ref, kv_ref, wuv_ref, o_ref, *, topk, q_base):
    tq, s_len = qb_ref.shape[0], kv_ref.shape[0]
    q0 = q_base + pl.program_id(1) * tq
    qpos, kpos, cend = _positions(q0, tq, s_len)
    allowed = kpos < cend
    ki = ki_ref[...]
    wi = wi_ref[...]
    zpad = jnp.zeros((tq, LANES - IDX_DIM), jnp.bfloat16)
    score = jnp.zeros((tq, s_len), jnp.float32)
    for h in range(IDX_HEADS):
        qh = jnp.concatenate([qi_ref[:, h * IDX_DIM:(h + 1) * IDX_DIM], zpad], axis=-1)
        score = score + jnp.maximum(_dot_nt(qh, ki), 0.0) * wi[:, h:h + 1]
    score = score * ((IDX_DIM ** -0.5) * (IDX_HEADS ** -0.5))
    sel = _topk_mask(score, allowed, kpos, topk, s_len)

    dist = jnp.abs(qpos - kpos).astype(jnp.float32)
    negmask = jnp.where(sel, 0.0, NEG)
    kv = kv_ref[...]
    for h in range(B_HEADS):
        slope = 2.0 ** (-8.0 * (h + 1) / B_HEADS)
        s = _dot_nt(qb_ref[:, h * B_LAT:(h + 1) * B_LAT], kv) * (B_LAT ** -0.5) + (negmask - slope * dist)
        p, l = _softmax_rows(s)
        o_lat = _dot(p.astype(jnp.bfloat16), kv) / l
        o = _dot(o_lat.astype(jnp.bfloat16), wuv_ref[h])
        o_ref[:, h * B_V_DIM:(h + 1) * B_V_DIM] = o.astype(o_ref.dtype)


def _dsa(qi, wi, ki, qb, kvb, w_uv, batch):
    t = qb.shape[0]
    s_len = t // batch
    topk = min(DSA_TOPK_MAX, s_len // 4)
    width = B_HEADS * B_V_DIM
    qi3, wi3, ki3, qb3, kv3 = (a.reshape(batch, s_len, a.shape[1]) for a in (qi, wi, ki, qb, kvb))
    wuv = w_uv.astype(jnp.bfloat16)
    outs = []
    for first, tiles, klen in _causal_groups(s_len):
        outs.append(pl.pallas_call(
            functools.partial(_dsa_kernel, topk=topk, q_base=first * ATT_TQ),
            grid=(batch, tiles),
            in_specs=[_qblk(first, I_Q), _qblk(first, LANES), _kblk(klen, LANES),
                      _qblk(first, B_Q), _kblk(klen, B_KV),
                      pl.BlockSpec((B_HEADS, B_LAT, B_V_DIM), lambda b, i: (0, 0, 0))],
            out_specs=_oblk(width),
            out_shape=jax.ShapeDtypeStruct((batch, tiles * ATT_TQ, width), jnp.bfloat16),
            compiler_params=_cparams(("arbitrary", "arbitrary")),
            name="dsa",
        )(qi3, wi3, ki3, qb3, kv3, wuv))
    return jnp.concatenate(outs, axis=1).reshape(t, width)


def _outproj_kernel(x_ref, oa_ref, ob_ref, wo_ref, g1_ref, sh_ref, sc_ref, gn_ref, wq_ref,
                    x1_ref, h2_ref, q_ref):
    na = oa_ref.shape[1]
    y = _dot(oa_ref[...], wo_ref[:na, :]) + _dot(ob_ref[...], wo_ref[na:, :])
    x1 = x_ref[...] + g1_ref[0] * y
    x1_ref[...] = x1
    h2 = _rms_rows(x1, gn_ref[...]) * (1.0 + sc_ref[0]) + sh_ref[0]
    h2_ref[...] = h2
    q_ref[...] = _dot(h2.astype(jnp.bfloat16), wq_ref[...]).astype(q_ref.dtype)


def _outproj(x2, oa, ob, w_out, g1, sh2, sc2, g2n, peer_wq, batch):
    t, d = x2.shape
    per_b = (t // batch) // PROJ_TM
    nq = peer_wq.shape[1]
    tok = lambda width: pl.BlockSpec((PROJ_TM, width), lambda i: (i, 0))
    per_batch = pl.BlockSpec((1, 1, d), lambda i: (i // per_b, 0, 0))
    full = lambda a, b: pl.BlockSpec((a, b), lambda i: (0, 0))
    return pl.pallas_call(
        _outproj_kernel,
        grid=(t // PROJ_TM,),
        in_specs=[tok(d), tok(oa.shape[1]), tok(ob.shape[1]), full(w_out.shape[0], d),
                  per_batch, per_batch, per_batch, full(1, d), full(d, nq)],
        out_specs=[tok(d), tok(d), tok(nq)],
        out_shape=[jax.ShapeDtypeStruct((t, d), jnp.float32),
                   jax.ShapeDtypeStruct((t, d), jnp.float32),
                   jax.ShapeDtypeStruct((t, nq), jnp.bfloat16)],
        compiler_params=_cparams(("arbitrary",)),
        name="outproj",
    )(x2, oa, ob, w_out.astype(jnp.bfloat16), g1.reshape(batch, 1, d), sh2.reshape(batch, 1, d),
      sc2.reshape(batch, 1, d), g2n.reshape(1, d), peer_wq.astype(jnp.bfloat16))


def _extract_topk(vals, payload, k):
    n, tm = vals.shape
    rows = lax.broadcasted_iota(jnp.int32, (n, tm), 0).astype(jnp.float32)
    top_v, top_p = [], []
    for _ in range(k):
        m = jnp.max(vals, axis=0, keepdims=True)
        pos = jnp.min(jnp.where(vals == m, rows, float(n)), axis=0, keepdims=True)
        hit = rows == pos
        top_v.append(m)
        top_p.append(pos if payload is None else jnp.max(jnp.where(hit, payload, -1.0), axis=0, keepdims=True))
        vals = jnp.where(hit, -jnp.inf, vals)
    return jnp.concatenate(top_v, axis=0), jnp.concatenate(top_p, axis=0)


def _stair_pairs():
    return [(a, b) for a in range(PEER_TOPK) for b in range(PEER_TOPK) if (a + 1) * (b + 1) <= PEER_TOPK]


N_STAIR = len(_stair_pairs())
N_STAIR_PAD = -(-N_STAIR // SUBLANES) * SUBLANES


def _copy_rows(sel, x):
    x0 = x.astype(jnp.bfloat16)
    r1 = x - x0.astype(jnp.float32)
    x1 = r1.astype(jnp.bfloat16)
    x2 = (r1 - x1.astype(jnp.float32)).astype(jnp.bfloat16)
    return _dot(sel, x0) + _dot(sel, x1) + _dot(sel, x2)


def _peer_select_kernel(q_ref, sk_ref, sela_ref, selb_ref, e_ref, g_ref):
    tm = q_ref.shape[0]
    sela, selb = sela_ref[...], selb_ref[...]
    pad_row = lax.broadcasted_iota(jnp.int32, (N_STAIR_PAD, tm), 0) >= N_STAIR
    e_rows, g_rows = [], []
    for h in range(PEER_HEADS):
        sub = _dot_nt(sk_ref[...], q_ref[:, h * PEER_QDIM:(h + 1) * PEER_QDIM])
        sv0, si0 = _extract_topk(sub[:PEER_NKEYS], None, PEER_TOPK)
        sv1, si1 = _extract_topk(sub[PEER_NKEYS:], None, PEER_TOPK)
        cand = jnp.where(pad_row, -jnp.inf, _copy_rows(sela, sv0) + _copy_rows(selb, sv1))
        ids = (_dot(sela, si0.astype(jnp.bfloat16)) * float(PEER_NKEYS)
               + _dot(selb, si1.astype(jnp.bfloat16))) * float(HALF_ROWS)
        top_s, top_e = _extract_topk(cand, ids, PEER_TOPK)
        p = jnp.exp(top_s - top_s[0:1])
        e_rows.append(top_e)
        g_rows.append(p / jnp.sum(p, axis=0, keepdims=True))
    e_t = jnp.concatenate(e_rows, axis=0)
    g_t = jnp.concatenate(g_rows, axis=0)
    e_ref[...] = e_t.T.astype(jnp.int32)
    g_ref[...] = g_t.T


def _peer_select(q, sub_keys):
    t, nq = q.shape
    half = PEER_QDIM // 2
    sk = jnp.zeros((2 * PEER_NKEYS, PEER_QDIM), jnp.bfloat16)
    sk = sk.at[:PEER_NKEYS, :half].set(sub_keys[0].astype(jnp.bfloat16))
    sk = sk.at[PEER_NKEYS:, half:].set(sub_keys[1].astype(jnp.bfloat16))
    sel = np.zeros((2, N_STAIR_PAD, PEER_TOPK), np.float32)
    for r, (a, b) in enumerate(_stair_pairs()):
        sel[0, r, a] = 1.0
        sel[1, r, b] = 1.0
    sel = jnp.asarray(sel, jnp.bfloat16)
    tok = lambda width: pl.BlockSpec((PEER_TM, width), lambda i: (i, 0))
    full = lambda a, b: pl.BlockSpec((a, b), lambda i: (0, 0))
    return pl.pallas_call(
        _peer_select_kernel,
        grid=(t // PEER_TM,),
        in_specs=[tok(nq), full(2 * PEER_NKEYS, PEER_QDIM),
                  full(N_STAIR_PAD, PEER_TOPK), full(N_STAIR_PAD, PEER_TOPK)],
        out_specs=[tok(PEER_NSEL), tok(PEER_NSEL)],
        out_shape=[jax.ShapeDtypeStruct((t, PEER_NSEL), jnp.int32),
                   jax.ShapeDtypeStruct((t, PEER_NSEL), jnp.float32)],
        compiler_params=_cparams(("arbitrary",)),
        name="peer_select",
    )(q, sk, sel[0], sel[1])


def _gather_row(tab_ref, row):
    w = tab_ref[pl.ds(pl.multiple_of(row, HALF_ROWS), HALF_ROWS), :]
    hi = pltpu.bitcast(w & jnp.uint32(0xFFFF0000), jnp.float32)
    lo = pltpu.bitcast(w << 16, jnp.float32)
    return hi, lo


def _peer_u_kernel(idx_ref, h_ref, gate_ref, tab_ref, coef_ref, abuf_ref):
    ones8 = jnp.ones((SUBLANES, LANES), jnp.bfloat16)
    rows_per_tok = PEER_NSEL * HALF_ROWS

    def batch(b, carry):
        def token(tb, c2):
            t = b * PEER_TB + tb
            base = pl.multiple_of(t * SUBLANES, SUBLANES)
            xh = h_ref[pl.ds(base, HALF_ROWS), :]
            xl = h_ref[pl.ds(base + HALF_ROWS, HALF_ROWS), :]
            off = pl.multiple_of(tb * rows_per_tok, rows_per_tok)
            for j in range(PEER_NSEL):
                hi, lo = _gather_row(tab_ref, idx_ref[0, t, j])
                abuf_ref[pl.ds(off + j * HALF_ROWS, HALF_ROWS), :] = hi * xh + lo * xl
            return c2

        lax.fori_loop(0, PEER_TB, token, 0)
        rows = []
        for tb in range(PEER_TB):
            ab = abuf_ref[pl.ds(tb * rows_per_tok, PEER_NSEL, stride=HALF_ROWS), :]
            for s in range(1, HALF_ROWS):
                ab = ab + abuf_ref[pl.ds(tb * rows_per_tok + s, PEER_NSEL, stride=HALF_ROWS), :]
            ab_hi, ab_lo = _split_bf16(ab)
            a8 = _dot_nt(ones8, ab_hi) + _dot_nt(ones8, ab_lo)
            rows.append(a8[0:1, :])
        a = jnp.concatenate(rows, axis=0)
        sl = pl.ds(pl.multiple_of(b * PEER_TB, PEER_TB), PEER_TB)
        gelu = 0.5 * a * (1.0 + lax.erf(a * (2.0 ** -0.5)))
        coef_ref[sl, :] = gelu * gate_ref[sl, :]
        return carry

    lax.fori_loop(0, PEER_TM // PEER_TB, batch, 0)


def _peer_v_kernel(idx_ref, coef_ref, x_ref, g2_ref, tab_ref, out_ref, cb_ref):
    g2 = g2_ref[...]
    ones = jnp.ones((LANES, LANES), jnp.bfloat16)
    eye = (lax.broadcasted_iota(jnp.int32, (PEER_NSEL, LANES), 0)
           == lax.broadcasted_iota(jnp.int32, (PEER_NSEL, LANES), 1))
    nacc = 2

    def batch(b, carry):
        for tb in range(PEER_TB):
            row = coef_ref[pl.ds(b * PEER_TB + tb, 1), :]
            diag = jnp.where(eye, jnp.broadcast_to(row, (PEER_NSEL, LANES)), 0.0)
            d0 = diag.astype(jnp.bfloat16)
            r1 = diag - d0.astype(jnp.float32)
            d1 = r1.astype(jnp.bfloat16)
            d2 = (r1 - d1.astype(jnp.float32)).astype(jnp.bfloat16)
            cb_ref[pl.ds(tb * PEER_NSEL, PEER_NSEL), :] = _dot(d0, ones) + _dot(d1, ones) + _dot(d2, ones)

        def token(tb, c2):
            t = b * PEER_TB + tb
            zero = jnp.zeros((HALF_ROWS, LANES), jnp.float32)
            acc_h = [zero] * nacc
            acc_l = [zero] * nacc
            cbase = pl.multiple_of(tb * PEER_NSEL, PEER_NSEL)
            for j in range(PEER_NSEL):
                hi, lo = _gather_row(tab_ref, idx_ref[0, t, j])
                c = jnp.broadcast_to(cb_ref[pl.ds(cbase + j, 1), :], (HALF_ROWS, LANES))
                acc_h[j % nacc] = acc_h[j % nacc] + c * hi
                acc_l[j % nacc] = acc_l[j % nacc] + c * lo
            base = pl.multiple_of(t * SUBLANES, SUBLANES)
            sh = pl.ds(base, HALF_ROWS)
            sl = pl.ds(base + HALF_ROWS, HALF_ROWS)
            out_ref[sh, :] = x_ref[sh, :] + g2[:HALF_ROWS] * (acc_h[0] + acc_h[1])
            out_ref[sl, :] = x_ref[sl, :] + g2[HALF_ROWS:] * (acc_l[0] + acc_l[1])
            return c2

        lax.fori_loop(0, PEER_TB, token, 0)
        return carry

    lax.fori_loop(0, PEER_TM // PEER_TB, batch, 0)


def _pack_table(tab):
    n, d = tab.shape
    bits = lax.bitcast_convert_type(tab.astype(jnp.bfloat16), jnp.uint16).astype(jnp.uint32)
    packed = (bits[:, : d // 2] << 16) | bits[:, d // 2:]
    return packed.reshape(n * HALF_ROWS, LANES)


def _peer_apply(h2, x1, g2, experts, gates, u_tab, v_tab, batch):
    t, d = h2.shape
    nblk = t // PEER_TM
    per_b = nblk // batch
    assert d == SUBLANES * LANES and d // 2 == HALF_ROWS * LANES
    idx3 = experts.reshape(nblk, PEER_TM, PEER_NSEL)
    smem_blk = pl.BlockSpec((1, PEER_TM, PEER_NSEL), lambda i: (i, 0, 0), memory_space=pltpu.SMEM)
    tok_blk = pl.BlockSpec((PEER_TM * SUBLANES, LANES), lambda i: (i, 0))
    sel_blk = pl.BlockSpec((PEER_TM, PEER_NSEL), lambda i: (i, 0))
    tab_spec = pl.BlockSpec(memory_space=pltpu.VMEM)

    coef = pl.pallas_call(
        _peer_u_kernel,
        grid=(nblk,),
        in_specs=[smem_blk, tok_blk, sel_blk, tab_spec],
        out_specs=sel_blk,
        out_shape=jax.ShapeDtypeStruct((t, PEER_NSEL), jnp.float32),
        scratch_shapes=[pltpu.VMEM((PEER_TB * PEER_NSEL * HALF_ROWS, LANES), jnp.float32)],
        compiler_params=_cparams(("arbitrary",)),
        name="peer_u",
    )(idx3, h2.reshape(t * SUBLANES, LANES), gates, _pack_table(u_tab))

    out = pl.pallas_call(
        _peer_v_kernel,
        grid=(nblk,),
        in_specs=[smem_blk, sel_blk, tok_blk,
                  pl.BlockSpec((SUBLANES, LANES), lambda i: (i // per_b, 0)),
                  tab_spec],
        out_specs=tok_blk,
        out_shape=jax.ShapeDtypeStruct((t * SUBLANES, LANES), jnp.float32),
        scratch_shapes=[pltpu.VMEM((PEER_TB * PEER_NSEL, LANES), jnp.float32)],
        compiler_params=_cparams(("arbitrary",)),
        name="peer_v",
    )(idx3, coef, x1.reshape(t * SUBLANES, LANES), g2.reshape(batch * SUBLANES, LANES), _pack_table(v_tab))
    return out.reshape(t, d)


def kernel(x, c, ada_w, ada_b, norm1_g, norm2_g, w_in, a_qk_gain, a_lambda, a_sub_gain, b_q_gain, b_kv_gain, b_w_uv, w_out, peer_wq, peer_subkeys, peer_u, peer_v):
    b, s, d = x.shape
    t = b * s
    x2 = x.reshape(t, d)
    for l in range(ada_w.shape[0]):
        mod = _adaln(c, ada_w[l], ada_b[l])
        sh1, sc1, g1, sh2, sc2, g2 = [mod[:, i * d:(i + 1) * d] for i in range(6)]
        qa, ka, va, qb, kvb, qi, ki, wi = _inproj(x2, sh1, sc1, norm1_g[l], w_in[l], a_qk_gain[l],
                                                  b_q_gain[l], b_kv_gain[l], b)
        lam_init = 0.8 - 0.6 * math.exp(-0.3 * l)
        oa = _diffattn(qa, ka, va, a_lambda[l], a_sub_gain[l], b, lam_init)
        ob = _dsa(qi, wi, ki, qb, kvb, b_w_uv[l], b)
        x1, h2, q = _outproj(x2, oa, ob, w_out[l], g1, sh2, sc2, norm2_g[l], peer_wq[l], b)
        experts, gates = _peer_select(q, peer_subkeys[l])
        x2 = _peer_apply(h2, x1, g2, experts, gates, peer_u[l], peer_v[l], b)
    return x2.reshape(b, s, d)
```

```python
import functools
import math

import jax
import jax.numpy as jnp
import numpy as np
from jax import lax
from jax.experimental import pallas as pl
from jax.experimental.pallas import tpu as pltpu

CHUNK = 64
A_HEADS, A_QK_DIM, A_V_DIM = 4, 64, 128
B_HEADS, B_LAT, B_V_DIM = 8, 128, 64
IDX_HEADS, IDX_DIM = 4, 64
DSA_TOPK_MAX = 256
A_Q = A_HEADS * 2 * A_QK_DIM
A_K = A_Q
A_V = A_HEADS * A_V_DIM
B_Q = B_HEADS * B_LAT
B_KV = B_LAT
I_Q = IDX_HEADS * IDX_DIM
I_K = IDX_DIM
I_W = IDX_HEADS
OFF_AQ, OFF_AK, OFF_AV = 0, A_Q, A_Q + A_K
OFF_BQ = OFF_AV + A_V
OFF_KV = OFF_BQ + B_Q
OFF_IQ = OFF_KV + B_KV
OFF_IK = OFF_IQ + I_Q
OFF_IW = OFF_IK + I_K
IN_COLS = OFF_IW + I_W
PEER_HEADS, PEER_NKEYS, PEER_QDIM, PEER_TOPK = 8, 128, 128, 16
EPS = 1e-6
NEG = -1e30
INT_MIN = -(2 ** 31)

SUBLANES = 8
LANES = 128
VMEM_LIMIT = 56 * 1024 * 1024

PROJ_TM = 256
ATT_TQ = 128
ATT_GROUP_SPAN = 256
PEER_TM = 128
PEER_TB = 8
PEER_NSEL = PEER_HEADS * PEER_TOPK
HALF_ROWS = 4


def _split_bf16(x):
    hi = x.astype(jnp.bfloat16)
    lo = (x - hi.astype(jnp.float32)).astype(jnp.bfloat16)
    return hi, lo


def _dot(a, b):
    return jnp.dot(a, b, preferred_element_type=jnp.float32)


def _dot_nt(a, b):
    return lax.dot_general(a, b, (((1,), (1,)), ((), ())), preferred_element_type=jnp.float32)


def _cparams(sem):
    return pltpu.CompilerParams(dimension_semantics=sem, vmem_limit_bytes=VMEM_LIMIT)


def _adaln_kernel(c_ref, w_ref, b_ref, o_ref):
    cf = c_ref[...]
    a = cf * (1.0 / (1.0 + jnp.exp(-cf)))
    a_hi, a_lo = _split_bf16(a)
    w_hi, w_lo = _split_bf16(w_ref[...])
    o_ref[...] = _dot(a_hi, w_hi) + _dot(a_hi, w_lo) + _dot(a_lo, w_hi) + b_ref[...]


def _adaln(c, w, b):
    bsz, d = c.shape
    n = w.shape[1]
    tn = 1024
    return pl.pallas_call(
        _adaln_kernel,
        grid=(n // tn,),
        in_specs=[pl.BlockSpec((bsz, d), lambda j: (0, 0)),
                  pl.BlockSpec((d, tn), lambda j: (0, j)),
                  pl.BlockSpec((1, tn), lambda j: (0, j))],
        out_specs=pl.BlockSpec((bsz, tn), lambda j: (0, j)),
        out_shape=jax.ShapeDtypeStruct((bsz, n), jnp.float32),
        compiler_params=_cparams(("arbitrary",)),
        name="adaln",
    )(c, w, b.reshape(1, n))


def _rms_rows(x, g):
    return x * lax.rsqrt(jnp.mean(x * x, axis=-1, keepdims=True) + EPS) * g


def _group_norm_block(p, gmat, gain, n):
    hi, lo = _split_bf16(p * p)
    ss = _dot(hi, gmat) + _dot(lo, gmat)
    return p * lax.rsqrt(ss * (1.0 / n) + EPS) * gain


def _inproj_kernel(x_ref, sh_ref, sc_ref, g_ref, w_ref, gq_ref, gk_ref, gbq_ref, gkv_ref,
                   qa_ref, ka_ref, va_ref, qb_ref, kv_ref, qi_ref, ki_ref, wi_ref):
    h = _rms_rows(x_ref[...], g_ref[...]) * (1.0 + sc_ref[0]) + sh_ref[0]
    hb = h.astype(jnp.bfloat16)
    row = lax.broadcasted_iota(jnp.int32, (LANES, LANES), 0)
    col = lax.broadcasted_iota(jnp.int32, (LANES, LANES), 1)
    g64 = jnp.where((row // A_QK_DIM) == (col // A_QK_DIM), 1.0, 0.0).astype(jnp.bfloat16)
    g128 = jnp.ones((LANES, LANES), jnp.bfloat16)

    def proj(off, width):
        return _dot(hb, w_ref[:, off:off + width])

    for blk in range(A_Q // LANES):
        sl = slice(blk * LANES, (blk + 1) * LANES)
        p = proj(OFF_AQ + blk * LANES, LANES)
        qa_ref[:, sl] = (_group_norm_block(p, g64, gq_ref[...], A_QK_DIM) * (A_QK_DIM ** -0.5)).astype(qa_ref.dtype)
        p = proj(OFF_AK + blk * LANES, LANES)
        ka_ref[:, sl] = _group_norm_block(p, g64, gk_ref[...], A_QK_DIM).astype(ka_ref.dtype)
    va_ref[...] = proj(OFF_AV, A_V).astype(va_ref.dtype)
    for blk in range(B_HEADS):
        sl = slice(blk * LANES, (blk + 1) * LANES)
        p = proj(OFF_BQ + blk * LANES, LANES)
        qb_ref[:, sl] = _group_norm_block(p, g128, gbq_ref[...], B_LAT).astype(qb_ref.dtype)
    p = proj(OFF_KV, B_KV)
    kv_ref[...] = _group_norm_block(p, g128, gkv_ref[...], B_LAT).astype(kv_ref.dtype)
    qi_ref[...] = proj(OFF_IQ, I_Q).astype(qi_ref.dtype)
    tail = proj(OFF_IK, 2 * LANES)
    ki_ref[...] = tail[:, :LANES].astype(ki_ref.dtype)
    wi_ref[...] = tail[:, I_K:I_K + LANES]


def _inproj(x2, sh1, sc1, g1n, w_in, a_qk_gain, b_q_gain, b_kv_gain, batch):
    t, d = x2.shape
    per_b = (t // batch) // PROJ_TM
    wpad = OFF_IK + 2 * LANES
    w = jnp.zeros((d, wpad), jnp.bfloat16).at[:, :IN_COLS].set(w_in.astype(jnp.bfloat16))
    gq = jnp.tile(a_qk_gain[0], 2).reshape(1, LANES)
    gk = jnp.tile(a_qk_gain[1], 2).reshape(1, LANES)
    tok = lambda width: pl.BlockSpec((PROJ_TM, width), lambda i: (i, 0))
    vec = lambda width: pl.BlockSpec((1, width), lambda i: (0, 0))
    per_batch = pl.BlockSpec((1, 1, d), lambda i: (i // per_b, 0, 0))
    bf = jnp.bfloat16
    outs = [(A_Q, bf), (A_K, bf), (A_V, bf), (B_Q, bf), (B_KV, bf), (I_Q, bf), (LANES, bf), (LANES, jnp.float32)]
    return pl.pallas_call(
        _inproj_kernel,
        grid=(t // PROJ_TM,),
        in_specs=[tok(d), per_batch, per_batch, vec(d),
                  pl.BlockSpec((d, wpad), lambda i: (0, 0)),
                  vec(LANES), vec(LANES), vec(LANES), vec(LANES)],
        out_specs=[tok(wd) for wd, _ in outs],
        out_shape=[jax.ShapeDtypeStruct((t, wd), dt) for wd, dt in outs],
        compiler_params=_cparams(("arbitrary",)),
        name="inproj",
    )(x2, sh1.reshape(batch, 1, d), sc1.reshape(batch, 1, d), g1n.reshape(1, d), w,
      gq, gk, b_q_gain.reshape(1, LANES), b_kv_gain.reshape(1, LANES))


def _positions(q0, tq, s_len):
    qpos = q0 + lax.broadcasted_iota(jnp.int32, (tq, s_len), 0)
    kpos = lax.broadcasted_iota(jnp.int32, (tq, s_len), 1)
    cend = (qpos // CHUNK + 1) * CHUNK
    return qpos, kpos, cend


def _softmax_rows(s):
    m = jnp.max(s, axis=-1, keepdims=True)
    p = jnp.exp(s - m)
    return p, jnp.sum(p, axis=-1, keepdims=True)


def _diffattn_kernel(q_ref, k_ref, v_ref, lam_ref, gain_ref, o_ref, *, lam_init, q_base):
    tq, s_len = q_ref.shape[0], k_ref.shape[0]
    q0 = q_base + pl.program_id(1) * tq
    qpos, kpos, cend = _positions(q0, tq, s_len)
    dist = jnp.abs(qpos - kpos).astype(jnp.float32)
    negmask = jnp.where(kpos < cend, 0.0, NEG)
    lf = lam_ref[...]
    lam = (jnp.exp(jnp.sum(lf[0:1] * lf[1:2], axis=-1, keepdims=True))
           - jnp.exp(jnp.sum(lf[2:3] * lf[3:4], axis=-1, keepdims=True)) + lam_init)
    for h in range(A_HEADS):
        slope = 2.0 ** (-8.0 * (h + 1) / A_HEADS)
        bias = negmask - slope * dist
        v = v_ref[:, h * A_V_DIM:(h + 1) * A_V_DIM]
        outs = []
        for m in range(2):
            c0 = (h * 2 + m) * A_QK_DIM
            s = _dot_nt(q_ref[:, c0:c0 + A_QK_DIM], k_ref[:, c0:c0 + A_QK_DIM]) + bias
            p, l = _softmax_rows(s)
            outs.append(_dot(p.astype(jnp.bfloat16), v) / l)
        o = outs[0] - lam * outs[1]
        o = _rms_rows(o, gain_ref[...]) * (1.0 - lam_init)
        o_ref[:, h * A_V_DIM:(h + 1) * A_V_DIM] = o.astype(o_ref.dtype)


def _causal_groups(s_len):
    span = min(ATT_GROUP_SPAN, s_len)
    tiles = span // ATT_TQ
    return [(g * tiles, tiles, (g + 1) * span) for g in range(s_len // span)]


def _qblk(first_tile, width):
    return pl.BlockSpec((None, ATT_TQ, width), lambda b, i: (b, first_tile + i, 0))


def _kblk(klen, width):
    return pl.BlockSpec((None, klen, width), lambda b, i: (b, 0, 0))


def _oblk(width):
    return pl.BlockSpec((None, ATT_TQ, width), lambda b, i: (b, i, 0))


def _diffattn(qa, ka, va, a_lambda, a_sub_gain, batch, lam_init):
    t = qa.shape[0]
    s_len = t // batch
    qa3, ka3, va3 = (a.reshape(batch, s_len, a.shape[1]) for a in (qa, ka, va))
    outs = []
    for first, tiles, klen in _causal_groups(s_len):
        outs.append(pl.pallas_call(
            functools.partial(_diffattn_kernel, lam_init=lam_init, q_base=first * ATT_TQ),
            grid=(batch, tiles),
            in_specs=[_qblk(first, A_Q), _kblk(klen, A_K), _kblk(klen, A_V),
                      pl.BlockSpec((4, A_QK_DIM), lambda b, i: (0, 0)),
                      pl.BlockSpec((1, A_V_DIM), lambda b, i: (0, 0))],
            out_specs=_oblk(A_V),
            out_shape=jax.ShapeDtypeStruct((batch, tiles * ATT_TQ, A_V), jnp.bfloat16),
            compiler_params=_cparams(("arbitrary", "arbitrary")),
            name="diffattn",
        )(qa3, ka3, va3, a_lambda, a_sub_gain.reshape(1, A_V_DIM)))
    return jnp.concatenate(outs, axis=1).reshape(t, A_V)


def _count(mask):
    return jnp.sum(jnp.where(mask, 1.0, 0.0), axis=-1, keepdims=True)


def _topk_mask(score, allowed, kpos, topk, s_len):
    bits = pltpu.bitcast(score + 0.0, jnp.int32)
    key = jnp.where(bits < 0, bits ^ jnp.int32(0x7FFFFFFF), bits)
    key = jnp.where(allowed, key, jnp.int32(INT_MIN))
    kf = float(topk)
    thr = jnp.where(_count(key >= 0) >= kf, jnp.int32(0), jnp.int32(INT_MIN))

    def value_bit(i, thr):
        cand = thr | (jnp.int32(1) << (30 - i))
        return jnp.where(_count(key >= cand) >= kf, cand, thr)

    thr = lax.fori_loop(0, 31, value_bit, thr)
    above = key > thr
    tie = key == thr
    need = kf - _count(above)
    nbits = (s_len - 1).bit_length()

    def pos_bit(i, pos):
        cand = pos | (jnp.int32(1) << (nbits - 1 - i))
        return jnp.where(_count(tie & (kpos < cand)) < need, cand, pos)

    pos = lax.fori_loop(0, nbits, pos_bit, jnp.zeros_like(thr))
    return allowed & (above | (tie & (kpos <= pos)))


def _dsa_kernel(qi_ref, wi_ref, ki_ref, qb_ref, kv_ref, wuv_ref, o_ref, *, topk, q_base):
    tq, s_len = qb_ref.shape[0], kv_ref.shape[0]
    q0 = q_base + pl.program_id(1) * tq
    qpos, kpos, cend = _positions(q0, tq, s_len)
    allowed = kpos < cend
    ki = ki_ref[...]
    wi = wi_ref[...]
    zpad = jnp.zeros((tq, LANES - IDX_DIM), jnp.bfloat16)
    score = jnp.zeros((tq, s_len), jnp.float32)
    for h in range(IDX_HEADS):
        qh = jnp.concatenate([qi_ref[:, h * IDX_DIM:(h + 1) * IDX_DIM], zpad], axis=-1)
        score = score + jnp.maximum(_dot_nt(qh, ki), 0.0) * wi[:, h:h + 1]
    score = score * ((IDX_DIM ** -0.5) * (IDX_HEADS ** -0.5))
    sel = _topk_mask(score, allowed, kpos, topk, s_len)

    dist = jnp.abs(qpos - kpos).astype(jnp.float32)
    negmask = jnp.where(sel, 0.0, NEG)
    kv = kv_ref[...]
    for h in range(B_HEADS):
        slope = 2.0 ** (-8.0 * (h + 1) / B_HEADS)
        s = _dot_nt(qb_ref[:, h * B_LAT:(h + 1) * B_LAT], kv) * (B_LAT ** -0.5) + (negmask - slope * dist)
        p, l = _softmax_rows(s)
        o_lat = _dot(p.astype(jnp.bfloat16), kv) / l
        o = _dot(o_lat.astype(jnp.bfloat16), wuv_ref[h])
        o_ref[:, h * B_V_DIM:(h + 1) * B_V_DIM] = o.astype(o_ref.dtype)


def _dsa(qi, wi, ki, qb, kvb, w_uv, batch):
    t = qb.shape[0]
    s_len = t // batch
    topk = min(DSA_TOPK_MAX, s_len // 4)
    width = B_HEADS * B_V_DIM
    qi3, wi3, ki3, qb3, kv3 = (a.reshape(batch, s_len, a.shape[1]) for a in (qi, wi, ki, qb, kvb))
    wuv = w_uv.astype(jnp.bfloat16)
    outs = []
    for first, tiles, klen in _causal_groups(s_len):
        outs.append(pl.pallas_call(
            functools.partial(_dsa_kernel, topk=topk, q_base=first * ATT_TQ),
            grid=(batch, tiles),
            in_specs=[_qblk(first, I_Q), _qblk(first, LANES), _kblk(klen, LANES),
                      _qblk(first, B_Q), _kblk(klen, B_KV),
                      pl.BlockSpec((B_HEADS, B_LAT, B_V_DIM), lambda b, i: (0, 0, 0))],
            out_specs=_oblk(width),
            out_shape=jax.ShapeDtypeStruct((batch, tiles * ATT_TQ, width), jnp.bfloat16),
            compiler_params=_cparams(("arbitrary", "arbitrary")),
            name="dsa",
        )(qi3, wi3, ki3, qb3, kv3, wuv))
    return jnp.concatenate(outs, axis=1).reshape(t, width)


def _outproj_kernel(x_ref, oa_ref, ob_ref, wo_ref, g1_ref, sh_ref, sc_ref, gn_ref, wq_ref,
                    x1_ref, h2_ref, q_ref):
    na = oa_ref.shape[1]
    y = _dot(oa_ref[...], wo_ref[:na, :]) + _dot(ob_ref[...], wo_ref[na:, :])
    x1 = x_ref[...] + g1_ref[0] * y
    x1_ref[...] = x1
    h2 = _rms_rows(x1, gn_ref[...]) * (1.0 + sc_ref[0]) + sh_ref[0]
    h2_ref[...] = h2
    q_ref[...] = _dot(h2.astype(jnp.bfloat16), wq_ref[...]).astype(q_ref.dtype)


def _outproj(x2, oa, ob, w_out, g1, sh2, sc2, g2n, peer_wq, batch):
    t, d = x2.shape
    per_b = (t // batch) // PROJ_TM
    nq = peer_wq.shape[1]
    tok = lambda width: pl.BlockSpec((PROJ_TM, width), lambda i: (i, 0))
    per_batch = pl.BlockSpec((1, 1, d), lambda i: (i // per_b, 0, 0))
    full = lambda a, b: pl.BlockSpec((a, b), lambda i: (0, 0))
    return pl.pallas_call(
        _outproj_kernel,
        grid=(t // PROJ_TM,),
        in_specs=[tok(d), tok(oa.shape[1]), tok(ob.shape[1]), full(w_out.shape[0], d),
                  per_batch, per_batch, per_batch, full(1, d), full(d, nq)],
        out_specs=[tok(d), tok(d), tok(nq)],
        out_shape=[jax.ShapeDtypeStruct((t, d), jnp.float32),
                   jax.ShapeDtypeStruct((t, d), jnp.float32),
                   jax.ShapeDtypeStruct((t, nq), jnp.bfloat16)],
        compiler_params=_cparams(("arbitrary",)),
        name="outproj",
    )(x2, oa, ob, w_out.astype(jnp.bfloat16), g1.reshape(batch, 1, d), sh2.reshape(batch, 1, d),
      sc2.reshape(batch, 1, d), g2n.reshape(1, d), peer_wq.astype(jnp.bfloat16))


def _extract_topk(vals, payload, k):
    n, tm = vals.shape
    rows = lax.broadcasted_iota(jnp.int32, (n, tm), 0).astype(jnp.float32)
    top_v, top_p = [], []
    for _ in range(k):
        m = jnp.max(vals, axis=0, keepdims=True)
        pos = jnp.min(jnp.where(vals == m, rows, float(n)), axis=0, keepdims=True)
        hit = rows == pos
        top_v.append(m)
        top_p.append(pos if payload is None else jnp.max(jnp.where(hit, payload, -1.0), axis=0, keepdims=True))
        vals = jnp.where(hit, -jnp.inf, vals)
    return jnp.concatenate(top_v, axis=0), jnp.concatenate(top_p, axis=0)


def _stair_pairs():
    return [(a, b) for a in range(PEER_TOPK) for b in range(PEER_TOPK) if (a + 1) * (b + 1) <= PEER_TOPK]


N_STAIR = len(_stair_pairs())
N_STAIR_PAD = -(-N_STAIR // SUBLANES) * SUBLANES


def _copy_rows(sel, x):
    x0 = x.astype(jnp.bfloat16)
    r1 = x - x0.astype(jnp.float32)
    x1 = r1.astype(jnp.bfloat16)
    x2 = (r1 - x1.astype(jnp.float32)).astype(jnp.bfloat16)
    return _dot(sel, x0) + _dot(sel, x1) + _dot(sel, x2)


def _peer_select_kernel(q_ref, sk_ref, sela_ref, selb_ref, e_ref, g_ref):
    tm = q_ref.shape[0]
    sela, selb = sela_ref[...], selb_ref[...]
    pad_row = lax.broadcasted_iota(jnp.int32, (N_STAIR_PAD, tm), 0) >= N_STAIR
    e_rows, g_rows = [], []
    for h in range(PEER_HEADS):
        sub = _dot_nt(sk_ref[...], q_ref[:, h * PEER_QDIM:(h + 1) * PEER_QDIM])
        sv0, si0 = _extract_topk(sub[:PEER_NKEYS], None, PEER_TOPK)
        sv1, si1 = _extract_topk(sub[PEER_NKEYS:], None, PEER_TOPK)
        cand = jnp.where(pad_row, -jnp.inf, _copy_rows(sela, sv0) + _copy_rows(selb, sv1))
        ids = (_dot(sela, si0.astype(jnp.bfloat16)) * float(PEER_NKEYS)
               + _dot(selb, si1.astype(jnp.bfloat16))) * float(HALF_ROWS)
        top_s, top_e = _extract_topk(cand, ids, PEER_TOPK)
        p = jnp.exp(top_s - top_s[0:1])
        e_rows.append(top_e)
        g_rows.append(p / jnp.sum(p, axis=0, keepdims=True))
    e_t = jnp.concatenate(e_rows, axis=0)
    g_t = jnp.concatenate(g_rows, axis=0)
    e_ref[...] = e_t.T.astype(jnp.int32)
    g_ref[...] = g_t.T


def _peer_select(q, sub_keys):
    t, nq = q.shape
    half = PEER_QDIM // 2
    sk = jnp.zeros((2 * PEER_NKEYS, PEER_QDIM), jnp.bfloat16)
    sk = sk.at[:PEER_NKEYS, :half].set(sub_keys[0].astype(jnp.bfloat16))
    sk = sk.at[PEER_NKEYS:, half:].set(sub_keys[1].astype(jnp.bfloat16))
    sel = np.zeros((2, N_STAIR_PAD, PEER_TOPK), np.float32)
    for r, (a, b) in enumerate(_stair_pairs()):
        sel[0, r, a] = 1.0
        sel[1, r, b] = 1.0
    sel = jnp.asarray(sel, jnp.bfloat16)
    tok = lambda width: pl.BlockSpec((PEER_TM, width), lambda i: (i, 0))
    full = lambda a, b: pl.BlockSpec((a, b), lambda i: (0, 0))
    return pl.pallas_call(
        _peer_select_kernel,
        grid=(t // PEER_TM,),
        in_specs=[tok(nq), full(2 * PEER_NKEYS, PEER_QDIM),
                  full(N_STAIR_PAD, PEER_TOPK), full(N_STAIR_PAD, PEER_TOPK)],
        out_specs=[tok(PEER_NSEL), tok(PEER_NSEL)],
        out_shape=[jax.ShapeDtypeStruct((t, PEER_NSEL), jnp.int32),
                   jax.ShapeDtypeStruct((t, PEER_NSEL), jnp.float32)],
        compiler_params=_cparams(("arbitrary",)),
        name="peer_select",
    )(q, sk, sel[0], sel[1])


def _gather_row(tab_ref, row):
    w = tab_ref[pl.ds(pl.multiple_of(row, HALF_ROWS), HALF_ROWS), :]
    hi = pltpu.bitcast(w & jnp.uint32(0xFFFF0000), jnp.float32)
    lo = pltpu.bitcast(w << 16, jnp.float32)
    return hi, lo


def _peer_u_kernel(idx_ref, h_ref, gate_ref, tab_ref, coef_ref, abuf_ref):
    ones = jnp.ones((2 * LANES, LANES), jnp.bfloat16)
    lane = lax.broadcasted_iota(jnp.int32, (PEER_NSEL, PEER_TM), 1)
    rows_per_tok = PEER_NSEL * HALF_ROWS

    def batch(b, a_t):
        def token(tb, c2):
            t = b * PEER_TB + tb
            x8 = h_ref[pl.ds(t, 1), :].reshape(SUBLANES, LANES)
            xh, xl = x8[:HALF_ROWS], x8[HALF_ROWS:]
            off = pl.multiple_of(tb * rows_per_tok, rows_per_tok)
            for j in range(PEER_NSEL):
                hi, lo = _gather_row(tab_ref, idx_ref[0, t, j])
                abuf_ref[pl.ds(off + j * HALF_ROWS, HALF_ROWS), :] = hi * xh + lo * xl
            return c2

        lax.fori_loop(0, PEER_TB, token, 0)
        parts = []
        for tb in range(PEER_TB):
            ab = abuf_ref[pl.ds(tb * rows_per_tok, PEER_NSEL, stride=HALF_ROWS), :]
            for s in range(1, HALF_ROWS):
                ab = ab + abuf_ref[pl.ds(tb * rows_per_tok + s, PEER_NSEL, stride=HALF_ROWS), :]
            parts.append(ab)
        ab_hi, ab_lo = _split_bf16(jnp.concatenate(parts, axis=0))
        r = _dot(jnp.concatenate([ab_hi, ab_lo], axis=1), ones)
        for tb in range(PEER_TB):
            a_t = jnp.where(lane == b * PEER_TB + tb, r[tb * PEER_NSEL:(tb + 1) * PEER_NSEL], a_t)
        return a_t

    a_t = lax.fori_loop(0, PEER_TM // PEER_TB, batch, jnp.zeros((PEER_NSEL, PEER_TM), jnp.float32))
    a = a_t.T
    coef_ref[...] = 0.5 * a * (1.0 + lax.erf(a * (2.0 ** -0.5))) * gate_ref[...]


def _peer_v_kernel(idx_ref, coef_ref, x_ref, g2_ref, tab_ref, out_ref, cb_ref):
    g2 = g2_ref[...]
    ones = jnp.ones((2 * LANES, LANES), jnp.bfloat16)
    eye = (lax.broadcasted_iota(jnp.int32, (PEER_NSEL, LANES), 0)
           == lax.broadcasted_iota(jnp.int32, (PEER_NSEL, LANES), 1))
    nacc = 2

    def batch(b, carry):
        for tb in range(PEER_TB):
            row = coef_ref[pl.ds(b * PEER_TB + tb, 1), :]
            d_hi, d_lo = _split_bf16(jnp.where(eye, jnp.broadcast_to(row, (PEER_NSEL, LANES)), 0.0))
            cb_ref[pl.ds(tb * PEER_NSEL, PEER_NSEL), :] = _dot(jnp.concatenate([d_hi, d_lo], axis=1), ones)

        def token(tb, c2):
            t = b * PEER_TB + tb
            zero = jnp.zeros((HALF_ROWS, LANES), jnp.float32)
            acc_h = [zero] * nacc
            acc_l = [zero] * nacc
            cbase = pl.multiple_of(tb * PEER_NSEL, PEER_NSEL)
            for j in range(PEER_NSEL):
                hi, lo = _gather_row(tab_ref, idx_ref[0, t, j])
                c = jnp.broadcast_to(cb_ref[pl.ds(cbase + j, 1), :], (HALF_ROWS, LANES))
                acc_h[j % nacc] = acc_h[j % nacc] + c * hi
                acc_l[j % nacc] = acc_l[j % nacc] + c * lo
            y8 = jnp.concatenate([acc_h[0] + acc_h[1], acc_l[0] + acc_l[1]], axis=0)
            out_ref[pl.ds(t, 1), :] = x_ref[pl.ds(t, 1), :] + g2 * y8.reshape(1, SUBLANES * LANES)
            return c2

        lax.fori_loop(0, PEER_TB, token, 0)
        return carry

    lax.fori_loop(0, PEER_TM // PEER_TB, batch, 0)


def _pack_table(tab):
    n, d = tab.shape
    bits = lax.bitcast_convert_type(tab.astype(jnp.bfloat16), jnp.uint16).astype(jnp.uint32)
    packed = (bits[:, : d // 2] << 16) | bits[:, d // 2:]
    return packed.reshape(n * HALF_ROWS, LANES)


def _peer_apply(h2, x1, g2, experts, gates, u_tab, v_tab, batch):
    t, d = h2.shape
    nblk = t // PEER_TM
    per_b = nblk // batch
    assert d == SUBLANES * LANES and d // 2 == HALF_ROWS * LANES and PEER_TM % PEER_TB == 0
    idx3 = experts.reshape(nblk, PEER_TM, PEER_NSEL)
    smem_blk = pl.BlockSpec((1, PEER_TM, PEER_NSEL), lambda i: (i, 0, 0), memory_space=pltpu.SMEM)
    tok_blk = pl.BlockSpec((PEER_TM, d), lambda i: (i, 0))
    sel_blk = pl.BlockSpec((PEER_TM, PEER_NSEL), lambda i: (i, 0))
    tab_spec = pl.BlockSpec(memory_space=pltpu.VMEM)

    coef = pl.pallas_call(
        _peer_u_kernel,
        grid=(nblk,),
        in_specs=[smem_blk, tok_blk, sel_blk, tab_spec],
        out_specs=sel_blk,
        out_shape=jax.ShapeDtypeStruct((t, PEER_NSEL), jnp.float32),
        scratch_shapes=[pltpu.VMEM((PEER_TB * PEER_NSEL * HALF_ROWS, LANES), jnp.float32)],
        compiler_params=_cparams(("arbitrary",)),
        name="peer_u",
    )(idx3, h2, gates, _pack_table(u_tab))

    return pl.pallas_call(
        _peer_v_kernel,
        grid=(nblk,),
        in_specs=[smem_blk, sel_blk, tok_blk,
                  pl.BlockSpec((None, 1, d), lambda i: (i // per_b, 0, 0)),
                  tab_spec],
        out_specs=tok_blk,
        out_shape=jax.ShapeDtypeStruct((t, d), jnp.float32),
        scratch_shapes=[pltpu.VMEM((PEER_TB * PEER_NSEL, LANES), jnp.float32)],
        compiler_params=_cparams(("arbitrary",)),
        name="peer_v",
    )(idx3, coef, x1, g2.reshape(batch, 1, d), _pack_table(v_tab))


def kernel(x, c, ada_w, ada_b, norm1_g, norm2_g, w_in, a_qk_gain, a_lambda, a_sub_gain, b_q_gain, b_kv_gain, b_w_uv, w_out, peer_wq, peer_subkeys, peer_u, peer_v):
    b, s, d = x.shape
    t = b * s
    x2 = x.reshape(t, d)
    for l in range(ada_w.shape[0]):
        mod = _adaln(c, ada_w[l], ada_b[l])
        sh1, sc1, g1, sh2, sc2, g2 = [mod[:, i * d:(i + 1) * d] for i in range(6)]
        qa, ka, va, qb, kvb, qi, ki, wi = _inproj(x2, sh1, sc1, norm1_g[l], w_in[l], a_qk_gain[l],
                                                  b_q_gain[l], b_kv_gain[l], b)
        lam_init = 0.8 - 0.6 * math.exp(-0.3 * l)
        oa = _diffattn(qa, ka, va, a_lambda[l], a_sub_gain[l], b, lam_init)
        ob = _dsa(qi, wi, ki, qb, kvb, b_w_uv[l], b)
        x1, h2, q = _outproj(x2, oa, ob, w_out[l], g1, sh2, sc2, norm2_g[l], peer_wq[l], b)
        experts, gates = _peer_select(q, peer_subkeys[l])
        x2 = _peer_apply(h2, x1, g2, experts, gates, peer_u[l], peer_v[l], b)
    return x2.reshape(b, s, d)
```

```python
import functools
import math

import jax
import jax.numpy as jnp
import numpy as np
from jax import lax
from jax.experimental import pallas as pl
from jax.experimental.pallas import tpu as pltpu
from jax.experimental.pallas import tpu_sc as plsc

CHUNK = 64
A_HEADS, A_QK_DIM, A_V_DIM = 4, 64, 128
B_HEADS, B_LAT, B_V_DIM = 8, 128, 64
IDX_HEADS, IDX_DIM = 4, 64
DSA_TOPK_MAX = 256
A_Q = A_HEADS * 2 * A_QK_DIM
A_K = A_Q
A_V = A_HEADS * A_V_DIM
B_Q = B_HEADS * B_LAT
B_KV = B_LAT
I_Q = IDX_HEADS * IDX_DIM
I_K = IDX_DIM
I_W = IDX_HEADS
OFF_AQ, OFF_AK, OFF_AV = 0, A_Q, A_Q + A_K
OFF_BQ = OFF_AV + A_V
OFF_KV = OFF_BQ + B_Q
OFF_IQ = OFF_KV + B_KV
OFF_IK = OFF_IQ + I_Q
OFF_IW = OFF_IK + I_K
IN_COLS = OFF_IW + I_W
PEER_HEADS, PEER_NKEYS, PEER_QDIM, PEER_TOPK = 8, 128, 128, 16
EPS = 1e-6
NEG = -1e30
INT_MIN = -(2 ** 31)

SUBLANES = 8
LANES = 128
VMEM_LIMIT = 56 * 1024 * 1024

PROJ_TM = 256
ATT_TQ = 128
ATT_GROUP_SPAN = 256
PEER_TM = 128
PEER_TB = 8
PEER_NSEL = PEER_HEADS * PEER_TOPK
PEER_SC_TOKENS = 16384
HALF_ROWS = 4


def _split_bf16(x):
    hi = x.astype(jnp.bfloat16)
    lo = (x - hi.astype(jnp.float32)).astype(jnp.bfloat16)
    return hi, lo


def _dot(a, b):
    return jnp.dot(a, b, preferred_element_type=jnp.float32)


def _dot_nt(a, b):
    return lax.dot_general(a, b, (((1,), (1,)), ((), ())), preferred_element_type=jnp.float32)


def _cparams(sem):
    return pltpu.CompilerParams(dimension_semantics=sem, vmem_limit_bytes=VMEM_LIMIT)


def _adaln_kernel(c_ref, w_ref, b_ref, o_ref):
    cf = c_ref[...]
    a = cf * (1.0 / (1.0 + jnp.exp(-cf)))
    a_hi, a_lo = _split_bf16(a)
    w_hi, w_lo = _split_bf16(w_ref[...])
    o_ref[...] = _dot(a_hi, w_hi) + _dot(a_hi, w_lo) + _dot(a_lo, w_hi) + b_ref[...]


def _adaln(c, w, b):
    bsz, d = c.shape
    n = w.shape[1]
    tn = 1024
    return pl.pallas_call(
        _adaln_kernel,
        grid=(n // tn,),
        in_specs=[pl.BlockSpec((bsz, d), lambda j: (0, 0)),
                  pl.BlockSpec((d, tn), lambda j: (0, j)),
                  pl.BlockSpec((1, tn), lambda j: (0, j))],
        out_specs=pl.BlockSpec((bsz, tn), lambda j: (0, j)),
        out_shape=jax.ShapeDtypeStruct((bsz, n), jnp.float32),
        compiler_params=_cparams(("arbitrary",)),
        name="adaln",
    )(c, w, b.reshape(1, n))


def _rms_rows(x, g):
    return x * lax.rsqrt(jnp.mean(x * x, axis=-1, keepdims=True) + EPS) * g


def _group_norm_block(p, gmat, gain, n):
    hi, lo = _split_bf16(p * p)
    ss = _dot(hi, gmat) + _dot(lo, gmat)
    return p * lax.rsqrt(ss * (1.0 / n) + EPS) * gain


def _inproj_kernel(x_ref, sh_ref, sc_ref, g_ref, w_ref, gq_ref, gk_ref, gbq_ref, gkv_ref,
                   qa_ref, ka_ref, va_ref, qb_ref, kv_ref, qi_ref, ki_ref, wi_ref):
    h = _rms_rows(x_ref[...], g_ref[...]) * (1.0 + sc_ref[0]) + sh_ref[0]
    hb = h.astype(jnp.bfloat16)
    row = lax.broadcasted_iota(jnp.int32, (LANES, LANES), 0)
    col = lax.broadcasted_iota(jnp.int32, (LANES, LANES), 1)
    g64 = jnp.where((row // A_QK_DIM) == (col // A_QK_DIM), 1.0, 0.0).astype(jnp.bfloat16)
    g128 = jnp.ones((LANES, LANES), jnp.bfloat16)

    def proj(off, width):
        return _dot(hb, w_ref[:, off:off + width])

    for blk in range(A_Q // LANES):
        sl = slice(blk * LANES, (blk + 1) * LANES)
        p = proj(OFF_AQ + blk * LANES, LANES)
        qa_ref[:, sl] = (_group_norm_block(p, g64, gq_ref[...], A_QK_DIM) * (A_QK_DIM ** -0.5)).astype(qa_ref.dtype)
        p = proj(OFF_AK + blk * LANES, LANES)
        ka_ref[:, sl] = _group_norm_block(p, g64, gk_ref[...], A_QK_DIM).astype(ka_ref.dtype)
    va_ref[...] = proj(OFF_AV, A_V).astype(va_ref.dtype)
    for blk in range(B_HEADS):
        sl = slice(blk * LANES, (blk + 1) * LANES)
        p = proj(OFF_BQ + blk * LANES, LANES)
        qb_ref[:, sl] = _group_norm_block(p, g128, gbq_ref[...], B_LAT).astype(qb_ref.dtype)
    p = proj(OFF_KV, B_KV)
    kv_ref[...] = _group_norm_block(p, g128, gkv_ref[...], B_LAT).astype(kv_ref.dtype)
    qi_ref[...] = proj(OFF_IQ, I_Q).astype(qi_ref.dtype)
    tail = proj(OFF_IK, 2 * LANES)
    ki_ref[...] = tail[:, :LANES].astype(ki_ref.dtype)
    wi_ref[...] = tail[:, I_K:I_K + LANES]


def _inproj(x2, sh1, sc1, g1n, w_in, a_qk_gain, b_q_gain, b_kv_gain, batch):
    t, d = x2.shape
    per_b = (t // batch) // PROJ_TM
    wpad = OFF_IK + 2 * LANES
    w = jnp.zeros((d, wpad), jnp.bfloat16).at[:, :IN_COLS].set(w_in.astype(jnp.bfloat16))
    gq = jnp.tile(a_qk_gain[0], 2).reshape(1, LANES)
    gk = jnp.tile(a_qk_gain[1], 2).reshape(1, LANES)
    tok = lambda width: pl.BlockSpec((PROJ_TM, width), lambda i: (i, 0))
    vec = lambda width: pl.BlockSpec((1, width), lambda i: (0, 0))
    per_batch = pl.BlockSpec((1, 1, d), lambda i: (i // per_b, 0, 0))
    bf = jnp.bfloat16
    outs = [(A_Q, bf), (A_K, bf), (A_V, bf), (B_Q, bf), (B_KV, bf), (I_Q, bf), (LANES, bf), (LANES, jnp.float32)]
    return pl.pallas_call(
        _inproj_kernel,
        grid=(t // PROJ_TM,),
        in_specs=[tok(d), per_batch, per_batch, vec(d),
                  pl.BlockSpec((d, wpad), lambda i: (0, 0)),
                  vec(LANES), vec(LANES), vec(LANES), vec(LANES)],
        out_specs=[tok(wd) for wd, _ in outs],
        out_shape=[jax.ShapeDtypeStruct((t, wd), dt) for wd, dt in outs],
        compiler_params=_cparams(("arbitrary",)),
        name="inproj",
    )(x2, sh1.reshape(batch, 1, d), sc1.reshape(batch, 1, d), g1n.reshape(1, d), w,
      gq, gk, b_q_gain.reshape(1, LANES), b_kv_gain.reshape(1, LANES))


def _positions(q0, tq, s_len):
    qpos = q0 + lax.broadcasted_iota(jnp.int32, (tq, s_len), 0)
    kpos = lax.broadcasted_iota(jnp.int32, (tq, s_len), 1)
    cend = (qpos // CHUNK + 1) * CHUNK
    return qpos, kpos, cend


def _softmax_rows(s):
    m = jnp.max(s, axis=-1, keepdims=True)
    p = jnp.exp(s - m)
    return p, jnp.sum(p, axis=-1, keepdims=True)


def _diffattn_kernel(q_ref, k_ref, v_ref, lam_ref, gain_ref, o_ref, *, lam_init, q_base):
    tq, s_len = q_ref.shape[0], k_ref.shape[0]
    q0 = q_base + pl.program_id(1) * tq
    qpos, kpos, cend = _positions(q0, tq, s_len)
    dist = jnp.abs(qpos - kpos).astype(jnp.float32)
    negmask = jnp.where(kpos < cend, 0.0, NEG)
    lf = lam_ref[...]
    lam = (jnp.exp(jnp.sum(lf[0:1] * lf[1:2], axis=-1, keepdims=True))
           - jnp.exp(jnp.sum(lf[2:3] * lf[3:4], axis=-1, keepdims=True)) + lam_init)
    for h in range(A_HEADS):
        slope = 2.0 ** (-8.0 * (h + 1) / A_HEADS)
        bias = negmask - slope * dist
        v = v_ref[:, h * A_V_DIM:(h + 1) * A_V_DIM]
        outs = []
        for m in range(2):
            c0 = (h * 2 + m) * A_QK_DIM
            s = _dot_nt(q_ref[:, c0:c0 + A_QK_DIM], k_ref[:, c0:c0 + A_QK_DIM]) + bias
            p, l = _softmax_rows(s)
            outs.append(_dot(p.astype(jnp.bfloat16), v) / l)
        o = outs[0] - lam * outs[1]
        o = _rms_rows(o, gain_ref[...]) * (1.0 - lam_init)
        o_ref[:, h * A_V_DIM:(h + 1) * A_V_DIM] = o.astype(o_ref.dtype)


def _causal_groups(s_len):
    span = min(ATT_GROUP_SPAN, s_len)
    tiles = span // ATT_TQ
    return [(g * tiles, tiles, (g + 1) * span) for g in range(s_len // span)]


def _qblk(first_tile, width):
    return pl.BlockSpec((None, ATT_TQ, width), lambda b, i: (b, first_tile + i, 0))


def _kblk(klen, width):
    return pl.BlockSpec((None, klen, width), lambda b, i: (b, 0, 0))


def _oblk(width):
    return pl.BlockSpec((None, ATT_TQ, width), lambda b, i: (b, i, 0))


def _diffattn(qa, ka, va, a_lambda, a_sub_gain, batch, lam_init):
    t = qa.shape[0]
    s_len = t // batch
    qa3, ka3, va3 = (a.reshape(batch, s_len, a.shape[1]) for a in (qa, ka, va))
    outs = []
    for first, tiles, klen in _causal_groups(s_len):
        outs.append(pl.pallas_call(
            functools.partial(_diffattn_kernel, lam_init=lam_init, q_base=first * ATT_TQ),
            grid=(batch, tiles),
            in_specs=[_qblk(first, A_Q), _kblk(klen, A_K), _kblk(klen, A_V),
                      pl.BlockSpec((4, A_QK_DIM), lambda b, i: (0, 0)),
                      pl.BlockSpec((1, A_V_DIM), lambda b, i: (0, 0))],
            out_specs=_oblk(A_V),
            out_shape=jax.ShapeDtypeStruct((batch, tiles * ATT_TQ, A_V), jnp.bfloat16),
            compiler_params=_cparams(("arbitrary", "arbitrary")),
            name="diffattn",
        )(qa3, ka3, va3, a_lambda, a_sub_gain.reshape(1, A_V_DIM)))
    return jnp.concatenate(outs, axis=1).reshape(t, A_V)


def _count(mask):
    return jnp.sum(jnp.where(mask, 1.0, 0.0), axis=-1, keepdims=True)


def _topk_mask(score, allowed, kpos, topk, s_len):
    bits = pltpu.bitcast(score + 0.0, jnp.int32)
    key = jnp.where(bits < 0, bits ^ jnp.int32(0x7FFFFFFF), bits)
    key = jnp.where(allowed, key, jnp.int32(INT_MIN))
    kf = float(topk)
    thr = jnp.where(_count(key >= 0) >= kf, jnp.int32(0), jnp.int32(INT_MIN))

    def value_bit(i, thr):
        cand = thr | (jnp.int32(1) << (30 - i))
        return jnp.where(_count(key >= cand) >= kf, cand, thr)

    thr = lax.fori_loop(0, 31, value_bit, thr)
    above = key > thr
    tie = key == thr
    need = kf - _count(above)
    nbits = (s_len - 1).bit_length()

    def pos_bit(i, pos):
        cand = pos | (jnp.int32(1) << (nbits - 1 - i))
        return jnp.where(_count(tie & (kpos < cand)) < need, cand, pos)

    pos = lax.fori_loop(0, nbits, pos_bit, jnp.zeros_like(thr))
    return allowed & (above | (tie & (kpos <= pos)))


def _dsa_kernel(qi_ref, wi_ref, ki_ref, qb_ref, kv_ref, wuv_ref, o_ref, *, topk, q_base):
    tq, s_len = qb_ref.shape[0], kv_ref.shape[0]
    q0 = q_base + pl.program_id(1) * tq
    qpos, kpos, cend = _positions(q0, tq, s_len)
    allowed = kpos < cend
    ki = ki_ref[...]
    wi = wi_ref[...]
    zpad = jnp.zeros((tq, LANES - IDX_DIM), jnp.bfloat16)
    score = jnp.zeros((tq, s_len), jnp.float32)
    for h in range(IDX_HEADS):
        qh = jnp.concatenate([qi_ref[:, h * IDX_DIM:(h + 1) * IDX_DIM], zpad], axis=-1)
        score = score + jnp.maximum(_dot_nt(qh, ki), 0.0) * wi[:, h:h + 1]
    score = score * ((IDX_DIM ** -0.5) * (IDX_HEADS ** -0.5))
    sel = _topk_mask(score, allowed, kpos, topk, s_len)

    dist = jnp.abs(qpos - kpos).astype(jnp.float32)
    negmask = jnp.where(sel, 0.0, NEG)
    kv = kv_ref[...]
    for h in range(B_HEADS):
        slope = 2.0 ** (-8.0 * (h + 1) / B_HEADS)
        s = _dot_nt(qb_ref[:, h * B_LAT:(h + 1) * B_LAT], kv) * (B_LAT ** -0.5) + (negmask - slope * dist)
        p, l = _softmax_rows(s)
        o_lat = _dot(p.astype(jnp.bfloat16), kv) / l
        o = _dot(o_lat.astype(jnp.bfloat16), wuv_ref[h])
        o_ref[:, h * B_V_DIM:(h + 1) * B_V_DIM] = o.astype(o_ref.dtype)


def _dsa(qi, wi, ki, qb, kvb, w_uv, batch):
    t = qb.shape[0]
    s_len = t // batch
    topk = min(DSA_TOPK_MAX, s_len // 4)
    width = B_HEADS * B_V_DIM
    qi3, wi3, ki3, qb3, kv3 = (a.reshape(batch, s_len, a.shape[1]) for a in (qi, wi, ki, qb, kvb))
    wuv = w_uv.astype(jnp.bfloat16)
    outs = []
    for first, tiles, klen in _causal_groups(s_len):
        outs.append(pl.pallas_call(
            functools.partial(_dsa_kernel, topk=topk, q_base=first * ATT_TQ),
            grid=(batch, tiles),
            in_specs=[_qblk(first, I_Q), _qblk(first, LANES), _kblk(klen, LANES),
                      _qblk(first, B_Q), _kblk(klen, B_KV),
                      pl.BlockSpec((B_HEADS, B_LAT, B_V_DIM), lambda b, i: (0, 0, 0))],
            out_specs=_oblk(width),
            out_shape=jax.ShapeDtypeStruct((batch, tiles * ATT_TQ, width), jnp.bfloat16),
            compiler_params=_cparams(("arbitrary", "arbitrary")),
            name="dsa",
        )(qi3, wi3, ki3, qb3, kv3, wuv))
    return jnp.concatenate(outs, axis=1).reshape(t, width)


def _outproj_kernel(x_ref, oa_ref, ob_ref, wo_ref, g1_ref, sh_ref, sc_ref, gn_ref, wq_ref,
                    x1_ref, h2_ref, q_ref):
    na = oa_ref.shape[1]
    y = _dot(oa_ref[...], wo_ref[:na, :]) + _dot(ob_ref[...], wo_ref[na:, :])
    x1 = x_ref[...] + g1_ref[0] * y
    x1_ref[...] = x1
    h2 = _rms_rows(x1, gn_ref[...]) * (1.0 + sc_ref[0]) + sh_ref[0]
    h2_ref[...] = h2
    q_ref[...] = _dot(h2.astype(jnp.bfloat16), wq_ref[...]).astype(q_ref.dtype)


def _outproj(x2, oa, ob, w_out, g1, sh2, sc2, g2n, peer_wq, batch):
    t, d = x2.shape
    per_b = (t // batch) // PROJ_TM
    nq = peer_wq.shape[1]
    tok = lambda width: pl.BlockSpec((PROJ_TM, width), lambda i: (i, 0))
    per_batch = pl.BlockSpec((1, 1, d), lambda i: (i // per_b, 0, 0))
    full = lambda a, b: pl.BlockSpec((a, b), lambda i: (0, 0))
    return pl.pallas_call(
        _outproj_kernel,
        grid=(t // PROJ_TM,),
        in_specs=[tok(d), tok(oa.shape[1]), tok(ob.shape[1]), full(w_out.shape[0], d),
                  per_batch, per_batch, per_batch, full(1, d), full(d, nq)],
        out_specs=[tok(d), tok(d), tok(nq)],
        out_shape=[jax.ShapeDtypeStruct((t, d), jnp.float32),
                   jax.ShapeDtypeStruct((t, d), jnp.float32),
                   jax.ShapeDtypeStruct((t, nq), jnp.bfloat16)],
        compiler_params=_cparams(("arbitrary",)),
        name="outproj",
    )(x2, oa, ob, w_out.astype(jnp.bfloat16), g1.reshape(batch, 1, d), sh2.reshape(batch, 1, d),
      sc2.reshape(batch, 1, d), g2n.reshape(1, d), peer_wq.astype(jnp.bfloat16))


def _extract_topk(vals, payload, k):
    n, tm = vals.shape
    rows = lax.broadcasted_iota(jnp.int32, (n, tm), 0).astype(jnp.float32)
    top_v, top_p = [], []
    for _ in range(k):
        m = jnp.max(vals, axis=0, keepdims=True)
        pos = jnp.min(jnp.where(vals == m, rows, float(n)), axis=0, keepdims=True)
        hit = rows == pos
        top_v.append(m)
        top_p.append(pos if payload is None else jnp.max(jnp.where(hit, payload, -1.0), axis=0, keepdims=True))
        vals = jnp.where(hit, -jnp.inf, vals)
    return jnp.concatenate(top_v, axis=0), jnp.concatenate(top_p, axis=0)


def _stair_pairs():
    return [(a, b) for a in range(PEER_TOPK) for b in range(PEER_TOPK) if (a + 1) * (b + 1) <= PEER_TOPK]


N_STAIR = len(_stair_pairs())
N_STAIR_PAD = -(-N_STAIR // SUBLANES) * SUBLANES


def _copy_rows(sel, x):
    x0 = x.astype(jnp.bfloat16)
    r1 = x - x0.astype(jnp.float32)
    x1 = r1.astype(jnp.bfloat16)
    x2 = (r1 - x1.astype(jnp.float32)).astype(jnp.bfloat16)
    return _dot(sel, x0) + _dot(sel, x1) + _dot(sel, x2)


def _peer_select_kernel(q_ref, sk_ref, sela_ref, selb_ref, e_ref, g_ref):
    tm = q_ref.shape[0]
    sela, selb = sela_ref[...], selb_ref[...]
    pad_row = lax.broadcasted_iota(jnp.int32, (N_STAIR_PAD, tm), 0) >= N_STAIR
    e_rows, g_rows = [], []
    for h in range(PEER_HEADS):
        sub = _dot_nt(sk_ref[...], q_ref[:, h * PEER_QDIM:(h + 1) * PEER_QDIM])
        sv0, si0 = _extract_topk(sub[:PEER_NKEYS], None, PEER_TOPK)
        sv1, si1 = _extract_topk(sub[PEER_NKEYS:], None, PEER_TOPK)
        cand = jnp.where(pad_row, -jnp.inf, _copy_rows(sela, sv0) + _copy_rows(selb, sv1))
        ids = (_dot(sela, si0.astype(jnp.bfloat16)) * float(PEER_NKEYS)
               + _dot(selb, si1.astype(jnp.bfloat16))) * float(HALF_ROWS)
        top_s, top_e = _extract_topk(cand, ids, PEER_TOPK)
        p = jnp.exp(top_s - top_s[0:1])
        e_rows.append(top_e)
        g_rows.append(p / jnp.sum(p, axis=0, keepdims=True))
    e_t = jnp.concatenate(e_rows, axis=0)
    g_t = jnp.concatenate(g_rows, axis=0)
    e_ref[...] = e_t.T.astype(jnp.int32)
    g_ref[...] = g_t.T


def _peer_select(q, sub_keys):
    t, nq = q.shape
    half = PEER_QDIM // 2
    sk = jnp.zeros((2 * PEER_NKEYS, PEER_QDIM), jnp.bfloat16)
    sk = sk.at[:PEER_NKEYS, :half].set(sub_keys[0].astype(jnp.bfloat16))
    sk = sk.at[PEER_NKEYS:, half:].set(sub_keys[1].astype(jnp.bfloat16))
    sel = np.zeros((2, N_STAIR_PAD, PEER_TOPK), np.float32)
    for r, (a, b) in enumerate(_stair_pairs()):
        sel[0, r, a] = 1.0
        sel[1, r, b] = 1.0
    sel = jnp.asarray(sel, jnp.bfloat16)
    tok = lambda width: pl.BlockSpec((PEER_TM, width), lambda i: (i, 0))
    full = lambda a, b: pl.BlockSpec((a, b), lambda i: (0, 0))
    return pl.pallas_call(
        _peer_select_kernel,
        grid=(t // PEER_TM,),
        in_specs=[tok(nq), full(2 * PEER_NKEYS, PEER_QDIM),
                  full(N_STAIR_PAD, PEER_TOPK), full(N_STAIR_PAD, PEER_TOPK)],
        out_specs=[tok(PEER_NSEL), tok(PEER_NSEL)],
        out_shape=[jax.ShapeDtypeStruct((t, PEER_NSEL), jnp.int32),
                   jax.ShapeDtypeStruct((t, PEER_NSEL), jnp.float32)],
        compiler_params=_cparams(("arbitrary",)),
        name="peer_select",
    )(q, sk, sel[0], sel[1])


def _gather_row(tab_ref, row):
    w = tab_ref[pl.ds(pl.multiple_of(row, HALF_ROWS), HALF_ROWS), :]
    hi = pltpu.bitcast(w & jnp.uint32(0xFFFF0000), jnp.float32)
    lo = pltpu.bitcast(w << 16, jnp.float32)
    return hi, lo


def _peer_u_kernel(idx_ref, h_ref, gate_ref, tab_ref, coef_ref, abuf_ref):
    ones = jnp.ones((2 * LANES, LANES), jnp.bfloat16)
    lane = lax.broadcasted_iota(jnp.int32, (PEER_NSEL, PEER_TM), 1)
    rows_per_tok = PEER_NSEL * HALF_ROWS

    def batch(b, a_t):
        def token(tb, c2):
            t = b * PEER_TB + tb
            x8 = h_ref[pl.ds(t, 1), :].reshape(SUBLANES, LANES)
            xh, xl = x8[:HALF_ROWS], x8[HALF_ROWS:]
            off = pl.multiple_of(tb * rows_per_tok, rows_per_tok)
            for j in range(PEER_NSEL):
                hi, lo = _gather_row(tab_ref, idx_ref[0, t, j])
                abuf_ref[pl.ds(off + j * HALF_ROWS, HALF_ROWS), :] = hi * xh + lo * xl
            return c2

        lax.fori_loop(0, PEER_TB, token, 0)
        parts = []
        for tb in range(PEER_TB):
            ab = abuf_ref[pl.ds(tb * rows_per_tok, PEER_NSEL, stride=HALF_ROWS), :]
            for s in range(1, HALF_ROWS):
                ab = ab + abuf_ref[pl.ds(tb * rows_per_tok + s, PEER_NSEL, stride=HALF_ROWS), :]
            parts.append(ab)
        ab_hi, ab_lo = _split_bf16(jnp.concatenate(parts, axis=0))
        r = _dot(jnp.concatenate([ab_hi, ab_lo], axis=1), ones)
        for tb in range(PEER_TB):
            a_t = jnp.where(lane == b * PEER_TB + tb, r[tb * PEER_NSEL:(tb + 1) * PEER_NSEL], a_t)
        return a_t

    a_t = lax.fori_loop(0, PEER_TM // PEER_TB, batch, jnp.zeros((PEER_NSEL, PEER_TM), jnp.float32))
    a = a_t.T
    coef_ref[...] = 0.5 * a * (1.0 + lax.erf(a * (2.0 ** -0.5))) * gate_ref[...]


def _peer_v_kernel(idx_ref, coef_ref, x_ref, g2_ref, tab_ref, out_ref, cb_ref):
    g2 = g2_ref[...]
    ones = jnp.ones((2 * LANES, LANES), jnp.bfloat16)
    eye = (lax.broadcasted_iota(jnp.int32, (PEER_NSEL, LANES), 0)
           == lax.broadcasted_iota(jnp.int32, (PEER_NSEL, LANES), 1))
    nacc = 2

    def batch(b, carry):
        for tb in range(PEER_TB):
            row = coef_ref[pl.ds(b * PEER_TB + tb, 1), :]
            d_hi, d_lo = _split_bf16(jnp.where(eye, jnp.broadcast_to(row, (PEER_NSEL, LANES)), 0.0))
            cb_ref[pl.ds(tb * PEER_NSEL, PEER_NSEL), :] = _dot(jnp.concatenate([d_hi, d_lo], axis=1), ones)

        def token(tb, c2):
            t = b * PEER_TB + tb
            zero = jnp.zeros((HALF_ROWS, LANES), jnp.float32)
            acc_h = [zero] * nacc
            acc_l = [zero] * nacc
            cbase = pl.multiple_of(tb * PEER_NSEL, PEER_NSEL)
            for j in range(PEER_NSEL):
                hi, lo = _gather_row(tab_ref, idx_ref[0, t, j])
                c = jnp.broadcast_to(cb_ref[pl.ds(cbase + j, 1), :], (HALF_ROWS, LANES))
                acc_h[j % nacc] = acc_h[j % nacc] + c * hi
                acc_l[j % nacc] = acc_l[j % nacc] + c * lo
            y8 = jnp.concatenate([acc_h[0] + acc_h[1], acc_l[0] + acc_l[1]], axis=0)
            out_ref[pl.ds(t, 1), :] = x_ref[pl.ds(t, 1), :] + g2 * y8.reshape(1, SUBLANES * LANES)
            return c2

        lax.fori_loop(0, PEER_TB, token, 0)
        return carry

    lax.fori_loop(0, PEER_TM // PEER_TB, batch, 0)


def _sc_peer_u(expert_ids, h2, tab, first, count):
    info = plsc.get_sparse_core_info()
    lanes, workers = info.num_lanes, info.num_cores * info.num_subcores
    half = tab.shape[1]
    per_w = count // workers
    nhalf = PEER_NSEL // 2
    assert per_w * workers == count and per_w >= 2 and nhalf % lanes == 0 and half % lanes == 0
    rows_blk = 4
    kunroll = 2
    mesh = plsc.VectorSubcoreMesh(core_axis_name="c", subcore_axis_name="s")

    @functools.partial(
        pl.kernel, mesh=mesh, name="peer_u_sc",
        compiler_params=pltpu.CompilerParams(needs_layout_passes=False),
        out_type=jax.ShapeDtypeStruct((count, PEER_NSEL), jnp.float32),
        scratch_types=[pltpu.VMEM((2, PEER_NSEL), jnp.int32),
                       pltpu.VMEM((nhalf, half), jnp.uint32), pltpu.VMEM((nhalf, half), jnp.uint32),
                       pltpu.VMEM((2, 2 * half), jnp.float32), pltpu.VMEM((2, PEER_NSEL), jnp.float32),
                       pltpu.SemaphoreType.DMA, pltpu.SemaphoreType.DMA, pltpu.SemaphoreType.DMA,
                       pltpu.SemaphoreType.DMA, pltpu.SemaphoreType.DMA])
    def run(idx_hbm, h_hbm, tab_hbm, out_hbm, idx_v, rows_a, rows_b, x_v, a_v, sem_a, sem_b, sem_i, sem_x, sem_o):
        wid = lax.axis_index("s") * info.num_cores + lax.axis_index("c")
        base = first + wid * per_w
        lane = lax.iota(jnp.int32, lanes)
        mask_hi = jnp.full((lanes,), 0xFFFF0000, jnp.uint32)

        def gather(p, part, rows, sem):
            return pltpu.make_async_copy(tab_hbm.at[idx_v.at[p, pl.ds(part * nhalf, nhalf)]], rows, sem)

        def load_idx(t, p):
            return pltpu.make_async_copy(idx_hbm.at[t], idx_v.at[p], sem_i)

        def load_x(t, p):
            return pltpu.make_async_copy(h_hbm.at[t], x_v.at[p], sem_x)

        def store_a(t, p):
            return pltpu.make_async_copy(a_v.at[p], out_hbm.at[t - first], sem_o)

        def compute(rows, p, part):
            def group(g, c2):
                vec = jnp.zeros((lanes,), jnp.float32)
                for rb in range(lanes // rows_blk):
                    def kslice(kk, accs):
                        accs = list(accs)
                        for ku in range(kunroll):
                            off = pl.multiple_of((kk * kunroll + ku) * lanes, lanes)
                            xh = x_v[p, pl.ds(off, lanes)]
                            xl = x_v[p, pl.ds(half + off, lanes)]
                            for r in range(rows_blk):
                                w = rows[g * lanes + rb * rows_blk + r, pl.ds(off, lanes)]
                                hi = plsc.bitcast(w & mask_hi, jnp.float32)
                                lo = plsc.bitcast(w << 16, jnp.float32)
                                accs[r] = accs[r] + hi * xh + lo * xl
                        return tuple(accs)

                    zero = jnp.zeros((lanes,), jnp.float32)
                    accs = lax.fori_loop(0, half // lanes // kunroll, kslice, (zero,) * rows_blk)
                    for r in range(rows_blk):
                        vec = jnp.where(lane == rb * rows_blk + r, jnp.sum(accs[r]), vec)
                a_v[p, pl.ds(pl.multiple_of(part * nhalf + g * lanes, lanes), lanes)] = vec
                return c2

            lax.fori_loop(0, nhalf // lanes, group, 0)

        load_idx(base, 0).start()
        load_x(base, 0).start()
        load_idx(base, 0).wait()
        gather(0, 0, rows_a, sem_a).start()
        gather(0, 1, rows_b, sem_b).start()
        load_x(base, 0).wait()

        def token(i, carry):
            t = base + i
            p = lax.rem(i, 2)
            q = 1 - p
            tn = jnp.minimum(t + 1, base + per_w - 1)
            load_idx(tn, q).start()
            load_x(tn, q).start()

            @pl.when(i >= 2)
            def _():
                store_a(t, p).wait()

            gather(p, 0, rows_a, sem_a).wait()
            compute(rows_a, p, 0)
            load_idx(tn, q).wait()
            gather(q, 0, rows_a, sem_a).start()
            gather(p, 1, rows_b, sem_b).wait()
            compute(rows_b, p, 1)
            gather(q, 1, rows_b, sem_b).start()
            load_x(tn, q).wait()
            store_a(t, p).start()
            return carry

        lax.fori_loop(0, per_w, token, 0)
        gather(0, 0, rows_a, sem_a).wait()
        gather(0, 1, rows_b, sem_b).wait()
        store_a(base, 0).wait()
        store_a(base, 1).wait()

    return run(expert_ids, h2, tab)


def _sc_peer_v(expert_ids, coef, x1, g2, tab, first, count, per_batch):
    info = plsc.get_sparse_core_info()
    lanes, workers = info.num_lanes, info.num_cores * info.num_subcores
    half = tab.shape[1]
    d = 2 * half
    per_w = count // workers
    nhalf = PEER_NSEL // 2
    sblk = 4
    runroll = 4
    assert per_w * workers == count and per_w >= 2 and nhalf % runroll == 0 and half % (lanes * sblk) == 0
    mesh = plsc.VectorSubcoreMesh(core_axis_name="c", subcore_axis_name="s")
    dma = pltpu.SemaphoreType.DMA

    @functools.partial(
        pl.kernel, mesh=mesh, name="peer_v_sc",
        compiler_params=pltpu.CompilerParams(needs_layout_passes=False),
        out_type=jax.ShapeDtypeStruct((count, d), jnp.float32),
        scratch_types=[pltpu.VMEM((2, PEER_NSEL), jnp.int32), pltpu.VMEM((2 * PEER_NSEL,), jnp.float32),
                       pltpu.VMEM((nhalf, half), jnp.uint32), pltpu.VMEM((nhalf, half), jnp.uint32),
                       pltpu.VMEM((2, d), jnp.float32), pltpu.VMEM((2, d), jnp.float32),
                       pltpu.VMEM((d,), jnp.float32), pltpu.VMEM((2, d), jnp.float32),
                       dma, dma, dma, dma, dma, dma, dma])
    def run(idx_hbm, coef_hbm, x_hbm, g_hbm, tab_hbm, out_hbm,
            idx_v, coef_v, rows_a, rows_b, x_v, g_v, y_v, o_v, sem_a, sem_b, sem_i, sem_c, sem_x, sem_g, sem_o):
        wid = lax.axis_index("s") * info.num_cores + lax.axis_index("c")
        base = first + wid * per_w
        mask_hi = jnp.full((lanes,), 0xFFFF0000, jnp.uint32)

        def gather(p, part, rows, sem):
            return pltpu.make_async_copy(tab_hbm.at[idx_v.at[p, pl.ds(part * nhalf, nhalf)]], rows, sem)

        def loads(t, p):
            cslot = coef_v.at[pl.ds(pl.multiple_of(p * PEER_NSEL, PEER_NSEL), PEER_NSEL)]
            return (pltpu.make_async_copy(idx_hbm.at[t], idx_v.at[p], sem_i),
                    pltpu.make_async_copy(coef_hbm.at[t], cslot, sem_c),
                    pltpu.make_async_copy(x_hbm.at[t], x_v.at[p], sem_x),
                    pltpu.make_async_copy(g_hbm.at[t // per_batch], g_v.at[p], sem_g))

        def store_o(t, p):
            return pltpu.make_async_copy(o_v.at[p], out_hbm.at[t - first], sem_o)

        def compute(rows, p, part):
            cbase = p * PEER_NSEL + part * nhalf
            for sb in range(half // lanes // sblk):
                offs = [(sb * sblk + s) * lanes for s in range(sblk)]
                if part == 0:
                    init = tuple(jnp.zeros((lanes,), jnp.float32) for _ in range(2 * sblk))
                else:
                    init = tuple([y_v[pl.ds(o, lanes)] for o in offs] + [y_v[pl.ds(half + o, lanes)] for o in offs])

                def rowloop(rr, accs):
                    accs = list(accs)
                    for ru in range(runroll):
                        r = rr * runroll + ru
                        c = plsc.load_gather(coef_v, [jnp.full((lanes,), cbase + r, jnp.int32)])
                        for s in range(sblk):
                            w = rows[r, pl.ds(offs[s], lanes)]
                            hi = plsc.bitcast(w & mask_hi, jnp.float32)
                            lo = plsc.bitcast(w << 16, jnp.float32)
                            accs[s] = accs[s] + c * hi
                            accs[sblk + s] = accs[sblk + s] + c * lo
                    return tuple(accs)

                accs = lax.fori_loop(0, nhalf // runroll, rowloop, init)
                for s in range(sblk):
                    for hl, o in ((0, offs[s]), (1, half + offs[s])):
                        if part == 0:
                            y_v[pl.ds(o, lanes)] = accs[hl * sblk + s]
                        else:
                            o_v[p, pl.ds(o, lanes)] = (x_v[p, pl.ds(o, lanes)]
                                                       + g_v[p, pl.ds(o, lanes)] * accs[hl * sblk + s])

        head = loads(base, 0)
        for c in head:
            c.start()
        head[0].wait()
        gather(0, 0, rows_a, sem_a).start()
        gather(0, 1, rows_b, sem_b).start()
        for c in head[1:]:
            c.wait()

        def token(i, carry):
            t = base + i
            p = lax.rem(i, 2)
            q = 1 - p
            tn = jnp.minimum(t + 1, base + per_w - 1)
            nxt = loads(tn, q)
            for c in nxt:
                c.start()

            @pl.when(i >= 2)
            def _():
                store_o(t, p).wait()

            gather(p, 0, rows_a, sem_a).wait()
            compute(rows_a, p, 0)
            nxt[0].wait()
            gather(q, 0, rows_a, sem_a).start()
            gather(p, 1, rows_b, sem_b).wait()
            compute(rows_b, p, 1)
            gather(q, 1, rows_b, sem_b).start()
            for c in nxt[1:]:
                c.wait()
            store_o(t, p).start()
            return carry

        lax.fori_loop(0, per_w, token, 0)
        gather(0, 0, rows_a, sem_a).wait()
        gather(0, 1, rows_b, sem_b).wait()
        store_o(base, 0).wait()
        store_o(base, 1).wait()

    return run(expert_ids, coef, x1, g2, tab)


def _gelu_gate_kernel(a_ref, g_ref, o_ref):
    a = a_ref[...]
    o_ref[...] = 0.5 * a * (1.0 + lax.erf(a * (2.0 ** -0.5))) * g_ref[...]


def _gelu_gate(a, gates, first):
    count = a.shape[0]
    return pl.pallas_call(
        _gelu_gate_kernel,
        grid=(count // PEER_TM,),
        in_specs=[pl.BlockSpec((PEER_TM, PEER_NSEL), lambda i: (i, 0)),
                  pl.BlockSpec((PEER_TM, PEER_NSEL), lambda i: (first // PEER_TM + i, 0))],
        out_specs=pl.BlockSpec((PEER_TM, PEER_NSEL), lambda i: (i, 0)),
        out_shape=jax.ShapeDtypeStruct((count, PEER_NSEL), jnp.float32),
        compiler_params=_cparams(("arbitrary",)),
        name="gelu_gate",
    )(a, gates)


def _pack_table(tab):
    n, d = tab.shape
    bits = lax.bitcast_convert_type(tab.astype(jnp.bfloat16), jnp.uint16).astype(jnp.uint32)
    packed = (bits[:, : d // 2] << 16) | bits[:, d // 2:]
    return packed.reshape(n * HALF_ROWS, LANES)


def _peer_apply(h2, x1, g2, experts, gates, u_tab, v_tab, batch):
    t, d = h2.shape
    nblk = t // PEER_TM
    per_b = nblk // batch
    assert d == SUBLANES * LANES and d // 2 == HALF_ROWS * LANES and PEER_TM % PEER_TB == 0
    idx3 = experts.reshape(nblk, PEER_TM, PEER_NSEL)
    smem_blk = pl.BlockSpec((1, PEER_TM, PEER_NSEL), lambda i: (i, 0, 0), memory_space=pltpu.SMEM)
    tok_blk = pl.BlockSpec((PEER_TM, d), lambda i: (i, 0))
    sel_blk = pl.BlockSpec((PEER_TM, PEER_NSEL), lambda i: (i, 0))
    tab_spec = pl.BlockSpec(memory_space=pltpu.VMEM)

    n_sc = PEER_SC_TOKENS
    n_tc = t - n_sc
    u_packed = _pack_table(u_tab)
    coef = pl.pallas_call(
        _peer_u_kernel,
        grid=(n_tc // PEER_TM,),
        in_specs=[smem_blk, tok_blk, sel_blk, tab_spec],
        out_specs=sel_blk,
        out_shape=jax.ShapeDtypeStruct((n_tc, PEER_NSEL), jnp.float32),
        scratch_shapes=[pltpu.VMEM((PEER_TB * PEER_NSEL * HALF_ROWS, LANES), jnp.float32)],
        compiler_params=_cparams(("arbitrary",)),
        name="peer_u",
    )(idx3, h2, gates, u_packed)
    expert_ids = experts // HALF_ROWS
    if n_sc:
        a_sc = _sc_peer_u(expert_ids, h2, u_packed.reshape(-1, d // 2), n_tc, n_sc)
        coef = jnp.concatenate([coef, _gelu_gate(a_sc, gates, n_tc)], axis=0)

    v_packed = _pack_table(v_tab)
    out = pl.pallas_call(
        _peer_v_kernel,
        grid=(n_tc // PEER_TM,),
        in_specs=[smem_blk, sel_blk, tok_blk,
                  pl.BlockSpec((None, 1, d), lambda i: (i // per_b, 0, 0)),
                  tab_spec],
        out_specs=tok_blk,
        out_shape=jax.ShapeDtypeStruct((n_tc, d), jnp.float32),
        scratch_shapes=[pltpu.VMEM((PEER_TB * PEER_NSEL, LANES), jnp.float32)],
        compiler_params=_cparams(("arbitrary",)),
        name="peer_v",
    )(idx3, coef, x1, g2.reshape(batch, 1, d), v_packed)
    if n_sc:
        out_sc = _sc_peer_v(expert_ids, coef, x1, g2, v_packed.reshape(-1, d // 2), n_tc, n_sc, t // batch)
        out = jnp.concatenate([out, out_sc], axis=0)
    return out


def kernel(x, c, ada_w, ada_b, norm1_g, norm2_g, w_in, a_qk_gain, a_lambda, a_sub_gain, b_q_gain, b_kv_gain, b_w_uv, w_out, peer_wq, peer_subkeys, peer_u, peer_v):
    b, s, d = x.shape
    t = b * s
    x2 = x.reshape(t, d)
    for l in range(ada_w.shape[0]):
        mod = _adaln(c, ada_w[l], ada_b[l])
        sh1, sc1, g1, sh2, sc2, g2 = [mod[:, i * d:(i + 1) * d] for i in range(6)]
        qa, ka, va, qb, kvb, qi, ki, wi = _inproj(x2, sh1, sc1, norm1_g[l], w_in[l], a_qk_gain[l],
                                                  b_q_gain[l], b_kv_gain[l], b)
        lam_init = 0.8 - 0.6 * math.exp(-0.3 * l)
        oa = _diffattn(qa, ka, va, a_lambda[l], a_sub_gain[l], b, lam_init)
        ob = _dsa(qi, wi, ki, qb, kvb, b_w_uv[l], b)
        x1, h2, q = _outproj(x2, oa, ob, w_out[l], g1, sh2, sc2, norm2_g[l], peer_wq[l], b)
        experts, gates = _peer_select(q, peer_subkeys[l])
        x2 = _peer_apply(h2, x1, g2, experts, gates, peer_u[l], peer_v[l], b)
    return x2.reshape(b, s, d)
```

```python
import functools
import math

import jax
import jax.numpy as jnp
import numpy as np
from jax import lax
from jax.experimental import pallas as pl
from jax.experimental.pallas import tpu as pltpu
from jax.experimental.pallas import tpu_sc as plsc

CHUNK = 64
A_HEADS, A_QK_DIM, A_V_DIM = 4, 64, 128
B_HEADS, B_LAT, B_V_DIM = 8, 128, 64
IDX_HEADS, IDX_DIM = 4, 64
DSA_TOPK_MAX = 256
A_Q = A_HEADS * 2 * A_QK_DIM
A_K = A_Q
A_V = A_HEADS * A_V_DIM
B_Q = B_HEADS * B_LAT
B_KV = B_LAT
I_Q = IDX_HEADS * IDX_DIM
I_K = IDX_DIM
I_W = IDX_HEADS
OFF_AQ, OFF_AK, OFF_AV = 0, A_Q, A_Q + A_K
OFF_BQ = OFF_AV + A_V
OFF_KV = OFF_BQ + B_Q
OFF_IQ = OFF_KV + B_KV
OFF_IK = OFF_IQ + I_Q
OFF_IW = OFF_IK + I_K
IN_COLS = OFF_IW + I_W
PEER_HEADS, PEER_NKEYS, PEER_QDIM, PEER_TOPK = 8, 128, 128, 16
EPS = 1e-6
NEG = -1e30
INT_MIN = -(2 ** 31)

SUBLANES = 8
LANES = 128
VMEM_LIMIT = 56 * 1024 * 1024

PROJ_TM = 256
ATT_TQ = 128
ATT_GROUP_SPAN = 256
PEER_TM = 128
PEER_TB = 8
PEER_NSEL = PEER_HEADS * PEER_TOPK
PEER_SC_SHARE = ((16384, 16384), (2048, 7168))
HALF_ROWS = 4


def _split_bf16(x):
    hi = x.astype(jnp.bfloat16)
    lo = (x - hi.astype(jnp.float32)).astype(jnp.bfloat16)
    return hi, lo


def _dot(a, b):
    return jnp.dot(a, b, preferred_element_type=jnp.float32)


def _dot_nt(a, b):
    return lax.dot_general(a, b, (((1,), (1,)), ((), ())), preferred_element_type=jnp.float32)


def _cparams(sem):
    return pltpu.CompilerParams(dimension_semantics=sem, vmem_limit_bytes=VMEM_LIMIT)


def _adaln_kernel(c_ref, w_ref, b_ref, o_ref):
    cf = c_ref[...]
    a = cf * (1.0 / (1.0 + jnp.exp(-cf)))
    a_hi, a_lo = _split_bf16(a)
    w_hi, w_lo = _split_bf16(w_ref[...])
    o_ref[...] = _dot(a_hi, w_hi) + _dot(a_hi, w_lo) + _dot(a_lo, w_hi) + b_ref[...]


def _adaln(c, w, b):
    bsz, d = c.shape
    n = w.shape[1]
    tn = 1024
    return pl.pallas_call(
        _adaln_kernel,
        grid=(n // tn,),
        in_specs=[pl.BlockSpec((bsz, d), lambda j: (0, 0)),
                  pl.BlockSpec((d, tn), lambda j: (0, j)),
                  pl.BlockSpec((1, tn), lambda j: (0, j))],
        out_specs=pl.BlockSpec((bsz, tn), lambda j: (0, j)),
        out_shape=jax.ShapeDtypeStruct((bsz, n), jnp.float32),
        compiler_params=_cparams(("arbitrary",)),
        name="adaln",
    )(c, w, b.reshape(1, n))


def _rms_rows(x, g):
    return x * lax.rsqrt(jnp.mean(x * x, axis=-1, keepdims=True) + EPS) * g


def _group_norm_block(p, gmat, gain, n):
    hi, lo = _split_bf16(p * p)
    ss = _dot(hi, gmat) + _dot(lo, gmat)
    return p * lax.rsqrt(ss * (1.0 / n) + EPS) * gain


def _inproj_kernel(x_ref, sh_ref, sc_ref, g_ref, w_ref, gq_ref, gk_ref, gbq_ref, gkv_ref,
                   qa_ref, ka_ref, va_ref, qb_ref, kv_ref, qi_ref, ki_ref, wi_ref):
    h = _rms_rows(x_ref[...], g_ref[...]) * (1.0 + sc_ref[0]) + sh_ref[0]
    hb = h.astype(jnp.bfloat16)
    row = lax.broadcasted_iota(jnp.int32, (LANES, LANES), 0)
    col = lax.broadcasted_iota(jnp.int32, (LANES, LANES), 1)
    g64 = jnp.where((row // A_QK_DIM) == (col // A_QK_DIM), 1.0, 0.0).astype(jnp.bfloat16)
    g128 = jnp.ones((LANES, LANES), jnp.bfloat16)

    def proj(off, width):
        return _dot(hb, w_ref[:, off:off + width])

    for blk in range(A_Q // LANES):
        sl = slice(blk * LANES, (blk + 1) * LANES)
        p = proj(OFF_AQ + blk * LANES, LANES)
        qa_ref[:, sl] = (_group_norm_block(p, g64, gq_ref[...], A_QK_DIM) * (A_QK_DIM ** -0.5)).astype(qa_ref.dtype)
        p = proj(OFF_AK + blk * LANES, LANES)
        ka_ref[:, sl] = _group_norm_block(p, g64, gk_ref[...], A_QK_DIM).astype(ka_ref.dtype)
    va_ref[...] = proj(OFF_AV, A_V).astype(va_ref.dtype)
    for blk in range(B_HEADS):
        sl = slice(blk * LANES, (blk + 1) * LANES)
        p = proj(OFF_BQ + blk * LANES, LANES)
        qb_ref[:, sl] = _group_norm_block(p, g128, gbq_ref[...], B_LAT).astype(qb_ref.dtype)
    p = proj(OFF_KV, B_KV)
    kv_ref[...] = _group_norm_block(p, g128, gkv_ref[...], B_LAT).astype(kv_ref.dtype)
    qi_ref[...] = proj(OFF_IQ, I_Q).astype(qi_ref.dtype)
    tail = proj(OFF_IK, 2 * LANES)
    ki_ref[...] = tail[:, :LANES].astype(ki_ref.dtype)
    wi_ref[...] = tail[:, I_K:I_K + LANES]


def _inproj(x2, sh1, sc1, g1n, w_in, a_qk_gain, b_q_gain, b_kv_gain, batch):
    t, d = x2.shape
    per_b = (t // batch) // PROJ_TM
    wpad = OFF_IK + 2 * LANES
    w = jnp.zeros((d, wpad), jnp.bfloat16).at[:, :IN_COLS].set(w_in.astype(jnp.bfloat16))
    gq = jnp.tile(a_qk_gain[0], 2).reshape(1, LANES)
    gk = jnp.tile(a_qk_gain[1], 2).reshape(1, LANES)
    tok = lambda width: pl.BlockSpec((PROJ_TM, width), lambda i: (i, 0))
    vec = lambda width: pl.BlockSpec((1, width), lambda i: (0, 0))
    per_batch = pl.BlockSpec((1, 1, d), lambda i: (i // per_b, 0, 0))
    bf = jnp.bfloat16
    outs = [(A_Q, bf), (A_K, bf), (A_V, bf), (B_Q, bf), (B_KV, bf), (I_Q, bf), (LANES, bf), (LANES, jnp.float32)]
    return pl.pallas_call(
        _inproj_kernel,
        grid=(t // PROJ_TM,),
        in_specs=[tok(d), per_batch, per_batch, vec(d),
                  pl.BlockSpec((d, wpad), lambda i: (0, 0)),
                  vec(LANES), vec(LANES), vec(LANES), vec(LANES)],
        out_specs=[tok(wd) for wd, _ in outs],
        out_shape=[jax.ShapeDtypeStruct((t, wd), dt) for wd, dt in outs],
        compiler_params=_cparams(("arbitrary",)),
        name="inproj",
    )(x2, sh1.reshape(batch, 1, d), sc1.reshape(batch, 1, d), g1n.reshape(1, d), w,
      gq, gk, b_q_gain.reshape(1, LANES), b_kv_gain.reshape(1, LANES))


def _positions(q0, tq, s_len):
    qpos = q0 + lax.broadcasted_iota(jnp.int32, (tq, s_len), 0)
    kpos = lax.broadcasted_iota(jnp.int32, (tq, s_len), 1)
    cend = (qpos // CHUNK + 1) * CHUNK
    return qpos, kpos, cend


def _softmax_rows(s):
    m = jnp.max(s, axis=-1, keepdims=True)
    p = jnp.exp(s - m)
    return p, jnp.sum(p, axis=-1, keepdims=True)


def _diffattn_kernel(q_ref, k_ref, v_ref, lam_ref, gain_ref, o_ref, *, lam_init, q_base):
    tq, s_len = q_ref.shape[0], k_ref.shape[0]
    q0 = q_base + pl.program_id(1) * tq
    qpos, kpos, cend = _positions(q0, tq, s_len)
    dist = jnp.abs(qpos - kpos).astype(jnp.float32)
    negmask = jnp.where(kpos < cend, 0.0, NEG)
    lf = lam_ref[...]
    lam = (jnp.exp(jnp.sum(lf[0:1] * lf[1:2], axis=-1, keepdims=True))
           - jnp.exp(jnp.sum(lf[2:3] * lf[3:4], axis=-1, keepdims=True)) + lam_init)
    for h in range(A_HEADS):
        slope = 2.0 ** (-8.0 * (h + 1) / A_HEADS)
        bias = negmask - slope * dist
        v = v_ref[:, h * A_V_DIM:(h + 1) * A_V_DIM]
        outs = []
        for m in range(2):
            c0 = (h * 2 + m) * A_QK_DIM
            s = _dot_nt(q_ref[:, c0:c0 + A_QK_DIM], k_ref[:, c0:c0 + A_QK_DIM]) + bias
            p, l = _softmax_rows(s)
            outs.append(_dot(p.astype(jnp.bfloat16), v) / l)
        o = outs[0] - lam * outs[1]
        o = _rms_rows(o, gain_ref[...]) * (1.0 - lam_init)
        o_ref[:, h * A_V_DIM:(h + 1) * A_V_DIM] = o.astype(o_ref.dtype)


def _causal_groups(s_len):
    span = min(ATT_GROUP_SPAN, s_len)
    tiles = span // ATT_TQ
    return [(g * tiles, tiles, (g + 1) * span) for g in range(s_len // span)]


def _qblk(first_tile, width):
    return pl.BlockSpec((None, ATT_TQ, width), lambda b, i: (b, first_tile + i, 0))


def _kblk(klen, width):
    return pl.BlockSpec((None, klen, width), lambda b, i: (b, 0, 0))


def _oblk(width):
    return pl.BlockSpec((None, ATT_TQ, width), lambda b, i: (b, i, 0))


def _diffattn(qa, ka, va, a_lambda, a_sub_gain, batch, lam_init):
    t = qa.shape[0]
    s_len = t // batch
    qa3, ka3, va3 = (a.reshape(batch, s_len, a.shape[1]) for a in (qa, ka, va))
    outs = []
    for first, tiles, klen in _causal_groups(s_len):
        outs.append(pl.pallas_call(
            functools.partial(_diffattn_kernel, lam_init=lam_init, q_base=first * ATT_TQ),
            grid=(batch, tiles),
            in_specs=[_qblk(first, A_Q), _kblk(klen, A_K), _kblk(klen, A_V),
                      pl.BlockSpec((4, A_QK_DIM), lambda b, i: (0, 0)),
                      pl.BlockSpec((1, A_V_DIM), lambda b, i: (0, 0))],
            out_specs=_oblk(A_V),
            out_shape=jax.ShapeDtypeStruct((batch, tiles * ATT_TQ, A_V), jnp.bfloat16),
            compiler_params=_cparams(("arbitrary", "arbitrary")),
            name="diffattn",
        )(qa3, ka3, va3, a_lambda, a_sub_gain.reshape(1, A_V_DIM)))
    return jnp.concatenate(outs, axis=1).reshape(t, A_V)


def _count(mask):
    return jnp.sum(jnp.where(mask, 1.0, 0.0), axis=-1, keepdims=True)


def _topk_mask(score, allowed, kpos, topk, s_len):
    bits = pltpu.bitcast(score + 0.0, jnp.int32)
    key = jnp.where(bits < 0, bits ^ jnp.int32(0x7FFFFFFF), bits)
    key = jnp.where(allowed, key, jnp.int32(INT_MIN))
    kf = float(topk)
    thr = jnp.where(_count(key >= 0) >= kf, jnp.int32(0), jnp.int32(INT_MIN))

    def value_bit(i, thr):
        cand = thr | (jnp.int32(1) << (30 - i))
        return jnp.where(_count(key >= cand) >= kf, cand, thr)

    thr = lax.fori_loop(0, 31, value_bit, thr)
    above = key > thr
    tie = key == thr
    need = kf - _count(above)
    nbits = (s_len - 1).bit_length()

    def pos_bit(i, pos):
        cand = pos | (jnp.int32(1) << (nbits - 1 - i))
        return jnp.where(_count(tie & (kpos < cand)) < need, cand, pos)

    pos = lax.fori_loop(0, nbits, pos_bit, jnp.zeros_like(thr))
    return allowed & (above | (tie & (kpos <= pos)))


def _dsa_kernel(qi_ref, wi_ref, ki_ref, qb_ref, kv_ref, wuv_ref, o_ref, *, topk, q_base):
    tq, s_len = qb_ref.shape[0], kv_ref.shape[0]
    q0 = q_base + pl.program_id(1) * tq
    qpos, kpos, cend = _positions(q0, tq, s_len)
    allowed = kpos < cend
    ki = ki_ref[...]
    wi = wi_ref[...]
    zpad = jnp.zeros((tq, LANES - IDX_DIM), jnp.bfloat16)
    score = jnp.zeros((tq, s_len), jnp.float32)
    for h in range(IDX_HEADS):
        qh = jnp.concatenate([qi_ref[:, h * IDX_DIM:(h + 1) * IDX_DIM], zpad], axis=-1)
        score = score + jnp.maximum(_dot_nt(qh, ki), 0.0) * wi[:, h:h + 1]
    score = score * ((IDX_DIM ** -0.5) * (IDX_HEADS ** -0.5))
    sel = _topk_mask(score, allowed, kpos, topk, s_len)

    dist = jnp.abs(qpos - kpos).astype(jnp.float32)
    negmask = jnp.where(sel, 0.0, NEG)
    kv = kv_ref[...]
    for h in range(B_HEADS):
        slope = 2.0 ** (-8.0 * (h + 1) / B_HEADS)
        s = _dot_nt(qb_ref[:, h * B_LAT:(h + 1) * B_LAT], kv) * (B_LAT ** -0.5) + (negmask - slope * dist)
        p, l = _softmax_rows(s)
        o_lat = _dot(p.astype(jnp.bfloat16), kv) / l
        o = _dot(o_lat.astype(jnp.bfloat16), wuv_ref[h])
        o_ref[:, h * B_V_DIM:(h + 1) * B_V_DIM] = o.astype(o_ref.dtype)


def _dsa(qi, wi, ki, qb, kvb, w_uv, batch):
    t = qb.shape[0]
    s_len = t // batch
    topk = min(DSA_TOPK_MAX, s_len // 4)
    width = B_HEADS * B_V_DIM
    qi3, wi3, ki3, qb3, kv3 = (a.reshape(batch, s_len, a.shape[1]) for a in (qi, wi, ki, qb, kvb))
    wuv = w_uv.astype(jnp.bfloat16)
    outs = []
    for first, tiles, klen in _causal_groups(s_len):
        outs.append(pl.pallas_call(
            functools.partial(_dsa_kernel, topk=topk, q_base=first * ATT_TQ),
            grid=(batch, tiles),
            in_specs=[_qblk(first, I_Q), _qblk(first, LANES), _kblk(klen, LANES),
                      _qblk(first, B_Q), _kblk(klen, B_KV),
                      pl.BlockSpec((B_HEADS, B_LAT, B_V_DIM), lambda b, i: (0, 0, 0))],
            out_specs=_oblk(width),
            out_shape=jax.ShapeDtypeStruct((batch, tiles * ATT_TQ, width), jnp.bfloat16),
            compiler_params=_cparams(("arbitrary", "arbitrary")),
            name="dsa",
        )(qi3, wi3, ki3, qb3, kv3, wuv))
    return jnp.concatenate(outs, axis=1).reshape(t, width)


def _outproj_kernel(x_ref, oa_ref, ob_ref, wo_ref, g1_ref, sh_ref, sc_ref, gn_ref, wq_ref,
                    x1_ref, h2_ref, q_ref):
    na = oa_ref.shape[1]
    y = _dot(oa_ref[...], wo_ref[:na, :]) + _dot(ob_ref[...], wo_ref[na:, :])
    x1 = x_ref[...] + g1_ref[0] * y
    x1_ref[...] = x1
    h2 = _rms_rows(x1, gn_ref[...]) * (1.0 + sc_ref[0]) + sh_ref[0]
    h2_ref[...] = h2
    q_ref[...] = _dot(h2.astype(jnp.bfloat16), wq_ref[...]).astype(q_ref.dtype)


def _outproj(x2, oa, ob, w_out, g1, sh2, sc2, g2n, peer_wq, batch):
    t, d = x2.shape
    per_b = (t // batch) // PROJ_TM
    nq = peer_wq.shape[1]
    tok = lambda width: pl.BlockSpec((PROJ_TM, width), lambda i: (i, 0))
    per_batch = pl.BlockSpec((1, 1, d), lambda i: (i // per_b, 0, 0))
    full = lambda a, b: pl.BlockSpec((a, b), lambda i: (0, 0))
    return pl.pallas_call(
        _outproj_kernel,
        grid=(t // PROJ_TM,),
        in_specs=[tok(d), tok(oa.shape[1]), tok(ob.shape[1]), full(w_out.shape[0], d),
                  per_batch, per_batch, per_batch, full(1, d), full(d, nq)],
        out_specs=[tok(d), tok(d), tok(nq)],
        out_shape=[jax.ShapeDtypeStruct((t, d), jnp.float32),
                   jax.ShapeDtypeStruct((t, d), jnp.float32),
                   jax.ShapeDtypeStruct((t, nq), jnp.bfloat16)],
        compiler_params=_cparams(("arbitrary",)),
        name="outproj",
    )(x2, oa, ob, w_out.astype(jnp.bfloat16), g1.reshape(batch, 1, d), sh2.reshape(batch, 1, d),
      sc2.reshape(batch, 1, d), g2n.reshape(1, d), peer_wq.astype(jnp.bfloat16))


def _extract_topk(vals, payload, k):
    n, tm = vals.shape
    rows = lax.broadcasted_iota(jnp.int32, (n, tm), 0).astype(jnp.float32)
    top_v, top_p = [], []
    for _ in range(k):
        m = jnp.max(vals, axis=0, keepdims=True)
        pos = jnp.min(jnp.where(vals == m, rows, float(n)), axis=0, keepdims=True)
        hit = rows == pos
        top_v.append(m)
        top_p.append(pos if payload is None else jnp.max(jnp.where(hit, payload, -1.0), axis=0, keepdims=True))
        vals = jnp.where(hit, -jnp.inf, vals)
    return jnp.concatenate(top_v, axis=0), jnp.concatenate(top_p, axis=0)


def _stair_pairs():
    return [(a, b) for a in range(PEER_TOPK) for b in range(PEER_TOPK) if (a + 1) * (b + 1) <= PEER_TOPK]


N_STAIR = len(_stair_pairs())
N_STAIR_PAD = -(-N_STAIR // SUBLANES) * SUBLANES


def _copy_rows(sel, x):
    x0 = x.astype(jnp.bfloat16)
    r1 = x - x0.astype(jnp.float32)
    x1 = r1.astype(jnp.bfloat16)
    x2 = (r1 - x1.astype(jnp.float32)).astype(jnp.bfloat16)
    return _dot(sel, x0) + _dot(sel, x1) + _dot(sel, x2)


def _peer_select_kernel(q_ref, sk_ref, sela_ref, selb_ref, e_ref, g_ref):
    tm = q_ref.shape[0]
    sela, selb = sela_ref[...], selb_ref[...]
    pad_row = lax.broadcasted_iota(jnp.int32, (N_STAIR_PAD, tm), 0) >= N_STAIR
    e_rows, g_rows = [], []
    for h in range(PEER_HEADS):
        sub = _dot_nt(sk_ref[...], q_ref[:, h * PEER_QDIM:(h + 1) * PEER_QDIM])
        sv0, si0 = _extract_topk(sub[:PEER_NKEYS], None, PEER_TOPK)
        sv1, si1 = _extract_topk(sub[PEER_NKEYS:], None, PEER_TOPK)
        cand = jnp.where(pad_row, -jnp.inf, _copy_rows(sela, sv0) + _copy_rows(selb, sv1))
        ids = (_dot(sela, si0.astype(jnp.bfloat16)) * float(PEER_NKEYS)
               + _dot(selb, si1.astype(jnp.bfloat16))) * float(HALF_ROWS)
        top_s, top_e = _extract_topk(cand, ids, PEER_TOPK)
        p = jnp.exp(top_s - top_s[0:1])
        e_rows.append(top_e)
        g_rows.append(p / jnp.sum(p, axis=0, keepdims=True))
    e_t = jnp.concatenate(e_rows, axis=0)
    g_t = jnp.concatenate(g_rows, axis=0)
    e_ref[...] = e_t.T.astype(jnp.int32)
    g_ref[...] = g_t.T


def _peer_select(q, sub_keys):
    t, nq = q.shape
    half = PEER_QDIM // 2
    sk = jnp.zeros((2 * PEER_NKEYS, PEER_QDIM), jnp.bfloat16)
    sk = sk.at[:PEER_NKEYS, :half].set(sub_keys[0].astype(jnp.bfloat16))
    sk = sk.at[PEER_NKEYS:, half:].set(sub_keys[1].astype(jnp.bfloat16))
    sel = np.zeros((2, N_STAIR_PAD, PEER_TOPK), np.float32)
    for r, (a, b) in enumerate(_stair_pairs()):
        sel[0, r, a] = 1.0
        sel[1, r, b] = 1.0
    sel = jnp.asarray(sel, jnp.bfloat16)
    tok = lambda width: pl.BlockSpec((PEER_TM, width), lambda i: (i, 0))
    full = lambda a, b: pl.BlockSpec((a, b), lambda i: (0, 0))
    return pl.pallas_call(
        _peer_select_kernel,
        grid=(t // PEER_TM,),
        in_specs=[tok(nq), full(2 * PEER_NKEYS, PEER_QDIM),
                  full(N_STAIR_PAD, PEER_TOPK), full(N_STAIR_PAD, PEER_TOPK)],
        out_specs=[tok(PEER_NSEL), tok(PEER_NSEL)],
        out_shape=[jax.ShapeDtypeStruct((t, PEER_NSEL), jnp.int32),
                   jax.ShapeDtypeStruct((t, PEER_NSEL), jnp.float32)],
        compiler_params=_cparams(("arbitrary",)),
        name="peer_select",
    )(q, sk, sel[0], sel[1])


def _gather_row(tab_ref, row):
    w = tab_ref[pl.ds(pl.multiple_of(row, HALF_ROWS), HALF_ROWS), :]
    hi = pltpu.bitcast(w & jnp.uint32(0xFFFF0000), jnp.float32)
    lo = pltpu.bitcast(w << 16, jnp.float32)
    return hi, lo


def _peer_u_kernel(idx_ref, h_ref, gate_ref, tab_ref, coef_ref, abuf_ref):
    ones = jnp.ones((2 * LANES, LANES), jnp.bfloat16)
    lane = lax.broadcasted_iota(jnp.int32, (PEER_NSEL, PEER_TM), 1)
    rows_per_tok = PEER_NSEL * HALF_ROWS

    def batch(b, a_t):
        def token(tb, c2):
            t = b * PEER_TB + tb
            x8 = h_ref[pl.ds(t, 1), :].reshape(SUBLANES, LANES)
            xh, xl = x8[:HALF_ROWS], x8[HALF_ROWS:]
            off = pl.multiple_of(tb * rows_per_tok, rows_per_tok)
            for j in range(PEER_NSEL):
                hi, lo = _gather_row(tab_ref, idx_ref[0, t, j])
                abuf_ref[pl.ds(off + j * HALF_ROWS, HALF_ROWS), :] = hi * xh + lo * xl
            return c2

        lax.fori_loop(0, PEER_TB, token, 0)
        parts = []
        for tb in range(PEER_TB):
            ab = abuf_ref[pl.ds(tb * rows_per_tok, PEER_NSEL, stride=HALF_ROWS), :]
            for s in range(1, HALF_ROWS):
                ab = ab + abuf_ref[pl.ds(tb * rows_per_tok + s, PEER_NSEL, stride=HALF_ROWS), :]
            parts.append(ab)
        ab_hi, ab_lo = _split_bf16(jnp.concatenate(parts, axis=0))
        r = _dot(jnp.concatenate([ab_hi, ab_lo], axis=1), ones)
        for tb in range(PEER_TB):
            a_t = jnp.where(lane == b * PEER_TB + tb, r[tb * PEER_NSEL:(tb + 1) * PEER_NSEL], a_t)
        return a_t

    a_t = lax.fori_loop(0, PEER_TM // PEER_TB, batch, jnp.zeros((PEER_NSEL, PEER_TM), jnp.float32))
    a = a_t.T
    coef_ref[...] = 0.5 * a * (1.0 + lax.erf(a * (2.0 ** -0.5))) * gate_ref[...]


def _peer_v_kernel(idx_ref, coef_ref, x_ref, g2_ref, tab_ref, out_ref, cb_ref):
    g2 = g2_ref[...]
    ones = jnp.ones((2 * LANES, LANES), jnp.bfloat16)
    eye = (lax.broadcasted_iota(jnp.int32, (PEER_NSEL, LANES), 0)
           == lax.broadcasted_iota(jnp.int32, (PEER_NSEL, LANES), 1))
    nacc = 2

    def batch(b, carry):
        for tb in range(PEER_TB):
            row = coef_ref[pl.ds(b * PEER_TB + tb, 1), :]
            d_hi, d_lo = _split_bf16(jnp.where(eye, jnp.broadcast_to(row, (PEER_NSEL, LANES)), 0.0))
            cb_ref[pl.ds(tb * PEER_NSEL, PEER_NSEL), :] = _dot(jnp.concatenate([d_hi, d_lo], axis=1), ones)

        def token(tb, c2):
            t = b * PEER_TB + tb
            zero = jnp.zeros((HALF_ROWS, LANES), jnp.float32)
            acc_h = [zero] * nacc
            acc_l = [zero] * nacc
            cbase = pl.multiple_of(tb * PEER_NSEL, PEER_NSEL)
            for j in range(PEER_NSEL):
                hi, lo = _gather_row(tab_ref, idx_ref[0, t, j])
                c = jnp.broadcast_to(cb_ref[pl.ds(cbase + j, 1), :], (HALF_ROWS, LANES))
                acc_h[j % nacc] = acc_h[j % nacc] + c * hi
                acc_l[j % nacc] = acc_l[j % nacc] + c * lo
            y8 = jnp.concatenate([acc_h[0] + acc_h[1], acc_l[0] + acc_l[1]], axis=0)
            out_ref[pl.ds(t, 1), :] = x_ref[pl.ds(t, 1), :] + g2 * y8.reshape(1, SUBLANES * LANES)
            return c2

        lax.fori_loop(0, PEER_TB, token, 0)
        return carry

    lax.fori_loop(0, PEER_TM // PEER_TB, batch, 0)


def _sc_peer_u(expert_ids, h2, tab, first, count):
    info = plsc.get_sparse_core_info()
    lanes, workers = info.num_lanes, info.num_cores * info.num_subcores
    half = tab.shape[1]
    per_w = count // workers
    nhalf = PEER_NSEL // 2
    assert per_w * workers == count and per_w >= 2 and nhalf % lanes == 0 and half % lanes == 0
    rows_blk = 4
    kunroll = 2
    mesh = plsc.VectorSubcoreMesh(core_axis_name="c", subcore_axis_name="s")

    @functools.partial(
        pl.kernel, mesh=mesh, name="peer_u_sc",
        compiler_params=pltpu.CompilerParams(needs_layout_passes=False),
        out_type=jax.ShapeDtypeStruct((count, PEER_NSEL), jnp.float32),
        scratch_types=[pltpu.VMEM((2, PEER_NSEL), jnp.int32),
                       pltpu.VMEM((nhalf, half), jnp.uint32), pltpu.VMEM((nhalf, half), jnp.uint32),
                       pltpu.VMEM((2, 2 * half), jnp.float32), pltpu.VMEM((2, PEER_NSEL), jnp.float32),
                       pltpu.SemaphoreType.DMA, pltpu.SemaphoreType.DMA, pltpu.SemaphoreType.DMA,
                       pltpu.SemaphoreType.DMA, pltpu.SemaphoreType.DMA])
    def run(idx_hbm, h_hbm, tab_hbm, out_hbm, idx_v, rows_a, rows_b, x_v, a_v, sem_a, sem_b, sem_i, sem_x, sem_o):
        wid = lax.axis_index("s") * info.num_cores + lax.axis_index("c")
        base = first + wid * per_w
        lane = lax.iota(jnp.int32, lanes)
        mask_hi = jnp.full((lanes,), 0xFFFF0000, jnp.uint32)

        def gather(p, part, rows, sem):
            return pltpu.make_async_copy(tab_hbm.at[idx_v.at[p, pl.ds(part * nhalf, nhalf)]], rows, sem)

        def load_idx(t, p):
            return pltpu.make_async_copy(idx_hbm.at[t], idx_v.at[p], sem_i)

        def load_x(t, p):
            return pltpu.make_async_copy(h_hbm.at[t], x_v.at[p], sem_x)

        def store_a(t, p):
            return pltpu.make_async_copy(a_v.at[p], out_hbm.at[t - first], sem_o)

        def compute(rows, p, part):
            def group(g, c2):
                vec = jnp.zeros((lanes,), jnp.float32)
                for rb in range(lanes // rows_blk):
                    def kslice(kk, accs):
                        accs = list(accs)
                        for ku in range(kunroll):
                            off = pl.multiple_of((kk * kunroll + ku) * lanes, lanes)
                            xh = x_v[p, pl.ds(off, lanes)]
                            xl = x_v[p, pl.ds(half + off, lanes)]
                            for r in range(rows_blk):
                                w = rows[g * lanes + rb * rows_blk + r, pl.ds(off, lanes)]
                                hi = plsc.bitcast(w & mask_hi, jnp.float32)
                                lo = plsc.bitcast(w << 16, jnp.float32)
                                accs[r] = accs[r] + hi * xh + lo * xl
                        return tuple(accs)

                    zero = jnp.zeros((lanes,), jnp.float32)
                    accs = lax.fori_loop(0, half // lanes // kunroll, kslice, (zero,) * rows_blk)
                    for r in range(rows_blk):
                        vec = jnp.where(lane == rb * rows_blk + r, jnp.sum(accs[r]), vec)
                a_v[p, pl.ds(pl.multiple_of(part * nhalf + g * lanes, lanes), lanes)] = vec
                return c2

            lax.fori_loop(0, nhalf // lanes, group, 0)

        load_idx(base, 0).start()
        load_x(base, 0).start()
        load_idx(base, 0).wait()
        gather(0, 0, rows_a, sem_a).start()
        gather(0, 1, rows_b, sem_b).start()
        load_x(base, 0).wait()

        def token(i, carry):
            t = base + i
            p = lax.rem(i, 2)
            q = 1 - p
            tn = jnp.minimum(t + 1, base + per_w - 1)
            load_idx(tn, q).start()
            load_x(tn, q).start()

            @pl.when(i >= 2)
            def _():
                store_a(t, p).wait()

            gather(p, 0, rows_a, sem_a).wait()
            compute(rows_a, p, 0)
            load_idx(tn, q).wait()
            gather(q, 0, rows_a, sem_a).start()
            gather(p, 1, rows_b, sem_b).wait()
            compute(rows_b, p, 1)
            gather(q, 1, rows_b, sem_b).start()
            load_x(tn, q).wait()
            store_a(t, p).start()
            return carry

        lax.fori_loop(0, per_w, token, 0)
        gather(0, 0, rows_a, sem_a).wait()
        gather(0, 1, rows_b, sem_b).wait()
        store_a(base, 0).wait()
        store_a(base, 1).wait()

    return run(expert_ids, h2, tab)


def _sc_peer_v(expert_ids, coef, x1, g2, tab, first, count, per_batch):
    info = plsc.get_sparse_core_info()
    lanes, workers = info.num_lanes, info.num_cores * info.num_subcores
    half = tab.shape[1]
    d = 2 * half
    per_w = count // workers
    nhalf = PEER_NSEL // 2
    sblk = 4
    runroll = 4
    assert per_w * workers == count and per_w >= 2 and nhalf % runroll == 0 and half % (lanes * sblk) == 0
    mesh = plsc.VectorSubcoreMesh(core_axis_name="c", subcore_axis_name="s")
    dma = pltpu.SemaphoreType.DMA

    @functools.partial(
        pl.kernel, mesh=mesh, name="peer_v_sc",
        compiler_params=pltpu.CompilerParams(needs_layout_passes=False),
        out_type=jax.ShapeDtypeStruct((count, d), jnp.float32),
        scratch_types=[pltpu.VMEM((2, PEER_NSEL), jnp.int32), pltpu.VMEM((2 * PEER_NSEL,), jnp.float32),
                       pltpu.VMEM((nhalf, half), jnp.uint32), pltpu.VMEM((nhalf, half), jnp.uint32),
                       pltpu.VMEM((2, d), jnp.float32), pltpu.VMEM((2, d), jnp.float32),
                       pltpu.VMEM((d,), jnp.float32), pltpu.VMEM((2, d), jnp.float32),
                       dma, dma, dma, dma, dma, dma, dma])
    def run(idx_hbm, coef_hbm, x_hbm, g_hbm, tab_hbm, out_hbm,
            idx_v, coef_v, rows_a, rows_b, x_v, g_v, y_v, o_v, sem_a, sem_b, sem_i, sem_c, sem_x, sem_g, sem_o):
        wid = lax.axis_index("s") * info.num_cores + lax.axis_index("c")
        base = first + wid * per_w
        mask_hi = jnp.full((lanes,), 0xFFFF0000, jnp.uint32)

        def gather(p, part, rows, sem):
            return pltpu.make_async_copy(tab_hbm.at[idx_v.at[p, pl.ds(part * nhalf, nhalf)]], rows, sem)

        def loads(t, p):
            cslot = coef_v.at[pl.ds(pl.multiple_of(p * PEER_NSEL, PEER_NSEL), PEER_NSEL)]
            return (pltpu.make_async_copy(idx_hbm.at[t], idx_v.at[p], sem_i),
                    pltpu.make_async_copy(coef_hbm.at[t], cslot, sem_c),
                    pltpu.make_async_copy(x_hbm.at[t], x_v.at[p], sem_x),
                    pltpu.make_async_copy(g_hbm.at[t // per_batch], g_v.at[p], sem_g))

        def store_o(t, p):
            return pltpu.make_async_copy(o_v.at[p], out_hbm.at[t - first], sem_o)

        def compute(rows, p, part):
            cbase = p * PEER_NSEL + part * nhalf
            for sb in range(half // lanes // sblk):
                offs = [(sb * sblk + s) * lanes for s in range(sblk)]
                if part == 0:
                    init = tuple(jnp.zeros((lanes,), jnp.float32) for _ in range(2 * sblk))
                else:
                    init = tuple([y_v[pl.ds(o, lanes)] for o in offs] + [y_v[pl.ds(half + o, lanes)] for o in offs])

                def rowloop(rr, accs):
                    accs = list(accs)
                    for ru in range(runroll):
                        r = rr * runroll + ru
                        c = plsc.load_gather(coef_v, [jnp.full((lanes,), cbase + r, jnp.int32)])
                        for s in range(sblk):
                            w = rows[r, pl.ds(offs[s], lanes)]
                            hi = plsc.bitcast(w & mask_hi, jnp.float32)
                            lo = plsc.bitcast(w << 16, jnp.float32)
                            accs[s] = accs[s] + c * hi
                            accs[sblk + s] = accs[sblk + s] + c * lo
                    return tuple(accs)

                accs = lax.fori_loop(0, nhalf // runroll, rowloop, init)
                for s in range(sblk):
                    for hl, o in ((0, offs[s]), (1, half + offs[s])):
                        if part == 0:
                            y_v[pl.ds(o, lanes)] = accs[hl * sblk + s]
                        else:
                            o_v[p, pl.ds(o, lanes)] = (x_v[p, pl.ds(o, lanes)]
                                                       + g_v[p, pl.ds(o, lanes)] * accs[hl * sblk + s])

        head = loads(base, 0)
        for c in head:
            c.start()
        head[0].wait()
        gather(0, 0, rows_a, sem_a).start()
        gather(0, 1, rows_b, sem_b).start()
        for c in head[1:]:
            c.wait()

        def token(i, carry):
            t = base + i
            p = lax.rem(i, 2)
            q = 1 - p
            tn = jnp.minimum(t + 1, base + per_w - 1)
            nxt = loads(tn, q)
            for c in nxt:
                c.start()

            @pl.when(i >= 2)
            def _():
                store_o(t, p).wait()

            gather(p, 0, rows_a, sem_a).wait()
            compute(rows_a, p, 0)
            nxt[0].wait()
            gather(q, 0, rows_a, sem_a).start()
            gather(p, 1, rows_b, sem_b).wait()
            compute(rows_b, p, 1)
            gather(q, 1, rows_b, sem_b).start()
            for c in nxt[1:]:
                c.wait()
            store_o(t, p).start()
            return carry

        lax.fori_loop(0, per_w, token, 0)
        gather(0, 0, rows_a, sem_a).wait()
        gather(0, 1, rows_b, sem_b).wait()
        store_o(base, 0).wait()
        store_o(base, 1).wait()

    return run(expert_ids, coef, x1, g2, tab)


def _gelu_gate_kernel(a_ref, g_ref, o_ref):
    a = a_ref[...]
    o_ref[...] = 0.5 * a * (1.0 + lax.erf(a * (2.0 ** -0.5))) * g_ref[...]


def _gelu_gate(a, gates, first):
    count = a.shape[0]
    return pl.pallas_call(
        _gelu_gate_kernel,
        grid=(count // PEER_TM,),
        in_specs=[pl.BlockSpec((PEER_TM, PEER_NSEL), lambda i: (i, 0)),
                  pl.BlockSpec((PEER_TM, PEER_NSEL), lambda i: (first // PEER_TM + i, 0))],
        out_specs=pl.BlockSpec((PEER_TM, PEER_NSEL), lambda i: (i, 0)),
        out_shape=jax.ShapeDtypeStruct((count, PEER_NSEL), jnp.float32),
        compiler_params=_cparams(("arbitrary",)),
        name="gelu_gate",
    )(a, gates)


def _pack_table(tab):
    n, d = tab.shape
    bits = lax.bitcast_convert_type(tab.astype(jnp.bfloat16), jnp.uint16).astype(jnp.uint32)
    packed = (bits[:, : d // 2] << 16) | bits[:, d // 2:]
    return packed.reshape(n * HALF_ROWS, LANES)


def _peer_apply(h2, x1, g2, experts, gates, u_packed, v_packed, batch, n_sc_u, n_sc_v):
    t, d = h2.shape
    nblk = t // PEER_TM
    per_b = nblk // batch
    assert d == SUBLANES * LANES and d // 2 == HALF_ROWS * LANES and PEER_TM % PEER_TB == 0
    idx3 = experts.reshape(nblk, PEER_TM, PEER_NSEL)
    expert_ids = experts // HALF_ROWS
    smem_blk = pl.BlockSpec((1, PEER_TM, PEER_NSEL), lambda i: (i, 0, 0), memory_space=pltpu.SMEM)
    tok_blk = pl.BlockSpec((PEER_TM, d), lambda i: (i, 0))
    sel_blk = pl.BlockSpec((PEER_TM, PEER_NSEL), lambda i: (i, 0))
    tab_spec = pl.BlockSpec(memory_space=pltpu.VMEM)

    parts = []
    n_tc = t - n_sc_u
    if n_tc:
        parts.append(pl.pallas_call(
            _peer_u_kernel,
            grid=(n_tc // PEER_TM,),
            in_specs=[smem_blk, tok_blk, sel_blk, tab_spec],
            out_specs=sel_blk,
            out_shape=jax.ShapeDtypeStruct((n_tc, PEER_NSEL), jnp.float32),
            scratch_shapes=[pltpu.VMEM((PEER_TB * PEER_NSEL * HALF_ROWS, LANES), jnp.float32)],
            compiler_params=_cparams(("arbitrary",)),
            name="peer_u",
        )(idx3, h2, gates, u_packed))
    if n_sc_u:
        a_sc = _sc_peer_u(expert_ids, h2, u_packed.reshape(-1, d // 2), n_tc, n_sc_u)
        parts.append(_gelu_gate(a_sc, gates, n_tc))
    coef = parts[0] if len(parts) == 1 else jnp.concatenate(parts, axis=0)

    parts = []
    n_tc = t - n_sc_v
    if n_tc:
        parts.append(pl.pallas_call(
            _peer_v_kernel,
            grid=(n_tc // PEER_TM,),
            in_specs=[smem_blk, sel_blk, tok_blk,
                      pl.BlockSpec((None, 1, d), lambda i: (i // per_b, 0, 0)),
                      tab_spec],
            out_specs=tok_blk,
            out_shape=jax.ShapeDtypeStruct((n_tc, d), jnp.float32),
            scratch_shapes=[pltpu.VMEM((PEER_TB * PEER_NSEL, LANES), jnp.float32)],
            compiler_params=_cparams(("arbitrary",)),
            name="peer_v",
        )(idx3, coef, x1, g2.reshape(batch, 1, d), v_packed))
    if n_sc_v:
        parts.append(_sc_peer_v(expert_ids, coef, x1, g2, v_packed.reshape(-1, d // 2), n_tc, n_sc_v, t // batch))
    return parts


def kernel(x, c, ada_w, ada_b, norm1_g, norm2_g, w_in, a_qk_gain, a_lambda, a_sub_gain, b_q_gain, b_kv_gain, b_w_uv, w_out, peer_wq, peer_subkeys, peer_u, peer_v):
    b, s, d = x.shape
    t = b * s
    x2 = x.reshape(t, d)
    nchunk = len(PEER_SC_SHARE)
    bc = b // nchunk
    tc = bc * s
    for l in range(ada_w.shape[0]):
        mod = _adaln(c, ada_w[l], ada_b[l])
        u_packed, v_packed = _pack_table(peer_u[l]), _pack_table(peer_v[l])
        lam_init = 0.8 - 0.6 * math.exp(-0.3 * l)
        outs = []
        experts = None
        for ck, (n_sc_u, n_sc_v) in enumerate(PEER_SC_SHARE):
            xc = x2[ck * tc:(ck + 1) * tc]
            if experts is not None:
                xc, experts = lax.optimization_barrier((xc, experts))
            sh1, sc1, g1, sh2, sc2, g2 = [mod[ck * bc:(ck + 1) * bc, i * d:(i + 1) * d] for i in range(6)]
            qa, ka, va, qb, kvb, qi, ki, wi = _inproj(xc, sh1, sc1, norm1_g[l], w_in[l], a_qk_gain[l],
                                                      b_q_gain[l], b_kv_gain[l], bc)
            oa = _diffattn(qa, ka, va, a_lambda[l], a_sub_gain[l], bc, lam_init)
            ob = _dsa(qi, wi, ki, qb, kvb, b_w_uv[l], bc)
            x1, h2, q = _outproj(xc, oa, ob, w_out[l], g1, sh2, sc2, norm2_g[l], peer_wq[l], bc)
            experts, gates = _peer_select(q, peer_subkeys[l])
            outs += _peer_apply(h2, x1, g2, experts, gates, u_packed, v_packed, bc, n_sc_u, n_sc_v)
        x2 = jnp.concatenate(outs, axis=0)
    return x2.reshape(b, s, d)
```

```python
import functools
import math

import jax
import jax.numpy as jnp
import numpy as np
from jax import lax
from jax.experimental import pallas as pl
from jax.experimental.pallas import tpu as pltpu
from jax.experimental.pallas import tpu_sc as plsc

CHUNK = 64
A_HEADS, A_QK_DIM, A_V_DIM = 4, 64, 128
B_HEADS, B_LAT, B_V_DIM = 8, 128, 64
IDX_HEADS, IDX_DIM = 4, 64
DSA_TOPK_MAX = 256
A_Q = A_HEADS * 2 * A_QK_DIM
A_K = A_Q
A_V = A_HEADS * A_V_DIM
B_Q = B_HEADS * B_LAT
B_KV = B_LAT
I_Q = IDX_HEADS * IDX_DIM
I_K = IDX_DIM
I_W = IDX_HEADS
OFF_AQ, OFF_AK, OFF_AV = 0, A_Q, A_Q + A_K
OFF_BQ = OFF_AV + A_V
OFF_KV = OFF_BQ + B_Q
OFF_IQ = OFF_KV + B_KV
OFF_IK = OFF_IQ + I_Q
OFF_IW = OFF_IK + I_K
IN_COLS = OFF_IW + I_W
PEER_HEADS, PEER_NKEYS, PEER_QDIM, PEER_TOPK = 8, 128, 128, 16
EPS = 1e-6
NEG = -1e30
INT_MIN = -(2 ** 31)

SUBLANES = 8
LANES = 128
VMEM_LIMIT = 56 * 1024 * 1024

PROJ_TM = 256
ATT_TQ = 256
ATT_GROUP_SPAN = 256
PEER_TM = 128
PEER_TB = 8
PEER_NSEL = PEER_HEADS * PEER_TOPK
PEER_PLAN = ((8192, 8192, (1, "outproj")), (8192, 8192, (3, "diffattn")), (8192, 4096, None), (0, 0, None))
HALF_ROWS = 4


def _split_bf16(x):
    hi = x.astype(jnp.bfloat16)
    lo = (x - hi.astype(jnp.float32)).astype(jnp.bfloat16)
    return hi, lo


def _dot(a, b):
    return jnp.dot(a, b, preferred_element_type=jnp.float32)


def _dot_nt(a, b):
    return lax.dot_general(a, b, (((1,), (1,)), ((), ())), preferred_element_type=jnp.float32)


def _cparams(sem):
    return pltpu.CompilerParams(dimension_semantics=sem, vmem_limit_bytes=VMEM_LIMIT)


def _adaln_kernel(c_ref, w_ref, b_ref, o_ref):
    cf = c_ref[...]
    a = cf * (1.0 / (1.0 + jnp.exp(-cf)))
    a_hi, a_lo = _split_bf16(a)
    w_hi, w_lo = _split_bf16(w_ref[...])
    o_ref[...] = _dot(a_hi, w_hi) + _dot(a_hi, w_lo) + _dot(a_lo, w_hi) + b_ref[...]


def _adaln(c, w, b):
    bsz, d = c.shape
    n = w.shape[1]
    tn = 1024
    return pl.pallas_call(
        _adaln_kernel,
        grid=(n // tn,),
        in_specs=[pl.BlockSpec((bsz, d), lambda j: (0, 0)),
                  pl.BlockSpec((d, tn), lambda j: (0, j)),
                  pl.BlockSpec((1, tn), lambda j: (0, j))],
        out_specs=pl.BlockSpec((bsz, tn), lambda j: (0, j)),
        out_shape=jax.ShapeDtypeStruct((bsz, n), jnp.float32),
        compiler_params=_cparams(("arbitrary",)),
        name="adaln",
    )(c, w, b.reshape(1, n))


def _rms_rows(x, g):
    return x * lax.rsqrt(jnp.mean(x * x, axis=-1, keepdims=True) + EPS) * g


def _group_norm_block(p, gmat, gain, n):
    hi, lo = _split_bf16(p * p)
    ss = _dot(hi, gmat) + _dot(lo, gmat)
    return p * lax.rsqrt(ss * (1.0 / n) + EPS) * gain


def _inproj_kernel(x_ref, sh_ref, sc_ref, g_ref, w_ref, gq_ref, gk_ref, gbq_ref, gkv_ref,
                   qa_ref, ka_ref, va_ref, qb_ref, kv_ref, qi_ref, ki_ref, wi_ref):
    h = _rms_rows(x_ref[...], g_ref[...]) * (1.0 + sc_ref[0]) + sh_ref[0]
    hb = h.astype(jnp.bfloat16)
    row = lax.broadcasted_iota(jnp.int32, (LANES, LANES), 0)
    col = lax.broadcasted_iota(jnp.int32, (LANES, LANES), 1)
    g64 = jnp.where((row // A_QK_DIM) == (col // A_QK_DIM), 1.0, 0.0).astype(jnp.bfloat16)
    g128 = jnp.ones((LANES, LANES), jnp.bfloat16)

    def proj(off, width):
        return _dot(hb, w_ref[:, off:off + width])

    for blk in range(A_Q // LANES):
        sl = slice(blk * LANES, (blk + 1) * LANES)
        p = proj(OFF_AQ + blk * LANES, LANES)
        qa_ref[:, sl] = (_group_norm_block(p, g64, gq_ref[...], A_QK_DIM) * (A_QK_DIM ** -0.5)).astype(qa_ref.dtype)
        p = proj(OFF_AK + blk * LANES, LANES)
        ka_ref[:, sl] = _group_norm_block(p, g64, gk_ref[...], A_QK_DIM).astype(ka_ref.dtype)
    va_ref[...] = proj(OFF_AV, A_V).astype(va_ref.dtype)
    for blk in range(B_HEADS):
        sl = slice(blk * LANES, (blk + 1) * LANES)
        p = proj(OFF_BQ + blk * LANES, LANES)
        qb_ref[:, sl] = _group_norm_block(p, g128, gbq_ref[...], B_LAT).astype(qb_ref.dtype)
    p = proj(OFF_KV, B_KV)
    kv_ref[...] = _group_norm_block(p, g128, gkv_ref[...], B_LAT).astype(kv_ref.dtype)
    qi_ref[...] = proj(OFF_IQ, I_Q).astype(qi_ref.dtype)
    tail = proj(OFF_IK, 2 * LANES)
    ki_ref[...] = tail[:, :LANES].astype(ki_ref.dtype)
    wi_ref[...] = tail[:, I_K:I_K + LANES]


def _inproj(x2, sh1, sc1, g1n, w_in, a_qk_gain, b_q_gain, b_kv_gain, batch):
    t, d = x2.shape
    per_b = (t // batch) // PROJ_TM
    wpad = OFF_IK + 2 * LANES
    w = jnp.zeros((d, wpad), jnp.bfloat16).at[:, :IN_COLS].set(w_in.astype(jnp.bfloat16))
    gq = jnp.tile(a_qk_gain[0], 2).reshape(1, LANES)
    gk = jnp.tile(a_qk_gain[1], 2).reshape(1, LANES)
    tok = lambda width: pl.BlockSpec((PROJ_TM, width), lambda i: (i, 0))
    vec = lambda width: pl.BlockSpec((1, width), lambda i: (0, 0))
    per_batch = pl.BlockSpec((1, 1, d), lambda i: (i // per_b, 0, 0))
    bf = jnp.bfloat16
    outs = [(A_Q, bf), (A_K, bf), (A_V, bf), (B_Q, bf), (B_KV, bf), (I_Q, bf), (LANES, bf), (LANES, jnp.float32)]
    return pl.pallas_call(
        _inproj_kernel,
        grid=(t // PROJ_TM,),
        in_specs=[tok(d), per_batch, per_batch, vec(d),
                  pl.BlockSpec((d, wpad), lambda i: (0, 0)),
                  vec(LANES), vec(LANES), vec(LANES), vec(LANES)],
        out_specs=[tok(wd) for wd, _ in outs],
        out_shape=[jax.ShapeDtypeStruct((t, wd), dt) for wd, dt in outs],
        compiler_params=_cparams(("arbitrary",)),
        name="inproj",
    )(x2, sh1.reshape(batch, 1, d), sc1.reshape(batch, 1, d), g1n.reshape(1, d), w,
      gq, gk, b_q_gain.reshape(1, LANES), b_kv_gain.reshape(1, LANES))


def _positions(q0, tq, s_len):
    qpos = q0 + lax.broadcasted_iota(jnp.int32, (tq, s_len), 0)
    kpos = lax.broadcasted_iota(jnp.int32, (tq, s_len), 1)
    cend = (qpos // CHUNK + 1) * CHUNK
    return qpos, kpos, cend


def _softmax_rows(s):
    m = jnp.max(s, axis=-1, keepdims=True)
    p = jnp.exp(s - m)
    return p, jnp.sum(p, axis=-1, keepdims=True)


def _diffattn_kernel(q_ref, k_ref, v_ref, lam_ref, gain_ref, o_ref, *, lam_init, q_base):
    tq, s_len = q_ref.shape[0], k_ref.shape[0]
    q0 = q_base + pl.program_id(1) * tq
    qpos, kpos, cend = _positions(q0, tq, s_len)
    dist = jnp.abs(qpos - kpos).astype(jnp.float32)
    negmask = jnp.where(kpos < cend, 0.0, NEG)
    lf = lam_ref[...]
    lam = (jnp.exp(jnp.sum(lf[0:1] * lf[1:2], axis=-1, keepdims=True))
           - jnp.exp(jnp.sum(lf[2:3] * lf[3:4], axis=-1, keepdims=True)) + lam_init)
    for h in range(A_HEADS):
        slope = 2.0 ** (-8.0 * (h + 1) / A_HEADS)
        bias = negmask - slope * dist
        v = v_ref[:, h * A_V_DIM:(h + 1) * A_V_DIM]
        outs = []
        for m in range(2):
            c0 = (h * 2 + m) * A_QK_DIM
            s = _dot_nt(q_ref[:, c0:c0 + A_QK_DIM], k_ref[:, c0:c0 + A_QK_DIM]) + bias
            p, l = _softmax_rows(s)
            outs.append(_dot(p.astype(jnp.bfloat16), v) / l)
        o = outs[0] - lam * outs[1]
        o = _rms_rows(o, gain_ref[...]) * (1.0 - lam_init)
        o_ref[:, h * A_V_DIM:(h + 1) * A_V_DIM] = o.astype(o_ref.dtype)


def _causal_groups(s_len):
    span = min(ATT_GROUP_SPAN, s_len)
    tiles = span // ATT_TQ
    return [(g * tiles, tiles, (g + 1) * span) for g in range(s_len // span)]


def _qblk(first_tile, width):
    return pl.BlockSpec((None, ATT_TQ, width), lambda b, i: (b, first_tile + i, 0))


def _kblk(klen, width):
    return pl.BlockSpec((None, klen, width), lambda b, i: (b, 0, 0))


def _oblk(width):
    return pl.BlockSpec((None, ATT_TQ, width), lambda b, i: (b, i, 0))


def _diffattn(qa, ka, va, a_lambda, a_sub_gain, batch, lam_init):
    t = qa.shape[0]
    s_len = t // batch
    qa3, ka3, va3 = (a.reshape(batch, s_len, a.shape[1]) for a in (qa, ka, va))
    outs = []
    for first, tiles, klen in _causal_groups(s_len):
        outs.append(pl.pallas_call(
            functools.partial(_diffattn_kernel, lam_init=lam_init, q_base=first * ATT_TQ),
            grid=(batch, tiles),
            in_specs=[_qblk(first, A_Q), _kblk(klen, A_K), _kblk(klen, A_V),
                      pl.BlockSpec((4, A_QK_DIM), lambda b, i: (0, 0)),
                      pl.BlockSpec((1, A_V_DIM), lambda b, i: (0, 0))],
            out_specs=_oblk(A_V),
            out_shape=jax.ShapeDtypeStruct((batch, tiles * ATT_TQ, A_V), jnp.bfloat16),
            compiler_params=_cparams(("arbitrary", "arbitrary")),
            name="diffattn",
        )(qa3, ka3, va3, a_lambda, a_sub_gain.reshape(1, A_V_DIM)))
    return jnp.concatenate(outs, axis=1).reshape(t, A_V)


def _count(mask):
    return jnp.sum(jnp.where(mask, 1.0, 0.0), axis=-1, keepdims=True)


def _topk_mask(score, allowed, kpos, topk, s_len):
    bits = pltpu.bitcast(score + 0.0, jnp.int32)
    key = jnp.where(bits < 0, bits ^ jnp.int32(0x7FFFFFFF), bits)
    key = jnp.where(allowed, key, jnp.int32(INT_MIN))
    kf = float(topk)
    thr = jnp.where(_count(key >= 0) >= kf, jnp.int32(0), jnp.int32(INT_MIN))

    def value_bit(i, thr):
        cand = thr | (jnp.int32(1) << (30 - i))
        return jnp.where(_count(key >= cand) >= kf, cand, thr)

    thr = lax.fori_loop(0, 31, value_bit, thr)
    above = key > thr
    tie = key == thr
    need = kf - _count(above)
    nbits = (s_len - 1).bit_length()

    def pos_bit(i, pos):
        cand = pos | (jnp.int32(1) << (nbits - 1 - i))
        return jnp.where(_count(tie & (kpos < cand)) < need, cand, pos)

    pos = lax.fori_loop(0, nbits, pos_bit, jnp.zeros_like(thr))
    return allowed & (above | (tie & (kpos <= pos)))


def _dsa_kernel(qi_ref, wi_ref, ki_ref, qb_ref, kv_ref, wuv_ref, o_ref, *, topk, q_base):
    tq, s_len = qb_ref.shape[0], kv_ref.shape[0]
    q0 = q_base + pl.program_id(1) * tq
    qpos, kpos, cend = _positions(q0, tq, s_len)
    allowed = kpos < cend
    ki = ki_ref[...]
    wi = wi_ref[...]
    zpad = jnp.zeros((tq, LANES - IDX_DIM), jnp.bfloat16)
    score = jnp.zeros((tq, s_len), jnp.float32)
    for h in range(IDX_HEADS):
        qh = jnp.concatenate([qi_ref[:, h * IDX_DIM:(h + 1) * IDX_DIM], zpad], axis=-1)
        score = score + jnp.maximum(_dot_nt(qh, ki), 0.0) * wi[:, h:h + 1]
    score = score * ((IDX_DIM ** -0.5) * (IDX_HEADS ** -0.5))
    sel = _topk_mask(score, allowed, kpos, topk, s_len)

    dist = jnp.abs(qpos - kpos).astype(jnp.float32)
    negmask = jnp.where(sel, 0.0, NEG)
    kv = kv_ref[...]
    for h in range(B_HEADS):
        slope = 2.0 ** (-8.0 * (h + 1) / B_HEADS)
        s = _dot_nt(qb_ref[:, h * B_LAT:(h + 1) * B_LAT], kv) * (B_LAT ** -0.5) + (negmask - slope * dist)
        p, l = _softmax_rows(s)
        o_lat = _dot(p.astype(jnp.bfloat16), kv) / l
        o = _dot(o_lat.astype(jnp.bfloat16), wuv_ref[h])
        o_ref[:, h * B_V_DIM:(h + 1) * B_V_DIM] = o.astype(o_ref.dtype)


def _dsa(qi, wi, ki, qb, kvb, w_uv, batch):
    t = qb.shape[0]
    s_len = t // batch
    topk = min(DSA_TOPK_MAX, s_len // 4)
    width = B_HEADS * B_V_DIM
    qi3, wi3, ki3, qb3, kv3 = (a.reshape(batch, s_len, a.shape[1]) for a in (qi, wi, ki, qb, kvb))
    wuv = w_uv.astype(jnp.bfloat16)
    outs = []
    for first, tiles, klen in _causal_groups(s_len):
        outs.append(pl.pallas_call(
            functools.partial(_dsa_kernel, topk=topk, q_base=first * ATT_TQ),
            grid=(batch, tiles),
            in_specs=[_qblk(first, I_Q), _qblk(first, LANES), _kblk(klen, LANES),
                      _qblk(first, B_Q), _kblk(klen, B_KV),
                      pl.BlockSpec((B_HEADS, B_LAT, B_V_DIM), lambda b, i: (0, 0, 0))],
            out_specs=_oblk(width),
            out_shape=jax.ShapeDtypeStruct((batch, tiles * ATT_TQ, width), jnp.bfloat16),
            compiler_params=_cparams(("arbitrary", "arbitrary")),
            name="dsa",
        )(qi3, wi3, ki3, qb3, kv3, wuv))
    return jnp.concatenate(outs, axis=1).reshape(t, width)


def _outproj_kernel(x_ref, oa_ref, ob_ref, wo_ref, g1_ref, sh_ref, sc_ref, gn_ref, wq_ref,
                    x1_ref, h2_ref, q_ref):
    na = oa_ref.shape[1]
    y = _dot(oa_ref[...], wo_ref[:na, :]) + _dot(ob_ref[...], wo_ref[na:, :])
    x1 = x_ref[...] + g1_ref[0] * y
    x1_ref[...] = x1
    h2 = _rms_rows(x1, gn_ref[...]) * (1.0 + sc_ref[0]) + sh_ref[0]
    h2_ref[...] = h2
    q_ref[...] = _dot(h2.astype(jnp.bfloat16), wq_ref[...]).astype(q_ref.dtype)


def _outproj(x2, oa, ob, w_out, g1, sh2, sc2, g2n, peer_wq, batch):
    t, d = x2.shape
    per_b = (t // batch) // PROJ_TM
    nq = peer_wq.shape[1]
    tok = lambda width: pl.BlockSpec((PROJ_TM, width), lambda i: (i, 0))
    per_batch = pl.BlockSpec((1, 1, d), lambda i: (i // per_b, 0, 0))
    full = lambda a, b: pl.BlockSpec((a, b), lambda i: (0, 0))
    return pl.pallas_call(
        _outproj_kernel,
        grid=(t // PROJ_TM,),
        in_specs=[tok(d), tok(oa.shape[1]), tok(ob.shape[1]), full(w_out.shape[0], d),
                  per_batch, per_batch, per_batch, full(1, d), full(d, nq)],
        out_specs=[tok(d), tok(d), tok(nq)],
        out_shape=[jax.ShapeDtypeStruct((t, d), jnp.float32),
                   jax.ShapeDtypeStruct((t, d), jnp.float32),
                   jax.ShapeDtypeStruct((t, nq), jnp.bfloat16)],
        compiler_params=_cparams(("arbitrary",)),
        name="outproj",
    )(x2, oa, ob, w_out.astype(jnp.bfloat16), g1.reshape(batch, 1, d), sh2.reshape(batch, 1, d),
      sc2.reshape(batch, 1, d), g2n.reshape(1, d), peer_wq.astype(jnp.bfloat16))


def _extract_topk(vals, payload, k):
    n, tm = vals.shape
    rows = lax.broadcasted_iota(jnp.int32, (n, tm), 0).astype(jnp.float32)
    top_v, top_p = [], []
    for _ in range(k):
        m = jnp.max(vals, axis=0, keepdims=True)
        pos = jnp.min(jnp.where(vals == m, rows, float(n)), axis=0, keepdims=True)
        hit = rows == pos
        top_v.append(m)
        top_p.append(pos if payload is None else jnp.max(jnp.where(hit, payload, -1.0), axis=0, keepdims=True))
        vals = jnp.where(hit, -jnp.inf, vals)
    return jnp.concatenate(top_v, axis=0), jnp.concatenate(top_p, axis=0)


def _stair_pairs():
    return [(a, b) for a in range(PEER_TOPK) for b in range(PEER_TOPK) if (a + 1) * (b + 1) <= PEER_TOPK]


N_STAIR = len(_stair_pairs())
N_STAIR_PAD = -(-N_STAIR // SUBLANES) * SUBLANES


def _copy_rows(sel, x):
    x0 = x.astype(jnp.bfloat16)
    r1 = x - x0.astype(jnp.float32)
    x1 = r1.astype(jnp.bfloat16)
    x2 = (r1 - x1.astype(jnp.float32)).astype(jnp.bfloat16)
    return _dot(sel, x0) + _dot(sel, x1) + _dot(sel, x2)


def _peer_select_kernel(q_ref, sk_ref, sela_ref, selb_ref, e_ref, g_ref):
    tm = q_ref.shape[0]
    sela, selb = sela_ref[...], selb_ref[...]
    pad_row = lax.broadcasted_iota(jnp.int32, (N_STAIR_PAD, tm), 0) >= N_STAIR
    e_rows, g_rows = [], []
    for h in range(PEER_HEADS):
        sub = _dot_nt(sk_ref[...], q_ref[:, h * PEER_QDIM:(h + 1) * PEER_QDIM])
        sv0, si0 = _extract_topk(sub[:PEER_NKEYS], None, PEER_TOPK)
        sv1, si1 = _extract_topk(sub[PEER_NKEYS:], None, PEER_TOPK)
        cand = jnp.where(pad_row, -jnp.inf, _copy_rows(sela, sv0) + _copy_rows(selb, sv1))
        ids = (_dot(sela, si0.astype(jnp.bfloat16)) * float(PEER_NKEYS)
               + _dot(selb, si1.astype(jnp.bfloat16))) * float(HALF_ROWS)
        top_s, top_e = _extract_topk(cand, ids, PEER_TOPK)
        p = jnp.exp(top_s - top_s[0:1])
        e_rows.append(top_e)
        g_rows.append(p / jnp.sum(p, axis=0, keepdims=True))
    e_t = jnp.concatenate(e_rows, axis=0)
    g_t = jnp.concatenate(g_rows, axis=0)
    e_ref[...] = e_t.T.astype(jnp.int32)
    g_ref[...] = g_t.T


def _peer_select(q, sub_keys):
    t, nq = q.shape
    half = PEER_QDIM // 2
    sk = jnp.zeros((2 * PEER_NKEYS, PEER_QDIM), jnp.bfloat16)
    sk = sk.at[:PEER_NKEYS, :half].set(sub_keys[0].astype(jnp.bfloat16))
    sk = sk.at[PEER_NKEYS:, half:].set(sub_keys[1].astype(jnp.bfloat16))
    sel = np.zeros((2, N_STAIR_PAD, PEER_TOPK), np.float32)
    for r, (a, b) in enumerate(_stair_pairs()):
        sel[0, r, a] = 1.0
        sel[1, r, b] = 1.0
    sel = jnp.asarray(sel, jnp.bfloat16)
    tok = lambda width: pl.BlockSpec((PEER_TM, width), lambda i: (i, 0))
    full = lambda a, b: pl.BlockSpec((a, b), lambda i: (0, 0))
    return pl.pallas_call(
        _peer_select_kernel,
        grid=(t // PEER_TM,),
        in_specs=[tok(nq), full(2 * PEER_NKEYS, PEER_QDIM),
                  full(N_STAIR_PAD, PEER_TOPK), full(N_STAIR_PAD, PEER_TOPK)],
        out_specs=[tok(PEER_NSEL), tok(PEER_NSEL)],
        out_shape=[jax.ShapeDtypeStruct((t, PEER_NSEL), jnp.int32),
                   jax.ShapeDtypeStruct((t, PEER_NSEL), jnp.float32)],
        compiler_params=_cparams(("arbitrary",)),
        name="peer_select",
    )(q, sk, sel[0], sel[1])


def _gather_row(tab_ref, row):
    w = tab_ref[pl.ds(pl.multiple_of(row, HALF_ROWS), HALF_ROWS), :]
    hi = pltpu.bitcast(w & jnp.uint32(0xFFFF0000), jnp.float32)
    lo = pltpu.bitcast(w << 16, jnp.float32)
    return hi, lo


def _peer_u_kernel(idx_ref, h_ref, gate_ref, tab_ref, coef_ref, abuf_ref):
    ones = jnp.ones((2 * LANES, LANES), jnp.bfloat16)
    lane = lax.broadcasted_iota(jnp.int32, (PEER_NSEL, PEER_TM), 1)
    rows_per_tok = PEER_NSEL * HALF_ROWS

    def batch(b, a_t):
        def token(tb, c2):
            t = b * PEER_TB + tb
            x8 = h_ref[pl.ds(t, 1), :].reshape(SUBLANES, LANES)
            xh, xl = x8[:HALF_ROWS], x8[HALF_ROWS:]
            off = pl.multiple_of(tb * rows_per_tok, rows_per_tok)
            for j in range(PEER_NSEL):
                hi, lo = _gather_row(tab_ref, idx_ref[0, t, j])
                abuf_ref[pl.ds(off + j * HALF_ROWS, HALF_ROWS), :] = hi * xh + lo * xl
            return c2

        lax.fori_loop(0, PEER_TB, token, 0)
        parts = []
        for tb in range(PEER_TB):
            ab = abuf_ref[pl.ds(tb * rows_per_tok, PEER_NSEL, stride=HALF_ROWS), :]
            for s in range(1, HALF_ROWS):
                ab = ab + abuf_ref[pl.ds(tb * rows_per_tok + s, PEER_NSEL, stride=HALF_ROWS), :]
            parts.append(ab)
        ab_hi, ab_lo = _split_bf16(jnp.concatenate(parts, axis=0))
        r = _dot(jnp.concatenate([ab_hi, ab_lo], axis=1), ones)
        for tb in range(PEER_TB):
            a_t = jnp.where(lane == b * PEER_TB + tb, r[tb * PEER_NSEL:(tb + 1) * PEER_NSEL], a_t)
        return a_t

    a_t = lax.fori_loop(0, PEER_TM // PEER_TB, batch, jnp.zeros((PEER_NSEL, PEER_TM), jnp.float32))
    a = a_t.T
    coef_ref[...] = 0.5 * a * (1.0 + lax.erf(a * (2.0 ** -0.5))) * gate_ref[...]


def _peer_v_kernel(idx_ref, coef_ref, x_ref, g2_ref, tab_ref, out_ref, cb_ref):
    g2 = g2_ref[...]
    ones = jnp.ones((2 * LANES, LANES), jnp.bfloat16)
    eye = (lax.broadcasted_iota(jnp.int32, (PEER_NSEL, LANES), 0)
           == lax.broadcasted_iota(jnp.int32, (PEER_NSEL, LANES), 1))
    nacc = 2

    def batch(b, carry):
        for tb in range(PEER_TB):
            row = coef_ref[pl.ds(b * PEER_TB + tb, 1), :]
            d_hi, d_lo = _split_bf16(jnp.where(eye, jnp.broadcast_to(row, (PEER_NSEL, LANES)), 0.0))
            cb_ref[pl.ds(tb * PEER_NSEL, PEER_NSEL), :] = _dot(jnp.concatenate([d_hi, d_lo], axis=1), ones)

        def token(tb, c2):
            t = b * PEER_TB + tb
            zero = jnp.zeros((HALF_ROWS, LANES), jnp.float32)
            acc_h = [zero] * nacc
            acc_l = [zero] * nacc
            cbase = pl.multiple_of(tb * PEER_NSEL, PEER_NSEL)
            for j in range(PEER_NSEL):
                hi, lo = _gather_row(tab_ref, idx_ref[0, t, j])
                c = jnp.broadcast_to(cb_ref[pl.ds(cbase + j, 1), :], (HALF_ROWS, LANES))
                acc_h[j % nacc] = acc_h[j % nacc] + c * hi
                acc_l[j % nacc] = acc_l[j % nacc] + c * lo
            y8 = jnp.concatenate([acc_h[0] + acc_h[1], acc_l[0] + acc_l[1]], axis=0)
            out_ref[pl.ds(t, 1), :] = x_ref[pl.ds(t, 1), :] + g2 * y8.reshape(1, SUBLANES * LANES)
            return c2

        lax.fori_loop(0, PEER_TB, token, 0)
        return carry

    lax.fori_loop(0, PEER_TM // PEER_TB, batch, 0)


def _sc_peer_u(expert_ids, h2, tab, first, count):
    info = plsc.get_sparse_core_info()
    lanes, workers = info.num_lanes, info.num_cores * info.num_subcores
    half = tab.shape[1]
    per_w = count // workers
    nhalf = PEER_NSEL // 2
    assert per_w * workers == count and per_w >= 2 and nhalf % lanes == 0 and half % lanes == 0
    rows_blk = 4
    kunroll = 2
    mesh = plsc.VectorSubcoreMesh(core_axis_name="c", subcore_axis_name="s")

    @functools.partial(
        pl.kernel, mesh=mesh, name="peer_u_sc",
        compiler_params=pltpu.CompilerParams(needs_layout_passes=False),
        out_type=jax.ShapeDtypeStruct((count, PEER_NSEL), jnp.float32),
        scratch_types=[pltpu.VMEM((2, PEER_NSEL), jnp.int32),
                       pltpu.VMEM((nhalf, half), jnp.uint32), pltpu.VMEM((nhalf, half), jnp.uint32),
                       pltpu.VMEM((2, 2 * half), jnp.float32), pltpu.VMEM((2, PEER_NSEL), jnp.float32),
                       pltpu.SemaphoreType.DMA, pltpu.SemaphoreType.DMA, pltpu.SemaphoreType.DMA,
                       pltpu.SemaphoreType.DMA, pltpu.SemaphoreType.DMA])
    def run(idx_hbm, h_hbm, tab_hbm, out_hbm, idx_v, rows_a, rows_b, x_v, a_v, sem_a, sem_b, sem_i, sem_x, sem_o):
        wid = lax.axis_index("s") * info.num_cores + lax.axis_index("c")
        base = first + wid * per_w
        lane = lax.iota(jnp.int32, lanes)
        mask_hi = jnp.full((lanes,), 0xFFFF0000, jnp.uint32)

        def gather(p, part, rows, sem):
            return pltpu.make_async_copy(tab_hbm.at[idx_v.at[p, pl.ds(part * nhalf, nhalf)]], rows, sem)

        def load_idx(t, p):
            return pltpu.make_async_copy(idx_hbm.at[t], idx_v.at[p], sem_i)

        def load_x(t, p):
            return pltpu.make_async_copy(h_hbm.at[t], x_v.at[p], sem_x)

        def store_a(t, p):
            return pltpu.make_async_copy(a_v.at[p], out_hbm.at[t - first], sem_o)

        def compute(rows, p, part):
            def group(g, c2):
                vec = jnp.zeros((lanes,), jnp.float32)
                for rb in range(lanes // rows_blk):
                    def kslice(kk, accs):
                        accs = list(accs)
                        for ku in range(kunroll):
                            off = pl.multiple_of((kk * kunroll + ku) * lanes, lanes)
                            xh = x_v[p, pl.ds(off, lanes)]
                            xl = x_v[p, pl.ds(half + off, lanes)]
                            for r in range(rows_blk):
                                w = rows[g * lanes + rb * rows_blk + r, pl.ds(off, lanes)]
                                hi = plsc.bitcast(w & mask_hi, jnp.float32)
                                lo = plsc.bitcast(w << 16, jnp.float32)
                                accs[r] = accs[r] + hi * xh + lo * xl
                        return tuple(accs)

                    zero = jnp.zeros((lanes,), jnp.float32)
                    accs = lax.fori_loop(0, half // lanes // kunroll, kslice, (zero,) * rows_blk)
                    for r in range(rows_blk):
                        vec = jnp.where(lane == rb * rows_blk + r, jnp.sum(accs[r]), vec)
                a_v[p, pl.ds(pl.multiple_of(part * nhalf + g * lanes, lanes), lanes)] = vec
                return c2

            lax.fori_loop(0, nhalf // lanes, group, 0)

        load_idx(base, 0).start()
        load_x(base, 0).start()
        load_idx(base, 0).wait()
        gather(0, 0, rows_a, sem_a).start()
        gather(0, 1, rows_b, sem_b).start()
        load_x(base, 0).wait()

        def token(i, carry):
            t = base + i
            p = lax.rem(i, 2)
            q = 1 - p
            tn = jnp.minimum(t + 1, base + per_w - 1)
            load_idx(tn, q).start()
            load_x(tn, q).start()

            @pl.when(i >= 2)
            def _():
                store_a(t, p).wait()

            gather(p, 0, rows_a, sem_a).wait()
            compute(rows_a, p, 0)
            load_idx(tn, q).wait()
            gather(q, 0, rows_a, sem_a).start()
            gather(p, 1, rows_b, sem_b).wait()
            compute(rows_b, p, 1)
            gather(q, 1, rows_b, sem_b).start()
            load_x(tn, q).wait()
            store_a(t, p).start()
            return carry

        lax.fori_loop(0, per_w, token, 0)
        gather(0, 0, rows_a, sem_a).wait()
        gather(0, 1, rows_b, sem_b).wait()
        store_a(base, 0).wait()
        store_a(base, 1).wait()

    return run(expert_ids, h2, tab)


def _sc_peer_v(expert_ids, coef, x1, g2, tab, first, count, per_batch):
    info = plsc.get_sparse_core_info()
    lanes, workers = info.num_lanes, info.num_cores * info.num_subcores
    half = tab.shape[1]
    d = 2 * half
    per_w = count // workers
    nhalf = PEER_NSEL // 2
    sblk = 4
    runroll = 4
    assert per_w * workers == count and per_w >= 2 and nhalf % runroll == 0 and half % (lanes * sblk) == 0
    mesh = plsc.VectorSubcoreMesh(core_axis_name="c", subcore_axis_name="s")
    dma = pltpu.SemaphoreType.DMA

    @functools.partial(
        pl.kernel, mesh=mesh, name="peer_v_sc",
        compiler_params=pltpu.CompilerParams(needs_layout_passes=False),
        out_type=jax.ShapeDtypeStruct((count, d), jnp.float32),
        scratch_types=[pltpu.VMEM((2, PEER_NSEL), jnp.int32), pltpu.VMEM((2 * PEER_NSEL,), jnp.float32),
                       pltpu.VMEM((nhalf, half), jnp.uint32), pltpu.VMEM((nhalf, half), jnp.uint32),
                       pltpu.VMEM((2, d), jnp.float32), pltpu.VMEM((2, d), jnp.float32),
                       pltpu.VMEM((d,), jnp.float32), pltpu.VMEM((2, d), jnp.float32),
                       dma, dma, dma, dma, dma, dma, dma])
    def run(idx_hbm, coef_hbm, x_hbm, g_hbm, tab_hbm, out_hbm,
            idx_v, coef_v, rows_a, rows_b, x_v, g_v, y_v, o_v, sem_a, sem_b, sem_i, sem_c, sem_x, sem_g, sem_o):
        wid = lax.axis_index("s") * info.num_cores + lax.axis_index("c")
        base = first + wid * per_w
        mask_hi = jnp.full((lanes,), 0xFFFF0000, jnp.uint32)

        def gather(p, part, rows, sem):
            return pltpu.make_async_copy(tab_hbm.at[idx_v.at[p, pl.ds(part * nhalf, nhalf)]], rows, sem)

        def loads(t, p):
            cslot = coef_v.at[pl.ds(pl.multiple_of(p * PEER_NSEL, PEER_NSEL), PEER_NSEL)]
            return (pltpu.make_async_copy(idx_hbm.at[t], idx_v.at[p], sem_i),
                    pltpu.make_async_copy(coef_hbm.at[t], cslot, sem_c),
                    pltpu.make_async_copy(x_hbm.at[t], x_v.at[p], sem_x),
                    pltpu.make_async_copy(g_hbm.at[t // per_batch], g_v.at[p], sem_g))

        def store_o(t, p):
            return pltpu.make_async_copy(o_v.at[p], out_hbm.at[t - first], sem_o)

        def compute(rows, p, part):
            cbase = p * PEER_NSEL + part * nhalf
            for sb in range(half // lanes // sblk):
                offs = [(sb * sblk + s) * lanes for s in range(sblk)]
                if part == 0:
                    init = tuple(jnp.zeros((lanes,), jnp.float32) for _ in range(2 * sblk))
                else:
                    init = tuple([y_v[pl.ds(o, lanes)] for o in offs] + [y_v[pl.ds(half + o, lanes)] for o in offs])

                def rowloop(rr, accs):
                    accs = list(accs)
                    for ru in range(runroll):
                        r = rr * runroll + ru
                        c = plsc.load_gather(coef_v, [jnp.full((lanes,), cbase + r, jnp.int32)])
                        for s in range(sblk):
                            w = rows[r, pl.ds(offs[s], lanes)]
                            hi = plsc.bitcast(w & mask_hi, jnp.float32)
                            lo = plsc.bitcast(w << 16, jnp.float32)
                            accs[s] = accs[s] + c * hi
                            accs[sblk + s] = accs[sblk + s] + c * lo
                    return tuple(accs)

                accs = lax.fori_loop(0, nhalf // runroll, rowloop, init)
                for s in range(sblk):
                    for hl, o in ((0, offs[s]), (1, half + offs[s])):
                        if part == 0:
                            y_v[pl.ds(o, lanes)] = accs[hl * sblk + s]
                        else:
                            o_v[p, pl.ds(o, lanes)] = (x_v[p, pl.ds(o, lanes)]
                                                       + g_v[p, pl.ds(o, lanes)] * accs[hl * sblk + s])

        head = loads(base, 0)
        for c in head:
            c.start()
        head[0].wait()
        gather(0, 0, rows_a, sem_a).start()
        gather(0, 1, rows_b, sem_b).start()
        for c in head[1:]:
            c.wait()

        def token(i, carry):
            t = base + i
            p = lax.rem(i, 2)
            q = 1 - p
            tn = jnp.minimum(t + 1, base + per_w - 1)
            nxt = loads(tn, q)
            for c in nxt:
                c.start()

            @pl.when(i >= 2)
            def _():
                store_o(t, p).wait()

            gather(p, 0, rows_a, sem_a).wait()
            compute(rows_a, p, 0)
            nxt[0].wait()
            gather(q, 0, rows_a, sem_a).start()
            gather(p, 1, rows_b, sem_b).wait()
            compute(rows_b, p, 1)
            gather(q, 1, rows_b, sem_b).start()
            for c in nxt[1:]:
                c.wait()
            store_o(t, p).start()
            return carry

        lax.fori_loop(0, per_w, token, 0)
        gather(0, 0, rows_a, sem_a).wait()
        gather(0, 1, rows_b, sem_b).wait()
        store_o(base, 0).wait()
        store_o(base, 1).wait()

    return run(expert_ids, coef, x1, g2, tab)


def _gelu_gate_kernel(a_ref, g_ref, o_ref):
    a = a_ref[...]
    o_ref[...] = 0.5 * a * (1.0 + lax.erf(a * (2.0 ** -0.5))) * g_ref[...]


def _gelu_gate(a, gates, first):
    count = a.shape[0]
    return pl.pallas_call(
        _gelu_gate_kernel,
        grid=(count // PEER_TM,),
        in_specs=[pl.BlockSpec((PEER_TM, PEER_NSEL), lambda i: (i, 0)),
                  pl.BlockSpec((PEER_TM, PEER_NSEL), lambda i: (first // PEER_TM + i, 0))],
        out_specs=pl.BlockSpec((PEER_TM, PEER_NSEL), lambda i: (i, 0)),
        out_shape=jax.ShapeDtypeStruct((count, PEER_NSEL), jnp.float32),
        compiler_params=_cparams(("arbitrary",)),
        name="gelu_gate",
    )(a, gates)


def _pack_table(tab):
    n, d = tab.shape
    bits = lax.bitcast_convert_type(tab.astype(jnp.bfloat16), jnp.uint16).astype(jnp.uint32)
    packed = (bits[:, : d // 2] << 16) | bits[:, d // 2:]
    return packed.reshape(n * HALF_ROWS, LANES)


def _peer_specs(t, d):
    smem_blk = pl.BlockSpec((1, PEER_TM, PEER_NSEL), lambda i: (i, 0, 0), memory_space=pltpu.SMEM)
    tok_blk = pl.BlockSpec((PEER_TM, d), lambda i: (i, 0))
    sel_blk = pl.BlockSpec((PEER_TM, PEER_NSEL), lambda i: (i, 0))
    return smem_blk, tok_blk, sel_blk, pl.BlockSpec(memory_space=pltpu.VMEM)


def _after(x, prev):
    return x if prev is None else lax.optimization_barrier((x, prev))[0]


def _peer_u_phase(h2, experts, gates, u_packed, n_sc, sc_prev):
    t, d = h2.shape
    assert d == SUBLANES * LANES and d // 2 == HALF_ROWS * LANES and PEER_TM % PEER_TB == 0
    smem_blk, tok_blk, sel_blk, tab_spec = _peer_specs(t, d)
    parts = []
    n_tc = t - n_sc
    if n_tc:
        parts.append(pl.pallas_call(
            _peer_u_kernel,
            grid=(n_tc // PEER_TM,),
            in_specs=[smem_blk, tok_blk, sel_blk, tab_spec],
            out_specs=sel_blk,
            out_shape=jax.ShapeDtypeStruct((n_tc, PEER_NSEL), jnp.float32),
            scratch_shapes=[pltpu.VMEM((PEER_TB * PEER_NSEL * HALF_ROWS, LANES), jnp.float32)],
            compiler_params=_cparams(("arbitrary",)),
            name="peer_u",
        )(experts.reshape(t // PEER_TM, PEER_TM, PEER_NSEL), h2, gates, u_packed))
    a_sc = None
    if n_sc:
        a_sc = _sc_peer_u(experts // HALF_ROWS, h2, u_packed.reshape(-1, d // 2), n_tc, n_sc)
    return (parts[0] if parts else None), a_sc


def _finish_coef(coef_tc, a_sc, gates):
    parts = [] if coef_tc is None else [coef_tc]
    if a_sc is not None:
        parts.append(_gelu_gate(a_sc, gates, gates.shape[0] - a_sc.shape[0]))
    return parts[0] if len(parts) == 1 else jnp.concatenate(parts, axis=0)


def _peer_v_phase(coef, x1, g2, experts, v_packed, batch, n_sc, sc_prev):
    t, d = x1.shape
    per_b = t // PEER_TM // batch
    smem_blk, tok_blk, sel_blk, tab_spec = _peer_specs(t, d)
    parts = []
    n_tc = t - n_sc
    if n_tc:
        parts.append(pl.pallas_call(
            _peer_v_kernel,
            grid=(n_tc // PEER_TM,),
            in_specs=[smem_blk, sel_blk, tok_blk,
                      pl.BlockSpec((None, 1, d), lambda i: (i // per_b, 0, 0)),
                      tab_spec],
            out_specs=tok_blk,
            out_shape=jax.ShapeDtypeStruct((n_tc, d), jnp.float32),
            scratch_shapes=[pltpu.VMEM((PEER_TB * PEER_NSEL, LANES), jnp.float32)],
            compiler_params=_cparams(("arbitrary",)),
            name="peer_v",
        )(experts.reshape(t // PEER_TM, PEER_TM, PEER_NSEL), coef, x1, g2.reshape(batch, 1, d), v_packed))
    if n_sc:
        sc_prev = _sc_peer_v(experts // HALF_ROWS, coef, x1, g2, v_packed.reshape(-1, d // 2),
                             n_tc, n_sc, t // batch)
        parts.append(sc_prev)
    return parts, sc_prev


def kernel(x, c, ada_w, ada_b, norm1_g, norm2_g, w_in, a_qk_gain, a_lambda, a_sub_gain, b_q_gain, b_kv_gain, b_w_uv, w_out, peer_wq, peer_subkeys, peer_u, peer_v):
    b, s, d = x.shape
    t = b * s
    x2 = x.reshape(t, d)
    nchunk = len(PEER_PLAN)
    bc = b // nchunk
    tc = bc * s
    for l in range(ada_w.shape[0]):
        mod = _adaln(c, ada_w[l], ada_b[l])
        u_packed, v_packed = _pack_table(peer_u[l]), _pack_table(peer_v[l])
        lam_init = 0.8 - 0.6 * math.exp(-0.3 * l)
        outs = [None] * nchunk
        pending = {}
        sc_prev = None
        prev_experts = None

        def issue_v(k, anchor):
            nonlocal sc_prev
            coef_tc, a_sc, gates, args = pending.pop(k)
            if a_sc is not None:
                anchor, a_sc = lax.optimization_barrier((anchor, a_sc))
                sc_prev = a_sc
            outs[k], sc_prev = _peer_v_phase(_finish_coef(coef_tc, a_sc, gates), *args, sc_prev)
            return anchor

        def issue_planned(at, anchor):
            for k in [k for k in pending if PEER_PLAN[k][2] == at]:
                anchor = issue_v(k, anchor)
            return anchor

        for ck, (n_sc_u, n_sc_v, _) in enumerate(PEER_PLAN):
            xc = _after(x2[ck * tc:(ck + 1) * tc], prev_experts)
            sh1, sc1, g1, sh2, sc2, g2 = [mod[ck * bc:(ck + 1) * bc, i * d:(i + 1) * d] for i in range(6)]
            qa, ka, va, qb, kvb, qi, ki, wi = _inproj(xc, sh1, sc1, norm1_g[l], w_in[l], a_qk_gain[l],
                                                      b_q_gain[l], b_kv_gain[l], bc)
            oa = _diffattn(qa, ka, va, a_lambda[l], a_sub_gain[l], bc, lam_init)
            oa = issue_planned((ck, "diffattn"), oa)
            ob = _dsa(qi, wi, ki, qb, kvb, b_w_uv[l], bc)
            x1, h2, q = _outproj(xc, oa, ob, w_out[l], g1, sh2, sc2, norm2_g[l], peer_wq[l], bc)
            q = issue_planned((ck, "outproj"), q)
            experts, gates = _peer_select(q, peer_subkeys[l])
            prev_experts = experts
            coef_tc, a_sc = _peer_u_phase(h2, experts, gates, u_packed, n_sc_u, sc_prev)
            sc_prev = sc_prev if a_sc is None else a_sc
            pending[ck] = (coef_tc, a_sc, gates, (x1, g2, experts, v_packed, bc, n_sc_v))
        anchor = pending[nchunk - 1][0]
        for k in sorted(pending):
            anchor = issue_v(k, anchor)
        x2 = jnp.concatenate([p for o in outs for p in o], axis=0)
    return x2.reshape(b, s, d)
```

```python
import functools
import math

import jax
import jax.numpy as jnp
import numpy as np
from jax import lax
from jax.experimental import pallas as pl
from jax.experimental.pallas import tpu as pltpu
from jax.experimental.pallas import tpu_sc as plsc

CHUNK = 64
A_HEADS, A_QK_DIM, A_V_DIM = 4, 64, 128
B_HEADS, B_LAT, B_V_DIM = 8, 128, 64
IDX_HEADS, IDX_DIM = 4, 64
DSA_TOPK_MAX = 256
A_Q = A_HEADS * 2 * A_QK_DIM
A_K = A_Q
A_V = A_HEADS * A_V_DIM
B_Q = B_HEADS * B_LAT
B_KV = B_LAT
I_Q = IDX_HEADS * IDX_DIM
I_K = IDX_DIM
I_W = IDX_HEADS
OFF_AQ, OFF_AK, OFF_AV = 0, A_Q, A_Q + A_K
OFF_BQ = OFF_AV + A_V
OFF_KV = OFF_BQ + B_Q
OFF_IQ = OFF_KV + B_KV
OFF_IK = OFF_IQ + I_Q
OFF_IW = OFF_IK + I_K
IN_COLS = OFF_IW + I_W
PEER_HEADS, PEER_NKEYS, PEER_QDIM, PEER_TOPK = 8, 128, 128, 16
EPS = 1e-6
NEG = -1e30
INT_MIN = -(2 ** 31)

SUBLANES = 8
LANES = 128
VMEM_LIMIT = 56 * 1024 * 1024

PROJ_TM = 256
A_TQ = 512
B_TQ = 256
PEER_TM = 128
PEER_TB = 8
PEER_NSEL = PEER_HEADS * PEER_TOPK
PEER_PLAN = ((8192, 8192, (1, "outproj")), (8192, 8192, (3, "diffattn")), (8192, 4096, None), (0, 0, None))
HALF_ROWS = 4


def _split_bf16(x):
    hi = x.astype(jnp.bfloat16)
    lo = (x - hi.astype(jnp.float32)).astype(jnp.bfloat16)
    return hi, lo


def _dot(a, b):
    return jnp.dot(a, b, preferred_element_type=jnp.float32)


def _dot_nt(a, b):
    return lax.dot_general(a, b, (((1,), (1,)), ((), ())), preferred_element_type=jnp.float32)


def _cparams(sem):
    return pltpu.CompilerParams(dimension_semantics=sem, vmem_limit_bytes=VMEM_LIMIT)


def _adaln_kernel(c_ref, w_ref, b_ref, o_ref):
    cf = c_ref[...]
    a = cf * (1.0 / (1.0 + jnp.exp(-cf)))
    a_hi, a_lo = _split_bf16(a)
    w_hi, w_lo = _split_bf16(w_ref[...])
    o_ref[...] = _dot(a_hi, w_hi) + _dot(a_hi, w_lo) + _dot(a_lo, w_hi) + b_ref[...]


def _adaln(c, w, b):
    bsz, d = c.shape
    n = w.shape[1]
    tn = 1024
    return pl.pallas_call(
        _adaln_kernel,
        grid=(n // tn,),
        in_specs=[pl.BlockSpec((bsz, d), lambda j: (0, 0)),
                  pl.BlockSpec((d, tn), lambda j: (0, j)),
                  pl.BlockSpec((1, tn), lambda j: (0, j))],
        out_specs=pl.BlockSpec((bsz, tn), lambda j: (0, j)),
        out_shape=jax.ShapeDtypeStruct((bsz, n), jnp.float32),
        compiler_params=_cparams(("arbitrary",)),
        name="adaln",
    )(c, w, b.reshape(1, n))


def _rms_rows(x, g):
    return x * lax.rsqrt(jnp.mean(x * x, axis=-1, keepdims=True) + EPS) * g


def _group_norm_block(p, gmat, gain, n):
    hi, lo = _split_bf16(p * p)
    ss = _dot(hi, gmat) + _dot(lo, gmat)
    return p * lax.rsqrt(ss * (1.0 / n) + EPS) * gain


def _inproj_kernel(x_ref, sh_ref, sc_ref, g_ref, w_ref, gq_ref, gk_ref, gbq_ref, gkv_ref,
                   qa_ref, ka_ref, va_ref, qb_ref, kv_ref, qi_ref, ki_ref, wi_ref):
    h = _rms_rows(x_ref[...], g_ref[...]) * (1.0 + sc_ref[0]) + sh_ref[0]
    hb = h.astype(jnp.bfloat16)
    row = lax.broadcasted_iota(jnp.int32, (LANES, LANES), 0)
    col = lax.broadcasted_iota(jnp.int32, (LANES, LANES), 1)
    g64 = jnp.where((row // A_QK_DIM) == (col // A_QK_DIM), 1.0, 0.0).astype(jnp.bfloat16)
    g128 = jnp.ones((LANES, LANES), jnp.bfloat16)

    def proj(off, width):
        return _dot(hb, w_ref[:, off:off + width])

    for blk in range(A_Q // LANES):
        sl = slice(blk * LANES, (blk + 1) * LANES)
        p = proj(OFF_AQ + blk * LANES, LANES)
        qa_ref[:, sl] = (_group_norm_block(p, g64, gq_ref[...], A_QK_DIM) * (A_QK_DIM ** -0.5)).astype(qa_ref.dtype)
        p = proj(OFF_AK + blk * LANES, LANES)
        ka_ref[:, sl] = _group_norm_block(p, g64, gk_ref[...], A_QK_DIM).astype(ka_ref.dtype)
    va_ref[...] = proj(OFF_AV, A_V).astype(va_ref.dtype)
    for blk in range(B_HEADS):
        sl = slice(blk * LANES, (blk + 1) * LANES)
        p = proj(OFF_BQ + blk * LANES, LANES)
        qb_ref[:, sl] = _group_norm_block(p, g128, gbq_ref[...], B_LAT).astype(qb_ref.dtype)
    p = proj(OFF_KV, B_KV)
    kv_ref[...] = _group_norm_block(p, g128, gkv_ref[...], B_LAT).astype(kv_ref.dtype)
    qi_ref[...] = proj(OFF_IQ, I_Q).astype(qi_ref.dtype)
    tail = proj(OFF_IK, 2 * LANES)
    ki_ref[...] = tail[:, :LANES].astype(ki_ref.dtype)
    wi_ref[...] = tail[:, I_K:I_K + LANES]


def _inproj(x2, sh1, sc1, g1n, w_in, a_qk_gain, b_q_gain, b_kv_gain, batch):
    t, d = x2.shape
    per_b = (t // batch) // PROJ_TM
    wpad = OFF_IK + 2 * LANES
    w = jnp.zeros((d, wpad), jnp.bfloat16).at[:, :IN_COLS].set(w_in.astype(jnp.bfloat16))
    gq = jnp.tile(a_qk_gain[0], 2).reshape(1, LANES)
    gk = jnp.tile(a_qk_gain[1], 2).reshape(1, LANES)
    tok = lambda width: pl.BlockSpec((PROJ_TM, width), lambda i: (i, 0))
    vec = lambda width: pl.BlockSpec((1, width), lambda i: (0, 0))
    per_batch = pl.BlockSpec((1, 1, d), lambda i: (i // per_b, 0, 0))
    bf = jnp.bfloat16
    outs = [(A_Q, bf), (A_K, bf), (A_V, bf), (B_Q, bf), (B_KV, bf), (I_Q, bf), (LANES, bf), (LANES, jnp.float32)]
    return pl.pallas_call(
        _inproj_kernel,
        grid=(t // PROJ_TM,),
        in_specs=[tok(d), per_batch, per_batch, vec(d),
                  pl.BlockSpec((d, wpad), lambda i: (0, 0)),
                  vec(LANES), vec(LANES), vec(LANES), vec(LANES)],
        out_specs=[tok(wd) for wd, _ in outs],
        out_shape=[jax.ShapeDtypeStruct((t, wd), dt) for wd, dt in outs],
        compiler_params=_cparams(("arbitrary",)),
        name="inproj",
    )(x2, sh1.reshape(batch, 1, d), sc1.reshape(batch, 1, d), g1n.reshape(1, d), w,
      gq, gk, b_q_gain.reshape(1, LANES), b_kv_gain.reshape(1, LANES))


def _positions(q0, tq, s_len):
    qpos = q0 + lax.broadcasted_iota(jnp.int32, (tq, s_len), 0)
    kpos = lax.broadcasted_iota(jnp.int32, (tq, s_len), 1)
    cend = (qpos // CHUNK + 1) * CHUNK
    return qpos, kpos, cend


def _softmax_rows(s):
    m = jnp.max(s, axis=-1, keepdims=True)
    p = jnp.exp(s - m)
    return p, jnp.sum(p, axis=-1, keepdims=True)


def _diffattn_kernel(q_ref, k_ref, v_ref, lam_ref, gain_ref, o_ref, *, lam_init, q_base):
    tq, s_len = q_ref.shape[0], k_ref.shape[0]
    q0 = q_base + pl.program_id(1) * tq
    qpos, kpos, cend = _positions(q0, tq, s_len)
    dist = jnp.abs(qpos - kpos).astype(jnp.float32)
    negmask = jnp.where(kpos < cend, 0.0, NEG)
    lf = lam_ref[...]
    lam = (jnp.exp(jnp.sum(lf[0:1] * lf[1:2], axis=-1, keepdims=True))
           - jnp.exp(jnp.sum(lf[2:3] * lf[3:4], axis=-1, keepdims=True)) + lam_init)
    for h in range(A_HEADS):
        slope = 2.0 ** (-8.0 * (h + 1) / A_HEADS)
        bias = negmask - slope * dist
        v = v_ref[:, h * A_V_DIM:(h + 1) * A_V_DIM]
        outs = []
        for m in range(2):
            c0 = (h * 2 + m) * A_QK_DIM
            s = _dot_nt(q_ref[:, c0:c0 + A_QK_DIM], k_ref[:, c0:c0 + A_QK_DIM]) + bias
            p, l = _softmax_rows(s)
            outs.append(_dot(p.astype(jnp.bfloat16), v) / l)
        o = outs[0] - lam * outs[1]
        o = _rms_rows(o, gain_ref[...]) * (1.0 - lam_init)
        o_ref[:, h * A_V_DIM:(h + 1) * A_V_DIM] = o.astype(o_ref.dtype)


def _causal_groups(s_len, tq):
    span = min(tq, s_len)
    tiles = span // tq
    return [(g * tiles, tiles, (g + 1) * span) for g in range(s_len // span)]


def _qblk(tq, first_tile, width):
    return pl.BlockSpec((None, tq, width), lambda b, i: (b, first_tile + i, 0))


def _kblk(klen, width):
    return pl.BlockSpec((None, klen, width), lambda b, i: (b, 0, 0))


def _oblk(tq, width):
    return pl.BlockSpec((None, tq, width), lambda b, i: (b, i, 0))


def _diffattn(qa, ka, va, a_lambda, a_sub_gain, batch, lam_init):
    t = qa.shape[0]
    s_len = t // batch
    qa3, ka3, va3 = (a.reshape(batch, s_len, a.shape[1]) for a in (qa, ka, va))
    outs = []
    for first, tiles, klen in _causal_groups(s_len, A_TQ):
        outs.append(pl.pallas_call(
            functools.partial(_diffattn_kernel, lam_init=lam_init, q_base=first * A_TQ),
            grid=(batch, tiles),
            in_specs=[_qblk(A_TQ, first, A_Q), _kblk(klen, A_K), _kblk(klen, A_V),
                      pl.BlockSpec((4, A_QK_DIM), lambda b, i: (0, 0)),
                      pl.BlockSpec((1, A_V_DIM), lambda b, i: (0, 0))],
            out_specs=_oblk(A_TQ, A_V),
            out_shape=jax.ShapeDtypeStruct((batch, tiles * A_TQ, A_V), jnp.bfloat16),
            compiler_params=_cparams(("arbitrary", "arbitrary")),
            name="diffattn",
        )(qa3, ka3, va3, a_lambda, a_sub_gain.reshape(1, A_V_DIM)))
    return jnp.concatenate(outs, axis=1).reshape(t, A_V)


def _count(mask):
    return jnp.sum(jnp.where(mask, 1.0, 0.0), axis=-1, keepdims=True)


def _topk_mask(score, allowed, kpos, topk, s_len):
    bits = pltpu.bitcast(score + 0.0, jnp.int32)
    key = jnp.where(bits < 0, bits ^ jnp.int32(0x7FFFFFFF), bits)
    key = jnp.where(allowed, key, jnp.int32(INT_MIN))
    kf = float(topk)
    thr = jnp.where(_count(key >= 0) >= kf, jnp.int32(0), jnp.int32(INT_MIN))

    def value_bit(i, thr):
        cand = thr | (jnp.int32(1) << (30 - i))
        return jnp.where(_count(key >= cand) >= kf, cand, thr)

    thr = lax.fori_loop(0, 31, value_bit, thr)
    above = key > thr
    tie = key == thr
    need = kf - _count(above)
    nbits = (s_len - 1).bit_length()

    def pos_bit(i, pos):
        cand = pos | (jnp.int32(1) << (nbits - 1 - i))
        return jnp.where(_count(tie & (kpos < cand)) < need, cand, pos)

    pos = lax.fori_loop(0, nbits, pos_bit, jnp.zeros_like(thr))
    return allowed & (above | (tie & (kpos <= pos)))


def _dsa_kernel(qi_ref, wi_ref, ki_ref, qb_ref, kv_ref, wuv_ref, o_ref, *, topk, q_base):
    tq, s_len = qb_ref.shape[0], kv_ref.shape[0]
    q0 = q_base + pl.program_id(1) * tq
    qpos, kpos, cend = _positions(q0, tq, s_len)
    allowed = kpos < cend
    ki = ki_ref[...]
    wi = wi_ref[...]
    zpad = jnp.zeros((tq, LANES - IDX_DIM), jnp.bfloat16)
    score = jnp.zeros((tq, s_len), jnp.float32)
    for h in range(IDX_HEADS):
        qh = jnp.concatenate([qi_ref[:, h * IDX_DIM:(h + 1) * IDX_DIM], zpad], axis=-1)
        score = score + jnp.maximum(_dot_nt(qh, ki), 0.0) * wi[:, h:h + 1]
    score = score * ((IDX_DIM ** -0.5) * (IDX_HEADS ** -0.5))
    sel = _topk_mask(score, allowed, kpos, topk, s_len)

    dist = jnp.abs(qpos - kpos).astype(jnp.float32)
    negmask = jnp.where(sel, 0.0, NEG)
    kv = kv_ref[...]
    for h in range(B_HEADS):
        slope = 2.0 ** (-8.0 * (h + 1) / B_HEADS)
        s = _dot_nt(qb_ref[:, h * B_LAT:(h + 1) * B_LAT], kv) * (B_LAT ** -0.5) + (negmask - slope * dist)
        p, l = _softmax_rows(s)
        o_lat = _dot(p.astype(jnp.bfloat16), kv) / l
        o = _dot(o_lat.astype(jnp.bfloat16), wuv_ref[h])
        o_ref[:, h * B_V_DIM:(h + 1) * B_V_DIM] = o.astype(o_ref.dtype)


def _dsa(qi, wi, ki, qb, kvb, w_uv, batch):
    t = qb.shape[0]
    s_len = t // batch
    topk = min(DSA_TOPK_MAX, s_len // 4)
    width = B_HEADS * B_V_DIM
    qi3, wi3, ki3, qb3, kv3 = (a.reshape(batch, s_len, a.shape[1]) for a in (qi, wi, ki, qb, kvb))
    wuv = w_uv.astype(jnp.bfloat16)
    outs = []
    for first, tiles, klen in _causal_groups(s_len, B_TQ):
        outs.append(pl.pallas_call(
            functools.partial(_dsa_kernel, topk=topk, q_base=first * B_TQ),
            grid=(batch, tiles),
            in_specs=[_qblk(B_TQ, first, I_Q), _qblk(B_TQ, first, LANES), _kblk(klen, LANES),
                      _qblk(B_TQ, first, B_Q), _kblk(klen, B_KV),
                      pl.BlockSpec((B_HEADS, B_LAT, B_V_DIM), lambda b, i: (0, 0, 0))],
            out_specs=_oblk(B_TQ, width),
            out_shape=jax.ShapeDtypeStruct((batch, tiles * B_TQ, width), jnp.bfloat16),
            compiler_params=_cparams(("arbitrary", "arbitrary")),
            name="dsa",
        )(qi3, wi3, ki3, qb3, kv3, wuv))
    return jnp.concatenate(outs, axis=1).reshape(t, width)


def _outproj_kernel(x_ref, oa_ref, ob_ref, wo_ref, g1_ref, sh_ref, sc_ref, gn_ref, wq_ref,
                    x1_ref, h2_ref, q_ref):
    na = oa_ref.shape[1]
    y = _dot(oa_ref[...], wo_ref[:na, :]) + _dot(ob_ref[...], wo_ref[na:, :])
    x1 = x_ref[...] + g1_ref[0] * y
    x1_ref[...] = x1
    h2 = _rms_rows(x1, gn_ref[...]) * (1.0 + sc_ref[0]) + sh_ref[0]
    h2_ref[...] = h2
    q_ref[...] = _dot(h2.astype(jnp.bfloat16), wq_ref[...]).astype(q_ref.dtype)


def _outproj(x2, oa, ob, w_out, g1, sh2, sc2, g2n, peer_wq, batch):
    t, d = x2.shape
    per_b = (t // batch) // PROJ_TM
    nq = peer_wq.shape[1]
    tok = lambda width: pl.BlockSpec((PROJ_TM, width), lambda i: (i, 0))
    per_batch = pl.BlockSpec((1, 1, d), lambda i: (i // per_b, 0, 0))
    full = lambda a, b: pl.BlockSpec((a, b), lambda i: (0, 0))
    return pl.pallas_call(
        _outproj_kernel,
        grid=(t // PROJ_TM,),
        in_specs=[tok(d), tok(oa.shape[1]), tok(ob.shape[1]), full(w_out.shape[0], d),
                  per_batch, per_batch, per_batch, full(1, d), full(d, nq)],
        out_specs=[tok(d), tok(d), tok(nq)],
        out_shape=[jax.ShapeDtypeStruct((t, d), jnp.float32),
                   jax.ShapeDtypeStruct((t, d), jnp.float32),
                   jax.ShapeDtypeStruct((t, nq), jnp.bfloat16)],
        compiler_params=_cparams(("arbitrary",)),
        name="outproj",
    )(x2, oa, ob, w_out.astype(jnp.bfloat16), g1.reshape(batch, 1, d), sh2.reshape(batch, 1, d),
      sc2.reshape(batch, 1, d), g2n.reshape(1, d), peer_wq.astype(jnp.bfloat16))


def _extract_topk(vals, payload, k):
    n, tm = vals.shape
    rows = lax.broadcasted_iota(jnp.int32, (n, tm), 0).astype(jnp.float32)
    top_v, top_p = [], []
    for _ in range(k):
        m = jnp.max(vals, axis=0, keepdims=True)
        pos = jnp.min(jnp.where(vals == m, rows, float(n)), axis=0, keepdims=True)
        hit = rows == pos
        top_v.append(m)
        top_p.append(pos if payload is None else jnp.max(jnp.where(hit, payload, -1.0), axis=0, keepdims=True))
        vals = jnp.where(hit, -jnp.inf, vals)
    return jnp.concatenate(top_v, axis=0), jnp.concatenate(top_p, axis=0)


def _stair_pairs():
    return [(a, b) for a in range(PEER_TOPK) for b in range(PEER_TOPK) if (a + 1) * (b + 1) <= PEER_TOPK]


N_STAIR = len(_stair_pairs())
N_STAIR_PAD = -(-N_STAIR // SUBLANES) * SUBLANES


def _copy_rows(sel, x):
    x0 = x.astype(jnp.bfloat16)
    r1 = x - x0.astype(jnp.float32)
    x1 = r1.astype(jnp.bfloat16)
    x2 = (r1 - x1.astype(jnp.float32)).astype(jnp.bfloat16)
    return _dot(sel, x0) + _dot(sel, x1) + _dot(sel, x2)


def _peer_select_kernel(q_ref, sk_ref, sela_ref, selb_ref, e_ref, g_ref):
    tm = q_ref.shape[0]
    sela, selb = sela_ref[...], selb_ref[...]
    pad_row = lax.broadcasted_iota(jnp.int32, (N_STAIR_PAD, tm), 0) >= N_STAIR
    e_rows, g_rows = [], []
    for h in range(PEER_HEADS):
        sub = _dot_nt(sk_ref[...], q_ref[:, h * PEER_QDIM:(h + 1) * PEER_QDIM])
        sv0, si0 = _extract_topk(sub[:PEER_NKEYS], None, PEER_TOPK)
        sv1, si1 = _extract_topk(sub[PEER_NKEYS:], None, PEER_TOPK)
        cand = jnp.where(pad_row, -jnp.inf, _copy_rows(sela, sv0) + _copy_rows(selb, sv1))
        ids = (_dot(sela, si0.astype(jnp.bfloat16)) * float(PEER_NKEYS)
               + _dot(selb, si1.astype(jnp.bfloat16))) * float(HALF_ROWS)
        top_s, top_e = _extract_topk(cand, ids, PEER_TOPK)
        p = jnp.exp(top_s - top_s[0:1])
        e_rows.append(top_e)
        g_rows.append(p / jnp.sum(p, axis=0, keepdims=True))
    e_t = jnp.concatenate(e_rows, axis=0)
    g_t = jnp.concatenate(g_rows, axis=0)
    e_ref[...] = e_t.T.astype(jnp.int32)
    g_ref[...] = g_t.T


def _peer_select(q, sub_keys):
    t, nq = q.shape
    half = PEER_QDIM // 2
    sk = jnp.zeros((2 * PEER_NKEYS, PEER_QDIM), jnp.bfloat16)
    sk = sk.at[:PEER_NKEYS, :half].set(sub_keys[0].astype(jnp.bfloat16))
    sk = sk.at[PEER_NKEYS:, half:].set(sub_keys[1].astype(jnp.bfloat16))
    sel = np.zeros((2, N_STAIR_PAD, PEER_TOPK), np.float32)
    for r, (a, b) in enumerate(_stair_pairs()):
        sel[0, r, a] = 1.0
        sel[1, r, b] = 1.0
    sel = jnp.asarray(sel, jnp.bfloat16)
    tok = lambda width: pl.BlockSpec((PEER_TM, width), lambda i: (i, 0))
    full = lambda a, b: pl.BlockSpec((a, b), lambda i: (0, 0))
    return pl.pallas_call(
        _peer_select_kernel,
        grid=(t // PEER_TM,),
        in_specs=[tok(nq), full(2 * PEER_NKEYS, PEER_QDIM),
                  full(N_STAIR_PAD, PEER_TOPK), full(N_STAIR_PAD, PEER_TOPK)],
        out_specs=[tok(PEER_NSEL), tok(PEER_NSEL)],
        out_shape=[jax.ShapeDtypeStruct((t, PEER_NSEL), jnp.int32),
                   jax.ShapeDtypeStruct((t, PEER_NSEL), jnp.float32)],
        compiler_params=_cparams(("arbitrary",)),
        name="peer_select",
    )(q, sk, sel[0], sel[1])


def _gather_row(tab_ref, row):
    w = tab_ref[pl.ds(pl.multiple_of(row, HALF_ROWS), HALF_ROWS), :]
    hi = pltpu.bitcast(w & jnp.uint32(0xFFFF0000), jnp.float32)
    lo = pltpu.bitcast(w << 16, jnp.float32)
    return hi, lo


def _peer_u_kernel(idx_ref, h_ref, gate_ref, tab_ref, coef_ref, abuf_ref):
    ones = jnp.ones((2 * LANES, LANES), jnp.bfloat16)
    lane = lax.broadcasted_iota(jnp.int32, (PEER_NSEL, PEER_TM), 1)
    rows_per_tok = PEER_NSEL * HALF_ROWS

    def batch(b, a_t):
        def token(tb, c2):
            t = b * PEER_TB + tb
            x8 = h_ref[pl.ds(t, 1), :].reshape(SUBLANES, LANES)
            xh, xl = x8[:HALF_ROWS], x8[HALF_ROWS:]
            off = pl.multiple_of(tb * rows_per_tok, rows_per_tok)
            for j in range(PEER_NSEL):
                hi, lo = _gather_row(tab_ref, idx_ref[0, t, j])
                abuf_ref[pl.ds(off + j * HALF_ROWS, HALF_ROWS), :] = hi * xh + lo * xl
            return c2

        lax.fori_loop(0, PEER_TB, token, 0)
        parts = []
        for tb in range(PEER_TB):
            ab = abuf_ref[pl.ds(tb * rows_per_tok, PEER_NSEL, stride=HALF_ROWS), :]
            for s in range(1, HALF_ROWS):
                ab = ab + abuf_ref[pl.ds(tb * rows_per_tok + s, PEER_NSEL, stride=HALF_ROWS), :]
            parts.append(ab)
        ab_hi, ab_lo = _split_bf16(jnp.concatenate(parts, axis=0))
        r = _dot(jnp.concatenate([ab_hi, ab_lo], axis=1), ones)
        for tb in range(PEER_TB):
            a_t = jnp.where(lane == b * PEER_TB + tb, r[tb * PEER_NSEL:(tb + 1) * PEER_NSEL], a_t)
        return a_t

    a_t = lax.fori_loop(0, PEER_TM // PEER_TB, batch, jnp.zeros((PEER_NSEL, PEER_TM), jnp.float32))
    a = a_t.T
    coef_ref[...] = 0.5 * a * (1.0 + lax.erf(a * (2.0 ** -0.5))) * gate_ref[...]


def _peer_v_kernel(idx_ref, coef_ref, x_ref, g2_ref, tab_ref, out_ref, cb_ref):
    g2 = g2_ref[...]
    ones = jnp.ones((2 * LANES, LANES), jnp.bfloat16)
    eye = (lax.broadcasted_iota(jnp.int32, (PEER_NSEL, LANES), 0)
           == lax.broadcasted_iota(jnp.int32, (PEER_NSEL, LANES), 1))
    nacc = 2

    def batch(b, carry):
        for tb in range(PEER_TB):
            row = coef_ref[pl.ds(b * PEER_TB + tb, 1), :]
            d_hi, d_lo = _split_bf16(jnp.where(eye, jnp.broadcast_to(row, (PEER_NSEL, LANES)), 0.0))
            cb_ref[pl.ds(tb * PEER_NSEL, PEER_NSEL), :] = _dot(jnp.concatenate([d_hi, d_lo], axis=1), ones)

        def token(tb, c2):
            t = b * PEER_TB + tb
            zero = jnp.zeros((HALF_ROWS, LANES), jnp.float32)
            acc_h = [zero] * nacc
            acc_l = [zero] * nacc
            cbase = pl.multiple_of(tb * PEER_NSEL, PEER_NSEL)
            for j in range(PEER_NSEL):
                hi, lo = _gather_row(tab_ref, idx_ref[0, t, j])
                c = jnp.broadcast_to(cb_ref[pl.ds(cbase + j, 1), :], (HALF_ROWS, LANES))
                acc_h[j % nacc] = acc_h[j % nacc] + c * hi
                acc_l[j % nacc] = acc_l[j % nacc] + c * lo
            y8 = jnp.concatenate([acc_h[0] + acc_h[1], acc_l[0] + acc_l[1]], axis=0)
            out_ref[pl.ds(t, 1), :] = x_ref[pl.ds(t, 1), :] + g2 * y8.reshape(1, SUBLANES * LANES)
            return c2

        lax.fori_loop(0, PEER_TB, token, 0)
        return carry

    lax.fori_loop(0, PEER_TM // PEER_TB, batch, 0)


def _sc_peer_u(expert_ids, h2, tab, first, count):
    info = plsc.get_sparse_core_info()
    lanes, workers = info.num_lanes, info.num_cores * info.num_subcores
    half = tab.shape[1]
    per_w = count // workers
    nhalf = PEER_NSEL // 2
    assert per_w * workers == count and per_w >= 2 and nhalf % lanes == 0 and half % lanes == 0
    rows_blk = 4
    kunroll = 2
    mesh = plsc.VectorSubcoreMesh(core_axis_name="c", subcore_axis_name="s")

    @functools.partial(
        pl.kernel, mesh=mesh, name="peer_u_sc",
        compiler_params=pltpu.CompilerParams(needs_layout_passes=False),
        out_type=jax.ShapeDtypeStruct((count, PEER_NSEL), jnp.float32),
        scratch_types=[pltpu.VMEM((2, PEER_NSEL), jnp.int32),
                       pltpu.VMEM((nhalf, half), jnp.uint32), pltpu.VMEM((nhalf, half), jnp.uint32),
                       pltpu.VMEM((2, 2 * half), jnp.float32), pltpu.VMEM((2, PEER_NSEL), jnp.float32),
                       pltpu.SemaphoreType.DMA, pltpu.SemaphoreType.DMA, pltpu.SemaphoreType.DMA,
                       pltpu.SemaphoreType.DMA, pltpu.SemaphoreType.DMA])
    def run(idx_hbm, h_hbm, tab_hbm, out_hbm, idx_v, rows_a, rows_b, x_v, a_v, sem_a, sem_b, sem_i, sem_x, sem_o):
        wid = lax.axis_index("s") * info.num_cores + lax.axis_index("c")
        base = first + wid * per_w
        lane = lax.iota(jnp.int32, lanes)
        mask_hi = jnp.full((lanes,), 0xFFFF0000, jnp.uint32)

        def gather(p, part, rows, sem):
            return pltpu.make_async_copy(tab_hbm.at[idx_v.at[p, pl.ds(part * nhalf, nhalf)]], rows, sem)

        def load_idx(t, p):
            return pltpu.make_async_copy(idx_hbm.at[t], idx_v.at[p], sem_i)

        def load_x(t, p):
            return pltpu.make_async_copy(h_hbm.at[t], x_v.at[p], sem_x)

        def store_a(t, p):
            return pltpu.make_async_copy(a_v.at[p], out_hbm.at[t - first], sem_o)

        def compute(rows, p, part):
            def group(g, c2):
                vec = jnp.zeros((lanes,), jnp.float32)
                for rb in range(lanes // rows_blk):
                    def kslice(kk, accs):
                        accs = list(accs)
                        for ku in range(kunroll):
                            off = pl.multiple_of((kk * kunroll + ku) * lanes, lanes)
                            xh = x_v[p, pl.ds(off, lanes)]
                            xl = x_v[p, pl.ds(half + off, lanes)]
                            for r in range(rows_blk):
                                w = rows[g * lanes + rb * rows_blk + r, pl.ds(off, lanes)]
                                hi = plsc.bitcast(w & mask_hi, jnp.float32)
                                lo = plsc.bitcast(w << 16, jnp.float32)
                                accs[r] = accs[r] + hi * xh + lo * xl
                        return tuple(accs)

                    zero = jnp.zeros((lanes,), jnp.float32)
                    accs = lax.fori_loop(0, half // lanes // kunroll, kslice, (zero,) * rows_blk)
                    for r in range(rows_blk):
                        vec = jnp.where(lane == rb * rows_blk + r, jnp.sum(accs[r]), vec)
                a_v[p, pl.ds(pl.multiple_of(part * nhalf + g * lanes, lanes), lanes)] = vec
                return c2

            lax.fori_loop(0, nhalf // lanes, group, 0)

        load_idx(base, 0).start()
        load_x(base, 0).start()
        load_idx(base, 0).wait()
        gather(0, 0, rows_a, sem_a).start()
        gather(0, 1, rows_b, sem_b).start()
        load_x(base, 0).wait()

        def token(i, carry):
            t = base + i
            p = lax.rem(i, 2)
            q = 1 - p
            tn = jnp.minimum(t + 1, base + per_w - 1)
            load_idx(tn, q).start()
            load_x(tn, q).start()

            @pl.when(i >= 2)
            def _():
                store_a(t, p).wait()

            gather(p, 0, rows_a, sem_a).wait()
            compute(rows_a, p, 0)
            load_idx(tn, q).wait()
            gather(q, 0, rows_a, sem_a).start()
            gather(p, 1, rows_b, sem_b).wait()
            compute(rows_b, p, 1)
            gather(q, 1, rows_b, sem_b).start()
            load_x(tn, q).wait()
            store_a(t, p).start()
            return carry

        lax.fori_loop(0, per_w, token, 0)
        gather(0, 0, rows_a, sem_a).wait()
        gather(0, 1, rows_b, sem_b).wait()
        store_a(base, 0).wait()
        store_a(base, 1).wait()

    return run(expert_ids, h2, tab)


def _sc_peer_v(expert_ids, coef, x1, g2, tab, first, count, per_batch):
    info = plsc.get_sparse_core_info()
    lanes, workers = info.num_lanes, info.num_cores * info.num_subcores
    half = tab.shape[1]
    d = 2 * half
    per_w = count // workers
    nhalf = PEER_NSEL // 2
    sblk = 4
    runroll = 4
    assert per_w * workers == count and per_w >= 2 and nhalf % runroll == 0 and half % (lanes * sblk) == 0
    mesh = plsc.VectorSubcoreMesh(core_axis_name="c", subcore_axis_name="s")
    dma = pltpu.SemaphoreType.DMA

    @functools.partial(
        pl.kernel, mesh=mesh, name="peer_v_sc",
        compiler_params=pltpu.CompilerParams(needs_layout_passes=False),
        out_type=jax.ShapeDtypeStruct((count, d), jnp.float32),
        scratch_types=[pltpu.VMEM((2, PEER_NSEL), jnp.int32), pltpu.VMEM((2 * PEER_NSEL,), jnp.float32),
                       pltpu.VMEM((nhalf, half), jnp.uint32), pltpu.VMEM((nhalf, half), jnp.uint32),
                       pltpu.VMEM((2, d), jnp.float32), pltpu.VMEM((2, d), jnp.float32),
                       pltpu.VMEM((d,), jnp.float32), pltpu.VMEM((2, d), jnp.float32),
                       dma, dma, dma, dma, dma, dma, dma])
    def run(idx_hbm, coef_hbm, x_hbm, g_hbm, tab_hbm, out_hbm,
            idx_v, coef_v, rows_a, rows_b, x_v, g_v, y_v, o_v, sem_a, sem_b, sem_i, sem_c, sem_x, sem_g, sem_o):
        wid = lax.axis_index("s") * info.num_cores + lax.axis_index("c")
        base = first + wid * per_w
        mask_hi = jnp.full((lanes,), 0xFFFF0000, jnp.uint32)

        def gather(p, part, rows, sem):
            return pltpu.make_async_copy(tab_hbm.at[idx_v.at[p, pl.ds(part * nhalf, nhalf)]], rows, sem)

        def loads(t, p):
            cslot = coef_v.at[pl.ds(pl.multiple_of(p * PEER_NSEL, PEER_NSEL), PEER_NSEL)]
            return (pltpu.make_async_copy(idx_hbm.at[t], idx_v.at[p], sem_i),
                    pltpu.make_async_copy(coef_hbm.at[t], cslot, sem_c),
                    pltpu.make_async_copy(x_hbm.at[t], x_v.at[p], sem_x),
                    pltpu.make_async_copy(g_hbm.at[t // per_batch], g_v.at[p], sem_g))

        def store_o(t, p):
            return pltpu.make_async_copy(o_v.at[p], out_hbm.at[t - first], sem_o)

        def compute(rows, p, part):
            cbase = p * PEER_NSEL + part * nhalf
            for sb in range(half // lanes // sblk):
                offs = [(sb * sblk + s) * lanes for s in range(sblk)]
                if part == 0:
                    init = tuple(jnp.zeros((lanes,), jnp.float32) for _ in range(2 * sblk))
                else:
                    init = tuple([y_v[pl.ds(o, lanes)] for o in offs] + [y_v[pl.ds(half + o, lanes)] for o in offs])

                def rowloop(rr, accs):
                    accs = list(accs)
                    for ru in range(runroll):
                        r = rr * runroll + ru
                        c = plsc.load_gather(coef_v, [jnp.full((lanes,), cbase + r, jnp.int32)])
                        for s in range(sblk):
                            w = rows[r, pl.ds(offs[s], lanes)]
                            hi = plsc.bitcast(w & mask_hi, jnp.float32)
                            lo = plsc.bitcast(w << 16, jnp.float32)
                            accs[s] = accs[s] + c * hi
                            accs[sblk + s] = accs[sblk + s] + c * lo
                    return tuple(accs)

                accs = lax.fori_loop(0, nhalf // runroll, rowloop, init)
                for s in range(sblk):
                    for hl, o in ((0, offs[s]), (1, half + offs[s])):
                        if part == 0:
                            y_v[pl.ds(o, lanes)] = accs[hl * sblk + s]
                        else:
                            o_v[p, pl.ds(o, lanes)] = (x_v[p, pl.ds(o, lanes)]
                                                       + g_v[p, pl.ds(o, lanes)] * accs[hl * sblk + s])

        head = loads(base, 0)
        for c in head:
            c.start()
        head[0].wait()
        gather(0, 0, rows_a, sem_a).start()
        gather(0, 1, rows_b, sem_b).start()
        for c in head[1:]:
            c.wait()

        def token(i, carry):
            t = base + i
            p = lax.rem(i, 2)
            q = 1 - p
            tn = jnp.minimum(t + 1, base + per_w - 1)
            nxt = loads(tn, q)
            for c in nxt:
                c.start()

            @pl.when(i >= 2)
            def _():
                store_o(t, p).wait()

            gather(p, 0, rows_a, sem_a).wait()
            compute(rows_a, p, 0)
            nxt[0].wait()
            gather(q, 0, rows_a, sem_a).start()
            gather(p, 1, rows_b, sem_b).wait()
            compute(rows_b, p, 1)
            gather(q, 1, rows_b, sem_b).start()
            for c in nxt[1:]:
                c.wait()
            store_o(t, p).start()
            return carry

        lax.fori_loop(0, per_w, token, 0)
        gather(0, 0, rows_a, sem_a).wait()
        gather(0, 1, rows_b, sem_b).wait()
        store_o(base, 0).wait()
        store_o(base, 1).wait()

    return run(expert_ids, coef, x1, g2, tab)


def _gelu_gate_kernel(a_ref, g_ref, o_ref):
    a = a_ref[...]
    o_ref[...] = 0.5 * a * (1.0 + lax.erf(a * (2.0 ** -0.5))) * g_ref[...]


def _gelu_gate(a, gates, first):
    count = a.shape[0]
    return pl.pallas_call(
        _gelu_gate_kernel,
        grid=(count // PEER_TM,),
        in_specs=[pl.BlockSpec((PEER_TM, PEER_NSEL), lambda i: (i, 0)),
                  pl.BlockSpec((PEER_TM, PEER_NSEL), lambda i: (first // PEER_TM + i, 0))],
        out_specs=pl.BlockSpec((PEER_TM, PEER_NSEL), lambda i: (i, 0)),
        out_shape=jax.ShapeDtypeStruct((count, PEER_NSEL), jnp.float32),
        compiler_params=_cparams(("arbitrary",)),
        name="gelu_gate",
    )(a, gates)


def _pack_table(tab):
    n, d = tab.shape
    bits = lax.bitcast_convert_type(tab.astype(jnp.bfloat16), jnp.uint16).astype(jnp.uint32)
    packed = (bits[:, : d // 2] << 16) | bits[:, d // 2:]
    return packed.reshape(n * HALF_ROWS, LANES)


def _peer_specs(t, d):
    smem_blk = pl.BlockSpec((1, PEER_TM, PEER_NSEL), lambda i: (i, 0, 0), memory_space=pltpu.SMEM)
    tok_blk = pl.BlockSpec((PEER_TM, d), lambda i: (i, 0))
    sel_blk = pl.BlockSpec((PEER_TM, PEER_NSEL), lambda i: (i, 0))
    return smem_blk, tok_blk, sel_blk, pl.BlockSpec(memory_space=pltpu.VMEM)


def _after(x, prev):
    return x if prev is None else lax.optimization_barrier((x, prev))[0]


def _peer_u_phase(h2, experts, gates, u_packed, n_sc, sc_prev):
    t, d = h2.shape
    assert d == SUBLANES * LANES and d // 2 == HALF_ROWS * LANES and PEER_TM % PEER_TB == 0
    smem_blk, tok_blk, sel_blk, tab_spec = _peer_specs(t, d)
    parts = []
    n_tc = t - n_sc
    if n_tc:
        parts.append(pl.pallas_call(
            _peer_u_kernel,
            grid=(n_tc // PEER_TM,),
            in_specs=[smem_blk, tok_blk, sel_blk, tab_spec],
            out_specs=sel_blk,
            out_shape=jax.ShapeDtypeStruct((n_tc, PEER_NSEL), jnp.float32),
            scratch_shapes=[pltpu.VMEM((PEER_TB * PEER_NSEL * HALF_ROWS, LANES), jnp.float32)],
            compiler_params=_cparams(("arbitrary",)),
            name="peer_u",
        )(experts.reshape(t // PEER_TM, PEER_TM, PEER_NSEL), h2, gates, u_packed))
    a_sc = None
    if n_sc:
        a_sc = _sc_peer_u(experts // HALF_ROWS, h2, u_packed.reshape(-1, d // 2), n_tc, n_sc)
    return (parts[0] if parts else None), a_sc


def _finish_coef(coef_tc, a_sc, gates):
    parts = [] if coef_tc is None else [coef_tc]
    if a_sc is not None:
        parts.append(_gelu_gate(a_sc, gates, gates.shape[0] - a_sc.shape[0]))
    return parts[0] if len(parts) == 1 else jnp.concatenate(parts, axis=0)


def _peer_v_phase(coef, x1, g2, experts, v_packed, batch, n_sc, sc_prev):
    t, d = x1.shape
    per_b = t // PEER_TM // batch
    smem_blk, tok_blk, sel_blk, tab_spec = _peer_specs(t, d)
    parts = []
    n_tc = t - n_sc
    if n_tc:
        parts.append(pl.pallas_call(
            _peer_v_kernel,
            grid=(n_tc // PEER_TM,),
            in_specs=[smem_blk, sel_blk, tok_blk,
                      pl.BlockSpec((None, 1, d), lambda i: (i // per_b, 0, 0)),
                      tab_spec],
            out_specs=tok_blk,
            out_shape=jax.ShapeDtypeStruct((n_tc, d), jnp.float32),
            scratch_shapes=[pltpu.VMEM((PEER_TB * PEER_NSEL, LANES), jnp.float32)],
            compiler_params=_cparams(("arbitrary",)),
            name="peer_v",
        )(experts.reshape(t // PEER_TM, PEER_TM, PEER_NSEL), coef, x1, g2.reshape(batch, 1, d), v_packed))
    if n_sc:
        sc_prev = _sc_peer_v(experts // HALF_ROWS, coef, x1, g2, v_packed.reshape(-1, d // 2),
                             n_tc, n_sc, t // batch)
        parts.append(sc_prev)
    return parts, sc_prev


def kernel(x, c, ada_w, ada_b, norm1_g, norm2_g, w_in, a_qk_gain, a_lambda, a_sub_gain, b_q_gain, b_kv_gain, b_w_uv, w_out, peer_wq, peer_subkeys, peer_u, peer_v):
    b, s, d = x.shape
    t = b * s
    x2 = x.reshape(t, d)
    nchunk = len(PEER_PLAN)
    bc = b // nchunk
    tc = bc * s
    for l in range(ada_w.shape[0]):
        mod = _adaln(c, ada_w[l], ada_b[l])
        u_packed, v_packed = _pack_table(peer_u[l]), _pack_table(peer_v[l])
        lam_init = 0.8 - 0.6 * math.exp(-0.3 * l)
        outs = [None] * nchunk
        pending = {}
        sc_prev = None
        prev_experts = None

        def issue_v(k, anchor):
            nonlocal sc_prev
            coef_tc, a_sc, gates, args = pending.pop(k)
            if a_sc is not None:
                anchor, a_sc = lax.optimization_barrier((anchor, a_sc))
                sc_prev = a_sc
            outs[k], sc_prev = _peer_v_phase(_finish_coef(coef_tc, a_sc, gates), *args, sc_prev)
            return anchor

        def issue_planned(at, anchor):
            for k in [k for k in pending if PEER_PLAN[k][2] == at]:
                anchor = issue_v(k, anchor)
            return anchor

        for ck, (n_sc_u, n_sc_v, _) in enumerate(PEER_PLAN):
            xc = _after(x2[ck * tc:(ck + 1) * tc], prev_experts)
            sh1, sc1, g1, sh2, sc2, g2 = [mod[ck * bc:(ck + 1) * bc, i * d:(i + 1) * d] for i in range(6)]
            qa, ka, va, qb, kvb, qi, ki, wi = _inproj(xc, sh1, sc1, norm1_g[l], w_in[l], a_qk_gain[l],
                                                      b_q_gain[l], b_kv_gain[l], bc)
            oa = _diffattn(qa, ka, va, a_lambda[l], a_sub_gain[l], bc, lam_init)
            oa = issue_planned((ck, "diffattn"), oa)
            ob = _dsa(qi, wi, ki, qb, kvb, b_w_uv[l], bc)
            x1, h2, q = _outproj(xc, oa, ob, w_out[l], g1, sh2, sc2, norm2_g[l], peer_wq[l], bc)
            q = issue_planned((ck, "outproj"), q)
            experts, gates = _peer_select(q, peer_subkeys[l])
            prev_experts = experts
            coef_tc, a_sc = _peer_u_phase(h2, experts, gates, u_packed, n_sc_u, sc_prev)
            sc_prev = sc_prev if a_sc is None else a_sc
            pending[ck] = (coef_tc, a_sc, gates, (x1, g2, experts, v_packed, bc, n_sc_v))
        anchor = pending[nchunk - 1][0]
        for k in sorted(pending):
            anchor = issue_v(k, anchor)
        x2 = jnp.concatenate([p for o in outs for p in o], axis=0)
    return x2.reshape(b, s, d)
```

```python
import functools
import math

import jax
import jax.numpy as jnp
import numpy as np
from jax import lax
from jax.experimental import pallas as pl
from jax.experimental.pallas import tpu as pltpu
from jax.experimental.pallas import tpu_sc as plsc

CHUNK = 64
A_HEADS, A_QK_DIM, A_V_DIM = 4, 64, 128
B_HEADS, B_LAT, B_V_DIM = 8, 128, 64
IDX_HEADS, IDX_DIM = 4, 64
DSA_TOPK_MAX = 256
A_Q = A_HEADS * 2 * A_QK_DIM
A_K = A_Q
A_V = A_HEADS * A_V_DIM
B_Q = B_HEADS * B_LAT
B_KV = B_LAT
I_Q = IDX_HEADS * IDX_DIM
I_K = IDX_DIM
I_W = IDX_HEADS
OFF_AQ, OFF_AK, OFF_AV = 0, A_Q, A_Q + A_K
OFF_BQ = OFF_AV + A_V
OFF_KV = OFF_BQ + B_Q
OFF_IQ = OFF_KV + B_KV
OFF_IK = OFF_IQ + I_Q
OFF_IW = OFF_IK + I_K
IN_COLS = OFF_IW + I_W
PEER_HEADS, PEER_NKEYS, PEER_QDIM, PEER_TOPK = 8, 128, 128, 16
EPS = 1e-6
NEG = -1e30
INT_MIN = -(2 ** 31)

SUBLANES = 8
LANES = 128
VMEM_LIMIT = 56 * 1024 * 1024

PROJ_TM = 256
A_TQ = 512
B_TQ = 256
PEER_TM = 128
PEER_TB = 8
PEER_NSEL = PEER_HEADS * PEER_TOPK
PEER_PLAN = ((8192, 8192, (1, "outproj")), (8192, 8192, (3, "diffattn")), (8192, 3584, None), (0, 0, None))
HALF_ROWS = 4


def _split_bf16(x):
    hi = x.astype(jnp.bfloat16)
    lo = (x - hi.astype(jnp.float32)).astype(jnp.bfloat16)
    return hi, lo


def _dot(a, b):
    return jnp.dot(a, b, preferred_element_type=jnp.float32)


def _dot_nt(a, b):
    return lax.dot_general(a, b, (((1,), (1,)), ((), ())), preferred_element_type=jnp.float32)


def _cparams(sem):
    return pltpu.CompilerParams(dimension_semantics=sem, vmem_limit_bytes=VMEM_LIMIT)


def _adaln_kernel(c_ref, w_ref, b_ref, o_ref):
    cf = c_ref[...]
    a = cf * (1.0 / (1.0 + jnp.exp(-cf)))
    a_hi, a_lo = _split_bf16(a)
    w_hi, w_lo = _split_bf16(w_ref[...])
    o_ref[...] = _dot(a_hi, w_hi) + _dot(a_hi, w_lo) + _dot(a_lo, w_hi) + b_ref[...]


def _adaln(c, w, b):
    bsz, d = c.shape
    n = w.shape[1]
    tn = 1024
    return pl.pallas_call(
        _adaln_kernel,
        grid=(n // tn,),
        in_specs=[pl.BlockSpec((bsz, d), lambda j: (0, 0)),
                  pl.BlockSpec((d, tn), lambda j: (0, j)),
                  pl.BlockSpec((1, tn), lambda j: (0, j))],
        out_specs=pl.BlockSpec((bsz, tn), lambda j: (0, j)),
        out_shape=jax.ShapeDtypeStruct((bsz, n), jnp.float32),
        compiler_params=_cparams(("arbitrary",)),
        name="adaln",
    )(c, w, b.reshape(1, n))


def _rms_rows(x, g):
    return x * lax.rsqrt(jnp.mean(x * x, axis=-1, keepdims=True) + EPS) * g


def _group_norm_block(p, gmat, gain, n):
    hi, lo = _split_bf16(p * p)
    ss = _dot(hi, gmat) + _dot(lo, gmat)
    return p * lax.rsqrt(ss * (1.0 / n) + EPS) * gain


def _inproj_kernel(x_ref, sh_ref, sc_ref, g_ref, w_ref, gq_ref, gk_ref, gbq_ref, gkv_ref,
                   qa_ref, ka_ref, va_ref, qb_ref, kv_ref, qi_ref, ki_ref, wi_ref):
    h = _rms_rows(x_ref[...], g_ref[...]) * (1.0 + sc_ref[0]) + sh_ref[0]
    hb = h.astype(jnp.bfloat16)
    row = lax.broadcasted_iota(jnp.int32, (LANES, LANES), 0)
    col = lax.broadcasted_iota(jnp.int32, (LANES, LANES), 1)
    g64 = jnp.where((row // A_QK_DIM) == (col // A_QK_DIM), 1.0, 0.0).astype(jnp.bfloat16)
    g128 = jnp.ones((LANES, LANES), jnp.bfloat16)

    def proj(off, width):
        return _dot(hb, w_ref[:, off:off + width])

    for blk in range(A_Q // LANES):
        sl = slice(blk * LANES, (blk + 1) * LANES)
        p = proj(OFF_AQ + blk * LANES, LANES)
        qa_ref[:, sl] = (_group_norm_block(p, g64, gq_ref[...], A_QK_DIM) * (A_QK_DIM ** -0.5)).astype(qa_ref.dtype)
        p = proj(OFF_AK + blk * LANES, LANES)
        ka_ref[:, sl] = _group_norm_block(p, g64, gk_ref[...], A_QK_DIM).astype(ka_ref.dtype)
    va_ref[...] = proj(OFF_AV, A_V).astype(va_ref.dtype)
    for blk in range(B_HEADS):
        sl = slice(blk * LANES, (blk + 1) * LANES)
        p = proj(OFF_BQ + blk * LANES, LANES)
        qb_ref[:, sl] = _group_norm_block(p, g128, gbq_ref[...], B_LAT).astype(qb_ref.dtype)
    p = proj(OFF_KV, B_KV)
    kv_ref[...] = _group_norm_block(p, g128, gkv_ref[...], B_LAT).astype(kv_ref.dtype)
    qi_ref[...] = proj(OFF_IQ, I_Q).astype(qi_ref.dtype)
    tail = proj(OFF_IK, 2 * LANES)
    ki_ref[...] = tail[:, :LANES].astype(ki_ref.dtype)
    wi_ref[...] = tail[:, I_K:I_K + LANES]


def _inproj(x2, sh1, sc1, g1n, w_in, a_qk_gain, b_q_gain, b_kv_gain, batch):
    t, d = x2.shape
    per_b = (t // batch) // PROJ_TM
    wpad = OFF_IK + 2 * LANES
    w = jnp.zeros((d, wpad), jnp.bfloat16).at[:, :IN_COLS].set(w_in.astype(jnp.bfloat16))
    gq = jnp.tile(a_qk_gain[0], 2).reshape(1, LANES)
    gk = jnp.tile(a_qk_gain[1], 2).reshape(1, LANES)
    tok = lambda width: pl.BlockSpec((PROJ_TM, width), lambda i: (i, 0))
    vec = lambda width: pl.BlockSpec((1, width), lambda i: (0, 0))
    per_batch = pl.BlockSpec((1, 1, d), lambda i: (i // per_b, 0, 0))
    bf = jnp.bfloat16
    outs = [(A_Q, bf), (A_K, bf), (A_V, bf), (B_Q, bf), (B_KV, bf), (I_Q, bf), (LANES, bf), (LANES, jnp.float32)]
    return pl.pallas_call(
        _inproj_kernel,
        grid=(t // PROJ_TM,),
        in_specs=[tok(d), per_batch, per_batch, vec(d),
                  pl.BlockSpec((d, wpad), lambda i: (0, 0)),
                  vec(LANES), vec(LANES), vec(LANES), vec(LANES)],
        out_specs=[tok(wd) for wd, _ in outs],
        out_shape=[jax.ShapeDtypeStruct((t, wd), dt) for wd, dt in outs],
        compiler_params=_cparams(("arbitrary",)),
        name="inproj",
    )(x2, sh1.reshape(batch, 1, d), sc1.reshape(batch, 1, d), g1n.reshape(1, d), w,
      gq, gk, b_q_gain.reshape(1, LANES), b_kv_gain.reshape(1, LANES))


def _positions(q0, tq, s_len):
    qpos = q0 + lax.broadcasted_iota(jnp.int32, (tq, s_len), 0)
    kpos = lax.broadcasted_iota(jnp.int32, (tq, s_len), 1)
    cend = (qpos // CHUNK + 1) * CHUNK
    return qpos, kpos, cend


def _softmax_rows(s):
    m = jnp.max(s, axis=-1, keepdims=True)
    p = jnp.exp(s - m)
    return p, jnp.sum(p, axis=-1, keepdims=True)


def _diffattn_kernel(q_ref, k_ref, v_ref, lam_ref, gain_ref, o_ref, *, lam_init, q_base):
    tq, s_len = q_ref.shape[0], k_ref.shape[0]
    q0 = q_base + pl.program_id(1) * tq
    qpos, kpos, cend = _positions(q0, tq, s_len)
    dist = jnp.abs(qpos - kpos).astype(jnp.float32)
    negmask = jnp.where(kpos < cend, 0.0, NEG)
    lf = lam_ref[...]
    lam = (jnp.exp(jnp.sum(lf[0:1] * lf[1:2], axis=-1, keepdims=True))
           - jnp.exp(jnp.sum(lf[2:3] * lf[3:4], axis=-1, keepdims=True)) + lam_init)
    for h in range(A_HEADS):
        slope = 2.0 ** (-8.0 * (h + 1) / A_HEADS)
        bias = negmask - slope * dist
        v = v_ref[:, h * A_V_DIM:(h + 1) * A_V_DIM]
        outs = []
        for m in range(2):
            c0 = (h * 2 + m) * A_QK_DIM
            s = _dot_nt(q_ref[:, c0:c0 + A_QK_DIM], k_ref[:, c0:c0 + A_QK_DIM]) + bias
            p, l = _softmax_rows(s)
            outs.append(_dot(p.astype(jnp.bfloat16), v) / l)
        o = outs[0] - lam * outs[1]
        o = _rms_rows(o, gain_ref[...]) * (1.0 - lam_init)
        o_ref[:, h * A_V_DIM:(h + 1) * A_V_DIM] = o.astype(o_ref.dtype)


def _causal_groups(s_len, tq):
    span = min(tq, s_len)
    tiles = span // tq
    return [(g * tiles, tiles, (g + 1) * span) for g in range(s_len // span)]


def _qblk(tq, first_tile, width):
    return pl.BlockSpec((None, tq, width), lambda b, i: (b, first_tile + i, 0))


def _kblk(klen, width):
    return pl.BlockSpec((None, klen, width), lambda b, i: (b, 0, 0))


def _oblk(tq, width):
    return pl.BlockSpec((None, tq, width), lambda b, i: (b, i, 0))


def _diffattn(qa, ka, va, a_lambda, a_sub_gain, batch, lam_init):
    t = qa.shape[0]
    s_len = t // batch
    qa3, ka3, va3 = (a.reshape(batch, s_len, a.shape[1]) for a in (qa, ka, va))
    outs = []
    for first, tiles, klen in _causal_groups(s_len, A_TQ):
        outs.append(pl.pallas_call(
            functools.partial(_diffattn_kernel, lam_init=lam_init, q_base=first * A_TQ),
            grid=(batch, tiles),
            in_specs=[_qblk(A_TQ, first, A_Q), _kblk(klen, A_K), _kblk(klen, A_V),
                      pl.BlockSpec((4, A_QK_DIM), lambda b, i: (0, 0)),
                      pl.BlockSpec((1, A_V_DIM), lambda b, i: (0, 0))],
            out_specs=_oblk(A_TQ, A_V),
            out_shape=jax.ShapeDtypeStruct((batch, tiles * A_TQ, A_V), jnp.bfloat16),
            compiler_params=_cparams(("arbitrary", "arbitrary")),
            name="diffattn",
        )(qa3, ka3, va3, a_lambda, a_sub_gain.reshape(1, A_V_DIM)))
    return jnp.concatenate(outs, axis=1).reshape(t, A_V)


def _count(mask):
    return jnp.sum(jnp.where(mask, 1.0, 0.0), axis=-1, keepdims=True)


def _topk_mask(score, allowed, kpos, topk, s_len):
    bits = pltpu.bitcast(score + 0.0, jnp.int32)
    key = jnp.where(bits < 0, bits ^ jnp.int32(0x7FFFFFFF), bits)
    key = jnp.where(allowed, key, jnp.int32(INT_MIN))
    kf = float(topk)
    thr = jnp.where(_count(key >= 0) >= kf, jnp.int32(0), jnp.int32(INT_MIN))

    def value_bit(i, thr):
        cand = thr | (jnp.int32(1) << (30 - i))
        return jnp.where(_count(key >= cand) >= kf, cand, thr)

    thr = lax.fori_loop(0, 31, value_bit, thr)
    above = key > thr
    tie = key == thr
    need = kf - _count(above)
    nbits = (s_len - 1).bit_length()

    def pos_bit(i, pos):
        cand = pos | (jnp.int32(1) << (nbits - 1 - i))
        return jnp.where(_count(tie & (kpos < cand)) < need, cand, pos)

    pos = lax.fori_loop(0, nbits, pos_bit, jnp.zeros_like(thr))
    return allowed & (above | (tie & (kpos <= pos)))


def _dsa_kernel(qi_ref, wi_ref, ki_ref, qb_ref, kv_ref, wuv_ref, o_ref, *, topk, q_base):
    tq, s_len = qb_ref.shape[0], kv_ref.shape[0]
    q0 = q_base + pl.program_id(1) * tq
    qpos, kpos, cend = _positions(q0, tq, s_len)
    allowed = kpos < cend
    ki = ki_ref[...]
    wi = wi_ref[...]
    zpad = jnp.zeros((tq, LANES - IDX_DIM), jnp.bfloat16)
    score = jnp.zeros((tq, s_len), jnp.float32)
    for h in range(IDX_HEADS):
        qh = jnp.concatenate([qi_ref[:, h * IDX_DIM:(h + 1) * IDX_DIM], zpad], axis=-1)
        score = score + jnp.maximum(_dot_nt(qh, ki), 0.0) * wi[:, h:h + 1]
    score = score * ((IDX_DIM ** -0.5) * (IDX_HEADS ** -0.5))
    sel = _topk_mask(score, allowed, kpos, topk, s_len)

    dist = jnp.abs(qpos - kpos).astype(jnp.float32)
    negmask = jnp.where(sel, 0.0, NEG)
    kv = kv_ref[...]
    for h in range(B_HEADS):
        slope = 2.0 ** (-8.0 * (h + 1) / B_HEADS)
        s = _dot_nt(qb_ref[:, h * B_LAT:(h + 1) * B_LAT], kv) * (B_LAT ** -0.5) + (negmask - slope * dist)
        p, l = _softmax_rows(s)
        o_lat = _dot(p.astype(jnp.bfloat16), kv) / l
        o = _dot(o_lat.astype(jnp.bfloat16), wuv_ref[h])
        o_ref[:, h * B_V_DIM:(h + 1) * B_V_DIM] = o.astype(o_ref.dtype)


def _dsa(qi, wi, ki, qb, kvb, w_uv, batch):
    t = qb.shape[0]
    s_len = t // batch
    topk = min(DSA_TOPK_MAX, s_len // 4)
    width = B_HEADS * B_V_DIM
    qi3, wi3, ki3, qb3, kv3 = (a.reshape(batch, s_len, a.shape[1]) for a in (qi, wi, ki, qb, kvb))
    wuv = w_uv.astype(jnp.bfloat16)
    outs = []
    for first, tiles, klen in _causal_groups(s_len, B_TQ):
        outs.append(pl.pallas_call(
            functools.partial(_dsa_kernel, topk=topk, q_base=first * B_TQ),
            grid=(batch, tiles),
            in_specs=[_qblk(B_TQ, first, I_Q), _qblk(B_TQ, first, LANES), _kblk(klen, LANES),
                      _qblk(B_TQ, first, B_Q), _kblk(klen, B_KV),
                      pl.BlockSpec((B_HEADS, B_LAT, B_V_DIM), lambda b, i: (0, 0, 0))],
            out_specs=_oblk(B_TQ, width),
            out_shape=jax.ShapeDtypeStruct((batch, tiles * B_TQ, width), jnp.bfloat16),
            compiler_params=_cparams(("arbitrary", "arbitrary")),
            name="dsa",
        )(qi3, wi3, ki3, qb3, kv3, wuv))
    return jnp.concatenate(outs, axis=1).reshape(t, width)


def _outproj_kernel(x_ref, oa_ref, ob_ref, wo_ref, g1_ref, sh_ref, sc_ref, gn_ref, wq_ref,
                    x1_ref, h2_ref, q_ref):
    na = oa_ref.shape[1]
    y = _dot(oa_ref[...], wo_ref[:na, :]) + _dot(ob_ref[...], wo_ref[na:, :])
    x1 = x_ref[...] + g1_ref[0] * y
    x1_ref[...] = x1
    h2 = _rms_rows(x1, gn_ref[...]) * (1.0 + sc_ref[0]) + sh_ref[0]
    h2_ref[...] = h2
    q_ref[...] = _dot(h2.astype(jnp.bfloat16), wq_ref[...]).astype(q_ref.dtype)


def _outproj(x2, oa, ob, w_out, g1, sh2, sc2, g2n, peer_wq, batch):
    t, d = x2.shape
    per_b = (t // batch) // PROJ_TM
    nq = peer_wq.shape[1]
    tok = lambda width: pl.BlockSpec((PROJ_TM, width), lambda i: (i, 0))
    per_batch = pl.BlockSpec((1, 1, d), lambda i: (i // per_b, 0, 0))
    full = lambda a, b: pl.BlockSpec((a, b), lambda i: (0, 0))
    return pl.pallas_call(
        _outproj_kernel,
        grid=(t // PROJ_TM,),
        in_specs=[tok(d), tok(oa.shape[1]), tok(ob.shape[1]), full(w_out.shape[0], d),
                  per_batch, per_batch, per_batch, full(1, d), full(d, nq)],
        out_specs=[tok(d), tok(d), tok(nq)],
        out_shape=[jax.ShapeDtypeStruct((t, d), jnp.float32),
                   jax.ShapeDtypeStruct((t, d), jnp.float32),
                   jax.ShapeDtypeStruct((t, nq), jnp.bfloat16)],
        compiler_params=_cparams(("arbitrary",)),
        name="outproj",
    )(x2, oa, ob, w_out.astype(jnp.bfloat16), g1.reshape(batch, 1, d), sh2.reshape(batch, 1, d),
      sc2.reshape(batch, 1, d), g2n.reshape(1, d), peer_wq.astype(jnp.bfloat16))


def _extract_topk(vals, payload, k):
    n, tm = vals.shape
    rows = lax.broadcasted_iota(jnp.int32, (n, tm), 0).astype(jnp.float32)
    top_v, top_p = [], []
    for _ in range(k):
        m = jnp.max(vals, axis=0, keepdims=True)
        pos = jnp.min(jnp.where(vals == m, rows, float(n)), axis=0, keepdims=True)
        hit = rows == pos
        top_v.append(m)
        top_p.append(pos if payload is None else jnp.max(jnp.where(hit, payload, -1.0), axis=0, keepdims=True))
        vals = jnp.where(hit, -jnp.inf, vals)
    return jnp.concatenate(top_v, axis=0), jnp.concatenate(top_p, axis=0)


def _stair_pairs():
    return [(a, b) for a in range(PEER_TOPK) for b in range(PEER_TOPK) if (a + 1) * (b + 1) <= PEER_TOPK]


N_STAIR = len(_stair_pairs())
N_STAIR_PAD = -(-N_STAIR // SUBLANES) * SUBLANES


def _copy_rows(sel, x):
    x0 = x.astype(jnp.bfloat16)
    r1 = x - x0.astype(jnp.float32)
    x1 = r1.astype(jnp.bfloat16)
    x2 = (r1 - x1.astype(jnp.float32)).astype(jnp.bfloat16)
    return _dot(sel, x0) + _dot(sel, x1) + _dot(sel, x2)


def _peer_select_kernel(q_ref, sk_ref, sela_ref, selb_ref, e_ref, g_ref):
    tm = q_ref.shape[0]
    sela, selb = sela_ref[...], selb_ref[...]
    pad_row = lax.broadcasted_iota(jnp.int32, (N_STAIR_PAD, tm), 0) >= N_STAIR
    e_rows, g_rows = [], []
    for h in range(PEER_HEADS):
        sub = _dot_nt(sk_ref[...], q_ref[:, h * PEER_QDIM:(h + 1) * PEER_QDIM])
        sv0, si0 = _extract_topk(sub[:PEER_NKEYS], None, PEER_TOPK)
        sv1, si1 = _extract_topk(sub[PEER_NKEYS:], None, PEER_TOPK)
        cand = jnp.where(pad_row, -jnp.inf, _copy_rows(sela, sv0) + _copy_rows(selb, sv1))
        ids = (_dot(sela, si0.astype(jnp.bfloat16)) * float(PEER_NKEYS)
               + _dot(selb, si1.astype(jnp.bfloat16))) * float(HALF_ROWS)
        top_s, top_e = _extract_topk(cand, ids, PEER_TOPK)
        p = jnp.exp(top_s - top_s[0:1])
        e_rows.append(top_e)
        g_rows.append(p / jnp.sum(p, axis=0, keepdims=True))
    e_t = jnp.concatenate(e_rows, axis=0)
    g_t = jnp.concatenate(g_rows, axis=0)
    e_ref[...] = e_t.T.astype(jnp.int32)
    g_ref[...] = g_t.T


def _peer_select(q, sub_keys):
    t, nq = q.shape
    half = PEER_QDIM // 2
    sk = jnp.zeros((2 * PEER_NKEYS, PEER_QDIM), jnp.bfloat16)
    sk = sk.at[:PEER_NKEYS, :half].set(sub_keys[0].astype(jnp.bfloat16))
    sk = sk.at[PEER_NKEYS:, half:].set(sub_keys[1].astype(jnp.bfloat16))
    sel = np.zeros((2, N_STAIR_PAD, PEER_TOPK), np.float32)
    for r, (a, b) in enumerate(_stair_pairs()):
        sel[0, r, a] = 1.0
        sel[1, r, b] = 1.0
    sel = jnp.asarray(sel, jnp.bfloat16)
    tok = lambda width: pl.BlockSpec((PEER_TM, width), lambda i: (i, 0))
    full = lambda a, b: pl.BlockSpec((a, b), lambda i: (0, 0))
    return pl.pallas_call(
        _peer_select_kernel,
        grid=(t // PEER_TM,),
        in_specs=[tok(nq), full(2 * PEER_NKEYS, PEER_QDIM),
                  full(N_STAIR_PAD, PEER_TOPK), full(N_STAIR_PAD, PEER_TOPK)],
        out_specs=[tok(PEER_NSEL), tok(PEER_NSEL)],
        out_shape=[jax.ShapeDtypeStruct((t, PEER_NSEL), jnp.int32),
                   jax.ShapeDtypeStruct((t, PEER_NSEL), jnp.float32)],
        compiler_params=_cparams(("arbitrary",)),
        name="peer_select",
    )(q, sk, sel[0], sel[1])


def _gather_row(tab_ref, row):
    w = tab_ref[pl.ds(pl.multiple_of(row, HALF_ROWS), HALF_ROWS), :]
    hi = pltpu.bitcast(w & jnp.uint32(0xFFFF0000), jnp.float32)
    lo = pltpu.bitcast(w << 16, jnp.float32)
    return hi, lo


def _peer_u_kernel(idx_ref, h_ref, gate_ref, tab_ref, coef_ref, abuf_ref):
    ones = jnp.ones((2 * LANES, LANES), jnp.bfloat16)
    lane = lax.broadcasted_iota(jnp.int32, (PEER_NSEL, PEER_TM), 1)
    rows_per_tok = PEER_NSEL * HALF_ROWS

    def batch(b, a_t):
        def token(tb, c2):
            t = b * PEER_TB + tb
            x8 = h_ref[pl.ds(t, 1), :].reshape(SUBLANES, LANES)
            xh, xl = x8[:HALF_ROWS], x8[HALF_ROWS:]
            off = pl.multiple_of(tb * rows_per_tok, rows_per_tok)
            for j in range(PEER_NSEL):
                hi, lo = _gather_row(tab_ref, idx_ref[0, t, j])
                abuf_ref[pl.ds(off + j * HALF_ROWS, HALF_ROWS), :] = hi * xh + lo * xl
            return c2

        lax.fori_loop(0, PEER_TB, token, 0)
        parts = []
        for tb in range(PEER_TB):
            ab = abuf_ref[pl.ds(tb * rows_per_tok, PEER_NSEL, stride=HALF_ROWS), :]
            for s in range(1, HALF_ROWS):
                ab = ab + abuf_ref[pl.ds(tb * rows_per_tok + s, PEER_NSEL, stride=HALF_ROWS), :]
            parts.append(ab)
        ab_hi, ab_lo = _split_bf16(jnp.concatenate(parts, axis=0))
        r = _dot(jnp.concatenate([ab_hi, ab_lo], axis=1), ones)
        for tb in range(PEER_TB):
            a_t = jnp.where(lane == b * PEER_TB + tb, r[tb * PEER_NSEL:(tb + 1) * PEER_NSEL], a_t)
        return a_t

    a_t = lax.fori_loop(0, PEER_TM // PEER_TB, batch, jnp.zeros((PEER_NSEL, PEER_TM), jnp.float32))
    a = a_t.T
    coef_ref[...] = 0.5 * a * (1.0 + lax.erf(a * (2.0 ** -0.5))) * gate_ref[...]


def _peer_v_kernel(idx_ref, coef_ref, x_ref, g2_ref, tab_ref, out_ref, cb_ref):
    g2 = g2_ref[...]
    ones = jnp.ones((2 * LANES, LANES), jnp.bfloat16)
    eye = (lax.broadcasted_iota(jnp.int32, (PEER_NSEL, LANES), 0)
           == lax.broadcasted_iota(jnp.int32, (PEER_NSEL, LANES), 1))
    nacc = 2

    def batch(b, carry):
        for tb in range(PEER_TB):
            row = coef_ref[pl.ds(b * PEER_TB + tb, 1), :]
            d_hi, d_lo = _split_bf16(jnp.where(eye, jnp.broadcast_to(row, (PEER_NSEL, LANES)), 0.0))
            cb_ref[pl.ds(tb * PEER_NSEL, PEER_NSEL), :] = _dot(jnp.concatenate([d_hi, d_lo], axis=1), ones)

        def token(tb, c2):
            t = b * PEER_TB + tb
            zero = jnp.zeros((HALF_ROWS, LANES), jnp.float32)
            acc_h = [zero] * nacc
            acc_l = [zero] * nacc
            cbase = pl.multiple_of(tb * PEER_NSEL, PEER_NSEL)
            for j in range(PEER_NSEL):
                hi, lo = _gather_row(tab_ref, idx_ref[0, t, j])
                c = jnp.broadcast_to(cb_ref[pl.ds(cbase + j, 1), :], (HALF_ROWS, LANES))
                acc_h[j % nacc] = acc_h[j % nacc] + c * hi
                acc_l[j % nacc] = acc_l[j % nacc] + c * lo
            y8 = jnp.concatenate([acc_h[0] + acc_h[1], acc_l[0] + acc_l[1]], axis=0)
            out_ref[pl.ds(t, 1), :] = x_ref[pl.ds(t, 1), :] + g2 * y8.reshape(1, SUBLANES * LANES)
            return c2

        lax.fori_loop(0, PEER_TB, token, 0)
        return carry

    lax.fori_loop(0, PEER_TM // PEER_TB, batch, 0)


def _sc_peer_u(expert_ids, h2, tab, first, count):
    info = plsc.get_sparse_core_info()
    lanes, workers = info.num_lanes, info.num_cores * info.num_subcores
    half = tab.shape[1]
    per_w = count // workers
    nhalf = PEER_NSEL // 2
    assert per_w * workers == count and per_w >= 2 and nhalf % lanes == 0 and half % lanes == 0
    rows_blk = 4
    kunroll = 2
    mesh = plsc.VectorSubcoreMesh(core_axis_name="c", subcore_axis_name="s")

    @functools.partial(
        pl.kernel, mesh=mesh, name="peer_u_sc",
        compiler_params=pltpu.CompilerParams(needs_layout_passes=False),
        out_type=jax.ShapeDtypeStruct((count, PEER_NSEL), jnp.float32),
        scratch_types=[pltpu.VMEM((2, PEER_NSEL), jnp.int32),
                       pltpu.VMEM((nhalf, half), jnp.uint32), pltpu.VMEM((nhalf, half), jnp.uint32),
                       pltpu.VMEM((2, 2 * half), jnp.float32), pltpu.VMEM((2, PEER_NSEL), jnp.float32),
                       pltpu.SemaphoreType.DMA, pltpu.SemaphoreType.DMA, pltpu.SemaphoreType.DMA,
                       pltpu.SemaphoreType.DMA, pltpu.SemaphoreType.DMA((2,))])
    def run(idx_hbm, h_hbm, tab_hbm, out_hbm, idx_v, rows_a, rows_b, x_v, a_v, sem_a, sem_b, sem_i, sem_x, sem_o):
        wid = lax.axis_index("s") * info.num_cores + lax.axis_index("c")
        base = first + wid * per_w
        lane = lax.iota(jnp.int32, lanes)
        mask_hi = jnp.full((lanes,), 0xFFFF0000, jnp.uint32)

        def gather(p, part, rows, sem):
            return pltpu.make_async_copy(tab_hbm.at[idx_v.at[p, pl.ds(part * nhalf, nhalf)]], rows, sem)

        def load_idx(t, p):
            return pltpu.make_async_copy(idx_hbm.at[t], idx_v.at[p], sem_i)

        def load_x(t, p):
            return pltpu.make_async_copy(h_hbm.at[t], x_v.at[p], sem_x)

        def store_a(t, p):
            return pltpu.make_async_copy(a_v.at[p], out_hbm.at[t - first], sem_o.at[p])

        def compute(rows, p, part):
            def group(g, c2):
                vec = jnp.zeros((lanes,), jnp.float32)
                for rb in range(lanes // rows_blk):
                    def kslice(kk, accs):
                        accs = list(accs)
                        for ku in range(kunroll):
                            off = pl.multiple_of((kk * kunroll + ku) * lanes, lanes)
                            xh = x_v[p, pl.ds(off, lanes)]
                            xl = x_v[p, pl.ds(half + off, lanes)]
                            for r in range(rows_blk):
                                w = rows[g * lanes + rb * rows_blk + r, pl.ds(off, lanes)]
                                hi = plsc.bitcast(w & mask_hi, jnp.float32)
                                lo = plsc.bitcast(w << 16, jnp.float32)
                                accs[r] = accs[r] + hi * xh + lo * xl
                        return tuple(accs)

                    zero = jnp.zeros((lanes,), jnp.float32)
                    accs = lax.fori_loop(0, half // lanes // kunroll, kslice, (zero,) * rows_blk)
                    for r in range(rows_blk):
                        vec = jnp.where(lane == rb * rows_blk + r, jnp.sum(accs[r]), vec)
                a_v[p, pl.ds(pl.multiple_of(part * nhalf + g * lanes, lanes), lanes)] = vec
                return c2

            lax.fori_loop(0, nhalf // lanes, group, 0)

        load_idx(base, 0).start()
        load_x(base, 0).start()
        load_idx(base, 0).wait()
        gather(0, 0, rows_a, sem_a).start()
        gather(0, 1, rows_b, sem_b).start()
        load_x(base, 0).wait()

        def token(i, carry):
            t = base + i
            p = lax.rem(i, 2)
            q = 1 - p
            tn = jnp.minimum(t + 1, base + per_w - 1)
            load_idx(tn, q).start()
            load_x(tn, q).start()

            @pl.when(i >= 2)
            def _():
                store_a(t, p).wait()

            gather(p, 0, rows_a, sem_a).wait()
            compute(rows_a, p, 0)
            load_idx(tn, q).wait()
            gather(q, 0, rows_a, sem_a).start()
            gather(p, 1, rows_b, sem_b).wait()
            compute(rows_b, p, 1)
            gather(q, 1, rows_b, sem_b).start()
            load_x(tn, q).wait()
            store_a(t, p).start()
            return carry

        lax.fori_loop(0, per_w, token, 0)
        gather(0, 0, rows_a, sem_a).wait()
        gather(0, 1, rows_b, sem_b).wait()
        store_a(base, 0).wait()
        store_a(base, 1).wait()

    return run(expert_ids, h2, tab)


def _sc_peer_v(expert_ids, coef, x1, g2, tab, first, count, per_batch):
    info = plsc.get_sparse_core_info()
    lanes, workers = info.num_lanes, info.num_cores * info.num_subcores
    half = tab.shape[1]
    d = 2 * half
    per_w = count // workers
    nhalf = PEER_NSEL // 2
    sblk = 4
    runroll = 4
    assert per_w * workers == count and per_w >= 2 and nhalf % runroll == 0 and half % (lanes * sblk) == 0
    mesh = plsc.VectorSubcoreMesh(core_axis_name="c", subcore_axis_name="s")
    dma = pltpu.SemaphoreType.DMA

    @functools.partial(
        pl.kernel, mesh=mesh, name="peer_v_sc",
        compiler_params=pltpu.CompilerParams(needs_layout_passes=False),
        out_type=jax.ShapeDtypeStruct((count, d), jnp.float32),
        scratch_types=[pltpu.VMEM((2, PEER_NSEL), jnp.int32), pltpu.VMEM((2 * PEER_NSEL,), jnp.float32),
                       pltpu.VMEM((nhalf, half), jnp.uint32), pltpu.VMEM((nhalf, half), jnp.uint32),
                       pltpu.VMEM((2, d), jnp.float32), pltpu.VMEM((2, d), jnp.float32),
                       pltpu.VMEM((d,), jnp.float32), pltpu.VMEM((2, d), jnp.float32),
                       dma, dma, dma, dma, dma, dma, dma((2,))])
    def run(idx_hbm, coef_hbm, x_hbm, g_hbm, tab_hbm, out_hbm,
            idx_v, coef_v, rows_a, rows_b, x_v, g_v, y_v, o_v, sem_a, sem_b, sem_i, sem_c, sem_x, sem_g, sem_o):
        wid = lax.axis_index("s") * info.num_cores + lax.axis_index("c")
        base = first + wid * per_w
        mask_hi = jnp.full((lanes,), 0xFFFF0000, jnp.uint32)

        def gather(p, part, rows, sem):
            return pltpu.make_async_copy(tab_hbm.at[idx_v.at[p, pl.ds(part * nhalf, nhalf)]], rows, sem)

        def loads(t, p):
            cslot = coef_v.at[pl.ds(pl.multiple_of(p * PEER_NSEL, PEER_NSEL), PEER_NSEL)]
            return (pltpu.make_async_copy(idx_hbm.at[t], idx_v.at[p], sem_i),
                    pltpu.make_async_copy(coef_hbm.at[t], cslot, sem_c),
                    pltpu.make_async_copy(x_hbm.at[t], x_v.at[p], sem_x),
                    pltpu.make_async_copy(g_hbm.at[t // per_batch], g_v.at[p], sem_g))

        def store_o(t, p):
            return pltpu.make_async_copy(o_v.at[p], out_hbm.at[t - first], sem_o.at[p])

        def compute(rows, p, part):
            cbase = p * PEER_NSEL + part * nhalf
            for sb in range(half // lanes // sblk):
                offs = [(sb * sblk + s) * lanes for s in range(sblk)]
                if part == 0:
                    init = tuple(jnp.zeros((lanes,), jnp.float32) for _ in range(2 * sblk))
                else:
                    init = tuple([y_v[pl.ds(o, lanes)] for o in offs] + [y_v[pl.ds(half + o, lanes)] for o in offs])

                def rowloop(rr, accs):
                    accs = list(accs)
                    for ru in range(runroll):
                        r = rr * runroll + ru
                        c = plsc.load_gather(coef_v, [jnp.full((lanes,), cbase + r, jnp.int32)])
                        for s in range(sblk):
                            w = rows[r, pl.ds(offs[s], lanes)]
                            hi = plsc.bitcast(w & mask_hi, jnp.float32)
                            lo = plsc.bitcast(w << 16, jnp.float32)
                            accs[s] = accs[s] + c * hi
                            accs[sblk + s] = accs[sblk + s] + c * lo
                    return tuple(accs)

                accs = lax.fori_loop(0, nhalf // runroll, rowloop, init)
                for s in range(sblk):
                    for hl, o in ((0, offs[s]), (1, half + offs[s])):
                        if part == 0:
                            y_v[pl.ds(o, lanes)] = accs[hl * sblk + s]
                        else:
                            o_v[p, pl.ds(o, lanes)] = (x_v[p, pl.ds(o, lanes)]
                                                       + g_v[p, pl.ds(o, lanes)] * accs[hl * sblk + s])

        head = loads(base, 0)
        for c in head:
            c.start()
        head[0].wait()
        gather(0, 0, rows_a, sem_a).start()
        gather(0, 1, rows_b, sem_b).start()
        for c in head[1:]:
            c.wait()

        def token(i, carry):
            t = base + i
            p = lax.rem(i, 2)
            q = 1 - p
            tn = jnp.minimum(t + 1, base + per_w - 1)
            nxt = loads(tn, q)
            for c in nxt:
                c.start()

            @pl.when(i >= 2)
            def _():
                store_o(t, p).wait()

            gather(p, 0, rows_a, sem_a).wait()
            compute(rows_a, p, 0)
            nxt[0].wait()
            gather(q, 0, rows_a, sem_a).start()
            gather(p, 1, rows_b, sem_b).wait()
            compute(rows_b, p, 1)
            gather(q, 1, rows_b, sem_b).start()
            for c in nxt[1:]:
                c.wait()
            store_o(t, p).start()
            return carry

        lax.fori_loop(0, per_w, token, 0)
        gather(0, 0, rows_a, sem_a).wait()
        gather(0, 1, rows_b, sem_b).wait()
        store_o(base, 0).wait()
        store_o(base, 1).wait()

    return run(expert_ids, coef, x1, g2, tab)


def _gelu_gate_kernel(a_ref, g_ref, o_ref):
    a = a_ref[...]
    o_ref[...] = 0.5 * a * (1.0 + lax.erf(a * (2.0 ** -0.5))) * g_ref[...]


def _gelu_gate(a, gates, first):
    count = a.shape[0]
    return pl.pallas_call(
        _gelu_gate_kernel,
        grid=(count // PEER_TM,),
        in_specs=[pl.BlockSpec((PEER_TM, PEER_NSEL), lambda i: (i, 0)),
                  pl.BlockSpec((PEER_TM, PEER_NSEL), lambda i: (first // PEER_TM + i, 0))],
        out_specs=pl.BlockSpec((PEER_TM, PEER_NSEL), lambda i: (i, 0)),
        out_shape=jax.ShapeDtypeStruct((count, PEER_NSEL), jnp.float32),
        compiler_params=_cparams(("arbitrary",)),
        name="gelu_gate",
    )(a, gates)


def _pack_table(tab):
    n, d = tab.shape
    bits = lax.bitcast_convert_type(tab.astype(jnp.bfloat16), jnp.uint16).astype(jnp.uint32)
    packed = (bits[:, : d // 2] << 16) | bits[:, d // 2:]
    return packed.reshape(n * HALF_ROWS, LANES)


def _peer_specs(t, d):
    smem_blk = pl.BlockSpec((1, PEER_TM, PEER_NSEL), lambda i: (i, 0, 0), memory_space=pltpu.SMEM)
    tok_blk = pl.BlockSpec((PEER_TM, d), lambda i: (i, 0))
    sel_blk = pl.BlockSpec((PEER_TM, PEER_NSEL), lambda i: (i, 0))
    return smem_blk, tok_blk, sel_blk, pl.BlockSpec(memory_space=pltpu.VMEM)


def _after(x, prev):
    return x if prev is None else lax.optimization_barrier((x, prev))[0]


def _peer_u_phase(h2, experts, gates, u_packed, n_sc, sc_prev):
    t, d = h2.shape
    assert d == SUBLANES * LANES and d // 2 == HALF_ROWS * LANES and PEER_TM % PEER_TB == 0
    smem_blk, tok_blk, sel_blk, tab_spec = _peer_specs(t, d)
    parts = []
    n_tc = t - n_sc
    if n_tc:
        parts.append(pl.pallas_call(
            _peer_u_kernel,
            grid=(n_tc // PEER_TM,),
            in_specs=[smem_blk, tok_blk, sel_blk, tab_spec],
            out_specs=sel_blk,
            out_shape=jax.ShapeDtypeStruct((n_tc, PEER_NSEL), jnp.float32),
            scratch_shapes=[pltpu.VMEM((PEER_TB * PEER_NSEL * HALF_ROWS, LANES), jnp.float32)],
            compiler_params=_cparams(("arbitrary",)),
            name="peer_u",
        )(experts.reshape(t // PEER_TM, PEER_TM, PEER_NSEL), h2, gates, u_packed))
    a_sc = None
    if n_sc:
        a_sc = _sc_peer_u(experts // HALF_ROWS, h2, u_packed.reshape(-1, d // 2), n_tc, n_sc)
    return (parts[0] if parts else None), a_sc


def _finish_coef(coef_tc, a_sc, gates):
    parts = [] if coef_tc is None else [coef_tc]
    if a_sc is not None:
        parts.append(_gelu_gate(a_sc, gates, gates.shape[0] - a_sc.shape[0]))
    return parts[0] if len(parts) == 1 else jnp.concatenate(parts, axis=0)


def _peer_v_phase(coef, x1, g2, experts, v_packed, batch, n_sc, sc_prev):
    t, d = x1.shape
    per_b = t // PEER_TM // batch
    smem_blk, tok_blk, sel_blk, tab_spec = _peer_specs(t, d)
    parts = []
    n_tc = t - n_sc
    if n_tc:
        parts.append(pl.pallas_call(
            _peer_v_kernel,
            grid=(n_tc // PEER_TM,),
            in_specs=[smem_blk, sel_blk, tok_blk,
                      pl.BlockSpec((None, 1, d), lambda i: (i // per_b, 0, 0)),
                      tab_spec],
            out_specs=tok_blk,
            out_shape=jax.ShapeDtypeStruct((n_tc, d), jnp.float32),
            scratch_shapes=[pltpu.VMEM((PEER_TB * PEER_NSEL, LANES), jnp.float32)],
            compiler_params=_cparams(("arbitrary",)),
            name="peer_v",
        )(experts.reshape(t // PEER_TM, PEER_TM, PEER_NSEL), coef, x1, g2.reshape(batch, 1, d), v_packed))
    if n_sc:
        sc_prev = _sc_peer_v(experts // HALF_ROWS, coef, x1, g2, v_packed.reshape(-1, d // 2),
                             n_tc, n_sc, t // batch)
        parts.append(sc_prev)
    return parts, sc_prev


def kernel(x, c, ada_w, ada_b, norm1_g, norm2_g, w_in, a_qk_gain, a_lambda, a_sub_gain, b_q_gain, b_kv_gain, b_w_uv, w_out, peer_wq, peer_subkeys, peer_u, peer_v):
    b, s, d = x.shape
    t = b * s
    x2 = x.reshape(t, d)
    nchunk = len(PEER_PLAN)
    bc = b // nchunk
    tc = bc * s
    for l in range(ada_w.shape[0]):
        mod = _adaln(c, ada_w[l], ada_b[l])
        u_packed, v_packed = _pack_table(peer_u[l]), _pack_table(peer_v[l])
        lam_init = 0.8 - 0.6 * math.exp(-0.3 * l)
        outs = [None] * nchunk
        pending = {}
        sc_prev = None
        prev_experts = None

        def issue_v(k, anchor):
            nonlocal sc_prev
            coef_tc, a_sc, gates, args = pending.pop(k)
            if a_sc is not None:
                anchor, a_sc = lax.optimization_barrier((anchor, a_sc))
                sc_prev = a_sc
            outs[k], sc_prev = _peer_v_phase(_finish_coef(coef_tc, a_sc, gates), *args, sc_prev)
            return anchor

        def issue_planned(at, anchor):
            for k in [k for k in pending if PEER_PLAN[k][2] == at]:
                anchor = issue_v(k, anchor)
            return anchor

        for ck, (n_sc_u, n_sc_v, _) in enumerate(PEER_PLAN):
            xc = _after(x2[ck * tc:(ck + 1) * tc], prev_experts)
            sh1, sc1, g1, sh2, sc2, g2 = [mod[ck * bc:(ck + 1) * bc, i * d:(i + 1) * d] for i in range(6)]
            qa, ka, va, qb, kvb, qi, ki, wi = _inproj(xc, sh1, sc1, norm1_g[l], w_in[l], a_qk_gain[l],
                                                      b_q_gain[l], b_kv_gain[l], bc)
            oa = _diffattn(qa, ka, va, a_lambda[l], a_sub_gain[l], bc, lam_init)
            oa = issue_planned((ck, "diffattn"), oa)
            ob = _dsa(qi, wi, ki, qb, kvb, b_w_uv[l], bc)
            x1, h2, q = _outproj(xc, oa, ob, w_out[l], g1, sh2, sc2, norm2_g[l], peer_wq[l], bc)
            q = issue_planned((ck, "outproj"), q)
            experts, gates = _peer_select(q, peer_subkeys[l])
            prev_experts = experts
            coef_tc, a_sc = _peer_u_phase(h2, experts, gates, u_packed, n_sc_u, sc_prev)
            sc_prev = sc_prev if a_sc is None else a_sc
            pending[ck] = (coef_tc, a_sc, gates, (x1, g2, experts, v_packed, bc, n_sc_v))
        anchor = pending[nchunk - 1][0]
        for k in sorted(pending):
            anchor = issue_v(k, anchor)
        x2 = jnp.concatenate([p for o in outs for p in o], axis=0)
    return x2.reshape(b, s, d)
```

```python
import functools
import math

import jax
import jax.numpy as jnp
import numpy as np
from jax import lax
from jax.experimental import pallas as pl
from jax.experimental.pallas import tpu as pltpu
from jax.experimental.pallas import tpu_sc as plsc

CHUNK = 64
A_HEADS, A_QK_DIM, A_V_DIM = 4, 64, 128
B_HEADS, B_LAT, B_V_DIM = 8, 128, 64
IDX_HEADS, IDX_DIM = 4, 64
DSA_TOPK_MAX = 256
A_Q = A_HEADS * 2 * A_QK_DIM
A_K = A_Q
A_V = A_HEADS * A_V_DIM
B_Q = B_HEADS * B_LAT
B_KV = B_LAT
I_Q = IDX_HEADS * IDX_DIM
I_K = IDX_DIM
I_W = IDX_HEADS
OFF_AQ, OFF_AK, OFF_AV = 0, A_Q, A_Q + A_K
OFF_BQ = OFF_AV + A_V
OFF_KV = OFF_BQ + B_Q
OFF_IQ = OFF_KV + B_KV
OFF_IK = OFF_IQ + I_Q
OFF_IW = OFF_IK + I_K
IN_COLS = OFF_IW + I_W
PEER_HEADS, PEER_NKEYS, PEER_QDIM, PEER_TOPK = 8, 128, 128, 16
EPS = 1e-6
NEG = -1e30
INT_MIN = -(2 ** 31)

SUBLANES = 8
LANES = 128
VMEM_LIMIT = 56 * 1024 * 1024

PROJ_TM = 256
A_TQ = 512
B_TQ = 256
PEER_TM = 128
PEER_TB = 8
PEER_NSEL = PEER_HEADS * PEER_TOPK
PEER_PLAN = ((8192, 8192, (1, "outproj")), (8192, 8192, (3, "diffattn")), (8192, 5376, None), (0, 0, None))
HALF_ROWS = 4


def _split_bf16(x):
    hi = x.astype(jnp.bfloat16)
    lo = (x - hi.astype(jnp.float32)).astype(jnp.bfloat16)
    return hi, lo


def _dot(a, b):
    return jnp.dot(a, b, preferred_element_type=jnp.float32)


def _dot_nt(a, b):
    return lax.dot_general(a, b, (((1,), (1,)), ((), ())), preferred_element_type=jnp.float32)


def _cparams(sem):
    return pltpu.CompilerParams(dimension_semantics=sem, vmem_limit_bytes=VMEM_LIMIT)


def _adaln_kernel(c_ref, w_ref, b_ref, o_ref):
    cf = c_ref[...]
    a = cf * (1.0 / (1.0 + jnp.exp(-cf)))
    a_hi, a_lo = _split_bf16(a)
    w_hi, w_lo = _split_bf16(w_ref[...])
    o_ref[...] = _dot(a_hi, w_hi) + _dot(a_hi, w_lo) + _dot(a_lo, w_hi) + b_ref[...]


def _adaln(c, w, b):
    bsz, d = c.shape
    n = w.shape[1]
    tn = 1024
    return pl.pallas_call(
        _adaln_kernel,
        grid=(n // tn,),
        in_specs=[pl.BlockSpec((bsz, d), lambda j: (0, 0)),
                  pl.BlockSpec((d, tn), lambda j: (0, j)),
                  pl.BlockSpec((1, tn), lambda j: (0, j))],
        out_specs=pl.BlockSpec((bsz, tn), lambda j: (0, j)),
        out_shape=jax.ShapeDtypeStruct((bsz, n), jnp.float32),
        compiler_params=_cparams(("arbitrary",)),
        name="adaln",
    )(c, w, b.reshape(1, n))


def _rms_rows(x, g):
    return x * lax.rsqrt(jnp.mean(x * x, axis=-1, keepdims=True) + EPS) * g


def _group_norm_block(p, gmat, gain, n):
    hi, lo = _split_bf16(p * p)
    ss = _dot(hi, gmat) + _dot(lo, gmat)
    return p * lax.rsqrt(ss * (1.0 / n) + EPS) * gain


def _inproj_kernel(x_ref, sh_ref, sc_ref, g_ref, w_ref, gq_ref, gk_ref, gbq_ref, gkv_ref,
                   qa_ref, ka_ref, va_ref, qb_ref, kv_ref, qi_ref, ki_ref, wi_ref):
    h = _rms_rows(x_ref[...], g_ref[...]) * (1.0 + sc_ref[0]) + sh_ref[0]
    hb = h.astype(jnp.bfloat16)
    row = lax.broadcasted_iota(jnp.int32, (LANES, LANES), 0)
    col = lax.broadcasted_iota(jnp.int32, (LANES, LANES), 1)
    g64 = jnp.where((row // A_QK_DIM) == (col // A_QK_DIM), 1.0, 0.0).astype(jnp.bfloat16)
    g128 = jnp.ones((LANES, LANES), jnp.bfloat16)

    def proj(off, width):
        return _dot(hb, w_ref[:, off:off + width])

    for blk in range(A_Q // LANES):
        sl = slice(blk * LANES, (blk + 1) * LANES)
        p = proj(OFF_AQ + blk * LANES, LANES)
        qa_ref[:, sl] = (_group_norm_block(p, g64, gq_ref[...], A_QK_DIM) * (A_QK_DIM ** -0.5)).astype(qa_ref.dtype)
        p = proj(OFF_AK + blk * LANES, LANES)
        ka_ref[:, sl] = _group_norm_block(p, g64, gk_ref[...], A_QK_DIM).astype(ka_ref.dtype)
    va_ref[...] = proj(OFF_AV, A_V).astype(va_ref.dtype)
    for blk in range(B_HEADS):
        sl = slice(blk * LANES, (blk + 1) * LANES)
        p = proj(OFF_BQ + blk * LANES, LANES)
        qb_ref[:, sl] = _group_norm_block(p, g128, gbq_ref[...], B_LAT).astype(qb_ref.dtype)
    p = proj(OFF_KV, B_KV)
    kv_ref[...] = _group_norm_block(p, g128, gkv_ref[...], B_LAT).astype(kv_ref.dtype)
    qi_ref[...] = proj(OFF_IQ, I_Q).astype(qi_ref.dtype)
    tail = proj(OFF_IK, 2 * LANES)
    ki_ref[...] = tail[:, :LANES].astype(ki_ref.dtype)
    wi_ref[...] = tail[:, I_K:I_K + LANES]


def _inproj(x2, sh1, sc1, g1n, w_in, a_qk_gain, b_q_gain, b_kv_gain, batch):
    t, d = x2.shape
    per_b = (t // batch) // PROJ_TM
    wpad = OFF_IK + 2 * LANES
    w = jnp.zeros((d, wpad), jnp.bfloat16).at[:, :IN_COLS].set(w_in.astype(jnp.bfloat16))
    gq = jnp.tile(a_qk_gain[0], 2).reshape(1, LANES)
    gk = jnp.tile(a_qk_gain[1], 2).reshape(1, LANES)
    tok = lambda width: pl.BlockSpec((PROJ_TM, width), lambda i: (i, 0))
    vec = lambda width: pl.BlockSpec((1, width), lambda i: (0, 0))
    per_batch = pl.BlockSpec((1, 1, d), lambda i: (i // per_b, 0, 0))
    bf = jnp.bfloat16
    outs = [(A_Q, bf), (A_K, bf), (A_V, bf), (B_Q, bf), (B_KV, bf), (I_Q, bf), (LANES, bf), (LANES, jnp.float32)]
    return pl.pallas_call(
        _inproj_kernel,
        grid=(t // PROJ_TM,),
        in_specs=[tok(d), per_batch, per_batch, vec(d),
                  pl.BlockSpec((d, wpad), lambda i: (0, 0)),
                  vec(LANES), vec(LANES), vec(LANES), vec(LANES)],
        out_specs=[tok(wd) for wd, _ in outs],
        out_shape=[jax.ShapeDtypeStruct((t, wd), dt) for wd, dt in outs],
        compiler_params=_cparams(("arbitrary",)),
        name="inproj",
    )(x2, sh1.reshape(batch, 1, d), sc1.reshape(batch, 1, d), g1n.reshape(1, d), w,
      gq, gk, b_q_gain.reshape(1, LANES), b_kv_gain.reshape(1, LANES))


def _positions(q0, tq, s_len):
    qpos = q0 + lax.broadcasted_iota(jnp.int32, (tq, s_len), 0)
    kpos = lax.broadcasted_iota(jnp.int32, (tq, s_len), 1)
    cend = (qpos // CHUNK + 1) * CHUNK
    return qpos, kpos, cend


def _softmax_rows(s):
    m = jnp.max(s, axis=-1, keepdims=True)
    p = jnp.exp(s - m)
    return p, jnp.sum(p, axis=-1, keepdims=True)


def _diffattn_kernel(q_ref, k_ref, v_ref, lam_ref, gain_ref, o_ref, *, lam_init, q_base):
    tq, s_len = q_ref.shape[0], k_ref.shape[0]
    q0 = q_base + pl.program_id(1) * tq
    qpos, kpos, cend = _positions(q0, tq, s_len)
    dist = jnp.abs(qpos - kpos).astype(jnp.float32)
    negmask = jnp.where(kpos < cend, 0.0, NEG)
    lf = lam_ref[...]
    lam = (jnp.exp(jnp.sum(lf[0:1] * lf[1:2], axis=-1, keepdims=True))
           - jnp.exp(jnp.sum(lf[2:3] * lf[3:4], axis=-1, keepdims=True)) + lam_init)
    for h in range(A_HEADS):
        slope = 2.0 ** (-8.0 * (h + 1) / A_HEADS)
        bias = negmask - slope * dist
        v = v_ref[:, h * A_V_DIM:(h + 1) * A_V_DIM]
        outs = []
        for m in range(2):
            c0 = (h * 2 + m) * A_QK_DIM
            s = _dot_nt(q_ref[:, c0:c0 + A_QK_DIM], k_ref[:, c0:c0 + A_QK_DIM]) + bias
            p, l = _softmax_rows(s)
            outs.append(_dot(p.astype(jnp.bfloat16), v) / l)
        o = outs[0] - lam * outs[1]
        o = _rms_rows(o, gain_ref[...]) * (1.0 - lam_init)
        o_ref[:, h * A_V_DIM:(h + 1) * A_V_DIM] = o.astype(o_ref.dtype)


def _causal_groups(s_len, tq):
    span = min(tq, s_len)
    tiles = span // tq
    return [(g * tiles, tiles, (g + 1) * span) for g in range(s_len // span)]


def _qblk(tq, first_tile, width):
    return pl.BlockSpec((None, tq, width), lambda b, i: (b, first_tile + i, 0))


def _kblk(klen, width):
    return pl.BlockSpec((None, klen, width), lambda b, i: (b, 0, 0))


def _oblk(tq, width):
    return pl.BlockSpec((None, tq, width), lambda b, i: (b, i, 0))


def _diffattn(qa, ka, va, a_lambda, a_sub_gain, batch, lam_init):
    t = qa.shape[0]
    s_len = t // batch
    qa3, ka3, va3 = (a.reshape(batch, s_len, a.shape[1]) for a in (qa, ka, va))
    outs = []
    for first, tiles, klen in _causal_groups(s_len, A_TQ):
        outs.append(pl.pallas_call(
            functools.partial(_diffattn_kernel, lam_init=lam_init, q_base=first * A_TQ),
            grid=(batch, tiles),
            in_specs=[_qblk(A_TQ, first, A_Q), _kblk(klen, A_K), _kblk(klen, A_V),
                      pl.BlockSpec((4, A_QK_DIM), lambda b, i: (0, 0)),
                      pl.BlockSpec((1, A_V_DIM), lambda b, i: (0, 0))],
            out_specs=_oblk(A_TQ, A_V),
            out_shape=jax.ShapeDtypeStruct((batch, tiles * A_TQ, A_V), jnp.bfloat16),
            compiler_params=_cparams(("arbitrary", "arbitrary")),
            name="diffattn",
        )(qa3, ka3, va3, a_lambda, a_sub_gain.reshape(1, A_V_DIM)))
    return jnp.concatenate(outs, axis=1).reshape(t, A_V)


def _count(mask):
    return jnp.sum(jnp.where(mask, 1.0, 0.0), axis=-1, keepdims=True)


def _topk_mask(score, allowed, kpos, topk, s_len):
    bits = pltpu.bitcast(score + 0.0, jnp.int32)
    key = jnp.where(bits < 0, bits ^ jnp.int32(0x7FFFFFFF), bits)
    key = jnp.where(allowed, key, jnp.int32(INT_MIN))
    kf = float(topk)
    thr = jnp.where(_count(key >= 0) >= kf, jnp.int32(0), jnp.int32(INT_MIN))

    def value_bit(i, thr):
        cand = thr | (jnp.int32(1) << (30 - i))
        return jnp.where(_count(key >= cand) >= kf, cand, thr)

    thr = lax.fori_loop(0, 31, value_bit, thr)
    above = key > thr
    tie = key == thr
    need = kf - _count(above)
    nbits = (s_len - 1).bit_length()

    def pos_bit(i, pos):
        cand = pos | (jnp.int32(1) << (nbits - 1 - i))
        return jnp.where(_count(tie & (kpos < cand)) < need, cand, pos)

    pos = lax.fori_loop(0, nbits, pos_bit, jnp.zeros_like(thr))
    return allowed & (above | (tie & (kpos <= pos)))


def _dsa_kernel(qi_ref, wi_ref, ki_ref, qb_ref, kv_ref, wuv_ref, o_ref, *, topk, q_base):
    tq, s_len = qb_ref.shape[0], kv_ref.shape[0]
    q0 = q_base + pl.program_id(1) * tq
    qpos, kpos, cend = _positions(q0, tq, s_len)
    allowed = kpos < cend
    ki = ki_ref[...]
    wi = wi_ref[...]
    zpad = jnp.zeros((tq, LANES - IDX_DIM), jnp.bfloat16)
    score = jnp.zeros((tq, s_len), jnp.float32)
    for h in range(IDX_HEADS):
        qh = jnp.concatenate([qi_ref[:, h * IDX_DIM:(h + 1) * IDX_DIM], zpad], axis=-1)
        score = score + jnp.maximum(_dot_nt(qh, ki), 0.0) * wi[:, h:h + 1]
    score = score * ((IDX_DIM ** -0.5) * (IDX_HEADS ** -0.5))
    sel = _topk_mask(score, allowed, kpos, topk, s_len)

    dist = jnp.abs(qpos - kpos).astype(jnp.float32)
    negmask = jnp.where(sel, 0.0, NEG)
    kv = kv_ref[...]
    for h in range(B_HEADS):
        slope = 2.0 ** (-8.0 * (h + 1) / B_HEADS)
        s = _dot_nt(qb_ref[:, h * B_LAT:(h + 1) * B_LAT], kv) * (B_LAT ** -0.5) + (negmask - slope * dist)
        p, l = _softmax_rows(s)
        o_lat = _dot(p.astype(jnp.bfloat16), kv) / l
        o = _dot(o_lat.astype(jnp.bfloat16), wuv_ref[h])
        o_ref[:, h * B_V_DIM:(h + 1) * B_V_DIM] = o.astype(o_ref.dtype)


def _dsa(qi, wi, ki, qb, kvb, w_uv, batch):
    t = qb.shape[0]
    s_len = t // batch
    topk = min(DSA_TOPK_MAX, s_len // 4)
    width = B_HEADS * B_V_DIM
    qi3, wi3, ki3, qb3, kv3 = (a.reshape(batch, s_len, a.shape[1]) for a in (qi, wi, ki, qb, kvb))
    wuv = w_uv.astype(jnp.bfloat16)
    outs = []
    for first, tiles, klen in _causal_groups(s_len, B_TQ):
        outs.append(pl.pallas_call(
            functools.partial(_dsa_kernel, topk=topk, q_base=first * B_TQ),
            grid=(batch, tiles),
            in_specs=[_qblk(B_TQ, first, I_Q), _qblk(B_TQ, first, LANES), _kblk(klen, LANES),
                      _qblk(B_TQ, first, B_Q), _kblk(klen, B_KV),
                      pl.BlockSpec((B_HEADS, B_LAT, B_V_DIM), lambda b, i: (0, 0, 0))],
            out_specs=_oblk(B_TQ, width),
            out_shape=jax.ShapeDtypeStruct((batch, tiles * B_TQ, width), jnp.bfloat16),
            compiler_params=_cparams(("arbitrary", "arbitrary")),
            name="dsa",
        )(qi3, wi3, ki3, qb3, kv3, wuv))
    return jnp.concatenate(outs, axis=1).reshape(t, width)


def _outproj_kernel(x_ref, oa_ref, ob_ref, wo_ref, g1_ref, sh_ref, sc_ref, gn_ref, wq_ref,
                    x1_ref, h2_ref, q_ref):
    na = oa_ref.shape[1]
    y = _dot(oa_ref[...], wo_ref[:na, :]) + _dot(ob_ref[...], wo_ref[na:, :])
    x1 = x_ref[...] + g1_ref[0] * y
    x1_ref[...] = x1
    h2 = _rms_rows(x1, gn_ref[...]) * (1.0 + sc_ref[0]) + sh_ref[0]
    h2_ref[...] = h2
    q_ref[...] = _dot(h2.astype(jnp.bfloat16), wq_ref[...]).astype(q_ref.dtype)


def _outproj(x2, oa, ob, w_out, g1, sh2, sc2, g2n, peer_wq, batch):
    t, d = x2.shape
    per_b = (t // batch) // PROJ_TM
    nq = peer_wq.shape[1]
    tok = lambda width: pl.BlockSpec((PROJ_TM, width), lambda i: (i, 0))
    per_batch = pl.BlockSpec((1, 1, d), lambda i: (i // per_b, 0, 0))
    full = lambda a, b: pl.BlockSpec((a, b), lambda i: (0, 0))
    return pl.pallas_call(
        _outproj_kernel,
        grid=(t // PROJ_TM,),
        in_specs=[tok(d), tok(oa.shape[1]), tok(ob.shape[1]), full(w_out.shape[0], d),
                  per_batch, per_batch, per_batch, full(1, d), full(d, nq)],
        out_specs=[tok(d), tok(d), tok(nq)],
        out_shape=[jax.ShapeDtypeStruct((t, d), jnp.float32),
                   jax.ShapeDtypeStruct((t, d), jnp.float32),
                   jax.ShapeDtypeStruct((t, nq), jnp.bfloat16)],
        compiler_params=_cparams(("arbitrary",)),
        name="outproj",
    )(x2, oa, ob, w_out.astype(jnp.bfloat16), g1.reshape(batch, 1, d), sh2.reshape(batch, 1, d),
      sc2.reshape(batch, 1, d), g2n.reshape(1, d), peer_wq.astype(jnp.bfloat16))


def _extract_topk(vals, payload, k):
    n, tm = vals.shape
    rows = lax.broadcasted_iota(jnp.int32, (n, tm), 0).astype(jnp.float32)
    top_v, top_p = [], []
    for _ in range(k):
        m = jnp.max(vals, axis=0, keepdims=True)
        pos = jnp.min(jnp.where(vals == m, rows, float(n)), axis=0, keepdims=True)
        hit = rows == pos
        top_v.append(m)
        top_p.append(pos if payload is None else jnp.max(jnp.where(hit, payload, -1.0), axis=0, keepdims=True))
        vals = jnp.where(hit, -jnp.inf, vals)
    return jnp.concatenate(top_v, axis=0), jnp.concatenate(top_p, axis=0)


def _stair_pairs():
    return [(a, b) for a in range(PEER_TOPK) for b in range(PEER_TOPK) if (a + 1) * (b + 1) <= PEER_TOPK]


N_STAIR = len(_stair_pairs())
N_STAIR_PAD = -(-N_STAIR // SUBLANES) * SUBLANES


def _copy_rows(sel, x):
    x0 = x.astype(jnp.bfloat16)
    r1 = x - x0.astype(jnp.float32)
    x1 = r1.astype(jnp.bfloat16)
    x2 = (r1 - x1.astype(jnp.float32)).astype(jnp.bfloat16)
    return _dot(sel, x0) + _dot(sel, x1) + _dot(sel, x2)


def _peer_select_kernel(q_ref, sk_ref, sela_ref, selb_ref, e_ref, g_ref):
    tm = q_ref.shape[0]
    sela, selb = sela_ref[...], selb_ref[...]
    pad_row = lax.broadcasted_iota(jnp.int32, (N_STAIR_PAD, tm), 0) >= N_STAIR
    e_rows, g_rows = [], []
    for h in range(PEER_HEADS):
        sub = _dot_nt(sk_ref[...], q_ref[:, h * PEER_QDIM:(h + 1) * PEER_QDIM])
        sv0, si0 = _extract_topk(sub[:PEER_NKEYS], None, PEER_TOPK)
        sv1, si1 = _extract_topk(sub[PEER_NKEYS:], None, PEER_TOPK)
        cand = jnp.where(pad_row, -jnp.inf, _copy_rows(sela, sv0) + _copy_rows(selb, sv1))
        ids = (_dot(sela, si0.astype(jnp.bfloat16)) * float(PEER_NKEYS)
               + _dot(selb, si1.astype(jnp.bfloat16))) * float(HALF_ROWS)
        top_s, top_e = _extract_topk(cand, ids, PEER_TOPK)
        p = jnp.exp(top_s - top_s[0:1])
        e_rows.append(top_e)
        g_rows.append(p / jnp.sum(p, axis=0, keepdims=True))
    e_t = jnp.concatenate(e_rows, axis=0)
    g_t = jnp.concatenate(g_rows, axis=0)
    e_ref[...] = e_t.T.astype(jnp.int32)
    g_ref[...] = g_t.T


def _peer_select(q, sub_keys):
    t, nq = q.shape
    half = PEER_QDIM // 2
    sk = jnp.zeros((2 * PEER_NKEYS, PEER_QDIM), jnp.bfloat16)
    sk = sk.at[:PEER_NKEYS, :half].set(sub_keys[0].astype(jnp.bfloat16))
    sk = sk.at[PEER_NKEYS:, half:].set(sub_keys[1].astype(jnp.bfloat16))
    sel = np.zeros((2, N_STAIR_PAD, PEER_TOPK), np.float32)
    for r, (a, b) in enumerate(_stair_pairs()):
        sel[0, r, a] = 1.0
        sel[1, r, b] = 1.0
    sel = jnp.asarray(sel, jnp.bfloat16)
    tok = lambda width: pl.BlockSpec((PEER_TM, width), lambda i: (i, 0))
    full = lambda a, b: pl.BlockSpec((a, b), lambda i: (0, 0))
    return pl.pallas_call(
        _peer_select_kernel,
        grid=(t // PEER_TM,),
        in_specs=[tok(nq), full(2 * PEER_NKEYS, PEER_QDIM),
                  full(N_STAIR_PAD, PEER_TOPK), full(N_STAIR_PAD, PEER_TOPK)],
        out_specs=[tok(PEER_NSEL), tok(PEER_NSEL)],
        out_shape=[jax.ShapeDtypeStruct((t, PEER_NSEL), jnp.int32),
                   jax.ShapeDtypeStruct((t, PEER_NSEL), jnp.float32)],
        compiler_params=_cparams(("arbitrary",)),
        name="peer_select",
    )(q, sk, sel[0], sel[1])


def _gather_row(tab_ref, row):
    w = tab_ref[pl.ds(pl.multiple_of(row, HALF_ROWS), HALF_ROWS), :]
    hi = pltpu.bitcast(w & jnp.uint32(0xFFFF0000), jnp.float32)
    lo = pltpu.bitcast(w << 16, jnp.float32)
    return hi, lo


def _peer_u_kernel(idx_ref, h_ref, gate_ref, tab_ref, coef_ref, abuf_ref):
    ones = jnp.ones((2 * LANES, LANES), jnp.bfloat16)
    lane = lax.broadcasted_iota(jnp.int32, (PEER_NSEL, PEER_TM), 1)
    rows_per_tok = PEER_NSEL * HALF_ROWS

    def batch(b, a_t):
        def token(tb, c2):
            t = b * PEER_TB + tb
            x8 = h_ref[pl.ds(t, 1), :].reshape(SUBLANES, LANES)
            xh, xl = x8[:HALF_ROWS], x8[HALF_ROWS:]
            off = pl.multiple_of(tb * rows_per_tok, rows_per_tok)
            for j in range(PEER_NSEL):
                hi, lo = _gather_row(tab_ref, idx_ref[0, t, j])
                abuf_ref[pl.ds(off + j * HALF_ROWS, HALF_ROWS), :] = hi * xh + lo * xl
            return c2

        lax.fori_loop(0, PEER_TB, token, 0)
        parts = []
        for tb in range(PEER_TB):
            ab = abuf_ref[pl.ds(tb * rows_per_tok, PEER_NSEL, stride=HALF_ROWS), :]
            for s in range(1, HALF_ROWS):
                ab = ab + abuf_ref[pl.ds(tb * rows_per_tok + s, PEER_NSEL, stride=HALF_ROWS), :]
            parts.append(ab)
        ab_hi, ab_lo = _split_bf16(jnp.concatenate(parts, axis=0))
        r = _dot(jnp.concatenate([ab_hi, ab_lo], axis=1), ones)
        for tb in range(PEER_TB):
            a_t = jnp.where(lane == b * PEER_TB + tb, r[tb * PEER_NSEL:(tb + 1) * PEER_NSEL], a_t)
        return a_t

    a_t = lax.fori_loop(0, PEER_TM // PEER_TB, batch, jnp.zeros((PEER_NSEL, PEER_TM), jnp.float32))
    a = a_t.T
    coef_ref[...] = 0.5 * a * (1.0 + lax.erf(a * (2.0 ** -0.5))) * gate_ref[...]


def _peer_v_kernel(idx_ref, coef_ref, x_ref, g2_ref, tab_ref, out_ref, cb_ref):
    g2 = g2_ref[...]
    ones = jnp.ones((2 * LANES, LANES), jnp.bfloat16)
    eye = (lax.broadcasted_iota(jnp.int32, (PEER_NSEL, LANES), 0)
           == lax.broadcasted_iota(jnp.int32, (PEER_NSEL, LANES), 1))
    nacc = 2

    def batch(b, carry):
        for tb in range(PEER_TB):
            row = coef_ref[pl.ds(b * PEER_TB + tb, 1), :]
            d_hi, d_lo = _split_bf16(jnp.where(eye, jnp.broadcast_to(row, (PEER_NSEL, LANES)), 0.0))
            cb_ref[pl.ds(tb * PEER_NSEL, PEER_NSEL), :] = _dot(jnp.concatenate([d_hi, d_lo], axis=1), ones)

        def token(tb, c2):
            t = b * PEER_TB + tb
            zero = jnp.zeros((HALF_ROWS, LANES), jnp.float32)
            acc_h = [zero] * nacc
            acc_l = [zero] * nacc
            cbase = pl.multiple_of(tb * PEER_NSEL, PEER_NSEL)
            for j in range(PEER_NSEL):
                hi, lo = _gather_row(tab_ref, idx_ref[0, t, j])
                c = jnp.broadcast_to(cb_ref[pl.ds(cbase + j, 1), :], (HALF_ROWS, LANES))
                acc_h[j % nacc] = acc_h[j % nacc] + c * hi
                acc_l[j % nacc] = acc_l[j % nacc] + c * lo
            y8 = jnp.concatenate([acc_h[0] + acc_h[1], acc_l[0] + acc_l[1]], axis=0)
            out_ref[pl.ds(t, 1), :] = x_ref[pl.ds(t, 1), :] + g2 * y8.reshape(1, SUBLANES * LANES)
            return c2

        lax.fori_loop(0, PEER_TB, token, 0)
        return carry

    lax.fori_loop(0, PEER_TM // PEER_TB, batch, 0)


def _sc_peer_u(expert_ids, h2, tab, first, count):
    info = plsc.get_sparse_core_info()
    lanes, workers = info.num_lanes, info.num_cores * info.num_subcores
    half = tab.shape[1]
    per_w = count // workers
    nhalf = PEER_NSEL // 2
    assert per_w * workers == count and per_w >= 2 and nhalf % lanes == 0 and half % lanes == 0
    rows_blk = 4
    kunroll = 2
    mesh = plsc.VectorSubcoreMesh(core_axis_name="c", subcore_axis_name="s")

    @functools.partial(
        pl.kernel, mesh=mesh, name="peer_u_sc",
        compiler_params=pltpu.CompilerParams(needs_layout_passes=False),
        out_type=jax.ShapeDtypeStruct((count, PEER_NSEL), jnp.float32),
        scratch_types=[pltpu.VMEM((2, PEER_NSEL), jnp.int32),
                       pltpu.VMEM((nhalf, half), jnp.uint32), pltpu.VMEM((nhalf, half), jnp.uint32),
                       pltpu.VMEM((2, 2 * half), jnp.float32), pltpu.VMEM((2, PEER_NSEL), jnp.float32),
                       pltpu.SemaphoreType.DMA, pltpu.SemaphoreType.DMA, pltpu.SemaphoreType.DMA,
                       pltpu.SemaphoreType.DMA, pltpu.SemaphoreType.DMA((2,))])
    def run(idx_hbm, h_hbm, tab_hbm, out_hbm, idx_v, rows_a, rows_b, x_v, a_v, sem_a, sem_b, sem_i, sem_x, sem_o):
        wid = lax.axis_index("s") * info.num_cores + lax.axis_index("c")
        base = first + wid * per_w
        lane = lax.iota(jnp.int32, lanes)
        mask_hi = jnp.full((lanes,), 0xFFFF0000, jnp.uint32)

        def gather(p, part, rows, sem):
            return pltpu.make_async_copy(tab_hbm.at[idx_v.at[p, pl.ds(part * nhalf, nhalf)]], rows, sem)

        def load_idx(t, p):
            return pltpu.make_async_copy(idx_hbm.at[t], idx_v.at[p], sem_i)

        def load_x(t, p):
            return pltpu.make_async_copy(h_hbm.at[t], x_v.at[p], sem_x)

        def store_a(t, p):
            return pltpu.make_async_copy(a_v.at[p], out_hbm.at[t - first], sem_o.at[p])

        def compute(rows, p, part):
            def group(g, c2):
                vec = jnp.zeros((lanes,), jnp.float32)
                for rb in range(lanes // rows_blk):
                    def kslice(kk, accs):
                        accs = list(accs)
                        for ku in range(kunroll):
                            off = pl.multiple_of((kk * kunroll + ku) * lanes, lanes)
                            xh = x_v[p, pl.ds(off, lanes)]
                            xl = x_v[p, pl.ds(half + off, lanes)]
                            for r in range(rows_blk):
                                w = rows[g * lanes + rb * rows_blk + r, pl.ds(off, lanes)]
                                hi = plsc.bitcast(w & mask_hi, jnp.float32)
                                lo = plsc.bitcast(w << 16, jnp.float32)
                                accs[r] = accs[r] + hi * xh + lo * xl
                        return tuple(accs)

                    zero = jnp.zeros((lanes,), jnp.float32)
                    accs = lax.fori_loop(0, half // lanes // kunroll, kslice, (zero,) * rows_blk)
                    for r in range(rows_blk):
                        vec = jnp.where(lane == rb * rows_blk + r, jnp.sum(accs[r]), vec)
                a_v[p, pl.ds(pl.multiple_of(part * nhalf + g * lanes, lanes), lanes)] = vec
                return c2

            lax.fori_loop(0, nhalf // lanes, group, 0)

        load_idx(base, 0).start()
        load_x(base, 0).start()
        load_idx(base, 0).wait()
        gather(0, 0, rows_a, sem_a).start()
        gather(0, 1, rows_b, sem_b).start()
        load_x(base, 0).wait()

        def token(i, carry):
            t = base + i
            p = lax.rem(i, 2)
            q = 1 - p
            tn = jnp.minimum(t + 1, base + per_w - 1)
            load_idx(tn, q).start()
            load_x(tn, q).start()

            @pl.when(i >= 2)
            def _():
                store_a(t, p).wait()

            gather(p, 0, rows_a, sem_a).wait()
            compute(rows_a, p, 0)
            load_idx(tn, q).wait()
            gather(q, 0, rows_a, sem_a).start()
            gather(p, 1, rows_b, sem_b).wait()
            compute(rows_b, p, 1)
            gather(q, 1, rows_b, sem_b).start()
            load_x(tn, q).wait()
            store_a(t, p).start()
            return carry

        lax.fori_loop(0, per_w, token, 0)
        gather(0, 0, rows_a, sem_a).wait()
        gather(0, 1, rows_b, sem_b).wait()
        store_a(base, 0).wait()
        store_a(base, 1).wait()

    return run(expert_ids, h2, tab)


def _sc_peer_v(expert_ids, coef, x1, g2, tab, first, count, per_batch):
    info = plsc.get_sparse_core_info()
    lanes, workers = info.num_lanes, info.num_cores * info.num_subcores
    half = tab.shape[1]
    d = 2 * half
    per_w = count // workers
    nhalf = PEER_NSEL // 2
    sblk = 16
    runroll = 1
    assert per_w * workers == count and per_w >= 2 and nhalf % runroll == 0 and half % (lanes * sblk) == 0
    mesh = plsc.VectorSubcoreMesh(core_axis_name="c", subcore_axis_name="s")
    dma = pltpu.SemaphoreType.DMA

    @functools.partial(
        pl.kernel, mesh=mesh, name="peer_v_sc",
        compiler_params=pltpu.CompilerParams(needs_layout_passes=False),
        out_type=jax.ShapeDtypeStruct((count, d), jnp.float32),
        scratch_types=[pltpu.VMEM((2, PEER_NSEL), jnp.int32), pltpu.VMEM((2 * PEER_NSEL,), jnp.float32),
                       pltpu.VMEM((nhalf, half), jnp.uint32), pltpu.VMEM((nhalf, half), jnp.uint32),
                       pltpu.VMEM((2, d), jnp.float32), pltpu.VMEM((2, d), jnp.float32),
                       pltpu.VMEM((d,), jnp.float32), pltpu.VMEM((2, d), jnp.float32),
                       dma, dma, dma, dma, dma, dma, dma((2,))])
    def run(idx_hbm, coef_hbm, x_hbm, g_hbm, tab_hbm, out_hbm,
            idx_v, coef_v, rows_a, rows_b, x_v, g_v, y_v, o_v, sem_a, sem_b, sem_i, sem_c, sem_x, sem_g, sem_o):
        wid = lax.axis_index("s") * info.num_cores + lax.axis_index("c")
        base = first + wid * per_w
        mask_hi = jnp.full((lanes,), 0xFFFF0000, jnp.uint32)

        def gather(p, part, rows, sem):
            return pltpu.make_async_copy(tab_hbm.at[idx_v.at[p, pl.ds(part * nhalf, nhalf)]], rows, sem)

        def loads(t, p):
            cslot = coef_v.at[pl.ds(pl.multiple_of(p * PEER_NSEL, PEER_NSEL), PEER_NSEL)]
            return (pltpu.make_async_copy(idx_hbm.at[t], idx_v.at[p], sem_i),
                    pltpu.make_async_copy(coef_hbm.at[t], cslot, sem_c),
                    pltpu.make_async_copy(x_hbm.at[t], x_v.at[p], sem_x),
                    pltpu.make_async_copy(g_hbm.at[t // per_batch], g_v.at[p], sem_g))

        def store_o(t, p):
            return pltpu.make_async_copy(o_v.at[p], out_hbm.at[t - first], sem_o.at[p])

        def compute(rows, p, part):
            cbase = p * PEER_NSEL + part * nhalf
            for sb in range(half // lanes // sblk):
                offs = [(sb * sblk + s) * lanes for s in range(sblk)]
                if part == 0:
                    init = tuple(jnp.zeros((lanes,), jnp.float32) for _ in range(2 * sblk))
                else:
                    init = tuple([y_v[pl.ds(o, lanes)] for o in offs] + [y_v[pl.ds(half + o, lanes)] for o in offs])

                def rowloop(rr, accs):
                    accs = list(accs)
                    for ru in range(runroll):
                        r = rr * runroll + ru
                        c = plsc.load_gather(coef_v, [jnp.full((lanes,), cbase + r, jnp.int32)])
                        for s in range(sblk):
                            w = rows[r, pl.ds(offs[s], lanes)]
                            hi = plsc.bitcast(w & mask_hi, jnp.float32)
                            lo = plsc.bitcast(w << 16, jnp.float32)
                            accs[s] = accs[s] + c * hi
                            accs[sblk + s] = accs[sblk + s] + c * lo
                    return tuple(accs)

                accs = lax.fori_loop(0, nhalf // runroll, rowloop, init)
                for s in range(sblk):
                    for hl, o in ((0, offs[s]), (1, half + offs[s])):
                        if part == 0:
                            y_v[pl.ds(o, lanes)] = accs[hl * sblk + s]
                        else:
                            o_v[p, pl.ds(o, lanes)] = (x_v[p, pl.ds(o, lanes)]
                                                       + g_v[p, pl.ds(o, lanes)] * accs[hl * sblk + s])

        head = loads(base, 0)
        for c in head:
            c.start()
        head[0].wait()
        gather(0, 0, rows_a, sem_a).start()
        gather(0, 1, rows_b, sem_b).start()
        for c in head[1:]:
            c.wait()

        def token(i, carry):
            t = base + i
            p = lax.rem(i, 2)
            q = 1 - p
            tn = jnp.minimum(t + 1, base + per_w - 1)
            nxt = loads(tn, q)
            for c in nxt:
                c.start()

            @pl.when(i >= 2)
            def _():
                store_o(t, p).wait()

            gather(p, 0, rows_a, sem_a).wait()
            compute(rows_a, p, 0)
            nxt[0].wait()
            gather(q, 0, rows_a, sem_a).start()
            gather(p, 1, rows_b, sem_b).wait()
            compute(rows_b, p, 1)
            gather(q, 1, rows_b, sem_b).start()
            for c in nxt[1:]:
                c.wait()
            store_o(t, p).start()
            return carry

        lax.fori_loop(0, per_w, token, 0)
        gather(0, 0, rows_a, sem_a).wait()
        gather(0, 1, rows_b, sem_b).wait()
        store_o(base, 0).wait()
        store_o(base, 1).wait()

    return run(expert_ids, coef, x1, g2, tab)


def _gelu_gate_kernel(a_ref, g_ref, o_ref):
    a = a_ref[...]
    o_ref[...] = 0.5 * a * (1.0 + lax.erf(a * (2.0 ** -0.5))) * g_ref[...]


def _gelu_gate(a, gates, first):
    count = a.shape[0]
    return pl.pallas_call(
        _gelu_gate_kernel,
        grid=(count // PEER_TM,),
        in_specs=[pl.BlockSpec((PEER_TM, PEER_NSEL), lambda i: (i, 0)),
                  pl.BlockSpec((PEER_TM, PEER_NSEL), lambda i: (first // PEER_TM + i, 0))],
        out_specs=pl.BlockSpec((PEER_TM, PEER_NSEL), lambda i: (i, 0)),
        out_shape=jax.ShapeDtypeStruct((count, PEER_NSEL), jnp.float32),
        compiler_params=_cparams(("arbitrary",)),
        name="gelu_gate",
    )(a, gates)


def _pack_table(tab):
    n, d = tab.shape
    bits = lax.bitcast_convert_type(tab.astype(jnp.bfloat16), jnp.uint16).astype(jnp.uint32)
    packed = (bits[:, : d // 2] << 16) | bits[:, d // 2:]
    return packed.reshape(n * HALF_ROWS, LANES)


def _peer_specs(t, d):
    smem_blk = pl.BlockSpec((1, PEER_TM, PEER_NSEL), lambda i: (i, 0, 0), memory_space=pltpu.SMEM)
    tok_blk = pl.BlockSpec((PEER_TM, d), lambda i: (i, 0))
    sel_blk = pl.BlockSpec((PEER_TM, PEER_NSEL), lambda i: (i, 0))
    return smem_blk, tok_blk, sel_blk, pl.BlockSpec(memory_space=pltpu.VMEM)


def _after(x, prev):
    return x if prev is None else lax.optimization_barrier((x, prev))[0]


def _peer_u_phase(h2, experts, gates, u_packed, n_sc, sc_prev):
    t, d = h2.shape
    assert d == SUBLANES * LANES and d // 2 == HALF_ROWS * LANES and PEER_TM % PEER_TB == 0
    smem_blk, tok_blk, sel_blk, tab_spec = _peer_specs(t, d)
    parts = []
    n_tc = t - n_sc
    if n_tc:
        parts.append(pl.pallas_call(
            _peer_u_kernel,
            grid=(n_tc // PEER_TM,),
            in_specs=[smem_blk, tok_blk, sel_blk, tab_spec],
            out_specs=sel_blk,
            out_shape=jax.ShapeDtypeStruct((n_tc, PEER_NSEL), jnp.float32),
            scratch_shapes=[pltpu.VMEM((PEER_TB * PEER_NSEL * HALF_ROWS, LANES), jnp.float32)],
            compiler_params=_cparams(("arbitrary",)),
            name="peer_u",
        )(experts.reshape(t // PEER_TM, PEER_TM, PEER_NSEL), h2, gates, u_packed))
    a_sc = None
    if n_sc:
        a_sc = _sc_peer_u(experts // HALF_ROWS, h2, u_packed.reshape(-1, d // 2), n_tc, n_sc)
    return (parts[0] if parts else None), a_sc


def _finish_coef(coef_tc, a_sc, gates):
    parts = [] if coef_tc is None else [coef_tc]
    if a_sc is not None:
        parts.append(_gelu_gate(a_sc, gates, gates.shape[0] - a_sc.shape[0]))
    return parts[0] if len(parts) == 1 else jnp.concatenate(parts, axis=0)


def _peer_v_phase(coef, x1, g2, experts, v_packed, batch, n_sc, sc_prev):
    t, d = x1.shape
    per_b = t // PEER_TM // batch
    smem_blk, tok_blk, sel_blk, tab_spec = _peer_specs(t, d)
    parts = []
    n_tc = t - n_sc
    if n_tc:
        parts.append(pl.pallas_call(
            _peer_v_kernel,
            grid=(n_tc // PEER_TM,),
            in_specs=[smem_blk, sel_blk, tok_blk,
                      pl.BlockSpec((None, 1, d), lambda i: (i // per_b, 0, 0)),
                      tab_spec],
            out_specs=tok_blk,
            out_shape=jax.ShapeDtypeStruct((n_tc, d), jnp.float32),
            scratch_shapes=[pltpu.VMEM((PEER_TB * PEER_NSEL, LANES), jnp.float32)],
            compiler_params=_cparams(("arbitrary",)),
            name="peer_v",
        )(experts.reshape(t // PEER_TM, PEER_TM, PEER_NSEL), coef, x1, g2.reshape(batch, 1, d), v_packed))
    if n_sc:
        sc_prev = _sc_peer_v(experts // HALF_ROWS, coef, x1, g2, v_packed.reshape(-1, d // 2),
                             n_tc, n_sc, t // batch)
        parts.append(sc_prev)
    return parts, sc_prev


def kernel(x, c, ada_w, ada_b, norm1_g, norm2_g, w_in, a_qk_gain, a_lambda, a_sub_gain, b_q_gain, b_kv_gain, b_w_uv, w_out, peer_wq, peer_subkeys, peer_u, peer_v):
    b, s, d = x.shape
    t = b * s
    x2 = x.reshape(t, d)
    nchunk = len(PEER_PLAN)
    bc = b // nchunk
    tc = bc * s
    for l in range(ada_w.shape[0]):
        mod = _adaln(c, ada_w[l], ada_b[l])
        u_packed, v_packed = _pack_table(peer_u[l]), _pack_table(peer_v[l])
        lam_init = 0.8 - 0.6 * math.exp(-0.3 * l)
        outs = [None] * nchunk
        pending = {}
        sc_prev = None
        prev_experts = None

        def issue_v(k, anchor):
            nonlocal sc_prev
            coef_tc, a_sc, gates, args = pending.pop(k)
            if a_sc is not None:
                anchor, a_sc = lax.optimization_barrier((anchor, a_sc))
                sc_prev = a_sc
            outs[k], sc_prev = _peer_v_phase(_finish_coef(coef_tc, a_sc, gates), *args, sc_prev)
            return anchor

        def issue_planned(at, anchor):
            for k in [k for k in pending if PEER_PLAN[k][2] == at]:
                anchor = issue_v(k, anchor)
            return anchor

        for ck, (n_sc_u, n_sc_v, _) in enumerate(PEER_PLAN):
            xc = _after(x2[ck * tc:(ck + 1) * tc], prev_experts)
            sh1, sc1, g1, sh2, sc2, g2 = [mod[ck * bc:(ck + 1) * bc, i * d:(i + 1) * d] for i in range(6)]
            qa, ka, va, qb, kvb, qi, ki, wi = _inproj(xc, sh1, sc1, norm1_g[l], w_in[l], a_qk_gain[l],
                                                      b_q_gain[l], b_kv_gain[l], bc)
            oa = _diffattn(qa, ka, va, a_lambda[l], a_sub_gain[l], bc, lam_init)
            oa = issue_planned((ck, "diffattn"), oa)
            ob = _dsa(qi, wi, ki, qb, kvb, b_w_uv[l], bc)
            x1, h2, q = _outproj(xc, oa, ob, w_out[l], g1, sh2, sc2, norm2_g[l], peer_wq[l], bc)
            q = issue_planned((ck, "outproj"), q)
            experts, gates = _peer_select(q, peer_subkeys[l])
            prev_experts = experts
            coef_tc, a_sc = _peer_u_phase(h2, experts, gates, u_packed, n_sc_u, sc_prev)
            sc_prev = sc_prev if a_sc is None else a_sc
            pending[ck] = (coef_tc, a_sc, gates, (x1, g2, experts, v_packed, bc, n_sc_v))
        anchor = pending[nchunk - 1][0]
        for k in sorted(pending):
            anchor = issue_v(k, anchor)
        x2 = jnp.concatenate([p for o in outs for p in o], axis=0)
    return x2.reshape(b, s, d)
```

```python
import functools
import math

import jax
import jax.numpy as jnp
import numpy as np
from jax import lax
from jax.experimental import pallas as pl
from jax.experimental.pallas import tpu as pltpu
from jax.experimental.pallas import tpu_sc as plsc

CHUNK = 64
A_HEADS, A_QK_DIM, A_V_DIM = 4, 64, 128
B_HEADS, B_LAT, B_V_DIM = 8, 128, 64
IDX_HEADS, IDX_DIM = 4, 64
DSA_TOPK_MAX = 256
A_Q = A_HEADS * 2 * A_QK_DIM
A_K = A_Q
A_V = A_HEADS * A_V_DIM
B_Q = B_HEADS * B_LAT
B_KV = B_LAT
I_Q = IDX_HEADS * IDX_DIM
I_K = IDX_DIM
I_W = IDX_HEADS
OFF_AQ, OFF_AK, OFF_AV = 0, A_Q, A_Q + A_K
OFF_BQ = OFF_AV + A_V
OFF_KV = OFF_BQ + B_Q
OFF_IQ = OFF_KV + B_KV
OFF_IK = OFF_IQ + I_Q
OFF_IW = OFF_IK + I_K
IN_COLS = OFF_IW + I_W
PEER_HEADS, PEER_NKEYS, PEER_QDIM, PEER_TOPK = 8, 128, 128, 16
EPS = 1e-6
NEG = -1e30
INT_MIN = -(2 ** 31)

SUBLANES = 8
LANES = 128
VMEM_LIMIT = 56 * 1024 * 1024

PROJ_TM = 256
A_TQ = 512
B_TQ = 256
PEER_TM = 128
PEER_TB = 8
PEER_NSEL = PEER_HEADS * PEER_TOPK
PEER_PLAN = ((8192, 8192, (1, "outproj")), (8192, 8192, (3, "diffattn")), (8192, 5376, None), (0, 0, None))
HALF_ROWS = 4


def _split_bf16(x):
    hi = x.astype(jnp.bfloat16)
    lo = (x - hi.astype(jnp.float32)).astype(jnp.bfloat16)
    return hi, lo


def _dot(a, b):
    return jnp.dot(a, b, preferred_element_type=jnp.float32)


def _dot_nt(a, b):
    return lax.dot_general(a, b, (((1,), (1,)), ((), ())), preferred_element_type=jnp.float32)


def _cparams(sem):
    return pltpu.CompilerParams(dimension_semantics=sem, vmem_limit_bytes=VMEM_LIMIT)


def _adaln_kernel(c_ref, w_ref, b_ref, o_ref):
    cf = c_ref[...]
    a = cf * (1.0 / (1.0 + jnp.exp(-cf)))
    a_hi, a_lo = _split_bf16(a)
    w_hi, w_lo = _split_bf16(w_ref[...])
    o_ref[...] = _dot(a_hi, w_hi) + _dot(a_hi, w_lo) + _dot(a_lo, w_hi) + b_ref[...]


def _adaln(c, w, b):
    bsz, d = c.shape
    n = w.shape[1]
    tn = 1024
    return pl.pallas_call(
        _adaln_kernel,
        grid=(n // tn,),
        in_specs=[pl.BlockSpec((bsz, d), lambda j: (0, 0)),
                  pl.BlockSpec((d, tn), lambda j: (0, j)),
                  pl.BlockSpec((1, tn), lambda j: (0, j))],
        out_specs=pl.BlockSpec((bsz, tn), lambda j: (0, j)),
        out_shape=jax.ShapeDtypeStruct((bsz, n), jnp.float32),
        compiler_params=_cparams(("arbitrary",)),
        name="adaln",
    )(c, w, b.reshape(1, n))


def _rms_rows(x, g):
    return x * lax.rsqrt(jnp.mean(x * x, axis=-1, keepdims=True) + EPS) * g


def _group_norm_block(p, gmat, gain, n):
    hi, lo = _split_bf16(p * p)
    ss = _dot(hi, gmat) + _dot(lo, gmat)
    return p * lax.rsqrt(ss * (1.0 / n) + EPS) * gain


def _inproj_kernel(x_ref, sh_ref, sc_ref, g_ref, w_ref, gq_ref, gk_ref, gbq_ref, gkv_ref,
                   qa_ref, ka_ref, va_ref, qb_ref, kv_ref, qi_ref, ki_ref, wi_ref):
    h = _rms_rows(x_ref[...], g_ref[...]) * (1.0 + sc_ref[0]) + sh_ref[0]
    hb = h.astype(jnp.bfloat16)
    row = lax.broadcasted_iota(jnp.int32, (LANES, LANES), 0)
    col = lax.broadcasted_iota(jnp.int32, (LANES, LANES), 1)
    g64 = jnp.where((row // A_QK_DIM) == (col // A_QK_DIM), 1.0, 0.0).astype(jnp.bfloat16)
    g128 = jnp.ones((LANES, LANES), jnp.bfloat16)

    def proj(off, width):
        return _dot(hb, w_ref[:, off:off + width])

    for blk in range(A_Q // LANES):
        sl = slice(blk * LANES, (blk + 1) * LANES)
        p = proj(OFF_AQ + blk * LANES, LANES)
        qa_ref[:, sl] = (_group_norm_block(p, g64, gq_ref[...], A_QK_DIM) * (A_QK_DIM ** -0.5)).astype(qa_ref.dtype)
        p = proj(OFF_AK + blk * LANES, LANES)
        ka_ref[:, sl] = _group_norm_block(p, g64, gk_ref[...], A_QK_DIM).astype(ka_ref.dtype)
    va_ref[...] = proj(OFF_AV, A_V).astype(va_ref.dtype)
    for blk in range(B_HEADS):
        sl = slice(blk * LANES, (blk + 1) * LANES)
        p = proj(OFF_BQ + blk * LANES, LANES)
        qb_ref[:, sl] = _group_norm_block(p, g128, gbq_ref[...], B_LAT).astype(qb_ref.dtype)
    p = proj(OFF_KV, B_KV)
    kv_ref[...] = _group_norm_block(p, g128, gkv_ref[...], B_LAT).astype(kv_ref.dtype)
    qi_ref[...] = proj(OFF_IQ, I_Q).astype(qi_ref.dtype)
    tail = proj(OFF_IK, 2 * LANES)
    ki_ref[...] = tail[:, :LANES].astype(ki_ref.dtype)
    wi_ref[...] = tail[:, I_K:I_K + LANES]


def _inproj(x2, sh1, sc1, g1n, w_in, a_qk_gain, b_q_gain, b_kv_gain, batch):
    t, d = x2.shape
    per_b = (t // batch) // PROJ_TM
    wpad = OFF_IK + 2 * LANES
    w = jnp.zeros((d, wpad), jnp.bfloat16).at[:, :IN_COLS].set(w_in.astype(jnp.bfloat16))
    gq = jnp.tile(a_qk_gain[0], 2).reshape(1, LANES)
    gk = jnp.tile(a_qk_gain[1], 2).reshape(1, LANES)
    tok = lambda width: pl.BlockSpec((PROJ_TM, width), lambda i: (i, 0))
    vec = lambda width: pl.BlockSpec((1, width), lambda i: (0, 0))
    per_batch = pl.BlockSpec((1, 1, d), lambda i: (i // per_b, 0, 0))
    bf = jnp.bfloat16
    outs = [(A_Q, bf), (A_K, bf), (A_V, bf), (B_Q, bf), (B_KV, bf), (I_Q, bf), (LANES, bf), (LANES, jnp.float32)]
    return pl.pallas_call(
        _inproj_kernel,
        grid=(t // PROJ_TM,),
        in_specs=[tok(d), per_batch, per_batch, vec(d),
                  pl.BlockSpec((d, wpad), lambda i: (0, 0)),
                  vec(LANES), vec(LANES), vec(LANES), vec(LANES)],
        out_specs=[tok(wd) for wd, _ in outs],
        out_shape=[jax.ShapeDtypeStruct((t, wd), dt) for wd, dt in outs],
        compiler_params=_cparams(("arbitrary",)),
        name="inproj",
    )(x2, sh1.reshape(batch, 1, d), sc1.reshape(batch, 1, d), g1n.reshape(1, d), w,
      gq, gk, b_q_gain.reshape(1, LANES), b_kv_gain.reshape(1, LANES))


def _positions(q0, tq, s_len):
    qpos = q0 + lax.broadcasted_iota(jnp.int32, (tq, s_len), 0)
    kpos = lax.broadcasted_iota(jnp.int32, (tq, s_len), 1)
    cend = (qpos // CHUNK + 1) * CHUNK
    return qpos, kpos, cend


def _softmax_rows(s):
    m = jnp.max(s, axis=-1, keepdims=True)
    p = jnp.exp(s - m)
    return p, jnp.sum(p, axis=-1, keepdims=True)


def _diffattn_kernel(q_ref, k_ref, v_ref, lam_ref, gain_ref, o_ref, *, lam_init, q_base):
    tq, s_len = q_ref.shape[0], k_ref.shape[0]
    q0 = q_base + pl.program_id(1) * tq
    qpos, kpos, cend = _positions(q0, tq, s_len)
    dist = jnp.abs(qpos - kpos).astype(jnp.float32)
    negmask = jnp.where(kpos < cend, 0.0, NEG)
    lf = lam_ref[...]
    lam = (jnp.exp(jnp.sum(lf[0:1] * lf[1:2], axis=-1, keepdims=True))
           - jnp.exp(jnp.sum(lf[2:3] * lf[3:4], axis=-1, keepdims=True)) + lam_init)
    for h in range(A_HEADS):
        slope = 2.0 ** (-8.0 * (h + 1) / A_HEADS)
        bias = negmask - slope * dist
        v = v_ref[:, h * A_V_DIM:(h + 1) * A_V_DIM]
        outs = []
        for m in range(2):
            c0 = (h * 2 + m) * A_QK_DIM
            s = _dot_nt(q_ref[:, c0:c0 + A_QK_DIM], k_ref[:, c0:c0 + A_QK_DIM]) + bias
            p, l = _softmax_rows(s)
            outs.append(_dot(p.astype(jnp.bfloat16), v) / l)
        o = outs[0] - lam * outs[1]
        o = _rms_rows(o, gain_ref[...]) * (1.0 - lam_init)
        o_ref[:, h * A_V_DIM:(h + 1) * A_V_DIM] = o.astype(o_ref.dtype)


def _causal_groups(s_len, tq):
    span = min(tq, s_len)
    tiles = span // tq
    return [(g * tiles, tiles, (g + 1) * span) for g in range(s_len // span)]


def _qblk(tq, first_tile, width):
    return pl.BlockSpec((None, tq, width), lambda b, i: (b, first_tile + i, 0))


def _kblk(klen, width):
    return pl.BlockSpec((None, klen, width), lambda b, i: (b, 0, 0))


def _oblk(tq, width):
    return pl.BlockSpec((None, tq, width), lambda b, i: (b, i, 0))


def _diffattn(qa, ka, va, a_lambda, a_sub_gain, batch, lam_init):
    t = qa.shape[0]
    s_len = t // batch
    qa3, ka3, va3 = (a.reshape(batch, s_len, a.shape[1]) for a in (qa, ka, va))
    outs = []
    for first, tiles, klen in _causal_groups(s_len, A_TQ):
        outs.append(pl.pallas_call(
            functools.partial(_diffattn_kernel, lam_init=lam_init, q_base=first * A_TQ),
            grid=(batch, tiles),
            in_specs=[_qblk(A_TQ, first, A_Q), _kblk(klen, A_K), _kblk(klen, A_V),
                      pl.BlockSpec((4, A_QK_DIM), lambda b, i: (0, 0)),
                      pl.BlockSpec((1, A_V_DIM), lambda b, i: (0, 0))],
            out_specs=_oblk(A_TQ, A_V),
            out_shape=jax.ShapeDtypeStruct((batch, tiles * A_TQ, A_V), jnp.bfloat16),
            compiler_params=_cparams(("arbitrary", "arbitrary")),
            name="diffattn",
        )(qa3, ka3, va3, a_lambda, a_sub_gain.reshape(1, A_V_DIM)))
    return jnp.concatenate(outs, axis=1).reshape(t, A_V)


def _count(mask):
    return jnp.sum(jnp.where(mask, 1.0, 0.0), axis=-1, keepdims=True)


def _topk_mask(score, allowed, kpos, topk, s_len):
    bits = pltpu.bitcast(score + 0.0, jnp.int32)
    key = jnp.where(bits < 0, bits ^ jnp.int32(0x7FFFFFFF), bits)
    key = jnp.where(allowed, key, jnp.int32(INT_MIN))
    kf = float(topk)
    thr = jnp.where(_count(key >= 0) >= kf, jnp.int32(0), jnp.int32(INT_MIN))

    def value_bit(i, thr):
        cand = thr | (jnp.int32(1) << (30 - i))
        return jnp.where(_count(key >= cand) >= kf, cand, thr)

    thr = lax.fori_loop(0, 31, value_bit, thr)
    above = key > thr
    tie = key == thr
    need = kf - _count(above)
    nbits = (s_len - 1).bit_length()

    def pos_bit(i, pos):
        cand = pos | (jnp.int32(1) << (nbits - 1 - i))
        return jnp.where(_count(tie & (kpos < cand)) < need, cand, pos)

    pos = lax.fori_loop(0, nbits, pos_bit, jnp.zeros_like(thr))
    return allowed & (above | (tie & (kpos <= pos)))


def _dsa_kernel(qi_ref, wi_ref, ki_ref, qb_ref, kv_ref, wuv_ref, o_ref, *, topk, q_base):
    tq, s_len = qb_ref.shape[0], kv_ref.shape[0]
    q0 = q_base + pl.program_id(1) * tq
    qpos, kpos, cend = _positions(q0, tq, s_len)
    allowed = kpos < cend
    ki = ki_ref[...]
    wi = wi_ref[...]
    zpad = jnp.zeros((tq, LANES - IDX_DIM), jnp.bfloat16)
    score = jnp.zeros((tq, s_len), jnp.float32)
    for h in range(IDX_HEADS):
        qh = jnp.concatenate([qi_ref[:, h * IDX_DIM:(h + 1) * IDX_DIM], zpad], axis=-1)
        score = score + jnp.maximum(_dot_nt(qh, ki), 0.0) * wi[:, h:h + 1]
    score = score * ((IDX_DIM ** -0.5) * (IDX_HEADS ** -0.5))
    sel = _topk_mask(score, allowed, kpos, topk, s_len)

    dist = jnp.abs(qpos - kpos).astype(jnp.float32)
    negmask = jnp.where(sel, 0.0, NEG)
    kv = kv_ref[...]
    for h in range(B_HEADS):
        slope = 2.0 ** (-8.0 * (h + 1) / B_HEADS)
        s = _dot_nt(qb_ref[:, h * B_LAT:(h + 1) * B_LAT], kv) * (B_LAT ** -0.5) + (negmask - slope * dist)
        p, l = _softmax_rows(s)
        o_lat = _dot(p.astype(jnp.bfloat16), kv) / l
        o = _dot(o_lat.astype(jnp.bfloat16), wuv_ref[h])
        o_ref[:, h * B_V_DIM:(h + 1) * B_V_DIM] = o.astype(o_ref.dtype)


def _dsa(qi, wi, ki, qb, kvb, w_uv, batch):
    t = qb.shape[0]
    s_len = t // batch
    topk = min(DSA_TOPK_MAX, s_len // 4)
    width = B_HEADS * B_V_DIM
    qi3, wi3, ki3, qb3, kv3 = (a.reshape(batch, s_len, a.shape[1]) for a in (qi, wi, ki, qb, kvb))
    wuv = w_uv.astype(jnp.bfloat16)
    outs = []
    for first, tiles, klen in _causal_groups(s_len, B_TQ):
        outs.append(pl.pallas_call(
            functools.partial(_dsa_kernel, topk=topk, q_base=first * B_TQ),
            grid=(batch, tiles),
            in_specs=[_qblk(B_TQ, first, I_Q), _qblk(B_TQ, first, LANES), _kblk(klen, LANES),
                      _qblk(B_TQ, first, B_Q), _kblk(klen, B_KV),
                      pl.BlockSpec((B_HEADS, B_LAT, B_V_DIM), lambda b, i: (0, 0, 0))],
            out_specs=_oblk(B_TQ, width),
            out_shape=jax.ShapeDtypeStruct((batch, tiles * B_TQ, width), jnp.bfloat16),
            compiler_params=_cparams(("arbitrary", "arbitrary")),
            name="dsa",
        )(qi3, wi3, ki3, qb3, kv3, wuv))
    return jnp.concatenate(outs, axis=1).reshape(t, width)


def _outproj_kernel(x_ref, oa_ref, ob_ref, wo_ref, g1_ref, sh_ref, sc_ref, gn_ref, wq_ref,
                    x1_ref, h2_ref, q_ref):
    na = oa_ref.shape[1]
    y = _dot(oa_ref[...], wo_ref[:na, :]) + _dot(ob_ref[...], wo_ref[na:, :])
    x1 = x_ref[...] + g1_ref[0] * y
    x1_ref[...] = x1
    h2 = _rms_rows(x1, gn_ref[...]) * (1.0 + sc_ref[0]) + sh_ref[0]
    h2_ref[...] = h2
    q_ref[...] = _dot(h2.astype(jnp.bfloat16), wq_ref[...]).astype(q_ref.dtype)


def _outproj(x2, oa, ob, w_out, g1, sh2, sc2, g2n, peer_wq, batch):
    t, d = x2.shape
    per_b = (t // batch) // PROJ_TM
    nq = peer_wq.shape[1]
    tok = lambda width: pl.BlockSpec((PROJ_TM, width), lambda i: (i, 0))
    per_batch = pl.BlockSpec((1, 1, d), lambda i: (i // per_b, 0, 0))
    full = lambda a, b: pl.BlockSpec((a, b), lambda i: (0, 0))
    return pl.pallas_call(
        _outproj_kernel,
        grid=(t // PROJ_TM,),
        in_specs=[tok(d), tok(oa.shape[1]), tok(ob.shape[1]), full(w_out.shape[0], d),
                  per_batch, per_batch, per_batch, full(1, d), full(d, nq)],
        out_specs=[tok(d), tok(d), tok(nq)],
        out_shape=[jax.ShapeDtypeStruct((t, d), jnp.float32),
                   jax.ShapeDtypeStruct((t, d), jnp.float32),
                   jax.ShapeDtypeStruct((t, nq), jnp.bfloat16)],
        compiler_params=_cparams(("arbitrary",)),
        name="outproj",
    )(x2, oa, ob, w_out.astype(jnp.bfloat16), g1.reshape(batch, 1, d), sh2.reshape(batch, 1, d),
      sc2.reshape(batch, 1, d), g2n.reshape(1, d), peer_wq.astype(jnp.bfloat16))


def _extract_topk(vals, payload, k):
    n, tm = vals.shape
    rows = lax.broadcasted_iota(jnp.int32, (n, tm), 0).astype(jnp.float32)
    top_v, top_p = [], []
    for _ in range(k):
        m = jnp.max(vals, axis=0, keepdims=True)
        pos = jnp.min(jnp.where(vals == m, rows, float(n)), axis=0, keepdims=True)
        hit = rows == pos
        top_v.append(m)
        top_p.append(pos if payload is None else jnp.max(jnp.where(hit, payload, -1.0), axis=0, keepdims=True))
        vals = jnp.where(hit, -jnp.inf, vals)
    return jnp.concatenate(top_v, axis=0), jnp.concatenate(top_p, axis=0)


def _stair_pairs():
    return [(a, b) for a in range(PEER_TOPK) for b in range(PEER_TOPK) if (a + 1) * (b + 1) <= PEER_TOPK]


N_STAIR = len(_stair_pairs())
N_STAIR_PAD = -(-N_STAIR // SUBLANES) * SUBLANES


def _copy_rows(sel, x):
    x0 = x.astype(jnp.bfloat16)
    r1 = x - x0.astype(jnp.float32)
    x1 = r1.astype(jnp.bfloat16)
    x2 = (r1 - x1.astype(jnp.float32)).astype(jnp.bfloat16)
    return _dot(sel, x0) + _dot(sel, x1) + _dot(sel, x2)


def _peer_select_kernel(q_ref, sk_ref, sela_ref, selb_ref, e_ref, g_ref):
    tm = q_ref.shape[0]
    sela, selb = sela_ref[...], selb_ref[...]
    pad_row = lax.broadcasted_iota(jnp.int32, (N_STAIR_PAD, tm), 0) >= N_STAIR
    e_rows, g_rows = [], []
    for h in range(PEER_HEADS):
        sub = _dot_nt(sk_ref[...], q_ref[:, h * PEER_QDIM:(h + 1) * PEER_QDIM])
        sv0, si0 = _extract_topk(sub[:PEER_NKEYS], None, PEER_TOPK)
        sv1, si1 = _extract_topk(sub[PEER_NKEYS:], None, PEER_TOPK)
        cand = jnp.where(pad_row, -jnp.inf, _copy_rows(sela, sv0) + _copy_rows(selb, sv1))
        ids = (_dot(sela, si0.astype(jnp.bfloat16)) * float(PEER_NKEYS)
               + _dot(selb, si1.astype(jnp.bfloat16))) * float(HALF_ROWS)
        top_s, top_e = _extract_topk(cand, ids, PEER_TOPK)
        p = jnp.exp(top_s - top_s[0:1])
        e_rows.append(top_e)
        g_rows.append(p / jnp.sum(p, axis=0, keepdims=True))
    e_t = jnp.concatenate(e_rows, axis=0)
    g_t = jnp.concatenate(g_rows, axis=0)
    e_ref[...] = e_t.T.astype(jnp.int32)
    g_ref[...] = g_t.T


def _peer_select(q, sub_keys):
    t, nq = q.shape
    half = PEER_QDIM // 2
    sk = jnp.zeros((2 * PEER_NKEYS, PEER_QDIM), jnp.bfloat16)
    sk = sk.at[:PEER_NKEYS, :half].set(sub_keys[0].astype(jnp.bfloat16))
    sk = sk.at[PEER_NKEYS:, half:].set(sub_keys[1].astype(jnp.bfloat16))
    sel = np.zeros((2, N_STAIR_PAD, PEER_TOPK), np.float32)
    for r, (a, b) in enumerate(_stair_pairs()):
        sel[0, r, a] = 1.0
        sel[1, r, b] = 1.0
    sel = jnp.asarray(sel, jnp.bfloat16)
    tok = lambda width: pl.BlockSpec((PEER_TM, width), lambda i: (i, 0))
    full = lambda a, b: pl.BlockSpec((a, b), lambda i: (0, 0))
    return pl.pallas_call(
        _peer_select_kernel,
        grid=(t // PEER_TM,),
        in_specs=[tok(nq), full(2 * PEER_NKEYS, PEER_QDIM),
                  full(N_STAIR_PAD, PEER_TOPK), full(N_STAIR_PAD, PEER_TOPK)],
        out_specs=[tok(PEER_NSEL), tok(PEER_NSEL)],
        out_shape=[jax.ShapeDtypeStruct((t, PEER_NSEL), jnp.int32),
                   jax.ShapeDtypeStruct((t, PEER_NSEL), jnp.float32)],
        compiler_params=_cparams(("arbitrary",)),
        name="peer_select",
    )(q, sk, sel[0], sel[1])


def _gather_row(tab_ref, row):
    w = tab_ref[pl.ds(pl.multiple_of(row, HALF_ROWS), HALF_ROWS), :]
    hi = pltpu.bitcast(w & jnp.uint32(0xFFFF0000), jnp.float32)
    lo = pltpu.bitcast(w << 16, jnp.float32)
    return hi, lo


def _peer_u_kernel(idx_ref, h_ref, gate_ref, tab_ref, coef_ref, abuf_ref):
    ones = jnp.ones((2 * LANES, LANES), jnp.bfloat16)
    lane = lax.broadcasted_iota(jnp.int32, (PEER_NSEL, PEER_TM), 1)
    rows_per_tok = PEER_NSEL * HALF_ROWS

    def batch(b, a_t):
        def token(tb, c2):
            t = b * PEER_TB + tb
            x8 = h_ref[pl.ds(t, 1), :].reshape(SUBLANES, LANES)
            xh, xl = x8[:HALF_ROWS], x8[HALF_ROWS:]
            off = pl.multiple_of(tb * rows_per_tok, rows_per_tok)
            for j in range(PEER_NSEL):
                hi, lo = _gather_row(tab_ref, idx_ref[0, t, j])
                abuf_ref[pl.ds(off + j * HALF_ROWS, HALF_ROWS), :] = hi * xh + lo * xl
            return c2

        lax.fori_loop(0, PEER_TB, token, 0)
        parts = []
        for tb in range(PEER_TB):
            ab = abuf_ref[pl.ds(tb * rows_per_tok, PEER_NSEL, stride=HALF_ROWS), :]
            for s in range(1, HALF_ROWS):
                ab = ab + abuf_ref[pl.ds(tb * rows_per_tok + s, PEER_NSEL, stride=HALF_ROWS), :]
            parts.append(ab)
        ab_hi, ab_lo = _split_bf16(jnp.concatenate(parts, axis=0))
        r = _dot(jnp.concatenate([ab_hi, ab_lo], axis=1), ones)
        for tb in range(PEER_TB):
            a_t = jnp.where(lane == b * PEER_TB + tb, r[tb * PEER_NSEL:(tb + 1) * PEER_NSEL], a_t)
        return a_t

    a_t = lax.fori_loop(0, PEER_TM // PEER_TB, batch, jnp.zeros((PEER_NSEL, PEER_TM), jnp.float32))
    a = a_t.T
    coef_ref[...] = 0.5 * a * (1.0 + lax.erf(a * (2.0 ** -0.5))) * gate_ref[...]


def _peer_v_kernel(idx_ref, coef_ref, x_ref, g2_ref, tab_ref, out_ref, cb_ref):
    g2 = g2_ref[...]
    ones = jnp.ones((2 * LANES, LANES), jnp.bfloat16)
    eye = (lax.broadcasted_iota(jnp.int32, (PEER_NSEL, LANES), 0)
           == lax.broadcasted_iota(jnp.int32, (PEER_NSEL, LANES), 1))
    nacc = 2

    def batch(b, carry):
        for tb in range(PEER_TB):
            row = coef_ref[pl.ds(b * PEER_TB + tb, 1), :]
            d_hi, d_lo = _split_bf16(jnp.where(eye, jnp.broadcast_to(row, (PEER_NSEL, LANES)), 0.0))
            cb_ref[pl.ds(tb * PEER_NSEL, PEER_NSEL), :] = _dot(jnp.concatenate([d_hi, d_lo], axis=1), ones)

        def token(tb, c2):
            t = b * PEER_TB + tb
            zero = jnp.zeros((HALF_ROWS, LANES), jnp.float32)
            acc_h = [zero] * nacc
            acc_l = [zero] * nacc
            cbase = pl.multiple_of(tb * PEER_NSEL, PEER_NSEL)
            for j in range(PEER_NSEL):
                hi, lo = _gather_row(tab_ref, idx_ref[0, t, j])
                c = jnp.broadcast_to(cb_ref[pl.ds(cbase + j, 1), :], (HALF_ROWS, LANES))
                acc_h[j % nacc] = acc_h[j % nacc] + c * hi
                acc_l[j % nacc] = acc_l[j % nacc] + c * lo
            y8 = jnp.concatenate([acc_h[0] + acc_h[1], acc_l[0] + acc_l[1]], axis=0)
            out_ref[pl.ds(t, 1), :] = x_ref[pl.ds(t, 1), :] + g2 * y8.reshape(1, SUBLANES * LANES)
            return c2

        lax.fori_loop(0, PEER_TB, token, 0)
        return carry

    lax.fori_loop(0, PEER_TM // PEER_TB, batch, 0)


def _sc_peer_u(expert_ids, h2, tab, first, count):
    info = plsc.get_sparse_core_info()
    lanes, workers = info.num_lanes, info.num_cores * info.num_subcores
    half = tab.shape[1]
    per_w = count // workers
    nhalf = PEER_NSEL // 2
    assert per_w * workers == count and per_w >= 2 and nhalf % lanes == 0 and half % lanes == 0
    rows_blk = 4
    kunroll = 2
    mesh = plsc.VectorSubcoreMesh(core_axis_name="c", subcore_axis_name="s")

    @functools.partial(
        pl.kernel, mesh=mesh, name="peer_u_sc",
        compiler_params=pltpu.CompilerParams(needs_layout_passes=False),
        out_type=jax.ShapeDtypeStruct((count, PEER_NSEL), jnp.float32),
        scratch_types=[pltpu.VMEM((2, PEER_NSEL), jnp.int32),
                       pltpu.VMEM((nhalf, half), jnp.uint32), pltpu.VMEM((nhalf, half), jnp.uint32),
                       pltpu.VMEM((2, 2 * half), jnp.float32), pltpu.VMEM((2, PEER_NSEL), jnp.float32),
                       pltpu.SemaphoreType.DMA, pltpu.SemaphoreType.DMA, pltpu.SemaphoreType.DMA,
                       pltpu.SemaphoreType.DMA, pltpu.SemaphoreType.DMA((2,))])
    def run(idx_hbm, h_hbm, tab_hbm, out_hbm, idx_v, rows_a, rows_b, x_v, a_v, sem_a, sem_b, sem_i, sem_x, sem_o):
        wid = lax.axis_index("s") * info.num_cores + lax.axis_index("c")
        base = first + wid * per_w
        lane = lax.iota(jnp.int32, lanes)
        mask_hi = jnp.full((lanes,), 0xFFFF0000, jnp.uint32)

        def gather(p, part, rows, sem):
            return pltpu.make_async_copy(tab_hbm.at[idx_v.at[p, pl.ds(part * nhalf, nhalf)]], rows, sem)

        def load_idx(t, p):
            return pltpu.make_async_copy(idx_hbm.at[t], idx_v.at[p], sem_i)

        def load_x(t, p):
            return pltpu.make_async_copy(h_hbm.at[t], x_v.at[p], sem_x)

        def store_a(t, p):
            return pltpu.make_async_copy(a_v.at[p], out_hbm.at[t - first], sem_o.at[p])

        def compute(rows, p, part):
            def group(g, c2):
                vec = jnp.zeros((lanes,), jnp.float32)
                for rb in range(lanes // rows_blk):
                    def kslice(kk, accs):
                        accs = list(accs)
                        for ku in range(kunroll):
                            off = pl.multiple_of((kk * kunroll + ku) * lanes, lanes)
                            xh = x_v[p, pl.ds(off, lanes)]
                            xl = x_v[p, pl.ds(half + off, lanes)]
                            for r in range(rows_blk):
                                w = rows[g * lanes + rb * rows_blk + r, pl.ds(off, lanes)]
                                hi = plsc.bitcast(w & mask_hi, jnp.float32)
                                lo = plsc.bitcast(w << 16, jnp.float32)
                                accs[r] = accs[r] + hi * xh + lo * xl
                        return tuple(accs)

                    zero = jnp.zeros((lanes,), jnp.float32)
                    accs = lax.fori_loop(0, half // lanes // kunroll, kslice, (zero,) * rows_blk)
                    for r in range(rows_blk):
                        vec = jnp.where(lane == rb * rows_blk + r, jnp.sum(accs[r]), vec)
                a_v[p, pl.ds(pl.multiple_of(part * nhalf + g * lanes, lanes), lanes)] = vec
                return c2

            lax.fori_loop(0, nhalf // lanes, group, 0)

        load_idx(base, 0).start()
        load_x(base, 0).start()
        load_idx(base, 0).wait()
        gather(0, 0, rows_a, sem_a).start()
        gather(0, 1, rows_b, sem_b).start()
        load_x(base, 0).wait()

        def token(i, carry):
            t = base + i
            p = lax.rem(i, 2)
            q = 1 - p
            tn = jnp.minimum(t + 1, base + per_w - 1)
            load_idx(tn, q).start()
            load_x(tn, q).start()

            @pl.when(i >= 2)
            def _():
                store_a(t, p).wait()

            gather(p, 0, rows_a, sem_a).wait()
            compute(rows_a, p, 0)
            load_idx(tn, q).wait()
            gather(q, 0, rows_a, sem_a).start()
            gather(p, 1, rows_b, sem_b).wait()
            compute(rows_b, p, 1)
            gather(q, 1, rows_b, sem_b).start()
            load_x(tn, q).wait()
            store_a(t, p).start()
            return carry

        lax.fori_loop(0, per_w, token, 0)
        gather(0, 0, rows_a, sem_a).wait()
        gather(0, 1, rows_b, sem_b).wait()
        store_a(base, 0).wait()
        store_a(base, 1).wait()

    return run(expert_ids, h2, tab)


def _sc_peer_v(expert_ids, coef, x1, g2, tab, first, count, per_batch):
    info = plsc.get_sparse_core_info()
    lanes, workers = info.num_lanes, info.num_cores * info.num_subcores
    half = tab.shape[1]
    d = 2 * half
    per_w = count // workers
    nhalf = PEER_NSEL // 2
    sblk = 16
    runroll = 1
    assert per_w * workers == count and per_w >= 2 and nhalf % runroll == 0 and half % (lanes * sblk) == 0
    mesh = plsc.VectorSubcoreMesh(core_axis_name="c", subcore_axis_name="s")
    dma = pltpu.SemaphoreType.DMA

    @functools.partial(
        pl.kernel, mesh=mesh, name="peer_v_sc",
        compiler_params=pltpu.CompilerParams(needs_layout_passes=False),
        out_type=jax.ShapeDtypeStruct((count, d), jnp.float32),
        scratch_types=[pltpu.VMEM((2, PEER_NSEL), jnp.int32), pltpu.VMEM((2 * PEER_NSEL,), jnp.float32),
                       pltpu.VMEM((nhalf, half), jnp.uint32), pltpu.VMEM((nhalf, half), jnp.uint32),
                       pltpu.VMEM((2, d), jnp.float32), pltpu.VMEM((2, d), jnp.float32),
                       pltpu.VMEM((d,), jnp.float32), pltpu.VMEM((2, d), jnp.float32),
                       dma, dma, dma, dma, dma, dma, dma((2,))])
    def run(idx_hbm, coef_hbm, x_hbm, g_hbm, tab_hbm, out_hbm,
            idx_v, coef_v, rows_a, rows_b, x_v, g_v, y_v, o_v, sem_a, sem_b, sem_i, sem_c, sem_x, sem_g, sem_o):
        wid = lax.axis_index("s") * info.num_cores + lax.axis_index("c")
        base = first + wid * per_w
        mask_hi = jnp.full((lanes,), 0xFFFF0000, jnp.uint32)

        def gather(p, part, rows, sem):
            return pltpu.make_async_copy(tab_hbm.at[idx_v.at[p, pl.ds(part * nhalf, nhalf)]], rows, sem)

        def loads(t, p):
            cslot = coef_v.at[pl.ds(pl.multiple_of(p * PEER_NSEL, PEER_NSEL), PEER_NSEL)]
            return (pltpu.make_async_copy(idx_hbm.at[t], idx_v.at[p], sem_i),
                    pltpu.make_async_copy(coef_hbm.at[t], cslot, sem_c),
                    pltpu.make_async_copy(x_hbm.at[t], x_v.at[p], sem_x),
                    pltpu.make_async_copy(g_hbm.at[t // per_batch], g_v.at[p], sem_g))

        def store_o(t, p):
            return pltpu.make_async_copy(o_v.at[p], out_hbm.at[t - first], sem_o.at[p])

        def compute(rows, p, part):
            cbase = p * PEER_NSEL + part * nhalf
            for sb in range(half // lanes // sblk):
                offs = [(sb * sblk + s) * lanes for s in range(sblk)]
                if part == 0:
                    init = tuple(jnp.zeros((lanes,), jnp.float32) for _ in range(2 * sblk))
                else:
                    init = tuple([y_v[pl.ds(o, lanes)] for o in offs] + [y_v[pl.ds(half + o, lanes)] for o in offs])

                def rowloop(rr, accs):
                    accs = list(accs)
                    for ru in range(runroll):
                        r = rr * runroll + ru
                        c = plsc.load_gather(coef_v, [jnp.full((lanes,), cbase + r, jnp.int32)])
                        for s in range(sblk):
                            w = rows[r, pl.ds(offs[s], lanes)]
                            hi = plsc.bitcast(w & mask_hi, jnp.float32)
                            lo = plsc.bitcast(w << 16, jnp.float32)
                            accs[s] = accs[s] + c * hi
                            accs[sblk + s] = accs[sblk + s] + c * lo
                    return tuple(accs)

                accs = lax.fori_loop(0, nhalf // runroll, rowloop, init)
                for s in range(sblk):
                    for hl, o in ((0, offs[s]), (1, half + offs[s])):
                        if part == 0:
                            y_v[pl.ds(o, lanes)] = accs[hl * sblk + s]
                        else:
                            o_v[p, pl.ds(o, lanes)] = (x_v[p, pl.ds(o, lanes)]
                                                       + g_v[p, pl.ds(o, lanes)] * accs[hl * sblk + s])

        head = loads(base, 0)
        for c in head:
            c.start()
        head[0].wait()
        gather(0, 0, rows_a, sem_a).start()
        gather(0, 1, rows_b, sem_b).start()
        for c in head[1:]:
            c.wait()

        def token(i, carry):
            t = base + i
            p = lax.rem(i, 2)
            q = 1 - p
            tn = jnp.minimum(t + 1, base + per_w - 1)
            nxt = loads(tn, q)
            for c in nxt:
                c.start()

            @pl.when(i >= 2)
            def _():
                store_o(t, p).wait()

            gather(p, 0, rows_a, sem_a).wait()
            compute(rows_a, p, 0)
            nxt[0].wait()
            gather(q, 0, rows_a, sem_a).start()
            gather(p, 1, rows_b, sem_b).wait()
            compute(rows_b, p, 1)
            gather(q, 1, rows_b, sem_b).start()
            for c in nxt[1:]:
                c.wait()
            store_o(t, p).start()
            return carry

        lax.fori_loop(0, per_w, token, 0)
        gather(0, 0, rows_a, sem_a).wait()
        gather(0, 1, rows_b, sem_b).wait()
        store_o(base, 0).wait()
        store_o(base, 1).wait()

    return run(expert_ids, coef, x1, g2, tab)


def _gelu_gate_kernel(a_ref, g_ref, o_ref):
    a = a_ref[...]
    o_ref[...] = 0.5 * a * (1.0 + lax.erf(a * (2.0 ** -0.5))) * g_ref[...]


def _gelu_gate(a, gates, first):
    count = a.shape[0]
    return pl.pallas_call(
        _gelu_gate_kernel,
        grid=(count // PEER_TM,),
        in_specs=[pl.BlockSpec((PEER_TM, PEER_NSEL), lambda i: (i, 0)),
                  pl.BlockSpec((PEER_TM, PEER_NSEL), lambda i: (first // PEER_TM + i, 0))],
        out_specs=pl.BlockSpec((PEER_TM, PEER_NSEL), lambda i: (i, 0)),
        out_shape=jax.ShapeDtypeStruct((count, PEER_NSEL), jnp.float32),
        compiler_params=_cparams(("arbitrary",)),
        name="gelu_gate",
    )(a, gates)


def _pack_table(tab):
    n, d = tab.shape
    bits = lax.bitcast_convert_type(tab.astype(jnp.bfloat16), jnp.uint16).astype(jnp.uint32)
    packed = (bits[:, : d // 2] << 16) | bits[:, d // 2:]
    return packed.reshape(n * HALF_ROWS, LANES)


def _peer_specs(t, d):
    smem_blk = pl.BlockSpec((1, PEER_TM, PEER_NSEL), lambda i: (i, 0, 0), memory_space=pltpu.SMEM)
    tok_blk = pl.BlockSpec((PEER_TM, d), lambda i: (i, 0))
    sel_blk = pl.BlockSpec((PEER_TM, PEER_NSEL), lambda i: (i, 0))
    return smem_blk, tok_blk, sel_blk, pl.BlockSpec(memory_space=pltpu.VMEM)


def _after(x, prev):
    return x if prev is None else lax.optimization_barrier((x, prev))[0]


def _peer_u_phase(h2, experts, gates, u_packed, n_sc):
    t, d = h2.shape
    assert d == SUBLANES * LANES and d // 2 == HALF_ROWS * LANES and PEER_TM % PEER_TB == 0
    smem_blk, tok_blk, sel_blk, tab_spec = _peer_specs(t, d)
    parts = []
    n_tc = t - n_sc
    if n_tc:
        parts.append(pl.pallas_call(
            _peer_u_kernel,
            grid=(n_tc // PEER_TM,),
            in_specs=[smem_blk, tok_blk, sel_blk, tab_spec],
            out_specs=sel_blk,
            out_shape=jax.ShapeDtypeStruct((n_tc, PEER_NSEL), jnp.float32),
            scratch_shapes=[pltpu.VMEM((PEER_TB * PEER_NSEL * HALF_ROWS, LANES), jnp.float32)],
            compiler_params=_cparams(("arbitrary",)),
            name="peer_u",
        )(experts.reshape(t // PEER_TM, PEER_TM, PEER_NSEL), h2, gates, u_packed))
    a_sc = None
    if n_sc:
        a_sc = _sc_peer_u(experts // HALF_ROWS, h2, u_packed.reshape(-1, d // 2), n_tc, n_sc)
    return (parts[0] if parts else None), a_sc


def _finish_coef(coef_tc, a_sc, gates):
    parts = [] if coef_tc is None else [coef_tc]
    if a_sc is not None:
        parts.append(_gelu_gate(a_sc, gates, gates.shape[0] - a_sc.shape[0]))
    return parts[0] if len(parts) == 1 else jnp.concatenate(parts, axis=0)


def _peer_v_phase(coef, x1, g2, experts, v_packed, batch, n_sc):
    t, d = x1.shape
    per_b = t // PEER_TM // batch
    smem_blk, tok_blk, sel_blk, tab_spec = _peer_specs(t, d)
    parts = []
    n_tc = t - n_sc
    if n_tc:
        parts.append(pl.pallas_call(
            _peer_v_kernel,
            grid=(n_tc // PEER_TM,),
            in_specs=[smem_blk, sel_blk, tok_blk,
                      pl.BlockSpec((None, 1, d), lambda i: (i // per_b, 0, 0)),
                      tab_spec],
            out_specs=tok_blk,
            out_shape=jax.ShapeDtypeStruct((n_tc, d), jnp.float32),
            scratch_shapes=[pltpu.VMEM((PEER_TB * PEER_NSEL, LANES), jnp.float32)],
            compiler_params=_cparams(("arbitrary",)),
            name="peer_v",
        )(experts.reshape(t // PEER_TM, PEER_TM, PEER_NSEL), coef, x1, g2.reshape(batch, 1, d), v_packed))
    if n_sc:
        parts.append(_sc_peer_v(experts // HALF_ROWS, coef, x1, g2, v_packed.reshape(-1, d // 2),
                                n_tc, n_sc, t // batch))
    return parts


def kernel(x, c, ada_w, ada_b, norm1_g, norm2_g, w_in, a_qk_gain, a_lambda, a_sub_gain, b_q_gain, b_kv_gain, b_w_uv, w_out, peer_wq, peer_subkeys, peer_u, peer_v):
    b, s, d = x.shape
    t = b * s
    x2 = x.reshape(t, d)
    nchunk = len(PEER_PLAN)
    bc = b // nchunk
    tc = bc * s
    for l in range(ada_w.shape[0]):
        mod = _adaln(c, ada_w[l], ada_b[l])
        u_packed, v_packed = _pack_table(peer_u[l]), _pack_table(peer_v[l])
        lam_init = 0.8 - 0.6 * math.exp(-0.3 * l)
        outs = [None] * nchunk
        pending = {}
        prev_experts = None

        def issue_v(k, anchor):
            coef_tc, a_sc, gates, args = pending.pop(k)
            if a_sc is not None:
                anchor, a_sc = lax.optimization_barrier((anchor, a_sc))
            outs[k] = _peer_v_phase(_finish_coef(coef_tc, a_sc, gates), *args)
            return anchor

        def issue_planned(at, anchor):
            for k in [k for k in pending if PEER_PLAN[k][2] == at]:
                anchor = issue_v(k, anchor)
            return anchor

        for ck, (n_sc_u, n_sc_v, _) in enumerate(PEER_PLAN):
            xc = _after(x2[ck * tc:(ck + 1) * tc], prev_experts)
            sh1, sc1, g1, sh2, sc2, g2 = [mod[ck * bc:(ck + 1) * bc, i * d:(i + 1) * d] for i in range(6)]
            qa, ka, va, qb, kvb, qi, ki, wi = _inproj(xc, sh1, sc1, norm1_g[l], w_in[l], a_qk_gain[l],
                                                      b_q_gain[l], b_kv_gain[l], bc)
            oa = _diffattn(qa, ka, va, a_lambda[l], a_sub_gain[l], bc, lam_init)
            oa = issue_planned((ck, "diffattn"), oa)
            ob = _dsa(qi, wi, ki, qb, kvb, b_w_uv[l], bc)
            x1, h2, q = _outproj(xc, oa, ob, w_out[l], g1, sh2, sc2, norm2_g[l], peer_wq[l], bc)
            q = issue_planned((ck, "outproj"), q)
            experts, gates = _peer_select(q, peer_subkeys[l])
            prev_experts = experts
            coef_tc, a_sc = _peer_u_phase(h2, experts, gates, u_packed, n_sc_u)
            pending[ck] = (coef_tc, a_sc, gates, (x1, g2, experts, v_packed, bc, n_sc_v))
        anchor = pending[nchunk - 1][0]
        for k in sorted(pending):
            anchor = issue_v(k, anchor)
        x2 = jnp.concatenate([p for o in outs for p in o], axis=0)
    return x2.reshape(b, s, d)
```

```python
import functools
import math

import jax
import jax.numpy as jnp
import numpy as np
from jax import lax
from jax.experimental import pallas as pl
from jax.experimental.pallas import tpu as pltpu
from jax.experimental.pallas import tpu_sc as plsc

CHUNK = 64
A_HEADS, A_QK_DIM, A_V_DIM = 4, 64, 128
B_HEADS, B_LAT, B_V_DIM = 8, 128, 64
IDX_HEADS, IDX_DIM = 4, 64
DSA_TOPK_MAX = 256
A_Q = A_HEADS * 2 * A_QK_DIM
A_K = A_Q
A_V = A_HEADS * A_V_DIM
B_Q = B_HEADS * B_LAT
B_KV = B_LAT
I_Q = IDX_HEADS * IDX_DIM
I_K = IDX_DIM
I_W = IDX_HEADS
OFF_AQ, OFF_AK, OFF_AV = 0, A_Q, A_Q + A_K
OFF_BQ = OFF_AV + A_V
OFF_KV = OFF_BQ + B_Q
OFF_IQ = OFF_KV + B_KV
OFF_IK = OFF_IQ + I_Q
OFF_IW = OFF_IK + I_K
IN_COLS = OFF_IW + I_W
PEER_HEADS, PEER_NKEYS, PEER_QDIM, PEER_TOPK = 8, 128, 128, 16
EPS = 1e-6
NEG = -1e30
INT_MIN = -(2 ** 31)

SUBLANES = 8
LANES = 128
VMEM_LIMIT = 56 * 1024 * 1024

PROJ_TM = 256
A_TQ = 512
B_TQ = 256
PEER_TM = 128
PEER_TB = 8
PEER_NSEL = PEER_HEADS * PEER_TOPK
PEER_PLAN = ((8192, 8192, (1, "outproj")), (8192, 8192, (3, "diffattn")), (8192, 5376, None), (0, 0, None))
HALF_ROWS = 4


def _split_bf16(x):
    hi = x.astype(jnp.bfloat16)
    lo = (x - hi.astype(jnp.float32)).astype(jnp.bfloat16)
    return hi, lo


def _dot(a, b):
    return jnp.dot(a, b, preferred_element_type=jnp.float32)


def _dot_nt(a, b):
    return lax.dot_general(a, b, (((1,), (1,)), ((), ())), preferred_element_type=jnp.float32)


def _cparams(sem):
    return pltpu.CompilerParams(dimension_semantics=sem, vmem_limit_bytes=VMEM_LIMIT)


def _adaln_kernel(c_ref, w_ref, b_ref, o_ref):
    cf = c_ref[...]
    a = cf * (1.0 / (1.0 + jnp.exp(-cf)))
    a_hi, a_lo = _split_bf16(a)
    w_hi, w_lo = _split_bf16(w_ref[...])
    o_ref[...] = _dot(a_hi, w_hi) + _dot(a_hi, w_lo) + _dot(a_lo, w_hi) + b_ref[...]


def _adaln(c, w, b):
    bsz, d = c.shape
    n = w.shape[1]
    tn = 1024
    return pl.pallas_call(
        _adaln_kernel,
        grid=(n // tn,),
        in_specs=[pl.BlockSpec((bsz, d), lambda j: (0, 0)),
                  pl.BlockSpec((d, tn), lambda j: (0, j)),
                  pl.BlockSpec((1, tn), lambda j: (0, j))],
        out_specs=pl.BlockSpec((bsz, tn), lambda j: (0, j)),
        out_shape=jax.ShapeDtypeStruct((bsz, n), jnp.float32),
        compiler_params=_cparams(("arbitrary",)),
        name="adaln",
    )(c, w, b.reshape(1, n))


def _rms_rows(x, g):
    return x * lax.rsqrt(jnp.mean(x * x, axis=-1, keepdims=True) + EPS) * g


def _group_norm_block(p, gmat, gain, n):
    hi, lo = _split_bf16(p * p)
    ss = _dot(hi, gmat) + _dot(lo, gmat)
    return p * lax.rsqrt(ss * (1.0 / n) + EPS) * gain


def _inproj_kernel(x_ref, sh_ref, sc_ref, g_ref, w_ref, gq_ref, gk_ref, gbq_ref, gkv_ref,
                   qa_ref, ka_ref, va_ref, qb_ref, kv_ref, qi_ref, ki_ref, wi_ref):
    h = _rms_rows(x_ref[...], g_ref[...]) * (1.0 + sc_ref[0]) + sh_ref[0]
    hb = h.astype(jnp.bfloat16)
    row = lax.broadcasted_iota(jnp.int32, (LANES, LANES), 0)
    col = lax.broadcasted_iota(jnp.int32, (LANES, LANES), 1)
    g64 = jnp.where((row // A_QK_DIM) == (col // A_QK_DIM), 1.0, 0.0).astype(jnp.bfloat16)
    g128 = jnp.ones((LANES, LANES), jnp.bfloat16)

    def proj(off, width):
        return _dot(hb, w_ref[:, off:off + width])

    for blk in range(A_Q // LANES):
        sl = slice(blk * LANES, (blk + 1) * LANES)
        p = proj(OFF_AQ + blk * LANES, LANES)
        qa_ref[:, sl] = (_group_norm_block(p, g64, gq_ref[...], A_QK_DIM) * (A_QK_DIM ** -0.5)).astype(qa_ref.dtype)
        p = proj(OFF_AK + blk * LANES, LANES)
        ka_ref[:, sl] = _group_norm_block(p, g64, gk_ref[...], A_QK_DIM).astype(ka_ref.dtype)
    va_ref[...] = proj(OFF_AV, A_V).astype(va_ref.dtype)
    for blk in range(B_HEADS):
        sl = slice(blk * LANES, (blk + 1) * LANES)
        p = proj(OFF_BQ + blk * LANES, LANES)
        qb_ref[:, sl] = _group_norm_block(p, g128, gbq_ref[...], B_LAT).astype(qb_ref.dtype)
    p = proj(OFF_KV, B_KV)
    kv_ref[...] = _group_norm_block(p, g128, gkv_ref[...], B_LAT).astype(kv_ref.dtype)
    qi_ref[...] = proj(OFF_IQ, I_Q).astype(qi_ref.dtype)
    tail = proj(OFF_IK, 2 * LANES)
    ki_ref[...] = tail[:, :LANES].astype(ki_ref.dtype)
    wi_ref[...] = tail[:, I_K:I_K + LANES]


def _inproj(x2, sh1, sc1, g1n, w_in, a_qk_gain, b_q_gain, b_kv_gain, batch):
    t, d = x2.shape
    per_b = (t // batch) // PROJ_TM
    wpad = OFF_IK + 2 * LANES
    w = jnp.zeros((d, wpad), jnp.bfloat16).at[:, :IN_COLS].set(w_in.astype(jnp.bfloat16))
    gq = jnp.tile(a_qk_gain[0], 2).reshape(1, LANES)
    gk = jnp.tile(a_qk_gain[1], 2).reshape(1, LANES)
    tok = lambda width: pl.BlockSpec((PROJ_TM, width), lambda i: (i, 0))
    vec = lambda width: pl.BlockSpec((1, width), lambda i: (0, 0))
    per_batch = pl.BlockSpec((1, 1, d), lambda i: (i // per_b, 0, 0))
    bf = jnp.bfloat16
    outs = [(A_Q, bf), (A_K, bf), (A_V, bf), (B_Q, bf), (B_KV, bf), (I_Q, bf), (LANES, bf), (LANES, jnp.float32)]
    return pl.pallas_call(
        _inproj_kernel,
        grid=(t // PROJ_TM,),
        in_specs=[tok(d), per_batch, per_batch, vec(d),
                  pl.BlockSpec((d, wpad), lambda i: (0, 0)),
                  vec(LANES), vec(LANES), vec(LANES), vec(LANES)],
        out_specs=[tok(wd) for wd, _ in outs],
        out_shape=[jax.ShapeDtypeStruct((t, wd), dt) for wd, dt in outs],
        compiler_params=_cparams(("arbitrary",)),
        name="inproj",
    )(x2, sh1.reshape(batch, 1, d), sc1.reshape(batch, 1, d), g1n.reshape(1, d), w,
      gq, gk, b_q_gain.reshape(1, LANES), b_kv_gain.reshape(1, LANES))


def _positions(q0, tq, s_len):
    qpos = q0 + lax.broadcasted_iota(jnp.int32, (tq, s_len), 0)
    kpos = lax.broadcasted_iota(jnp.int32, (tq, s_len), 1)
    cend = (qpos // CHUNK + 1) * CHUNK
    return qpos, kpos, cend


def _softmax_rows(s):
    m = jnp.max(s, axis=-1, keepdims=True)
    p = jnp.exp(s - m)
    return p, jnp.sum(p, axis=-1, keepdims=True)


def _diffattn_kernel(q_ref, k_ref, v_ref, lam_ref, gain_ref, o_ref, *, lam_init, q_base):
    tq, s_len = q_ref.shape[0], k_ref.shape[0]
    q0 = q_base + pl.program_id(1) * tq
    qpos, kpos, cend = _positions(q0, tq, s_len)
    dist = jnp.abs(qpos - kpos).astype(jnp.float32)
    negmask = jnp.where(kpos < cend, 0.0, NEG)
    lf = lam_ref[...]
    lam = (jnp.exp(jnp.sum(lf[0:1] * lf[1:2], axis=-1, keepdims=True))
           - jnp.exp(jnp.sum(lf[2:3] * lf[3:4], axis=-1, keepdims=True)) + lam_init)
    for h in range(A_HEADS):
        slope = 2.0 ** (-8.0 * (h + 1) / A_HEADS)
        bias = negmask - slope * dist
        v = v_ref[:, h * A_V_DIM:(h + 1) * A_V_DIM]
        outs = []
        for m in range(2):
            c0 = (h * 2 + m) * A_QK_DIM
            s = _dot_nt(q_ref[:, c0:c0 + A_QK_DIM], k_ref[:, c0:c0 + A_QK_DIM]) + bias
            p, l = _softmax_rows(s)
            outs.append(_dot(p.astype(jnp.bfloat16), v) / l)
        o = outs[0] - lam * outs[1]
        o = _rms_rows(o, gain_ref[...]) * (1.0 - lam_init)
        o_ref[:, h * A_V_DIM:(h + 1) * A_V_DIM] = o.astype(o_ref.dtype)


def _causal_groups(s_len, tq):
    span = min(tq, s_len)
    tiles = span // tq
    return [(g * tiles, tiles, (g + 1) * span) for g in range(s_len // span)]


def _qblk(tq, first_tile, width):
    return pl.BlockSpec((None, tq, width), lambda b, i: (b, first_tile + i, 0))


def _kblk(klen, width):
    return pl.BlockSpec((None, klen, width), lambda b, i: (b, 0, 0))


def _oblk(tq, width):
    return pl.BlockSpec((None, tq, width), lambda b, i: (b, i, 0))


def _diffattn(qa, ka, va, a_lambda, a_sub_gain, batch, lam_init):
    t = qa.shape[0]
    s_len = t // batch
    qa3, ka3, va3 = (a.reshape(batch, s_len, a.shape[1]) for a in (qa, ka, va))
    outs = []
    for first, tiles, klen in _causal_groups(s_len, A_TQ):
        outs.append(pl.pallas_call(
            functools.partial(_diffattn_kernel, lam_init=lam_init, q_base=first * A_TQ),
            grid=(batch, tiles),
            in_specs=[_qblk(A_TQ, first, A_Q), _kblk(klen, A_K), _kblk(klen, A_V),
                      pl.BlockSpec((4, A_QK_DIM), lambda b, i: (0, 0)),
                      pl.BlockSpec((1, A_V_DIM), lambda b, i: (0, 0))],
            out_specs=_oblk(A_TQ, A_V),
            out_shape=jax.ShapeDtypeStruct((batch, tiles * A_TQ, A_V), jnp.bfloat16),
            compiler_params=_cparams(("arbitrary", "arbitrary")),
            name="diffattn",
        )(qa3, ka3, va3, a_lambda, a_sub_gain.reshape(1, A_V_DIM)))
    return jnp.concatenate(outs, axis=1).reshape(t, A_V)


def _count(mask):
    return jnp.sum(jnp.where(mask, 1.0, 0.0), axis=-1, keepdims=True)


def _topk_mask(score, allowed, kpos, topk, s_len):
    bits = pltpu.bitcast(score + 0.0, jnp.int32)
    key = jnp.where(bits < 0, bits ^ jnp.int32(0x7FFFFFFF), bits)
    key = jnp.where(allowed, key, jnp.int32(INT_MIN))
    kf = float(topk)
    thr = jnp.where(_count(key >= 0) >= kf, jnp.int32(0), jnp.int32(INT_MIN))

    def value_bit(i, thr):
        cand = thr | (jnp.int32(1) << (30 - i))
        return jnp.where(_count(key >= cand) >= kf, cand, thr)

    thr = lax.fori_loop(0, 31, value_bit, thr)
    above = key > thr
    tie = key == thr
    need = kf - _count(above)
    nbits = (s_len - 1).bit_length()

    def pos_bit(i, pos):
        cand = pos | (jnp.int32(1) << (nbits - 1 - i))
        return jnp.where(_count(tie & (kpos < cand)) < need, cand, pos)

    contested = jnp.max(jnp.where(_count(tie & allowed) > need, 1, 0)) > 0
    start = jnp.where(contested, 0, (1 << nbits) - 1)
    pos = lax.fori_loop(0, jnp.where(contested, nbits, 0), pos_bit, jnp.zeros_like(thr) + start)
    return allowed & (above | (tie & (kpos <= pos)))


def _dsa_kernel(qi_ref, wi_ref, ki_ref, qb_ref, kv_ref, wuv_ref, o_ref, *, topk, q_base):
    tq, s_len = qb_ref.shape[0], kv_ref.shape[0]
    q0 = q_base + pl.program_id(1) * tq
    qpos, kpos, cend = _positions(q0, tq, s_len)
    allowed = kpos < cend
    ki = ki_ref[...]
    wi = wi_ref[...]
    zpad = jnp.zeros((tq, LANES - IDX_DIM), jnp.bfloat16)
    score = jnp.zeros((tq, s_len), jnp.float32)
    for h in range(IDX_HEADS):
        qh = jnp.concatenate([qi_ref[:, h * IDX_DIM:(h + 1) * IDX_DIM], zpad], axis=-1)
        score = score + jnp.maximum(_dot_nt(qh, ki), 0.0) * wi[:, h:h + 1]
    score = score * ((IDX_DIM ** -0.5) * (IDX_HEADS ** -0.5))
    sel = _topk_mask(score, allowed, kpos, topk, s_len)

    dist = jnp.abs(qpos - kpos).astype(jnp.float32)
    negmask = jnp.where(sel, 0.0, NEG)
    kv = kv_ref[...]
    for h in range(B_HEADS):
        slope = 2.0 ** (-8.0 * (h + 1) / B_HEADS)
        s = _dot_nt(qb_ref[:, h * B_LAT:(h + 1) * B_LAT], kv) * (B_LAT ** -0.5) + (negmask - slope * dist)
        p, l = _softmax_rows(s)
        o_lat = _dot(p.astype(jnp.bfloat16), kv) / l
        o = _dot(o_lat.astype(jnp.bfloat16), wuv_ref[h])
        o_ref[:, h * B_V_DIM:(h + 1) * B_V_DIM] = o.astype(o_ref.dtype)


def _dsa(qi, wi, ki, qb, kvb, w_uv, batch):
    t = qb.shape[0]
    s_len = t // batch
    topk = min(DSA_TOPK_MAX, s_len // 4)
    width = B_HEADS * B_V_DIM
    qi3, wi3, ki3, qb3, kv3 = (a.reshape(batch, s_len, a.shape[1]) for a in (qi, wi, ki, qb, kvb))
    wuv = w_uv.astype(jnp.bfloat16)
    outs = []
    for first, tiles, klen in _causal_groups(s_len, B_TQ):
        outs.append(pl.pallas_call(
            functools.partial(_dsa_kernel, topk=topk, q_base=first * B_TQ),
            grid=(batch, tiles),
            in_specs=[_qblk(B_TQ, first, I_Q), _qblk(B_TQ, first, LANES), _kblk(klen, LANES),
                      _qblk(B_TQ, first, B_Q), _kblk(klen, B_KV),
                      pl.BlockSpec((B_HEADS, B_LAT, B_V_DIM), lambda b, i: (0, 0, 0))],
            out_specs=_oblk(B_TQ, width),
            out_shape=jax.ShapeDtypeStruct((batch, tiles * B_TQ, width), jnp.bfloat16),
            compiler_params=_cparams(("arbitrary", "arbitrary")),
            name="dsa",
        )(qi3, wi3, ki3, qb3, kv3, wuv))
    return jnp.concatenate(outs, axis=1).reshape(t, width)


def _outproj_kernel(x_ref, oa_ref, ob_ref, wo_ref, g1_ref, sh_ref, sc_ref, gn_ref, wq_ref,
                    x1_ref, h2_ref, q_ref):
    na = oa_ref.shape[1]
    y = _dot(oa_ref[...], wo_ref[:na, :]) + _dot(ob_ref[...], wo_ref[na:, :])
    x1 = x_ref[...] + g1_ref[0] * y
    x1_ref[...] = x1
    h2 = _rms_rows(x1, gn_ref[...]) * (1.0 + sc_ref[0]) + sh_ref[0]
    h2_ref[...] = h2
    q_ref[...] = _dot(h2.astype(jnp.bfloat16), wq_ref[...]).astype(q_ref.dtype)


def _outproj(x2, oa, ob, w_out, g1, sh2, sc2, g2n, peer_wq, batch):
    t, d = x2.shape
    per_b = (t // batch) // PROJ_TM
    nq = peer_wq.shape[1]
    tok = lambda width: pl.BlockSpec((PROJ_TM, width), lambda i: (i, 0))
    per_batch = pl.BlockSpec((1, 1, d), lambda i: (i // per_b, 0, 0))
    full = lambda a, b: pl.BlockSpec((a, b), lambda i: (0, 0))
    return pl.pallas_call(
        _outproj_kernel,
        grid=(t // PROJ_TM,),
        in_specs=[tok(d), tok(oa.shape[1]), tok(ob.shape[1]), full(w_out.shape[0], d),
                  per_batch, per_batch, per_batch, full(1, d), full(d, nq)],
        out_specs=[tok(d), tok(d), tok(nq)],
        out_shape=[jax.ShapeDtypeStruct((t, d), jnp.float32),
                   jax.ShapeDtypeStruct((t, d), jnp.float32),
                   jax.ShapeDtypeStruct((t, nq), jnp.bfloat16)],
        compiler_params=_cparams(("arbitrary",)),
        name="outproj",
    )(x2, oa, ob, w_out.astype(jnp.bfloat16), g1.reshape(batch, 1, d), sh2.reshape(batch, 1, d),
      sc2.reshape(batch, 1, d), g2n.reshape(1, d), peer_wq.astype(jnp.bfloat16))


def _extract_topk(vals, payload, k):
    n, tm = vals.shape
    rows = lax.broadcasted_iota(jnp.int32, (n, tm), 0).astype(jnp.float32)
    top_v, top_p = [], []
    for _ in range(k):
        m = jnp.max(vals, axis=0, keepdims=True)
        pos = jnp.min(jnp.where(vals == m, rows, float(n)), axis=0, keepdims=True)
        hit = rows == pos
        top_v.append(m)
        top_p.append(pos if payload is None else jnp.max(jnp.where(hit, payload, -1.0), axis=0, keepdims=True))
        vals = jnp.where(hit, -jnp.inf, vals)
    return jnp.concatenate(top_v, axis=0), jnp.concatenate(top_p, axis=0)


def _stair_pairs():
    return [(a, b) for a in range(PEER_TOPK) for b in range(PEER_TOPK) if (a + 1) * (b + 1) <= PEER_TOPK]


N_STAIR = len(_stair_pairs())
N_STAIR_PAD = -(-N_STAIR // SUBLANES) * SUBLANES


def _copy_rows(sel, x):
    x0 = x.astype(jnp.bfloat16)
    r1 = x - x0.astype(jnp.float32)
    x1 = r1.astype(jnp.bfloat16)
    x2 = (r1 - x1.astype(jnp.float32)).astype(jnp.bfloat16)
    return _dot(sel, x0) + _dot(sel, x1) + _dot(sel, x2)


def _peer_select_kernel(q_ref, sk_ref, sela_ref, selb_ref, e_ref, g_ref):
    tm = q_ref.shape[0]
    sela, selb = sela_ref[...], selb_ref[...]
    pad_row = lax.broadcasted_iota(jnp.int32, (N_STAIR_PAD, tm), 0) >= N_STAIR
    e_rows, g_rows = [], []
    for h in range(PEER_HEADS):
        sub = _dot_nt(sk_ref[...], q_ref[:, h * PEER_QDIM:(h + 1) * PEER_QDIM])
        sv0, si0 = _extract_topk(sub[:PEER_NKEYS], None, PEER_TOPK)
        sv1, si1 = _extract_topk(sub[PEER_NKEYS:], None, PEER_TOPK)
        cand = jnp.where(pad_row, -jnp.inf, _copy_rows(sela, sv0) + _copy_rows(selb, sv1))
        ids = (_dot(sela, si0.astype(jnp.bfloat16)) * float(PEER_NKEYS)
               + _dot(selb, si1.astype(jnp.bfloat16))) * float(HALF_ROWS)
        top_s, top_e = _extract_topk(cand, ids, PEER_TOPK)
        p = jnp.exp(top_s - top_s[0:1])
        e_rows.append(top_e)
        g_rows.append(p / jnp.sum(p, axis=0, keepdims=True))
    e_t = jnp.concatenate(e_rows, axis=0)
    g_t = jnp.concatenate(g_rows, axis=0)
    e_ref[...] = e_t.T.astype(jnp.int32)
    g_ref[...] = g_t.T


def _peer_select(q, sub_keys):
    t, nq = q.shape
    half = PEER_QDIM // 2
    sk = jnp.zeros((2 * PEER_NKEYS, PEER_QDIM), jnp.bfloat16)
    sk = sk.at[:PEER_NKEYS, :half].set(sub_keys[0].astype(jnp.bfloat16))
    sk = sk.at[PEER_NKEYS:, half:].set(sub_keys[1].astype(jnp.bfloat16))
    sel = np.zeros((2, N_STAIR_PAD, PEER_TOPK), np.float32)
    for r, (a, b) in enumerate(_stair_pairs()):
        sel[0, r, a] = 1.0
        sel[1, r, b] = 1.0
    sel = jnp.asarray(sel, jnp.bfloat16)
    tok = lambda width: pl.BlockSpec((PEER_TM, width), lambda i: (i, 0))
    full = lambda a, b: pl.BlockSpec((a, b), lambda i: (0, 0))
    return pl.pallas_call(
        _peer_select_kernel,
        grid=(t // PEER_TM,),
        in_specs=[tok(nq), full(2 * PEER_NKEYS, PEER_QDIM),
                  full(N_STAIR_PAD, PEER_TOPK), full(N_STAIR_PAD, PEER_TOPK)],
        out_specs=[tok(PEER_NSEL), tok(PEER_NSEL)],
        out_shape=[jax.ShapeDtypeStruct((t, PEER_NSEL), jnp.int32),
                   jax.ShapeDtypeStruct((t, PEER_NSEL), jnp.float32)],
        compiler_params=_cparams(("arbitrary",)),
        name="peer_select",
    )(q, sk, sel[0], sel[1])


def _gather_row(tab_ref, row):
    w = tab_ref[pl.ds(pl.multiple_of(row, HALF_ROWS), HALF_ROWS), :]
    hi = pltpu.bitcast(w & jnp.uint32(0xFFFF0000), jnp.float32)
    lo = pltpu.bitcast(w << 16, jnp.float32)
    return hi, lo


def _peer_u_kernel(idx_ref, h_ref, gate_ref, tab_ref, coef_ref, abuf_ref):
    ones = jnp.ones((2 * LANES, LANES), jnp.bfloat16)
    lane = lax.broadcasted_iota(jnp.int32, (PEER_NSEL, PEER_TM), 1)
    rows_per_tok = PEER_NSEL * HALF_ROWS

    def batch(b, a_t):
        def token(tb, c2):
            t = b * PEER_TB + tb
            x8 = h_ref[pl.ds(t, 1), :].reshape(SUBLANES, LANES)
            xh, xl = x8[:HALF_ROWS], x8[HALF_ROWS:]
            off = pl.multiple_of(tb * rows_per_tok, rows_per_tok)
            for j in range(PEER_NSEL):
                hi, lo = _gather_row(tab_ref, idx_ref[0, t, j])
                abuf_ref[pl.ds(off + j * HALF_ROWS, HALF_ROWS), :] = hi * xh + lo * xl
            return c2

        lax.fori_loop(0, PEER_TB, token, 0)
        parts = []
        for tb in range(PEER_TB):
            ab = abuf_ref[pl.ds(tb * rows_per_tok, PEER_NSEL, stride=HALF_ROWS), :]
            for s in range(1, HALF_ROWS):
                ab = ab + abuf_ref[pl.ds(tb * rows_per_tok + s, PEER_NSEL, stride=HALF_ROWS), :]
            parts.append(ab)
        ab_hi, ab_lo = _split_bf16(jnp.concatenate(parts, axis=0))
        r = _dot(jnp.concatenate([ab_hi, ab_lo], axis=1), ones)
        for tb in range(PEER_TB):
            a_t = jnp.where(lane == b * PEER_TB + tb, r[tb * PEER_NSEL:(tb + 1) * PEER_NSEL], a_t)
        return a_t

    a_t = lax.fori_loop(0, PEER_TM // PEER_TB, batch, jnp.zeros((PEER_NSEL, PEER_TM), jnp.float32))
    a = a_t.T
    coef_ref[...] = 0.5 * a * (1.0 + lax.erf(a * (2.0 ** -0.5))) * gate_ref[...]


def _peer_v_kernel(idx_ref, coef_ref, x_ref, g2_ref, tab_ref, out_ref, cb_ref):
    g2 = g2_ref[...]
    ones = jnp.ones((2 * LANES, LANES), jnp.bfloat16)
    eye = (lax.broadcasted_iota(jnp.int32, (PEER_NSEL, LANES), 0)
           == lax.broadcasted_iota(jnp.int32, (PEER_NSEL, LANES), 1))
    nacc = 2

    def batch(b, carry):
        for tb in range(PEER_TB):
            row = coef_ref[pl.ds(b * PEER_TB + tb, 1), :]
            d_hi, d_lo = _split_bf16(jnp.where(eye, jnp.broadcast_to(row, (PEER_NSEL, LANES)), 0.0))
            cb_ref[pl.ds(tb * PEER_NSEL, PEER_NSEL), :] = _dot(jnp.concatenate([d_hi, d_lo], axis=1), ones)

        def token(tb, c2):
            t = b * PEER_TB + tb
            zero = jnp.zeros((HALF_ROWS, LANES), jnp.float32)
            acc_h = [zero] * nacc
            acc_l = [zero] * nacc
            cbase = pl.multiple_of(tb * PEER_NSEL, PEER_NSEL)
            for j in range(PEER_NSEL):
                hi, lo = _gather_row(tab_ref, idx_ref[0, t, j])
                c = jnp.broadcast_to(cb_ref[pl.ds(cbase + j, 1), :], (HALF_ROWS, LANES))
                acc_h[j % nacc] = acc_h[j % nacc] + c * hi
                acc_l[j % nacc] = acc_l[j % nacc] + c * lo
            y8 = jnp.concatenate([acc_h[0] + acc_h[1], acc_l[0] + acc_l[1]], axis=0)
            out_ref[pl.ds(t, 1), :] = x_ref[pl.ds(t, 1), :] + g2 * y8.reshape(1, SUBLANES * LANES)
            return c2

        lax.fori_loop(0, PEER_TB, token, 0)
        return carry

    lax.fori_loop(0, PEER_TM // PEER_TB, batch, 0)


def _sc_peer_u(expert_ids, h2, tab, first, count):
    info = plsc.get_sparse_core_info()
    lanes, workers = info.num_lanes, info.num_cores * info.num_subcores
    half = tab.shape[1]
    per_w = count // workers
    nhalf = PEER_NSEL // 2
    assert per_w * workers == count and per_w >= 2 and nhalf % lanes == 0 and half % lanes == 0
    rows_blk = 4
    kunroll = 2
    mesh = plsc.VectorSubcoreMesh(core_axis_name="c", subcore_axis_name="s")

    @functools.partial(
        pl.kernel, mesh=mesh, name="peer_u_sc",
        compiler_params=pltpu.CompilerParams(needs_layout_passes=False),
        out_type=jax.ShapeDtypeStruct((count, PEER_NSEL), jnp.float32),
        scratch_types=[pltpu.VMEM((2, PEER_NSEL), jnp.int32),
                       pltpu.VMEM((nhalf, half), jnp.uint32), pltpu.VMEM((nhalf, half), jnp.uint32),
                       pltpu.VMEM((2, 2 * half), jnp.float32), pltpu.VMEM((2, PEER_NSEL), jnp.float32),
                       pltpu.SemaphoreType.DMA, pltpu.SemaphoreType.DMA, pltpu.SemaphoreType.DMA,
                       pltpu.SemaphoreType.DMA, pltpu.SemaphoreType.DMA((2,))])
    def run(idx_hbm, h_hbm, tab_hbm, out_hbm, idx_v, rows_a, rows_b, x_v, a_v, sem_a, sem_b, sem_i, sem_x, sem_o):
        wid = lax.axis_index("s") * info.num_cores + lax.axis_index("c")
        base = first + wid * per_w
        lane = lax.iota(jnp.int32, lanes)
        mask_hi = jnp.full((lanes,), 0xFFFF0000, jnp.uint32)

        def gather(p, part, rows, sem):
            return pltpu.make_async_copy(tab_hbm.at[idx_v.at[p, pl.ds(part * nhalf, nhalf)]], rows, sem)

        def load_idx(t, p):
            return pltpu.make_async_copy(idx_hbm.at[t], idx_v.at[p], sem_i)

        def load_x(t, p):
            return pltpu.make_async_copy(h_hbm.at[t], x_v.at[p], sem_x)

        def store_a(t, p):
            return pltpu.make_async_copy(a_v.at[p], out_hbm.at[t - first], sem_o.at[p])

        def compute(rows, p, part):
            def group(g, c2):
                vec = jnp.zeros((lanes,), jnp.float32)
                for rb in range(lanes // rows_blk):
                    def kslice(kk, accs):
                        accs = list(accs)
                        for ku in range(kunroll):
                            off = pl.multiple_of((kk * kunroll + ku) * lanes, lanes)
                            xh = x_v[p, pl.ds(off, lanes)]
                            xl = x_v[p, pl.ds(half + off, lanes)]
                            for r in range(rows_blk):
                                w = rows[g * lanes + rb * rows_blk + r, pl.ds(off, lanes)]
                                hi = plsc.bitcast(w & mask_hi, jnp.float32)
                                lo = plsc.bitcast(w << 16, jnp.float32)
                                accs[r] = accs[r] + hi * xh + lo * xl
                        return tuple(accs)

                    zero = jnp.zeros((lanes,), jnp.float32)
                    accs = lax.fori_loop(0, half // lanes // kunroll, kslice, (zero,) * rows_blk)
                    for r in range(rows_blk):
                        vec = jnp.where(lane == rb * rows_blk + r, jnp.sum(accs[r]), vec)
                a_v[p, pl.ds(pl.multiple_of(part * nhalf + g * lanes, lanes), lanes)] = vec
                return c2

            lax.fori_loop(0, nhalf // lanes, group, 0)

        load_idx(base, 0).start()
        load_x(base, 0).start()
        load_idx(base, 0).wait()
        gather(0, 0, rows_a, sem_a).start()
        gather(0, 1, rows_b, sem_b).start()
        load_x(base, 0).wait()

        def token(i, carry):
            t = base + i
            p = lax.rem(i, 2)
            q = 1 - p
            tn = jnp.minimum(t + 1, base + per_w - 1)
            load_idx(tn, q).start()
            load_x(tn, q).start()

            @pl.when(i >= 2)
            def _():
                store_a(t, p).wait()

            gather(p, 0, rows_a, sem_a).wait()
            compute(rows_a, p, 0)
            load_idx(tn, q).wait()
            gather(q, 0, rows_a, sem_a).start()
            gather(p, 1, rows_b, sem_b).wait()
            compute(rows_b, p, 1)
            gather(q, 1, rows_b, sem_b).start()
            load_x(tn, q).wait()
            store_a(t, p).start()
            return carry

        lax.fori_loop(0, per_w, token, 0)
        gather(0, 0, rows_a, sem_a).wait()
        gather(0, 1, rows_b, sem_b).wait()
        store_a(base, 0).wait()
        store_a(base, 1).wait()

    return run(expert_ids, h2, tab)


def _sc_peer_v(expert_ids, coef, x1, g2, tab, first, count, per_batch):
    info = plsc.get_sparse_core_info()
    lanes, workers = info.num_lanes, info.num_cores * info.num_subcores
    half = tab.shape[1]
    d = 2 * half
    per_w = count // workers
    nhalf = PEER_NSEL // 2
    sblk = 16
    runroll = 1
    assert per_w * workers == count and per_w >= 2 and nhalf % runroll == 0 and half % (lanes * sblk) == 0
    mesh = plsc.VectorSubcoreMesh(core_axis_name="c", subcore_axis_name="s")
    dma = pltpu.SemaphoreType.DMA

    @functools.partial(
        pl.kernel, mesh=mesh, name="peer_v_sc",
        compiler_params=pltpu.CompilerParams(needs_layout_passes=False),
        out_type=jax.ShapeDtypeStruct((count, d), jnp.float32),
        scratch_types=[pltpu.VMEM((2, PEER_NSEL), jnp.int32), pltpu.VMEM((2 * PEER_NSEL,), jnp.float32),
                       pltpu.VMEM((nhalf, half), jnp.uint32), pltpu.VMEM((nhalf, half), jnp.uint32),
                       pltpu.VMEM((2, d), jnp.float32), pltpu.VMEM((2, d), jnp.float32),
                       pltpu.VMEM((d,), jnp.float32), pltpu.VMEM((2, d), jnp.float32),
                       dma, dma, dma, dma, dma, dma, dma((2,))])
    def run(idx_hbm, coef_hbm, x_hbm, g_hbm, tab_hbm, out_hbm,
            idx_v, coef_v, rows_a, rows_b, x_v, g_v, y_v, o_v, sem_a, sem_b, sem_i, sem_c, sem_x, sem_g, sem_o):
        wid = lax.axis_index("s") * info.num_cores + lax.axis_index("c")
        base = first + wid * per_w
        mask_hi = jnp.full((lanes,), 0xFFFF0000, jnp.uint32)

        def gather(p, part, rows, sem):
            return pltpu.make_async_copy(tab_hbm.at[idx_v.at[p, pl.ds(part * nhalf, nhalf)]], rows, sem)

        def loads(t, p):
            cslot = coef_v.at[pl.ds(pl.multiple_of(p * PEER_NSEL, PEER_NSEL), PEER_NSEL)]
            return (pltpu.make_async_copy(idx_hbm.at[t], idx_v.at[p], sem_i),
                    pltpu.make_async_copy(coef_hbm.at[t], cslot, sem_c),
                    pltpu.make_async_copy(x_hbm.at[t], x_v.at[p], sem_x),
                    pltpu.make_async_copy(g_hbm.at[t // per_batch], g_v.at[p], sem_g))

        def store_o(t, p):
            return pltpu.make_async_copy(o_v.at[p], out_hbm.at[t - first], sem_o.at[p])

        def compute(rows, p, part):
            cbase = p * PEER_NSEL + part * nhalf
            for sb in range(half // lanes // sblk):
                offs = [(sb * sblk + s) * lanes for s in range(sblk)]
                if part == 0:
                    init = tuple(jnp.zeros((lanes,), jnp.float32) for _ in range(2 * sblk))
                else:
                    init = tuple([y_v[pl.ds(o, lanes)] for o in offs] + [y_v[pl.ds(half + o, lanes)] for o in offs])

                def rowloop(rr, accs):
                    accs = list(accs)
                    for ru in range(runroll):
                        r = rr * runroll + ru
                        c = plsc.load_gather(coef_v, [jnp.full((lanes,), cbase + r, jnp.int32)])
                        for s in range(sblk):
                            w = rows[r, pl.ds(offs[s], lanes)]
                            hi = plsc.bitcast(w & mask_hi, jnp.float32)
                            lo = plsc.bitcast(w << 16, jnp.float32)
                            accs[s] = accs[s] + c * hi
                            accs[sblk + s] = accs[sblk + s] + c * lo
                    return tuple(accs)

                accs = lax.fori_loop(0, nhalf // runroll, rowloop, init)
                for s in range(sblk):
                    for hl, o in ((0, offs[s]), (1, half + offs[s])):
                        if part == 0:
                            y_v[pl.ds(o, lanes)] = accs[hl * sblk + s]
                        else:
                            o_v[p, pl.ds(o, lanes)] = (x_v[p, pl.ds(o, lanes)]
                                                       + g_v[p, pl.ds(o, lanes)] * accs[hl * sblk + s])

        head = loads(base, 0)
        for c in head:
            c.start()
        head[0].wait()
        gather(0, 0, rows_a, sem_a).start()
        gather(0, 1, rows_b, sem_b).start()
        for c in head[1:]:
            c.wait()

        def token(i, carry):
            t = base + i
            p = lax.rem(i, 2)
            q = 1 - p
            tn = jnp.minimum(t + 1, base + per_w - 1)
            nxt = loads(tn, q)
            for c in nxt:
                c.start()

            @pl.when(i >= 2)
            def _():
                store_o(t, p).wait()

            gather(p, 0, rows_a, sem_a).wait()
            compute(rows_a, p, 0)
            nxt[0].wait()
            gather(q, 0, rows_a, sem_a).start()
            gather(p, 1, rows_b, sem_b).wait()
            compute(rows_b, p, 1)
            gather(q, 1, rows_b, sem_b).start()
            for c in nxt[1:]:
                c.wait()
            store_o(t, p).start()
            return carry

        lax.fori_loop(0, per_w, token, 0)
        gather(0, 0, rows_a, sem_a).wait()
        gather(0, 1, rows_b, sem_b).wait()
        store_o(base, 0).wait()
        store_o(base, 1).wait()

    return run(expert_ids, coef, x1, g2, tab)


def _gelu_gate_kernel(a_ref, g_ref, o_ref):
    a = a_ref[...]
    o_ref[...] = 0.5 * a * (1.0 + lax.erf(a * (2.0 ** -0.5))) * g_ref[...]


def _gelu_gate(a, gates, first):
    count = a.shape[0]
    return pl.pallas_call(
        _gelu_gate_kernel,
        grid=(count // PEER_TM,),
        in_specs=[pl.BlockSpec((PEER_TM, PEER_NSEL), lambda i: (i, 0)),
                  pl.BlockSpec((PEER_TM, PEER_NSEL), lambda i: (first // PEER_TM + i, 0))],
        out_specs=pl.BlockSpec((PEER_TM, PEER_NSEL), lambda i: (i, 0)),
        out_shape=jax.ShapeDtypeStruct((count, PEER_NSEL), jnp.float32),
        compiler_params=_cparams(("arbitrary",)),
        name="gelu_gate",
    )(a, gates)


def _pack_table(tab):
    n, d = tab.shape
    bits = lax.bitcast_convert_type(tab.astype(jnp.bfloat16), jnp.uint16).astype(jnp.uint32)
    packed = (bits[:, : d // 2] << 16) | bits[:, d // 2:]
    return packed.reshape(n * HALF_ROWS, LANES)


def _peer_specs(t, d):
    smem_blk = pl.BlockSpec((1, PEER_TM, PEER_NSEL), lambda i: (i, 0, 0), memory_space=pltpu.SMEM)
    tok_blk = pl.BlockSpec((PEER_TM, d), lambda i: (i, 0))
    sel_blk = pl.BlockSpec((PEER_TM, PEER_NSEL), lambda i: (i, 0))
    return smem_blk, tok_blk, sel_blk, pl.BlockSpec(memory_space=pltpu.VMEM)


def _after(x, prev):
    return x if prev is None else lax.optimization_barrier((x, prev))[0]


def _peer_u_phase(h2, experts, gates, u_packed, n_sc):
    t, d = h2.shape
    assert d == SUBLANES * LANES and d // 2 == HALF_ROWS * LANES and PEER_TM % PEER_TB == 0
    smem_blk, tok_blk, sel_blk, tab_spec = _peer_specs(t, d)
    parts = []
    n_tc = t - n_sc
    if n_tc:
        parts.append(pl.pallas_call(
            _peer_u_kernel,
            grid=(n_tc // PEER_TM,),
            in_specs=[smem_blk, tok_blk, sel_blk, tab_spec],
            out_specs=sel_blk,
            out_shape=jax.ShapeDtypeStruct((n_tc, PEER_NSEL), jnp.float32),
            scratch_shapes=[pltpu.VMEM((PEER_TB * PEER_NSEL * HALF_ROWS, LANES), jnp.float32)],
            compiler_params=_cparams(("arbitrary",)),
            name="peer_u",
        )(experts.reshape(t // PEER_TM, PEER_TM, PEER_NSEL), h2, gates, u_packed))
    a_sc = None
    if n_sc:
        a_sc = _sc_peer_u(experts // HALF_ROWS, h2, u_packed.reshape(-1, d // 2), n_tc, n_sc)
    return (parts[0] if parts else None), a_sc


def _finish_coef(coef_tc, a_sc, gates):
    parts = [] if coef_tc is None else [coef_tc]
    if a_sc is not None:
        parts.append(_gelu_gate(a_sc, gates, gates.shape[0] - a_sc.shape[0]))
    return parts[0] if len(parts) == 1 else jnp.concatenate(parts, axis=0)


def _peer_v_phase(coef, x1, g2, experts, v_packed, batch, n_sc):
    t, d = x1.shape
    per_b = t // PEER_TM // batch
    smem_blk, tok_blk, sel_blk, tab_spec = _peer_specs(t, d)
    parts = []
    n_tc = t - n_sc
    if n_tc:
        parts.append(pl.pallas_call(
            _peer_v_kernel,
            grid=(n_tc // PEER_TM,),
            in_specs=[smem_blk, sel_blk, tok_blk,
                      pl.BlockSpec((None, 1, d), lambda i: (i // per_b, 0, 0)),
                      tab_spec],
            out_specs=tok_blk,
            out_shape=jax.ShapeDtypeStruct((n_tc, d), jnp.float32),
            scratch_shapes=[pltpu.VMEM((PEER_TB * PEER_NSEL, LANES), jnp.float32)],
            compiler_params=_cparams(("arbitrary",)),
            name="peer_v",
        )(experts.reshape(t // PEER_TM, PEER_TM, PEER_NSEL), coef, x1, g2.reshape(batch, 1, d), v_packed))
    if n_sc:
        parts.append(_sc_peer_v(experts // HALF_ROWS, coef, x1, g2, v_packed.reshape(-1, d // 2),
                                n_tc, n_sc, t // batch))
    return parts


def kernel(x, c, ada_w, ada_b, norm1_g, norm2_g, w_in, a_qk_gain, a_lambda, a_sub_gain, b_q_gain, b_kv_gain, b_w_uv, w_out, peer_wq, peer_subkeys, peer_u, peer_v):
    b, s, d = x.shape
    t = b * s
    x2 = x.reshape(t, d)
    nchunk = len(PEER_PLAN)
    bc = b // nchunk
    tc = bc * s
    for l in range(ada_w.shape[0]):
        mod = _adaln(c, ada_w[l], ada_b[l])
        u_packed, v_packed = _pack_table(peer_u[l]), _pack_table(peer_v[l])
        lam_init = 0.8 - 0.6 * math.exp(-0.3 * l)
        outs = [None] * nchunk
        pending = {}
        prev_experts = None

        def issue_v(k, anchor):
            coef_tc, a_sc, gates, args = pending.pop(k)
            if a_sc is not None:
                anchor, a_sc = lax.optimization_barrier((anchor, a_sc))
            outs[k] = _peer_v_phase(_finish_coef(coef_tc, a_sc, gates), *args)
            return anchor

        def issue_planned(at, anchor):
            for k in [k for k in pending if PEER_PLAN[k][2] == at]:
                anchor = issue_v(k, anchor)
            return anchor

        for ck, (n_sc_u, n_sc_v, _) in enumerate(PEER_PLAN):
            xc = _after(x2[ck * tc:(ck + 1) * tc], prev_experts)
            sh1, sc1, g1, sh2, sc2, g2 = [mod[ck * bc:(ck + 1) * bc, i * d:(i + 1) * d] for i in range(6)]
            qa, ka, va, qb, kvb, qi, ki, wi = _inproj(xc, sh1, sc1, norm1_g[l], w_in[l], a_qk_gain[l],
                                                      b_q_gain[l], b_kv_gain[l], bc)
            oa = _diffattn(qa, ka, va, a_lambda[l], a_sub_gain[l], bc, lam_init)
            oa = issue_planned((ck, "diffattn"), oa)
            ob = _dsa(qi, wi, ki, qb, kvb, b_w_uv[l], bc)
            x1, h2, q = _outproj(xc, oa, ob, w_out[l], g1, sh2, sc2, norm2_g[l], peer_wq[l], bc)
            q = issue_planned((ck, "outproj"), q)
            experts, gates = _peer_select(q, peer_subkeys[l])
            prev_experts = experts
            coef_tc, a_sc = _peer_u_phase(h2, experts, gates, u_packed, n_sc_u)
            pending[ck] = (coef_tc, a_sc, gates, (x1, g2, experts, v_packed, bc, n_sc_v))
        anchor = pending[nchunk - 1][0]
        for k in sorted(pending):
            anchor = issue_v(k, anchor)
        x2 = jnp.concatenate([p for o in outs for p in o], axis=0)
    return x2.reshape(b, s, d)
```

```python
import functools
import math

import jax
import jax.numpy as jnp
import numpy as np
from jax import lax
from jax.experimental import pallas as pl
from jax.experimental.pallas import tpu as pltpu
from jax.experimental.pallas import tpu_sc as plsc

CHUNK = 64
A_HEADS, A_QK_DIM, A_V_DIM = 4, 64, 128
B_HEADS, B_LAT, B_V_DIM = 8, 128, 64
IDX_HEADS, IDX_DIM = 4, 64
DSA_TOPK_MAX = 256
A_Q = A_HEADS * 2 * A_QK_DIM
A_K = A_Q
A_V = A_HEADS * A_V_DIM
B_Q = B_HEADS * B_LAT
B_KV = B_LAT
I_Q = IDX_HEADS * IDX_DIM
I_K = IDX_DIM
I_W = IDX_HEADS
OFF_AQ, OFF_AK, OFF_AV = 0, A_Q, A_Q + A_K
OFF_BQ = OFF_AV + A_V
OFF_KV = OFF_BQ + B_Q
OFF_IQ = OFF_KV + B_KV
OFF_IK = OFF_IQ + I_Q
OFF_IW = OFF_IK + I_K
IN_COLS = OFF_IW + I_W
PEER_HEADS, PEER_NKEYS, PEER_QDIM, PEER_TOPK = 8, 128, 128, 16
EPS = 1e-6
NEG = -1e30
INT_MIN = -(2 ** 31)

SUBLANES = 8
LANES = 128
VMEM_LIMIT = 56 * 1024 * 1024

PROJ_TM = 256
A_TQ = 512
B_TQ = 256
SELECT_TM = 512
PEER_TM = 128
PEER_TB = 8
PEER_NSEL = PEER_HEADS * PEER_TOPK
PEER_PLAN = ((8192, 8192, (1, "outproj")), (8192, 8192, (3, "diffattn")), (8192, 5376, None), (0, 0, None))
HALF_ROWS = 4


def _split_bf16(x):
    hi = x.astype(jnp.bfloat16)
    lo = (x - hi.astype(jnp.float32)).astype(jnp.bfloat16)
    return hi, lo


def _dot(a, b):
    return jnp.dot(a, b, preferred_element_type=jnp.float32)


def _dot_nt(a, b):
    return lax.dot_general(a, b, (((1,), (1,)), ((), ())), preferred_element_type=jnp.float32)


def _cparams(sem):
    return pltpu.CompilerParams(dimension_semantics=sem, vmem_limit_bytes=VMEM_LIMIT)


def _adaln_kernel(c_ref, w_ref, b_ref, o_ref):
    cf = c_ref[...]
    a = cf * (1.0 / (1.0 + jnp.exp(-cf)))
    a_hi, a_lo = _split_bf16(a)
    w_hi, w_lo = _split_bf16(w_ref[...])
    o_ref[...] = _dot(a_hi, w_hi) + _dot(a_hi, w_lo) + _dot(a_lo, w_hi) + b_ref[...]


def _adaln(c, w, b):
    bsz, d = c.shape
    n = w.shape[1]
    tn = 1024
    return pl.pallas_call(
        _adaln_kernel,
        grid=(n // tn,),
        in_specs=[pl.BlockSpec((bsz, d), lambda j: (0, 0)),
                  pl.BlockSpec((d, tn), lambda j: (0, j)),
                  pl.BlockSpec((1, tn), lambda j: (0, j))],
        out_specs=pl.BlockSpec((bsz, tn), lambda j: (0, j)),
        out_shape=jax.ShapeDtypeStruct((bsz, n), jnp.float32),
        compiler_params=_cparams(("arbitrary",)),
        name="adaln",
    )(c, w, b.reshape(1, n))


def _rms_rows(x, g):
    return x * lax.rsqrt(jnp.mean(x * x, axis=-1, keepdims=True) + EPS) * g


def _group_norm_block(p, gmat, gain, n):
    hi, lo = _split_bf16(p * p)
    ss = _dot(hi, gmat) + _dot(lo, gmat)
    return p * lax.rsqrt(ss * (1.0 / n) + EPS) * gain


def _inproj_kernel(x_ref, sh_ref, sc_ref, g_ref, w_ref, gq_ref, gk_ref, gbq_ref, gkv_ref,
                   qa_ref, ka_ref, va_ref, qb_ref, kv_ref, qi_ref, ki_ref, wi_ref):
    h = _rms_rows(x_ref[...], g_ref[...]) * (1.0 + sc_ref[0]) + sh_ref[0]
    hb = h.astype(jnp.bfloat16)
    row = lax.broadcasted_iota(jnp.int32, (LANES, LANES), 0)
    col = lax.broadcasted_iota(jnp.int32, (LANES, LANES), 1)
    g64 = jnp.where((row // A_QK_DIM) == (col // A_QK_DIM), 1.0, 0.0).astype(jnp.bfloat16)
    g128 = jnp.ones((LANES, LANES), jnp.bfloat16)

    def proj(off, width):
        return _dot(hb, w_ref[:, off:off + width])

    for blk in range(A_Q // LANES):
        sl = slice(blk * LANES, (blk + 1) * LANES)
        p = proj(OFF_AQ + blk * LANES, LANES)
        qa_ref[:, sl] = (_group_norm_block(p, g64, gq_ref[...], A_QK_DIM) * (A_QK_DIM ** -0.5)).astype(qa_ref.dtype)
        p = proj(OFF_AK + blk * LANES, LANES)
        ka_ref[:, sl] = _group_norm_block(p, g64, gk_ref[...], A_QK_DIM).astype(ka_ref.dtype)
    va_ref[...] = proj(OFF_AV, A_V).astype(va_ref.dtype)
    for blk in range(B_HEADS):
        sl = slice(blk * LANES, (blk + 1) * LANES)
        p = proj(OFF_BQ + blk * LANES, LANES)
        qb_ref[:, sl] = _group_norm_block(p, g128, gbq_ref[...], B_LAT).astype(qb_ref.dtype)
    p = proj(OFF_KV, B_KV)
    kv_ref[...] = _group_norm_block(p, g128, gkv_ref[...], B_LAT).astype(kv_ref.dtype)
    qi_ref[...] = proj(OFF_IQ, I_Q).astype(qi_ref.dtype)
    tail = proj(OFF_IK, 2 * LANES)
    ki_ref[...] = tail[:, :LANES].astype(ki_ref.dtype)
    wi_ref[...] = tail[:, I_K:I_K + LANES]


def _inproj(x2, sh1, sc1, g1n, w_in, a_qk_gain, b_q_gain, b_kv_gain, batch):
    t, d = x2.shape
    per_b = (t // batch) // PROJ_TM
    wpad = OFF_IK + 2 * LANES
    w = jnp.zeros((d, wpad), jnp.bfloat16).at[:, :IN_COLS].set(w_in.astype(jnp.bfloat16))
    gq = jnp.tile(a_qk_gain[0], 2).reshape(1, LANES)
    gk = jnp.tile(a_qk_gain[1], 2).reshape(1, LANES)
    tok = lambda width: pl.BlockSpec((PROJ_TM, width), lambda i: (i, 0))
    vec = lambda width: pl.BlockSpec((1, width), lambda i: (0, 0))
    per_batch = pl.BlockSpec((1, 1, d), lambda i: (i // per_b, 0, 0))
    bf = jnp.bfloat16
    outs = [(A_Q, bf), (A_K, bf), (A_V, bf), (B_Q, bf), (B_KV, bf), (I_Q, bf), (LANES, bf), (LANES, jnp.float32)]
    return pl.pallas_call(
        _inproj_kernel,
        grid=(t // PROJ_TM,),
        in_specs=[tok(d), per_batch, per_batch, vec(d),
                  pl.BlockSpec((d, wpad), lambda i: (0, 0)),
                  vec(LANES), vec(LANES), vec(LANES), vec(LANES)],
        out_specs=[tok(wd) for wd, _ in outs],
        out_shape=[jax.ShapeDtypeStruct((t, wd), dt) for wd, dt in outs],
        compiler_params=_cparams(("arbitrary",)),
        name="inproj",
    )(x2, sh1.reshape(batch, 1, d), sc1.reshape(batch, 1, d), g1n.reshape(1, d), w,
      gq, gk, b_q_gain.reshape(1, LANES), b_kv_gain.reshape(1, LANES))


def _positions(q0, tq, s_len):
    qpos = q0 + lax.broadcasted_iota(jnp.int32, (tq, s_len), 0)
    kpos = lax.broadcasted_iota(jnp.int32, (tq, s_len), 1)
    cend = (qpos // CHUNK + 1) * CHUNK
    return qpos, kpos, cend


def _softmax_rows(s):
    m = jnp.max(s, axis=-1, keepdims=True)
    p = jnp.exp(s - m)
    return p, jnp.sum(p, axis=-1, keepdims=True)


def _diffattn_kernel(q_ref, k_ref, v_ref, lam_ref, gain_ref, o_ref, *, lam_init, q_base):
    tq, s_len = q_ref.shape[0], k_ref.shape[0]
    q0 = q_base + pl.program_id(1) * tq
    qpos, kpos, cend = _positions(q0, tq, s_len)
    dist = jnp.abs(qpos - kpos).astype(jnp.float32)
    negmask = jnp.where(kpos < cend, 0.0, NEG)
    lf = lam_ref[...]
    lam = (jnp.exp(jnp.sum(lf[0:1] * lf[1:2], axis=-1, keepdims=True))
           - jnp.exp(jnp.sum(lf[2:3] * lf[3:4], axis=-1, keepdims=True)) + lam_init)
    for h in range(A_HEADS):
        slope = 2.0 ** (-8.0 * (h + 1) / A_HEADS)
        bias = negmask - slope * dist
        v = v_ref[:, h * A_V_DIM:(h + 1) * A_V_DIM]
        outs = []
        for m in range(2):
            c0 = (h * 2 + m) * A_QK_DIM
            s = _dot_nt(q_ref[:, c0:c0 + A_QK_DIM], k_ref[:, c0:c0 + A_QK_DIM]) + bias
            p, l = _softmax_rows(s)
            outs.append(_dot(p.astype(jnp.bfloat16), v) / l)
        o = outs[0] - lam * outs[1]
        o = _rms_rows(o, gain_ref[...]) * (1.0 - lam_init)
        o_ref[:, h * A_V_DIM:(h + 1) * A_V_DIM] = o.astype(o_ref.dtype)


def _causal_groups(s_len, tq):
    span = min(tq, s_len)
    tiles = span // tq
    return [(g * tiles, tiles, (g + 1) * span) for g in range(s_len // span)]


def _qblk(tq, first_tile, width):
    return pl.BlockSpec((None, tq, width), lambda b, i: (b, first_tile + i, 0))


def _kblk(klen, width):
    return pl.BlockSpec((None, klen, width), lambda b, i: (b, 0, 0))


def _oblk(tq, width):
    return pl.BlockSpec((None, tq, width), lambda b, i: (b, i, 0))


def _diffattn(qa, ka, va, a_lambda, a_sub_gain, batch, lam_init):
    t = qa.shape[0]
    s_len = t // batch
    qa3, ka3, va3 = (a.reshape(batch, s_len, a.shape[1]) for a in (qa, ka, va))
    outs = []
    for first, tiles, klen in _causal_groups(s_len, A_TQ):
        outs.append(pl.pallas_call(
            functools.partial(_diffattn_kernel, lam_init=lam_init, q_base=first * A_TQ),
            grid=(batch, tiles),
            in_specs=[_qblk(A_TQ, first, A_Q), _kblk(klen, A_K), _kblk(klen, A_V),
                      pl.BlockSpec((4, A_QK_DIM), lambda b, i: (0, 0)),
                      pl.BlockSpec((1, A_V_DIM), lambda b, i: (0, 0))],
            out_specs=_oblk(A_TQ, A_V),
            out_shape=jax.ShapeDtypeStruct((batch, tiles * A_TQ, A_V), jnp.bfloat16),
            compiler_params=_cparams(("arbitrary", "arbitrary")),
            name="diffattn",
        )(qa3, ka3, va3, a_lambda, a_sub_gain.reshape(1, A_V_DIM)))
    return jnp.concatenate(outs, axis=1).reshape(t, A_V)


def _count(mask):
    return jnp.sum(jnp.where(mask, 1.0, 0.0), axis=-1, keepdims=True)


def _topk_mask(score, allowed, kpos, topk, s_len):
    bits = pltpu.bitcast(score + 0.0, jnp.int32)
    key = jnp.where(bits < 0, bits ^ jnp.int32(0x7FFFFFFF), bits)
    key = jnp.where(allowed, key, jnp.int32(INT_MIN))
    kf = float(topk)
    thr = jnp.where(_count(key >= 0) >= kf, jnp.int32(0), jnp.int32(INT_MIN))

    def value_bit(i, thr):
        cand = thr | (jnp.int32(1) << (30 - i))
        return jnp.where(_count(key >= cand) >= kf, cand, thr)

    thr = lax.fori_loop(0, 31, value_bit, thr)
    above = key > thr
    tie = key == thr
    need = kf - _count(above)
    nbits = (s_len - 1).bit_length()

    def pos_bit(i, pos):
        cand = pos | (jnp.int32(1) << (nbits - 1 - i))
        return jnp.where(_count(tie & (kpos < cand)) < need, cand, pos)

    pos = lax.fori_loop(0, nbits, pos_bit, jnp.zeros_like(thr))
    return allowed & (above | (tie & (kpos <= pos)))


def _dsa_kernel(qi_ref, wi_ref, ki_ref, qb_ref, kv_ref, wuv_ref, o_ref, *, topk, q_base):
    tq, s_len = qb_ref.shape[0], kv_ref.shape[0]
    q0 = q_base + pl.program_id(1) * tq
    qpos, kpos, cend = _positions(q0, tq, s_len)
    allowed = kpos < cend
    ki = ki_ref[...]
    wi = wi_ref[...]
    zpad = jnp.zeros((tq, LANES - IDX_DIM), jnp.bfloat16)
    score = jnp.zeros((tq, s_len), jnp.float32)
    for h in range(IDX_HEADS):
        qh = jnp.concatenate([qi_ref[:, h * IDX_DIM:(h + 1) * IDX_DIM], zpad], axis=-1)
        score = score + jnp.maximum(_dot_nt(qh, ki), 0.0) * wi[:, h:h + 1]
    score = score * ((IDX_DIM ** -0.5) * (IDX_HEADS ** -0.5))
    sel = _topk_mask(score, allowed, kpos, topk, s_len)

    dist = jnp.abs(qpos - kpos).astype(jnp.float32)
    negmask = jnp.where(sel, 0.0, NEG)
    kv = kv_ref[...]
    for h in range(B_HEADS):
        slope = 2.0 ** (-8.0 * (h + 1) / B_HEADS)
        s = _dot_nt(qb_ref[:, h * B_LAT:(h + 1) * B_LAT], kv) * (B_LAT ** -0.5) + (negmask - slope * dist)
        p, l = _softmax_rows(s)
        o_lat = _dot(p.astype(jnp.bfloat16), kv) / l
        o = _dot(o_lat.astype(jnp.bfloat16), wuv_ref[h])
        o_ref[:, h * B_V_DIM:(h + 1) * B_V_DIM] = o.astype(o_ref.dtype)


def _dsa(qi, wi, ki, qb, kvb, w_uv, batch):
    t = qb.shape[0]
    s_len = t // batch
    topk = min(DSA_TOPK_MAX, s_len // 4)
    width = B_HEADS * B_V_DIM
    qi3, wi3, ki3, qb3, kv3 = (a.reshape(batch, s_len, a.shape[1]) for a in (qi, wi, ki, qb, kvb))
    wuv = w_uv.astype(jnp.bfloat16)
    outs = []
    for first, tiles, klen in _causal_groups(s_len, B_TQ):
        outs.append(pl.pallas_call(
            functools.partial(_dsa_kernel, topk=topk, q_base=first * B_TQ),
            grid=(batch, tiles),
            in_specs=[_qblk(B_TQ, first, I_Q), _qblk(B_TQ, first, LANES), _kblk(klen, LANES),
                      _qblk(B_TQ, first, B_Q), _kblk(klen, B_KV),
                      pl.BlockSpec((B_HEADS, B_LAT, B_V_DIM), lambda b, i: (0, 0, 0))],
            out_specs=_oblk(B_TQ, width),
            out_shape=jax.ShapeDtypeStruct((batch, tiles * B_TQ, width), jnp.bfloat16),
            compiler_params=_cparams(("arbitrary", "arbitrary")),
            name="dsa",
        )(qi3, wi3, ki3, qb3, kv3, wuv))
    return jnp.concatenate(outs, axis=1).reshape(t, width)


def _outproj_kernel(x_ref, oa_ref, ob_ref, wo_ref, g1_ref, sh_ref, sc_ref, gn_ref, wq_ref,
                    x1_ref, h2_ref, q_ref):
    na = oa_ref.shape[1]
    y = _dot(oa_ref[...], wo_ref[:na, :]) + _dot(ob_ref[...], wo_ref[na:, :])
    x1 = x_ref[...] + g1_ref[0] * y
    x1_ref[...] = x1
    h2 = _rms_rows(x1, gn_ref[...]) * (1.0 + sc_ref[0]) + sh_ref[0]
    h2_ref[...] = h2
    q_ref[...] = _dot(h2.astype(jnp.bfloat16), wq_ref[...]).astype(q_ref.dtype)


def _outproj(x2, oa, ob, w_out, g1, sh2, sc2, g2n, peer_wq, batch):
    t, d = x2.shape
    per_b = (t // batch) // PROJ_TM
    nq = peer_wq.shape[1]
    tok = lambda width: pl.BlockSpec((PROJ_TM, width), lambda i: (i, 0))
    per_batch = pl.BlockSpec((1, 1, d), lambda i: (i // per_b, 0, 0))
    full = lambda a, b: pl.BlockSpec((a, b), lambda i: (0, 0))
    return pl.pallas_call(
        _outproj_kernel,
        grid=(t // PROJ_TM,),
        in_specs=[tok(d), tok(oa.shape[1]), tok(ob.shape[1]), full(w_out.shape[0], d),
                  per_batch, per_batch, per_batch, full(1, d), full(d, nq)],
        out_specs=[tok(d), tok(d), tok(nq)],
        out_shape=[jax.ShapeDtypeStruct((t, d), jnp.float32),
                   jax.ShapeDtypeStruct((t, d), jnp.float32),
                   jax.ShapeDtypeStruct((t, nq), jnp.bfloat16)],
        compiler_params=_cparams(("arbitrary",)),
        name="outproj",
    )(x2, oa, ob, w_out.astype(jnp.bfloat16), g1.reshape(batch, 1, d), sh2.reshape(batch, 1, d),
      sc2.reshape(batch, 1, d), g2n.reshape(1, d), peer_wq.astype(jnp.bfloat16))


def _extract_topk(vals, payload, k):
    n, tm = vals.shape
    rows = lax.broadcasted_iota(jnp.int32, (n, tm), 0).astype(jnp.float32)
    top_v, top_p = [], []
    for _ in range(k):
        m = jnp.max(vals, axis=0, keepdims=True)
        pos = jnp.min(jnp.where(vals == m, rows, float(n)), axis=0, keepdims=True)
        hit = rows == pos
        top_v.append(m)
        top_p.append(pos if payload is None else jnp.max(jnp.where(hit, payload, -1.0), axis=0, keepdims=True))
        vals = jnp.where(hit, -jnp.inf, vals)
    return jnp.concatenate(top_v, axis=0), jnp.concatenate(top_p, axis=0)


def _stair_pairs():
    return [(a, b) for a in range(PEER_TOPK) for b in range(PEER_TOPK) if (a + 1) * (b + 1) <= PEER_TOPK]


N_STAIR = len(_stair_pairs())
N_STAIR_PAD = -(-N_STAIR // SUBLANES) * SUBLANES


def _copy_rows(sel, x):
    x0 = x.astype(jnp.bfloat16)
    r1 = x - x0.astype(jnp.float32)
    x1 = r1.astype(jnp.bfloat16)
    x2 = (r1 - x1.astype(jnp.float32)).astype(jnp.bfloat16)
    return _dot(sel, x0) + _dot(sel, x1) + _dot(sel, x2)


def _peer_select_kernel(q_ref, sk_ref, sela_ref, selb_ref, e_ref, g_ref):
    tm = q_ref.shape[0]
    sela, selb = sela_ref[...], selb_ref[...]
    pad_row = lax.broadcasted_iota(jnp.int32, (N_STAIR_PAD, tm), 0) >= N_STAIR
    e_rows, g_rows = [], []
    for h in range(PEER_HEADS):
        sub = _dot_nt(sk_ref[...], q_ref[:, h * PEER_QDIM:(h + 1) * PEER_QDIM])
        sv0, si0 = _extract_topk(sub[:PEER_NKEYS], None, PEER_TOPK)
        sv1, si1 = _extract_topk(sub[PEER_NKEYS:], None, PEER_TOPK)
        cand = jnp.where(pad_row, -jnp.inf, _copy_rows(sela, sv0) + _copy_rows(selb, sv1))
        ids = (_dot(sela, si0.astype(jnp.bfloat16)) * float(PEER_NKEYS)
               + _dot(selb, si1.astype(jnp.bfloat16))) * float(HALF_ROWS)
        top_s, top_e = _extract_topk(cand, ids, PEER_TOPK)
        p = jnp.exp(top_s - top_s[0:1])
        e_rows.append(top_e)
        g_rows.append(p / jnp.sum(p, axis=0, keepdims=True))
    e_t = jnp.concatenate(e_rows, axis=0)
    g_t = jnp.concatenate(g_rows, axis=0)
    e_ref[...] = e_t.T.astype(jnp.int32)
    g_ref[...] = g_t.T


def _peer_select(q, sub_keys):
    t, nq = q.shape
    half = PEER_QDIM // 2
    sk = jnp.zeros((2 * PEER_NKEYS, PEER_QDIM), jnp.bfloat16)
    sk = sk.at[:PEER_NKEYS, :half].set(sub_keys[0].astype(jnp.bfloat16))
    sk = sk.at[PEER_NKEYS:, half:].set(sub_keys[1].astype(jnp.bfloat16))
    sel = np.zeros((2, N_STAIR_PAD, PEER_TOPK), np.float32)
    for r, (a, b) in enumerate(_stair_pairs()):
        sel[0, r, a] = 1.0
        sel[1, r, b] = 1.0
    sel = jnp.asarray(sel, jnp.bfloat16)
    tok = lambda width: pl.BlockSpec((SELECT_TM, width), lambda i: (i, 0))
    full = lambda a, b: pl.BlockSpec((a, b), lambda i: (0, 0))
    return pl.pallas_call(
        _peer_select_kernel,
        grid=(t // SELECT_TM,),
        in_specs=[tok(nq), full(2 * PEER_NKEYS, PEER_QDIM),
                  full(N_STAIR_PAD, PEER_TOPK), full(N_STAIR_PAD, PEER_TOPK)],
        out_specs=[tok(PEER_NSEL), tok(PEER_NSEL)],
        out_shape=[jax.ShapeDtypeStruct((t, PEER_NSEL), jnp.int32),
                   jax.ShapeDtypeStruct((t, PEER_NSEL), jnp.float32)],
        compiler_params=_cparams(("arbitrary",)),
        name="peer_select",
    )(q, sk, sel[0], sel[1])


def _gather_row(tab_ref, row):
    w = tab_ref[pl.ds(pl.multiple_of(row, HALF_ROWS), HALF_ROWS), :]
    hi = pltpu.bitcast(w & jnp.uint32(0xFFFF0000), jnp.float32)
    lo = pltpu.bitcast(w << 16, jnp.float32)
    return hi, lo


def _peer_u_kernel(idx_ref, h_ref, gate_ref, tab_ref, coef_ref, abuf_ref):
    ones = jnp.ones((2 * LANES, LANES), jnp.bfloat16)
    lane = lax.broadcasted_iota(jnp.int32, (PEER_NSEL, PEER_TM), 1)
    rows_per_tok = PEER_NSEL * HALF_ROWS

    def batch(b, a_t):
        def token(tb, c2):
            t = b * PEER_TB + tb
            x8 = h_ref[pl.ds(t, 1), :].reshape(SUBLANES, LANES)
            xh, xl = x8[:HALF_ROWS], x8[HALF_ROWS:]
            off = pl.multiple_of(tb * rows_per_tok, rows_per_tok)
            for j in range(PEER_NSEL):
                hi, lo = _gather_row(tab_ref, idx_ref[0, t, j])
                abuf_ref[pl.ds(off + j * HALF_ROWS, HALF_ROWS), :] = hi * xh + lo * xl
            return c2

        lax.fori_loop(0, PEER_TB, token, 0)
        parts = []
        for tb in range(PEER_TB):
            ab = abuf_ref[pl.ds(tb * rows_per_tok, PEER_NSEL, stride=HALF_ROWS), :]
            for s in range(1, HALF_ROWS):
                ab = ab + abuf_ref[pl.ds(tb * rows_per_tok + s, PEER_NSEL, stride=HALF_ROWS), :]
            parts.append(ab)
        ab_hi, ab_lo = _split_bf16(jnp.concatenate(parts, axis=0))
        r = _dot(jnp.concatenate([ab_hi, ab_lo], axis=1), ones)
        for tb in range(PEER_TB):
            a_t = jnp.where(lane == b * PEER_TB + tb, r[tb * PEER_NSEL:(tb + 1) * PEER_NSEL], a_t)
        return a_t

    a_t = lax.fori_loop(0, PEER_TM // PEER_TB, batch, jnp.zeros((PEER_NSEL, PEER_TM), jnp.float32))
    a = a_t.T
    coef_ref[...] = 0.5 * a * (1.0 + lax.erf(a * (2.0 ** -0.5))) * gate_ref[...]


def _peer_v_kernel(idx_ref, coef_ref, x_ref, g2_ref, tab_ref, out_ref, cb_ref):
    g2 = g2_ref[...]
    ones = jnp.ones((2 * LANES, LANES), jnp.bfloat16)
    eye = (lax.broadcasted_iota(jnp.int32, (PEER_NSEL, LANES), 0)
           == lax.broadcasted_iota(jnp.int32, (PEER_NSEL, LANES), 1))
    nacc = 2

    def batch(b, carry):
        for tb in range(PEER_TB):
            row = coef_ref[pl.ds(b * PEER_TB + tb, 1), :]
            d_hi, d_lo = _split_bf16(jnp.where(eye, jnp.broadcast_to(row, (PEER_NSEL, LANES)), 0.0))
            cb_ref[pl.ds(tb * PEER_NSEL, PEER_NSEL), :] = _dot(jnp.concatenate([d_hi, d_lo], axis=1), ones)

        def token(tb, c2):
            t = b * PEER_TB + tb
            zero = jnp.zeros((HALF_ROWS, LANES), jnp.float32)
            acc_h = [zero] * nacc
            acc_l = [zero] * nacc
            cbase = pl.multiple_of(tb * PEER_NSEL, PEER_NSEL)
            for j in range(PEER_NSEL):
                hi, lo = _gather_row(tab_ref, idx_ref[0, t, j])
                c = jnp.broadcast_to(cb_ref[pl.ds(cbase + j, 1), :], (HALF_ROWS, LANES))
                acc_h[j % nacc] = acc_h[j % nacc] + c * hi
                acc_l[j % nacc] = acc_l[j % nacc] + c * lo
            y8 = jnp.concatenate([acc_h[0] + acc_h[1], acc_l[0] + acc_l[1]], axis=0)
            out_ref[pl.ds(t, 1), :] = x_ref[pl.ds(t, 1), :] + g2 * y8.reshape(1, SUBLANES * LANES)
            return c2

        lax.fori_loop(0, PEER_TB, token, 0)
        return carry

    lax.fori_loop(0, PEER_TM // PEER_TB, batch, 0)


def _sc_peer_u(expert_ids, h2, tab, first, count):
    info = plsc.get_sparse_core_info()
    lanes, workers = info.num_lanes, info.num_cores * info.num_subcores
    half = tab.shape[1]
    per_w = count // workers
    nhalf = PEER_NSEL // 2
    assert per_w * workers == count and per_w >= 2 and nhalf % lanes == 0 and half % lanes == 0
    rows_blk = 4
    kunroll = 2
    mesh = plsc.VectorSubcoreMesh(core_axis_name="c", subcore_axis_name="s")

    @functools.partial(
        pl.kernel, mesh=mesh, name="peer_u_sc",
        compiler_params=pltpu.CompilerParams(needs_layout_passes=False),
        out_type=jax.ShapeDtypeStruct((count, PEER_NSEL), jnp.float32),
        scratch_types=[pltpu.VMEM((2, PEER_NSEL), jnp.int32),
                       pltpu.VMEM((nhalf, half), jnp.uint32), pltpu.VMEM((nhalf, half), jnp.uint32),
                       pltpu.VMEM((2, 2 * half), jnp.float32), pltpu.VMEM((2, PEER_NSEL), jnp.float32),
                       pltpu.SemaphoreType.DMA, pltpu.SemaphoreType.DMA, pltpu.SemaphoreType.DMA,
                       pltpu.SemaphoreType.DMA, pltpu.SemaphoreType.DMA((2,))])
    def run(idx_hbm, h_hbm, tab_hbm, out_hbm, idx_v, rows_a, rows_b, x_v, a_v, sem_a, sem_b, sem_i, sem_x, sem_o):
        wid = lax.axis_index("s") * info.num_cores + lax.axis_index("c")
        base = first + wid * per_w
        lane = lax.iota(jnp.int32, lanes)
        mask_hi = jnp.full((lanes,), 0xFFFF0000, jnp.uint32)

        def gather(p, part, rows, sem):
            return pltpu.make_async_copy(tab_hbm.at[idx_v.at[p, pl.ds(part * nhalf, nhalf)]], rows, sem)

        def load_idx(t, p):
            return pltpu.make_async_copy(idx_hbm.at[t], idx_v.at[p], sem_i)

        def load_x(t, p):
            return pltpu.make_async_copy(h_hbm.at[t], x_v.at[p], sem_x)

        def store_a(t, p):
            return pltpu.make_async_copy(a_v.at[p], out_hbm.at[t - first], sem_o.at[p])

        def compute(rows, p, part):
            def group(g, c2):
                vec = jnp.zeros((lanes,), jnp.float32)
                for rb in range(lanes // rows_blk):
                    def kslice(kk, accs):
                        accs = list(accs)
                        for ku in range(kunroll):
                            off = pl.multiple_of((kk * kunroll + ku) * lanes, lanes)
                            xh = x_v[p, pl.ds(off, lanes)]
                            xl = x_v[p, pl.ds(half + off, lanes)]
                            for r in range(rows_blk):
                                w = rows[g * lanes + rb * rows_blk + r, pl.ds(off, lanes)]
                                hi = plsc.bitcast(w & mask_hi, jnp.float32)
                                lo = plsc.bitcast(w << 16, jnp.float32)
                                accs[r] = accs[r] + hi * xh + lo * xl
                        return tuple(accs)

                    zero = jnp.zeros((lanes,), jnp.float32)
                    accs = lax.fori_loop(0, half // lanes // kunroll, kslice, (zero,) * rows_blk)
                    for r in range(rows_blk):
                        vec = jnp.where(lane == rb * rows_blk + r, jnp.sum(accs[r]), vec)
                a_v[p, pl.ds(pl.multiple_of(part * nhalf + g * lanes, lanes), lanes)] = vec
                return c2

            lax.fori_loop(0, nhalf // lanes, group, 0)

        load_idx(base, 0).start()
        load_x(base, 0).start()
        load_idx(base, 0).wait()
        gather(0, 0, rows_a, sem_a).start()
        gather(0, 1, rows_b, sem_b).start()
        load_x(base, 0).wait()

        def token(i, carry):
            t = base + i
            p = lax.rem(i, 2)
            q = 1 - p
            tn = jnp.minimum(t + 1, base + per_w - 1)
            load_idx(tn, q).start()
            load_x(tn, q).start()

            @pl.when(i >= 2)
            def _():
                store_a(t, p).wait()

            gather(p, 0, rows_a, sem_a).wait()
            compute(rows_a, p, 0)
            load_idx(tn, q).wait()
            gather(q, 0, rows_a, sem_a).start()
            gather(p, 1, rows_b, sem_b).wait()
            compute(rows_b, p, 1)
            gather(q, 1, rows_b, sem_b).start()
            load_x(tn, q).wait()
            store_a(t, p).start()
            return carry

        lax.fori_loop(0, per_w, token, 0)
        gather(0, 0, rows_a, sem_a).wait()
        gather(0, 1, rows_b, sem_b).wait()
        store_a(base, 0).wait()
        store_a(base, 1).wait()

    return run(expert_ids, h2, tab)


def _sc_peer_v(expert_ids, coef, x1, g2, tab, first, count, per_batch):
    info = plsc.get_sparse_core_info()
    lanes, workers = info.num_lanes, info.num_cores * info.num_subcores
    half = tab.shape[1]
    d = 2 * half
    per_w = count // workers
    nhalf = PEER_NSEL // 2
    sblk = 16
    runroll = 1
    assert per_w * workers == count and per_w >= 2 and nhalf % runroll == 0 and half % (lanes * sblk) == 0
    mesh = plsc.VectorSubcoreMesh(core_axis_name="c", subcore_axis_name="s")
    dma = pltpu.SemaphoreType.DMA

    @functools.partial(
        pl.kernel, mesh=mesh, name="peer_v_sc",
        compiler_params=pltpu.CompilerParams(needs_layout_passes=False),
        out_type=jax.ShapeDtypeStruct((count, d), jnp.float32),
        scratch_types=[pltpu.VMEM((2, PEER_NSEL), jnp.int32), pltpu.VMEM((2 * PEER_NSEL,), jnp.float32),
                       pltpu.VMEM((nhalf, half), jnp.uint32), pltpu.VMEM((nhalf, half), jnp.uint32),
                       pltpu.VMEM((2, d), jnp.float32), pltpu.VMEM((2, d), jnp.float32),
                       pltpu.VMEM((d,), jnp.float32), pltpu.VMEM((2, d), jnp.float32),
                       dma, dma, dma, dma, dma, dma, dma((2,))])
    def run(idx_hbm, coef_hbm, x_hbm, g_hbm, tab_hbm, out_hbm,
            idx_v, coef_v, rows_a, rows_b, x_v, g_v, y_v, o_v, sem_a, sem_b, sem_i, sem_c, sem_x, sem_g, sem_o):
        wid = lax.axis_index("s") * info.num_cores + lax.axis_index("c")
        base = first + wid * per_w
        mask_hi = jnp.full((lanes,), 0xFFFF0000, jnp.uint32)

        def gather(p, part, rows, sem):
            return pltpu.make_async_copy(tab_hbm.at[idx_v.at[p, pl.ds(part * nhalf, nhalf)]], rows, sem)

        def loads(t, p):
            cslot = coef_v.at[pl.ds(pl.multiple_of(p * PEER_NSEL, PEER_NSEL), PEER_NSEL)]
            return (pltpu.make_async_copy(idx_hbm.at[t], idx_v.at[p], sem_i),
                    pltpu.make_async_copy(coef_hbm.at[t], cslot, sem_c),
                    pltpu.make_async_copy(x_hbm.at[t], x_v.at[p], sem_x),
                    pltpu.make_async_copy(g_hbm.at[t // per_batch], g_v.at[p], sem_g))

        def store_o(t, p):
            return pltpu.make_async_copy(o_v.at[p], out_hbm.at[t - first], sem_o.at[p])

        def compute(rows, p, part):
            cbase = p * PEER_NSEL + part * nhalf
            for sb in range(half // lanes // sblk):
                offs = [(sb * sblk + s) * lanes for s in range(sblk)]
                if part == 0:
                    init = tuple(jnp.zeros((lanes,), jnp.float32) for _ in range(2 * sblk))
                else:
                    init = tuple([y_v[pl.ds(o, lanes)] for o in offs] + [y_v[pl.ds(half + o, lanes)] for o in offs])

                def rowloop(rr, accs):
                    accs = list(accs)
                    for ru in range(runroll):
                        r = rr * runroll + ru
                        c = plsc.load_gather(coef_v, [jnp.full((lanes,), cbase + r, jnp.int32)])
                        for s in range(sblk):
                            w = rows[r, pl.ds(offs[s], lanes)]
                            hi = plsc.bitcast(w & mask_hi, jnp.float32)
                            lo = plsc.bitcast(w << 16, jnp.float32)
                            accs[s] = accs[s] + c * hi
                            accs[sblk + s] = accs[sblk + s] + c * lo
                    return tuple(accs)

                accs = lax.fori_loop(0, nhalf // runroll, rowloop, init)
                for s in range(sblk):
                    for hl, o in ((0, offs[s]), (1, half + offs[s])):
                        if part == 0:
                            y_v[pl.ds(o, lanes)] = accs[hl * sblk + s]
                        else:
                            o_v[p, pl.ds(o, lanes)] = (x_v[p, pl.ds(o, lanes)]
                                                       + g_v[p, pl.ds(o, lanes)] * accs[hl * sblk + s])

        head = loads(base, 0)
        for c in head:
            c.start()
        head[0].wait()
        gather(0, 0, rows_a, sem_a).start()
        gather(0, 1, rows_b, sem_b).start()
        for c in head[1:]:
            c.wait()

        def token(i, carry):
            t = base + i
            p = lax.rem(i, 2)
            q = 1 - p
            tn = jnp.minimum(t + 1, base + per_w - 1)
            nxt = loads(tn, q)
            for c in nxt:
                c.start()

            @pl.when(i >= 2)
            def _():
                store_o(t, p).wait()

            gather(p, 0, rows_a, sem_a).wait()
            compute(rows_a, p, 0)
            nxt[0].wait()
            gather(q, 0, rows_a, sem_a).start()
            gather(p, 1, rows_b, sem_b).wait()
            compute(rows_b, p, 1)
            gather(q, 1, rows_b, sem_b).start()
            for c in nxt[1:]:
                c.wait()
            store_o(t, p).start()
            return carry

        lax.fori_loop(0, per_w, token, 0)
        gather(0, 0, rows_a, sem_a).wait()
        gather(0, 1, rows_b, sem_b).wait()
        store_o(base, 0).wait()
        store_o(base, 1).wait()

    return run(expert_ids, coef, x1, g2, tab)


def _gelu_gate_kernel(a_ref, g_ref, o_ref):
    a = a_ref[...]
    o_ref[...] = 0.5 * a * (1.0 + lax.erf(a * (2.0 ** -0.5))) * g_ref[...]


def _gelu_gate(a, gates, first):
    count = a.shape[0]
    return pl.pallas_call(
        _gelu_gate_kernel,
        grid=(count // PEER_TM,),
        in_specs=[pl.BlockSpec((PEER_TM, PEER_NSEL), lambda i: (i, 0)),
                  pl.BlockSpec((PEER_TM, PEER_NSEL), lambda i: (first // PEER_TM + i, 0))],
        out_specs=pl.BlockSpec((PEER_TM, PEER_NSEL), lambda i: (i, 0)),
        out_shape=jax.ShapeDtypeStruct((count, PEER_NSEL), jnp.float32),
        compiler_params=_cparams(("arbitrary",)),
        name="gelu_gate",
    )(a, gates)


def _pack_table(tab):
    n, d = tab.shape
    bits = lax.bitcast_convert_type(tab.astype(jnp.bfloat16), jnp.uint16).astype(jnp.uint32)
    packed = (bits[:, : d // 2] << 16) | bits[:, d // 2:]
    return packed.reshape(n * HALF_ROWS, LANES)


def _peer_specs(t, d):
    smem_blk = pl.BlockSpec((1, PEER_TM, PEER_NSEL), lambda i: (i, 0, 0), memory_space=pltpu.SMEM)
    tok_blk = pl.BlockSpec((PEER_TM, d), lambda i: (i, 0))
    sel_blk = pl.BlockSpec((PEER_TM, PEER_NSEL), lambda i: (i, 0))
    return smem_blk, tok_blk, sel_blk, pl.BlockSpec(memory_space=pltpu.VMEM)


def _after(x, prev):
    return x if prev is None else lax.optimization_barrier((x, prev))[0]


def _peer_u_phase(h2, experts, gates, u_packed, n_sc):
    t, d = h2.shape
    assert d == SUBLANES * LANES and d // 2 == HALF_ROWS * LANES and PEER_TM % PEER_TB == 0
    smem_blk, tok_blk, sel_blk, tab_spec = _peer_specs(t, d)
    parts = []
    n_tc = t - n_sc
    if n_tc:
        parts.append(pl.pallas_call(
            _peer_u_kernel,
            grid=(n_tc // PEER_TM,),
            in_specs=[smem_blk, tok_blk, sel_blk, tab_spec],
            out_specs=sel_blk,
            out_shape=jax.ShapeDtypeStruct((n_tc, PEER_NSEL), jnp.float32),
            scratch_shapes=[pltpu.VMEM((PEER_TB * PEER_NSEL * HALF_ROWS, LANES), jnp.float32)],
            compiler_params=_cparams(("arbitrary",)),
            name="peer_u",
        )(experts.reshape(t // PEER_TM, PEER_TM, PEER_NSEL), h2, gates, u_packed))
    a_sc = None
    if n_sc:
        a_sc = _sc_peer_u(experts // HALF_ROWS, h2, u_packed.reshape(-1, d // 2), n_tc, n_sc)
    return (parts[0] if parts else None), a_sc


def _finish_coef(coef_tc, a_sc, gates):
    parts = [] if coef_tc is None else [coef_tc]
    if a_sc is not None:
        parts.append(_gelu_gate(a_sc, gates, gates.shape[0] - a_sc.shape[0]))
    return parts[0] if len(parts) == 1 else jnp.concatenate(parts, axis=0)


def _peer_v_phase(coef, x1, g2, experts, v_packed, batch, n_sc):
    t, d = x1.shape
    per_b = t // PEER_TM // batch
    smem_blk, tok_blk, sel_blk, tab_spec = _peer_specs(t, d)
    parts = []
    n_tc = t - n_sc
    if n_tc:
        parts.append(pl.pallas_call(
            _peer_v_kernel,
            grid=(n_tc // PEER_TM,),
            in_specs=[smem_blk, sel_blk, tok_blk,
                      pl.BlockSpec((None, 1, d), lambda i: (i // per_b, 0, 0)),
                      tab_spec],
            out_specs=tok_blk,
            out_shape=jax.ShapeDtypeStruct((n_tc, d), jnp.float32),
            scratch_shapes=[pltpu.VMEM((PEER_TB * PEER_NSEL, LANES), jnp.float32)],
            compiler_params=_cparams(("arbitrary",)),
            name="peer_v",
        )(experts.reshape(t // PEER_TM, PEER_TM, PEER_NSEL), coef, x1, g2.reshape(batch, 1, d), v_packed))
    if n_sc:
        parts.append(_sc_peer_v(experts // HALF_ROWS, coef, x1, g2, v_packed.reshape(-1, d // 2),
                                n_tc, n_sc, t // batch))
    return parts


def kernel(x, c, ada_w, ada_b, norm1_g, norm2_g, w_in, a_qk_gain, a_lambda, a_sub_gain, b_q_gain, b_kv_gain, b_w_uv, w_out, peer_wq, peer_subkeys, peer_u, peer_v):
    b, s, d = x.shape
    t = b * s
    x2 = x.reshape(t, d)
    nchunk = len(PEER_PLAN)
    bc = b // nchunk
    tc = bc * s
    for l in range(ada_w.shape[0]):
        mod = _adaln(c, ada_w[l], ada_b[l])
        u_packed, v_packed = _pack_table(peer_u[l]), _pack_table(peer_v[l])
        lam_init = 0.8 - 0.6 * math.exp(-0.3 * l)
        outs = [None] * nchunk
        pending = {}
        prev_experts = None

        def issue_v(k, anchor):
            coef_tc, a_sc, gates, args = pending.pop(k)
            if a_sc is not None:
                anchor, a_sc = lax.optimization_barrier((anchor, a_sc))
            outs[k] = _peer_v_phase(_finish_coef(coef_tc, a_sc, gates), *args)
            return anchor

        def issue_planned(at, anchor):
            for k in [k for k in pending if PEER_PLAN[k][2] == at]:
                anchor = issue_v(k, anchor)
            return anchor

        for ck, (n_sc_u, n_sc_v, _) in enumerate(PEER_PLAN):
            xc = _after(x2[ck * tc:(ck + 1) * tc], prev_experts)
            sh1, sc1, g1, sh2, sc2, g2 = [mod[ck * bc:(ck + 1) * bc, i * d:(i + 1) * d] for i in range(6)]
            qa, ka, va, qb, kvb, qi, ki, wi = _inproj(xc, sh1, sc1, norm1_g[l], w_in[l], a_qk_gain[l],
                                                      b_q_gain[l], b_kv_gain[l], bc)
            oa = _diffattn(qa, ka, va, a_lambda[l], a_sub_gain[l], bc, lam_init)
            oa = issue_planned((ck, "diffattn"), oa)
            ob = _dsa(qi, wi, ki, qb, kvb, b_w_uv[l], bc)
            x1, h2, q = _outproj(xc, oa, ob, w_out[l], g1, sh2, sc2, norm2_g[l], peer_wq[l], bc)
            q = issue_planned((ck, "outproj"), q)
            experts, gates = _peer_select(q, peer_subkeys[l])
            prev_experts = experts
            coef_tc, a_sc = _peer_u_phase(h2, experts, gates, u_packed, n_sc_u)
            pending[ck] = (coef_tc, a_sc, gates, (x1, g2, experts, v_packed, bc, n_sc_v))
        anchor = pending[nchunk - 1][0]
        for k in sorted(pending):
            anchor = issue_v(k, anchor)
        x2 = jnp.concatenate([p for o in outs for p in o], axis=0)
    return x2.reshape(b, s, d)
```

```python
import functools
import math

import jax
import jax.numpy as jnp
import numpy as np
from jax import lax
from jax.experimental import pallas as pl
from jax.experimental.pallas import tpu as pltpu
from jax.experimental.pallas import tpu_sc as plsc

CHUNK = 64
A_HEADS, A_QK_DIM, A_V_DIM = 4, 64, 128
B_HEADS, B_LAT, B_V_DIM = 8, 128, 64
IDX_HEADS, IDX_DIM = 4, 64
DSA_TOPK_MAX = 256
A_Q = A_HEADS * 2 * A_QK_DIM
A_K = A_Q
A_V = A_HEADS * A_V_DIM
B_Q = B_HEADS * B_LAT
B_KV = B_LAT
I_Q = IDX_HEADS * IDX_DIM
I_K = IDX_DIM
I_W = IDX_HEADS
OFF_AQ, OFF_AK, OFF_AV = 0, A_Q, A_Q + A_K
OFF_BQ = OFF_AV + A_V
OFF_KV = OFF_BQ + B_Q
OFF_IQ = OFF_KV + B_KV
OFF_IK = OFF_IQ + I_Q
OFF_IW = OFF_IK + I_K
IN_COLS = OFF_IW + I_W
PEER_HEADS, PEER_NKEYS, PEER_QDIM, PEER_TOPK = 8, 128, 128, 16
EPS = 1e-6
NEG = -1e30
INT_MIN = -(2 ** 31)

SUBLANES = 8
LANES = 128
VMEM_LIMIT = 56 * 1024 * 1024

PROJ_TM = 256
A_TQ = 512
B_TQ = 256
SELECT_TM = 512
PEER_TM = 128
PEER_TB = 8
PEER_NSEL = PEER_HEADS * PEER_TOPK
PEER_PLAN = ((8192, 8192, (1, "outproj")), (8192, 8192, (3, "diffattn")), (8192, 4864, None), (0, 0, None))
HALF_ROWS = 4


def _split_bf16(x):
    hi = x.astype(jnp.bfloat16)
    lo = (x - hi.astype(jnp.float32)).astype(jnp.bfloat16)
    return hi, lo


def _dot(a, b):
    return jnp.dot(a, b, preferred_element_type=jnp.float32)


def _dot_nt(a, b):
    return lax.dot_general(a, b, (((1,), (1,)), ((), ())), preferred_element_type=jnp.float32)


def _cparams(sem):
    return pltpu.CompilerParams(dimension_semantics=sem, vmem_limit_bytes=VMEM_LIMIT)


def _adaln_kernel(c_ref, w_ref, b_ref, o_ref):
    cf = c_ref[...]
    a = cf * (1.0 / (1.0 + jnp.exp(-cf)))
    a_hi, a_lo = _split_bf16(a)
    w_hi, w_lo = _split_bf16(w_ref[...])
    o_ref[...] = _dot(a_hi, w_hi) + _dot(a_hi, w_lo) + _dot(a_lo, w_hi) + b_ref[...]


def _adaln(c, w, b):
    bsz, d = c.shape
    n = w.shape[1]
    tn = 1024
    return pl.pallas_call(
        _adaln_kernel,
        grid=(n // tn,),
        in_specs=[pl.BlockSpec((bsz, d), lambda j: (0, 0)),
                  pl.BlockSpec((d, tn), lambda j: (0, j)),
                  pl.BlockSpec((1, tn), lambda j: (0, j))],
        out_specs=pl.BlockSpec((bsz, tn), lambda j: (0, j)),
        out_shape=jax.ShapeDtypeStruct((bsz, n), jnp.float32),
        compiler_params=_cparams(("arbitrary",)),
        name="adaln",
    )(c, w, b.reshape(1, n))


def _rms_rows(x, g):
    return x * lax.rsqrt(jnp.mean(x * x, axis=-1, keepdims=True) + EPS) * g


def _group_norm_block(p, gmat, gain, n):
    hi, lo = _split_bf16(p * p)
    ss = _dot(hi, gmat) + _dot(lo, gmat)
    return p * lax.rsqrt(ss * (1.0 / n) + EPS) * gain


def _inproj_kernel(x_ref, sh_ref, sc_ref, g_ref, w_ref, gq_ref, gk_ref, gbq_ref, gkv_ref,
                   qa_ref, ka_ref, va_ref, qb_ref, kv_ref, qi_ref, ki_ref, wi_ref):
    h = _rms_rows(x_ref[...], g_ref[...]) * (1.0 + sc_ref[0]) + sh_ref[0]
    hb = h.astype(jnp.bfloat16)
    row = lax.broadcasted_iota(jnp.int32, (LANES, LANES), 0)
    col = lax.broadcasted_iota(jnp.int32, (LANES, LANES), 1)
    g64 = jnp.where((row // A_QK_DIM) == (col // A_QK_DIM), 1.0, 0.0).astype(jnp.bfloat16)
    g128 = jnp.ones((LANES, LANES), jnp.bfloat16)

    def proj(off, width):
        return _dot(hb, w_ref[:, off:off + width])

    for blk in range(A_Q // LANES):
        sl = slice(blk * LANES, (blk + 1) * LANES)
        p = proj(OFF_AQ + blk * LANES, LANES)
        qa_ref[:, sl] = (_group_norm_block(p, g64, gq_ref[...], A_QK_DIM) * (A_QK_DIM ** -0.5)).astype(qa_ref.dtype)
        p = proj(OFF_AK + blk * LANES, LANES)
        ka_ref[:, sl] = _group_norm_block(p, g64, gk_ref[...], A_QK_DIM).astype(ka_ref.dtype)
    va_ref[...] = proj(OFF_AV, A_V).astype(va_ref.dtype)
    for blk in range(B_HEADS):
        sl = slice(blk * LANES, (blk + 1) * LANES)
        p = proj(OFF_BQ + blk * LANES, LANES)
        qb_ref[:, sl] = _group_norm_block(p, g128, gbq_ref[...], B_LAT).astype(qb_ref.dtype)
    p = proj(OFF_KV, B_KV)
    kv_ref[...] = _group_norm_block(p, g128, gkv_ref[...], B_LAT).astype(kv_ref.dtype)
    qi_ref[...] = proj(OFF_IQ, I_Q).astype(qi_ref.dtype)
    tail = proj(OFF_IK, 2 * LANES)
    ki_ref[...] = tail[:, :LANES].astype(ki_ref.dtype)
    wi_ref[...] = tail[:, I_K:I_K + LANES]


def _inproj(x2, sh1, sc1, g1n, w_in, a_qk_gain, b_q_gain, b_kv_gain, batch):
    t, d = x2.shape
    per_b = (t // batch) // PROJ_TM
    wpad = OFF_IK + 2 * LANES
    w = jnp.zeros((d, wpad), jnp.bfloat16).at[:, :IN_COLS].set(w_in.astype(jnp.bfloat16))
    gq = jnp.tile(a_qk_gain[0], 2).reshape(1, LANES)
    gk = jnp.tile(a_qk_gain[1], 2).reshape(1, LANES)
    tok = lambda width: pl.BlockSpec((PROJ_TM, width), lambda i: (i, 0))
    vec = lambda width: pl.BlockSpec((1, width), lambda i: (0, 0))
    per_batch = pl.BlockSpec((1, 1, d), lambda i: (i // per_b, 0, 0))
    bf = jnp.bfloat16
    outs = [(A_Q, bf), (A_K, bf), (A_V, bf), (B_Q, bf), (B_KV, bf), (I_Q, bf), (LANES, bf), (LANES, jnp.float32)]
    return pl.pallas_call(
        _inproj_kernel,
        grid=(t // PROJ_TM,),
        in_specs=[tok(d), per_batch, per_batch, vec(d),
                  pl.BlockSpec((d, wpad), lambda i: (0, 0)),
                  vec(LANES), vec(LANES), vec(LANES), vec(LANES)],
        out_specs=[tok(wd) for wd, _ in outs],
        out_shape=[jax.ShapeDtypeStruct((t, wd), dt) for wd, dt in outs],
        compiler_params=_cparams(("arbitrary",)),
        name="inproj",
    )(x2, sh1.reshape(batch, 1, d), sc1.reshape(batch, 1, d), g1n.reshape(1, d), w,
      gq, gk, b_q_gain.reshape(1, LANES), b_kv_gain.reshape(1, LANES))


def _positions(q0, tq, s_len):
    qpos = q0 + lax.broadcasted_iota(jnp.int32, (tq, s_len), 0)
    kpos = lax.broadcasted_iota(jnp.int32, (tq, s_len), 1)
    cend = (qpos // CHUNK + 1) * CHUNK
    return qpos, kpos, cend


def _softmax_rows(s):
    m = jnp.max(s, axis=-1, keepdims=True)
    p = jnp.exp(s - m)
    return p, jnp.sum(p, axis=-1, keepdims=True)


def _diffattn_kernel(q_ref, k_ref, v_ref, lam_ref, gain_ref, o_ref, *, lam_init, q_base):
    tq, s_len = q_ref.shape[0], k_ref.shape[0]
    q0 = q_base + pl.program_id(1) * tq
    qpos, kpos, cend = _positions(q0, tq, s_len)
    dist = jnp.abs(qpos - kpos).astype(jnp.float32)
    negmask = jnp.where(kpos < cend, 0.0, NEG)
    lf = lam_ref[...]
    lam = (jnp.exp(jnp.sum(lf[0:1] * lf[1:2], axis=-1, keepdims=True))
           - jnp.exp(jnp.sum(lf[2:3] * lf[3:4], axis=-1, keepdims=True)) + lam_init)
    for h in range(A_HEADS):
        slope = 2.0 ** (-8.0 * (h + 1) / A_HEADS)
        bias = negmask - slope * dist
        v = v_ref[:, h * A_V_DIM:(h + 1) * A_V_DIM]
        outs = []
        for m in range(2):
            c0 = (h * 2 + m) * A_QK_DIM
            s = _dot_nt(q_ref[:, c0:c0 + A_QK_DIM], k_ref[:, c0:c0 + A_QK_DIM]) + bias
            p, l = _softmax_rows(s)
            outs.append(_dot(p.astype(jnp.bfloat16), v) / l)
        o = outs[0] - lam * outs[1]
        o = _rms_rows(o, gain_ref[...]) * (1.0 - lam_init)
        o_ref[:, h * A_V_DIM:(h + 1) * A_V_DIM] = o.astype(o_ref.dtype)


def _causal_groups(s_len, tq):
    span = min(tq, s_len)
    tiles = span // tq
    return [(g * tiles, tiles, (g + 1) * span) for g in range(s_len // span)]


def _qblk(tq, first_tile, width):
    return pl.BlockSpec((None, tq, width), lambda b, i: (b, first_tile + i, 0))


def _kblk(klen, width):
    return pl.BlockSpec((None, klen, width), lambda b, i: (b, 0, 0))


def _oblk(tq, width):
    return pl.BlockSpec((None, tq, width), lambda b, i: (b, i, 0))


def _diffattn(qa, ka, va, a_lambda, a_sub_gain, batch, lam_init):
    t = qa.shape[0]
    s_len = t // batch
    qa3, ka3, va3 = (a.reshape(batch, s_len, a.shape[1]) for a in (qa, ka, va))
    outs = []
    for first, tiles, klen in _causal_groups(s_len, A_TQ):
        outs.append(pl.pallas_call(
            functools.partial(_diffattn_kernel, lam_init=lam_init, q_base=first * A_TQ),
            grid=(batch, tiles),
            in_specs=[_qblk(A_TQ, first, A_Q), _kblk(klen, A_K), _kblk(klen, A_V),
                      pl.BlockSpec((4, A_QK_DIM), lambda b, i: (0, 0)),
                      pl.BlockSpec((1, A_V_DIM), lambda b, i: (0, 0))],
            out_specs=_oblk(A_TQ, A_V),
            out_shape=jax.ShapeDtypeStruct((batch, tiles * A_TQ, A_V), jnp.bfloat16),
            compiler_params=_cparams(("arbitrary", "arbitrary")),
            name="diffattn",
        )(qa3, ka3, va3, a_lambda, a_sub_gain.reshape(1, A_V_DIM)))
    return jnp.concatenate(outs, axis=1).reshape(t, A_V)


def _count(mask):
    return jnp.sum(jnp.where(mask, 1.0, 0.0), axis=-1, keepdims=True)


def _topk_mask(score, allowed, kpos, topk, s_len):
    bits = pltpu.bitcast(score + 0.0, jnp.int32)
    key = jnp.where(bits < 0, bits ^ jnp.int32(0x7FFFFFFF), bits)
    key = jnp.where(allowed, key, jnp.int32(INT_MIN))
    kf = float(topk)
    thr = jnp.where(_count(key >= 0) >= kf, jnp.int32(0), jnp.int32(INT_MIN))

    def value_bit(i, thr):
        cand = thr | (jnp.int32(1) << (30 - i))
        return jnp.where(_count(key >= cand) >= kf, cand, thr)

    thr = lax.fori_loop(0, 31, value_bit, thr)
    above = key > thr
    tie = key == thr
    need = kf - _count(above)
    nbits = (s_len - 1).bit_length()

    def pos_bit(i, pos):
        cand = pos | (jnp.int32(1) << (nbits - 1 - i))
        return jnp.where(_count(tie & (kpos < cand)) < need, cand, pos)

    pos = lax.fori_loop(0, nbits, pos_bit, jnp.zeros_like(thr))
    return allowed & (above | (tie & (kpos <= pos)))


def _dsa_kernel(qi_ref, wi_ref, ki_ref, qb_ref, kv_ref, wuv_ref, o_ref, *, topk, q_base):
    tq, s_len = qb_ref.shape[0], kv_ref.shape[0]
    q0 = q_base + pl.program_id(1) * tq
    qpos, kpos, cend = _positions(q0, tq, s_len)
    allowed = kpos < cend
    ki = ki_ref[...]
    wi = wi_ref[...]
    zpad = jnp.zeros((tq, LANES - IDX_DIM), jnp.bfloat16)
    score = jnp.zeros((tq, s_len), jnp.float32)
    for h in range(IDX_HEADS):
        qh = jnp.concatenate([qi_ref[:, h * IDX_DIM:(h + 1) * IDX_DIM], zpad], axis=-1)
        score = score + jnp.maximum(_dot_nt(qh, ki), 0.0) * wi[:, h:h + 1]
    score = score * ((IDX_DIM ** -0.5) * (IDX_HEADS ** -0.5))
    sel = _topk_mask(score, allowed, kpos, topk, s_len)

    dist = jnp.abs(qpos - kpos).astype(jnp.float32)
    negmask = jnp.where(sel, 0.0, NEG)
    kv = kv_ref[...]
    for h in range(B_HEADS):
        slope = 2.0 ** (-8.0 * (h + 1) / B_HEADS)
        s = _dot_nt(qb_ref[:, h * B_LAT:(h + 1) * B_LAT], kv) * (B_LAT ** -0.5) + (negmask - slope * dist)
        p, l = _softmax_rows(s)
        o_lat = _dot(p.astype(jnp.bfloat16), kv) / l
        o = _dot(o_lat.astype(jnp.bfloat16), wuv_ref[h])
        o_ref[:, h * B_V_DIM:(h + 1) * B_V_DIM] = o.astype(o_ref.dtype)


def _dsa(qi, wi, ki, qb, kvb, w_uv, batch):
    t = qb.shape[0]
    s_len = t // batch
    topk = min(DSA_TOPK_MAX, s_len // 4)
    width = B_HEADS * B_V_DIM
    qi3, wi3, ki3, qb3, kv3 = (a.reshape(batch, s_len, a.shape[1]) for a in (qi, wi, ki, qb, kvb))
    wuv = w_uv.astype(jnp.bfloat16)
    outs = []
    for first, tiles, klen in _causal_groups(s_len, B_TQ):
        outs.append(pl.pallas_call(
            functools.partial(_dsa_kernel, topk=topk, q_base=first * B_TQ),
            grid=(batch, tiles),
            in_specs=[_qblk(B_TQ, first, I_Q), _qblk(B_TQ, first, LANES), _kblk(klen, LANES),
                      _qblk(B_TQ, first, B_Q), _kblk(klen, B_KV),
                      pl.BlockSpec((B_HEADS, B_LAT, B_V_DIM), lambda b, i: (0, 0, 0))],
            out_specs=_oblk(B_TQ, width),
            out_shape=jax.ShapeDtypeStruct((batch, tiles * B_TQ, width), jnp.bfloat16),
            compiler_params=_cparams(("arbitrary", "arbitrary")),
            name="dsa",
        )(qi3, wi3, ki3, qb3, kv3, wuv))
    return jnp.concatenate(outs, axis=1).reshape(t, width)


def _outproj_kernel(x_ref, oa_ref, ob_ref, wo_ref, g1_ref, sh_ref, sc_ref, gn_ref, wq_ref,
                    x1_ref, h2_ref, q_ref):
    na = oa_ref.shape[1]
    y = _dot(oa_ref[...], wo_ref[:na, :]) + _dot(ob_ref[...], wo_ref[na:, :])
    x1 = x_ref[...] + g1_ref[0] * y
    x1_ref[...] = x1
    h2 = _rms_rows(x1, gn_ref[...]) * (1.0 + sc_ref[0]) + sh_ref[0]
    h2_ref[...] = h2
    q_ref[...] = _dot(h2.astype(jnp.bfloat16), wq_ref[...]).astype(q_ref.dtype)


def _outproj(x2, oa, ob, w_out, g1, sh2, sc2, g2n, peer_wq, batch):
    t, d = x2.shape
    per_b = (t // batch) // PROJ_TM
    nq = peer_wq.shape[1]
    tok = lambda width: pl.BlockSpec((PROJ_TM, width), lambda i: (i, 0))
    per_batch = pl.BlockSpec((1, 1, d), lambda i: (i // per_b, 0, 0))
    full = lambda a, b: pl.BlockSpec((a, b), lambda i: (0, 0))
    return pl.pallas_call(
        _outproj_kernel,
        grid=(t // PROJ_TM,),
        in_specs=[tok(d), tok(oa.shape[1]), tok(ob.shape[1]), full(w_out.shape[0], d),
                  per_batch, per_batch, per_batch, full(1, d), full(d, nq)],
        out_specs=[tok(d), tok(d), tok(nq)],
        out_shape=[jax.ShapeDtypeStruct((t, d), jnp.float32),
                   jax.ShapeDtypeStruct((t, d), jnp.float32),
                   jax.ShapeDtypeStruct((t, nq), jnp.bfloat16)],
        compiler_params=_cparams(("arbitrary",)),
        name="outproj",
    )(x2, oa, ob, w_out.astype(jnp.bfloat16), g1.reshape(batch, 1, d), sh2.reshape(batch, 1, d),
      sc2.reshape(batch, 1, d), g2n.reshape(1, d), peer_wq.astype(jnp.bfloat16))


def _extract_topk(vals, payload, k):
    n, tm = vals.shape
    rows = lax.broadcasted_iota(jnp.int32, (n, tm), 0).astype(jnp.float32)
    top_v, top_p = [], []
    for _ in range(k):
        m = jnp.max(vals, axis=0, keepdims=True)
        pos = jnp.min(jnp.where(vals == m, rows, float(n)), axis=0, keepdims=True)
        hit = rows == pos
        top_v.append(m)
        top_p.append(pos if payload is None else jnp.max(jnp.where(hit, payload, -1.0), axis=0, keepdims=True))
        vals = jnp.where(hit, -jnp.inf, vals)
    return jnp.concatenate(top_v, axis=0), jnp.concatenate(top_p, axis=0)


def _stair_pairs():
    return [(a, b) for a in range(PEER_TOPK) for b in range(PEER_TOPK) if (a + 1) * (b + 1) <= PEER_TOPK]


N_STAIR = len(_stair_pairs())
N_STAIR_PAD = -(-N_STAIR // SUBLANES) * SUBLANES


def _copy_rows(sel, x):
    x0 = x.astype(jnp.bfloat16)
    r1 = x - x0.astype(jnp.float32)
    x1 = r1.astype(jnp.bfloat16)
    x2 = (r1 - x1.astype(jnp.float32)).astype(jnp.bfloat16)
    return _dot(sel, x0) + _dot(sel, x1) + _dot(sel, x2)


def _peer_select_kernel(q_ref, sk_ref, sela_ref, selb_ref, e_ref, g_ref):
    tm = q_ref.shape[0]
    sela, selb = sela_ref[...], selb_ref[...]
    pad_row = lax.broadcasted_iota(jnp.int32, (N_STAIR_PAD, tm), 0) >= N_STAIR
    e_rows, g_rows = [], []
    for h in range(PEER_HEADS):
        sub = _dot_nt(sk_ref[...], q_ref[:, h * PEER_QDIM:(h + 1) * PEER_QDIM])
        sv0, si0 = _extract_topk(sub[:PEER_NKEYS], None, PEER_TOPK)
        sv1, si1 = _extract_topk(sub[PEER_NKEYS:], None, PEER_TOPK)
        cand = jnp.where(pad_row, -jnp.inf, _copy_rows(sela, sv0) + _copy_rows(selb, sv1))
        ids = (_dot(sela, si0.astype(jnp.bfloat16)) * float(PEER_NKEYS)
               + _dot(selb, si1.astype(jnp.bfloat16))) * float(HALF_ROWS)
        top_s, top_e = _extract_topk(cand, ids, PEER_TOPK)
        p = jnp.exp(top_s - top_s[0:1])
        e_rows.append(top_e)
        g_rows.append(p / jnp.sum(p, axis=0, keepdims=True))
    e_t = jnp.concatenate(e_rows, axis=0)
    g_t = jnp.concatenate(g_rows, axis=0)
    e_ref[...] = e_t.T.astype(jnp.int32)
    g_ref[...] = g_t.T


def _peer_select(q, sub_keys):
    t, nq = q.shape
    half = PEER_QDIM // 2
    sk = jnp.zeros((2 * PEER_NKEYS, PEER_QDIM), jnp.bfloat16)
    sk = sk.at[:PEER_NKEYS, :half].set(sub_keys[0].astype(jnp.bfloat16))
    sk = sk.at[PEER_NKEYS:, half:].set(sub_keys[1].astype(jnp.bfloat16))
    sel = np.zeros((2, N_STAIR_PAD, PEER_TOPK), np.float32)
    for r, (a, b) in enumerate(_stair_pairs()):
        sel[0, r, a] = 1.0
        sel[1, r, b] = 1.0
    sel = jnp.asarray(sel, jnp.bfloat16)
    tok = lambda width: pl.BlockSpec((SELECT_TM, width), lambda i: (i, 0))
    full = lambda a, b: pl.BlockSpec((a, b), lambda i: (0, 0))
    return pl.pallas_call(
        _peer_select_kernel,
        grid=(t // SELECT_TM,),
        in_specs=[tok(nq), full(2 * PEER_NKEYS, PEER_QDIM),
                  full(N_STAIR_PAD, PEER_TOPK), full(N_STAIR_PAD, PEER_TOPK)],
        out_specs=[tok(PEER_NSEL), tok(PEER_NSEL)],
        out_shape=[jax.ShapeDtypeStruct((t, PEER_NSEL), jnp.int32),
                   jax.ShapeDtypeStruct((t, PEER_NSEL), jnp.float32)],
        compiler_params=_cparams(("arbitrary",)),
        name="peer_select",
    )(q, sk, sel[0], sel[1])


def _gather_row(tab_ref, row):
    w = tab_ref[pl.ds(pl.multiple_of(row, HALF_ROWS), HALF_ROWS), :]
    hi = pltpu.bitcast(w & jnp.uint32(0xFFFF0000), jnp.float32)
    lo = pltpu.bitcast(w << 16, jnp.float32)
    return hi, lo


def _peer_u_kernel(idx_ref, h_ref, gate_ref, tab_ref, coef_ref, abuf_ref):
    ones = jnp.ones((2 * LANES, LANES), jnp.bfloat16)
    lane = lax.broadcasted_iota(jnp.int32, (PEER_NSEL, PEER_TM), 1)
    rows_per_tok = PEER_NSEL * HALF_ROWS

    def batch(b, a_t):
        def token(tb, c2):
            t = b * PEER_TB + tb
            x8 = h_ref[pl.ds(t, 1), :].reshape(SUBLANES, LANES)
            xh, xl = x8[:HALF_ROWS], x8[HALF_ROWS:]
            off = pl.multiple_of(tb * rows_per_tok, rows_per_tok)
            for j in range(PEER_NSEL):
                hi, lo = _gather_row(tab_ref, idx_ref[0, t, j])
                abuf_ref[pl.ds(off + j * HALF_ROWS, HALF_ROWS), :] = hi * xh + lo * xl
            return c2

        lax.fori_loop(0, PEER_TB, token, 0)
        parts = []
        for tb in range(PEER_TB):
            ab = abuf_ref[pl.ds(tb * rows_per_tok, PEER_NSEL, stride=HALF_ROWS), :]
            for s in range(1, HALF_ROWS):
                ab = ab + abuf_ref[pl.ds(tb * rows_per_tok + s, PEER_NSEL, stride=HALF_ROWS), :]
            parts.append(ab)
        ab_hi, ab_lo = _split_bf16(jnp.concatenate(parts, axis=0))
        r = _dot(jnp.concatenate([ab_hi, ab_lo], axis=1), ones)
        for tb in range(PEER_TB):
            a_t = jnp.where(lane == b * PEER_TB + tb, r[tb * PEER_NSEL:(tb + 1) * PEER_NSEL], a_t)
        return a_t

    a_t = lax.fori_loop(0, PEER_TM // PEER_TB, batch, jnp.zeros((PEER_NSEL, PEER_TM), jnp.float32))
    a = a_t.T
    coef_ref[...] = 0.5 * a * (1.0 + lax.erf(a * (2.0 ** -0.5))) * gate_ref[...]


def _peer_v_kernel(idx_ref, coef_ref, x_ref, g2_ref, tab_ref, out_ref, cb_ref):
    g2 = g2_ref[...]
    ones = jnp.ones((2 * LANES, LANES), jnp.bfloat16)
    eye = (lax.broadcasted_iota(jnp.int32, (PEER_NSEL, LANES), 0)
           == lax.broadcasted_iota(jnp.int32, (PEER_NSEL, LANES), 1))
    nacc = 2

    def batch(b, carry):
        for tb in range(PEER_TB):
            row = coef_ref[pl.ds(b * PEER_TB + tb, 1), :]
            d_hi, d_lo = _split_bf16(jnp.where(eye, jnp.broadcast_to(row, (PEER_NSEL, LANES)), 0.0))
            cb_ref[pl.ds(tb * PEER_NSEL, PEER_NSEL), :] = _dot(jnp.concatenate([d_hi, d_lo], axis=1), ones)

        def token(tb, c2):
            t = b * PEER_TB + tb
            zero = jnp.zeros((HALF_ROWS, LANES), jnp.float32)
            acc_h = [zero] * nacc
            acc_l = [zero] * nacc
            cbase = pl.multiple_of(tb * PEER_NSEL, PEER_NSEL)
            for j in range(PEER_NSEL):
                hi, lo = _gather_row(tab_ref, idx_ref[0, t, j])
                c = jnp.broadcast_to(cb_ref[pl.ds(cbase + j, 1), :], (HALF_ROWS, LANES))
                acc_h[j % nacc] = acc_h[j % nacc] + c * hi
                acc_l[j % nacc] = acc_l[j % nacc] + c * lo
            y8 = jnp.concatenate([acc_h[0] + acc_h[1], acc_l[0] + acc_l[1]], axis=0)
            out_ref[pl.ds(t, 1), :] = x_ref[pl.ds(t, 1), :] + g2 * y8.reshape(1, SUBLANES * LANES)
            return c2

        lax.fori_loop(0, PEER_TB, token, 0)
        return carry

    lax.fori_loop(0, PEER_TM // PEER_TB, batch, 0)


def _sc_peer_u(expert_ids, h2, tab, first, count):
    info = plsc.get_sparse_core_info()
    lanes, workers = info.num_lanes, info.num_cores * info.num_subcores
    half = tab.shape[1]
    per_w = count // workers
    nhalf = PEER_NSEL // 2
    assert per_w * workers == count and per_w >= 2 and nhalf % lanes == 0 and half % lanes == 0
    rows_blk = 4
    kunroll = 2
    mesh = plsc.VectorSubcoreMesh(core_axis_name="c", subcore_axis_name="s")

    @functools.partial(
        pl.kernel, mesh=mesh, name="peer_u_sc",
        compiler_params=pltpu.CompilerParams(needs_layout_passes=False),
        out_type=jax.ShapeDtypeStruct((count, PEER_NSEL), jnp.float32),
        scratch_types=[pltpu.VMEM((2, PEER_NSEL), jnp.int32),
                       pltpu.VMEM((nhalf, half), jnp.uint32), pltpu.VMEM((nhalf, half), jnp.uint32),
                       pltpu.VMEM((2, 2 * half), jnp.float32), pltpu.VMEM((2, PEER_NSEL), jnp.float32),
                       pltpu.SemaphoreType.DMA, pltpu.SemaphoreType.DMA, pltpu.SemaphoreType.DMA,
                       pltpu.SemaphoreType.DMA, pltpu.SemaphoreType.DMA((2,))])
    def run(idx_hbm, h_hbm, tab_hbm, out_hbm, idx_v, rows_a, rows_b, x_v, a_v, sem_a, sem_b, sem_i, sem_x, sem_o):
        wid = lax.axis_index("s") * info.num_cores + lax.axis_index("c")
        base = first + wid * per_w
        lane = lax.iota(jnp.int32, lanes)
        mask_hi = jnp.full((lanes,), 0xFFFF0000, jnp.uint32)

        def gather(p, part, rows, sem):
            return pltpu.make_async_copy(tab_hbm.at[idx_v.at[p, pl.ds(part * nhalf, nhalf)]], rows, sem)

        def load_idx(t, p):
            return pltpu.make_async_copy(idx_hbm.at[t], idx_v.at[p], sem_i)

        def load_x(t, p):
            return pltpu.make_async_copy(h_hbm.at[t], x_v.at[p], sem_x)

        def store_a(t, p):
            return pltpu.make_async_copy(a_v.at[p], out_hbm.at[t - first], sem_o.at[p])

        def compute(rows, p, part):
            def group(g, c2):
                vec = jnp.zeros((lanes,), jnp.float32)
                for rb in range(lanes // rows_blk):
                    def kslice(kk, accs):
                        accs = list(accs)
                        for ku in range(kunroll):
                            off = pl.multiple_of((kk * kunroll + ku) * lanes, lanes)
                            xh = x_v[p, pl.ds(off, lanes)]
                            xl = x_v[p, pl.ds(half + off, lanes)]
                            for r in range(rows_blk):
                                w = rows[g * lanes + rb * rows_blk + r, pl.ds(off, lanes)]
                                hi = plsc.bitcast(w & mask_hi, jnp.float32)
                                lo = plsc.bitcast(w << 16, jnp.float32)
                                accs[r] = accs[r] + hi * xh + lo * xl
                        return tuple(accs)

                    zero = jnp.zeros((lanes,), jnp.float32)
                    accs = lax.fori_loop(0, half // lanes // kunroll, kslice, (zero,) * rows_blk)
                    for r in range(rows_blk):
                        vec = jnp.where(lane == rb * rows_blk + r, jnp.sum(accs[r]), vec)
                a_v[p, pl.ds(pl.multiple_of(part * nhalf + g * lanes, lanes), lanes)] = vec
                return c2

            lax.fori_loop(0, nhalf // lanes, group, 0)

        load_idx(base, 0).start()
        load_x(base, 0).start()
        load_idx(base, 0).wait()
        gather(0, 0, rows_a, sem_a).start()
        gather(0, 1, rows_b, sem_b).start()
        load_x(base, 0).wait()

        def token(i, carry):
            t = base + i
            p = lax.rem(i, 2)
            q = 1 - p
            tn = jnp.minimum(t + 1, base + per_w - 1)
            load_idx(tn, q).start()
            load_x(tn, q).start()

            @pl.when(i >= 2)
            def _():
                store_a(t, p).wait()

            gather(p, 0, rows_a, sem_a).wait()
            compute(rows_a, p, 0)
            load_idx(tn, q).wait()
            gather(q, 0, rows_a, sem_a).start()
            gather(p, 1, rows_b, sem_b).wait()
            compute(rows_b, p, 1)
            gather(q, 1, rows_b, sem_b).start()
            load_x(tn, q).wait()
            store_a(t, p).start()
            return carry

        lax.fori_loop(0, per_w, token, 0)
        gather(0, 0, rows_a, sem_a).wait()
        gather(0, 1, rows_b, sem_b).wait()
        store_a(base, 0).wait()
        store_a(base, 1).wait()

    return run(expert_ids, h2, tab)


def _sc_peer_v(expert_ids, coef, x1, g2, tab, first, count, per_batch):
    info = plsc.get_sparse_core_info()
    lanes, workers = info.num_lanes, info.num_cores * info.num_subcores
    half = tab.shape[1]
    d = 2 * half
    per_w = count // workers
    nhalf = PEER_NSEL // 2
    sblk = 16
    runroll = 1
    assert per_w * workers == count and per_w >= 2 and nhalf % runroll == 0 and half % (lanes * sblk) == 0
    mesh = plsc.VectorSubcoreMesh(core_axis_name="c", subcore_axis_name="s")
    dma = pltpu.SemaphoreType.DMA

    @functools.partial(
        pl.kernel, mesh=mesh, name="peer_v_sc",
        compiler_params=pltpu.CompilerParams(needs_layout_passes=False),
        out_type=jax.ShapeDtypeStruct((count, d), jnp.float32),
        scratch_types=[pltpu.VMEM((2, PEER_NSEL), jnp.int32), pltpu.VMEM((2 * PEER_NSEL,), jnp.float32),
                       pltpu.VMEM((nhalf, half), jnp.uint32), pltpu.VMEM((nhalf, half), jnp.uint32),
                       pltpu.VMEM((2, d), jnp.float32), pltpu.VMEM((2, d), jnp.float32),
                       pltpu.VMEM((d,), jnp.float32), pltpu.VMEM((2, d), jnp.float32),
                       dma, dma, dma, dma, dma, dma, dma((2,))])
    def run(idx_hbm, coef_hbm, x_hbm, g_hbm, tab_hbm, out_hbm,
            idx_v, coef_v, rows_a, rows_b, x_v, g_v, y_v, o_v, sem_a, sem_b, sem_i, sem_c, sem_x, sem_g, sem_o):
        wid = lax.axis_index("s") * info.num_cores + lax.axis_index("c")
        base = first + wid * per_w
        mask_hi = jnp.full((lanes,), 0xFFFF0000, jnp.uint32)

        def gather(p, part, rows, sem):
            return pltpu.make_async_copy(tab_hbm.at[idx_v.at[p, pl.ds(part * nhalf, nhalf)]], rows, sem)

        def loads(t, p):
            cslot = coef_v.at[pl.ds(pl.multiple_of(p * PEER_NSEL, PEER_NSEL), PEER_NSEL)]
            return (pltpu.make_async_copy(idx_hbm.at[t], idx_v.at[p], sem_i),
                    pltpu.make_async_copy(coef_hbm.at[t], cslot, sem_c),
                    pltpu.make_async_copy(x_hbm.at[t], x_v.at[p], sem_x),
                    pltpu.make_async_copy(g_hbm.at[t // per_batch], g_v.at[p], sem_g))

        def store_o(t, p):
            return pltpu.make_async_copy(o_v.at[p], out_hbm.at[t - first], sem_o.at[p])

        def compute(rows, p, part):
            cbase = p * PEER_NSEL + part * nhalf
            for sb in range(half // lanes // sblk):
                offs = [(sb * sblk + s) * lanes for s in range(sblk)]
                if part == 0:
                    init = tuple(jnp.zeros((lanes,), jnp.float32) for _ in range(2 * sblk))
                else:
                    init = tuple([y_v[pl.ds(o, lanes)] for o in offs] + [y_v[pl.ds(half + o, lanes)] for o in offs])

                def rowloop(rr, accs):
                    accs = list(accs)
                    for ru in range(runroll):
                        r = rr * runroll + ru
                        c = plsc.load_gather(coef_v, [jnp.full((lanes,), cbase + r, jnp.int32)])
                        for s in range(sblk):
                            w = rows[r, pl.ds(offs[s], lanes)]
                            hi = plsc.bitcast(w & mask_hi, jnp.float32)
                            lo = plsc.bitcast(w << 16, jnp.float32)
                            accs[s] = accs[s] + c * hi
                            accs[sblk + s] = accs[sblk + s] + c * lo
                    return tuple(accs)

                accs = lax.fori_loop(0, nhalf // runroll, rowloop, init)
                for s in range(sblk):
                    for hl, o in ((0, offs[s]), (1, half + offs[s])):
                        if part == 0:
                            y_v[pl.ds(o, lanes)] = accs[hl * sblk + s]
                        else:
                            o_v[p, pl.ds(o, lanes)] = (x_v[p, pl.ds(o, lanes)]
                                                       + g_v[p, pl.ds(o, lanes)] * accs[hl * sblk + s])

        head = loads(base, 0)
        for c in head:
            c.start()
        head[0].wait()
        gather(0, 0, rows_a, sem_a).start()
        gather(0, 1, rows_b, sem_b).start()
        for c in head[1:]:
            c.wait()

        def token(i, carry):
            t = base + i
            p = lax.rem(i, 2)
            q = 1 - p
            tn = jnp.minimum(t + 1, base + per_w - 1)
            nxt = loads(tn, q)
            for c in nxt:
                c.start()

            @pl.when(i >= 2)
            def _():
                store_o(t, p).wait()

            gather(p, 0, rows_a, sem_a).wait()
            compute(rows_a, p, 0)
            nxt[0].wait()
            gather(q, 0, rows_a, sem_a).start()
            gather(p, 1, rows_b, sem_b).wait()
            compute(rows_b, p, 1)
            gather(q, 1, rows_b, sem_b).start()
            for c in nxt[1:]:
                c.wait()
            store_o(t, p).start()
            return carry

        lax.fori_loop(0, per_w, token, 0)
        gather(0, 0, rows_a, sem_a).wait()
        gather(0, 1, rows_b, sem_b).wait()
        store_o(base, 0).wait()
        store_o(base, 1).wait()

    return run(expert_ids, coef, x1, g2, tab)


def _gelu_gate_kernel(a_ref, g_ref, o_ref):
    a = a_ref[...]
    o_ref[...] = 0.5 * a * (1.0 + lax.erf(a * (2.0 ** -0.5))) * g_ref[...]


def _gelu_gate(a, gates, first):
    count = a.shape[0]
    return pl.pallas_call(
        _gelu_gate_kernel,
        grid=(count // PEER_TM,),
        in_specs=[pl.BlockSpec((PEER_TM, PEER_NSEL), lambda i: (i, 0)),
                  pl.BlockSpec((PEER_TM, PEER_NSEL), lambda i: (first // PEER_TM + i, 0))],
        out_specs=pl.BlockSpec((PEER_TM, PEER_NSEL), lambda i: (i, 0)),
        out_shape=jax.ShapeDtypeStruct((count, PEER_NSEL), jnp.float32),
        compiler_params=_cparams(("arbitrary",)),
        name="gelu_gate",
    )(a, gates)


def _pack_table(tab):
    n, d = tab.shape
    bits = lax.bitcast_convert_type(tab.astype(jnp.bfloat16), jnp.uint16).astype(jnp.uint32)
    packed = (bits[:, : d // 2] << 16) | bits[:, d // 2:]
    return packed.reshape(n * HALF_ROWS, LANES)


def _peer_specs(t, d):
    smem_blk = pl.BlockSpec((1, PEER_TM, PEER_NSEL), lambda i: (i, 0, 0), memory_space=pltpu.SMEM)
    tok_blk = pl.BlockSpec((PEER_TM, d), lambda i: (i, 0))
    sel_blk = pl.BlockSpec((PEER_TM, PEER_NSEL), lambda i: (i, 0))
    return smem_blk, tok_blk, sel_blk, pl.BlockSpec(memory_space=pltpu.VMEM)


def _after(x, prev):
    return x if prev is None else lax.optimization_barrier((x, prev))[0]


def _peer_u_phase(h2, experts, gates, u_packed, n_sc):
    t, d = h2.shape
    assert d == SUBLANES * LANES and d // 2 == HALF_ROWS * LANES and PEER_TM % PEER_TB == 0
    smem_blk, tok_blk, sel_blk, tab_spec = _peer_specs(t, d)
    parts = []
    n_tc = t - n_sc
    if n_tc:
        parts.append(pl.pallas_call(
            _peer_u_kernel,
            grid=(n_tc // PEER_TM,),
            in_specs=[smem_blk, tok_blk, sel_blk, tab_spec],
            out_specs=sel_blk,
            out_shape=jax.ShapeDtypeStruct((n_tc, PEER_NSEL), jnp.float32),
            scratch_shapes=[pltpu.VMEM((PEER_TB * PEER_NSEL * HALF_ROWS, LANES), jnp.float32)],
            compiler_params=_cparams(("arbitrary",)),
            name="peer_u",
        )(experts.reshape(t // PEER_TM, PEER_TM, PEER_NSEL), h2, gates, u_packed))
    a_sc = None
    if n_sc:
        a_sc = _sc_peer_u(experts // HALF_ROWS, h2, u_packed.reshape(-1, d // 2), n_tc, n_sc)
    return (parts[0] if parts else None), a_sc


def _finish_coef(coef_tc, a_sc, gates):
    parts = [] if coef_tc is None else [coef_tc]
    if a_sc is not None:
        parts.append(_gelu_gate(a_sc, gates, gates.shape[0] - a_sc.shape[0]))
    return parts[0] if len(parts) == 1 else jnp.concatenate(parts, axis=0)


def _peer_v_phase(coef, x1, g2, experts, v_packed, batch, n_sc):
    t, d = x1.shape
    per_b = t // PEER_TM // batch
    smem_blk, tok_blk, sel_blk, tab_spec = _peer_specs(t, d)
    parts = []
    n_tc = t - n_sc
    if n_tc:
        parts.append(pl.pallas_call(
            _peer_v_kernel,
            grid=(n_tc // PEER_TM,),
            in_specs=[smem_blk, sel_blk, tok_blk,
                      pl.BlockSpec((None, 1, d), lambda i: (i // per_b, 0, 0)),
                      tab_spec],
            out_specs=tok_blk,
            out_shape=jax.ShapeDtypeStruct((n_tc, d), jnp.float32),
            scratch_shapes=[pltpu.VMEM((PEER_TB * PEER_NSEL, LANES), jnp.float32)],
            compiler_params=_cparams(("arbitrary",)),
            name="peer_v",
        )(experts.reshape(t // PEER_TM, PEER_TM, PEER_NSEL), coef, x1, g2.reshape(batch, 1, d), v_packed))
    if n_sc:
        parts.append(_sc_peer_v(experts // HALF_ROWS, coef, x1, g2, v_packed.reshape(-1, d // 2),
                                n_tc, n_sc, t // batch))
    return parts


def kernel(x, c, ada_w, ada_b, norm1_g, norm2_g, w_in, a_qk_gain, a_lambda, a_sub_gain, b_q_gain, b_kv_gain, b_w_uv, w_out, peer_wq, peer_subkeys, peer_u, peer_v):
    b, s, d = x.shape
    t = b * s
    x2 = x.reshape(t, d)
    nchunk = len(PEER_PLAN)
    bc = b // nchunk
    tc = bc * s
    for l in range(ada_w.shape[0]):
        mod = _adaln(c, ada_w[l], ada_b[l])
        u_packed, v_packed = _pack_table(peer_u[l]), _pack_table(peer_v[l])
        lam_init = 0.8 - 0.6 * math.exp(-0.3 * l)
        outs = [None] * nchunk
        pending = {}
        prev_experts = None

        def issue_v(k, anchor):
            coef_tc, a_sc, gates, args = pending.pop(k)
            if a_sc is not None:
                anchor, a_sc = lax.optimization_barrier((anchor, a_sc))
            outs[k] = _peer_v_phase(_finish_coef(coef_tc, a_sc, gates), *args)
            return anchor

        def issue_planned(at, anchor):
            for k in [k for k in pending if PEER_PLAN[k][2] == at]:
                anchor = issue_v(k, anchor)
            return anchor

        for ck, (n_sc_u, n_sc_v, _) in enumerate(PEER_PLAN):
            xc = _after(x2[ck * tc:(ck + 1) * tc], prev_experts)
            sh1, sc1, g1, sh2, sc2, g2 = [mod[ck * bc:(ck + 1) * bc, i * d:(i + 1) * d] for i in range(6)]
            qa, ka, va, qb, kvb, qi, ki, wi = _inproj(xc, sh1, sc1, norm1_g[l], w_in[l], a_qk_gain[l],
                                                      b_q_gain[l], b_kv_gain[l], bc)
            oa = _diffattn(qa, ka, va, a_lambda[l], a_sub_gain[l], bc, lam_init)
            oa = issue_planned((ck, "diffattn"), oa)
            ob = _dsa(qi, wi, ki, qb, kvb, b_w_uv[l], bc)
            x1, h2, q = _outproj(xc, oa, ob, w_out[l], g1, sh2, sc2, norm2_g[l], peer_wq[l], bc)
            q = issue_planned((ck, "outproj"), q)
            experts, gates = _peer_select(q, peer_subkeys[l])
            prev_experts = experts
            coef_tc, a_sc = _peer_u_phase(h2, experts, gates, u_packed, n_sc_u)
            pending[ck] = (coef_tc, a_sc, gates, (x1, g2, experts, v_packed, bc, n_sc_v))
        anchor = pending[nchunk - 1][0]
        for k in sorted(pending):
            anchor = issue_v(k, anchor)
        x2 = jnp.concatenate([p for o in outs for p in o], axis=0)
    return x2.reshape(b, s, d)
```

```python
import functools
import math

import jax
import jax.numpy as jnp
import numpy as np
from jax import lax
from jax.experimental import pallas as pl
from jax.experimental.pallas import tpu as pltpu
from jax.experimental.pallas import tpu_sc as plsc

CHUNK = 64
A_HEADS, A_QK_DIM, A_V_DIM = 4, 64, 128
B_HEADS, B_LAT, B_V_DIM = 8, 128, 64
IDX_HEADS, IDX_DIM = 4, 64
DSA_TOPK_MAX = 256
A_Q = A_HEADS * 2 * A_QK_DIM
A_K = A_Q
A_V = A_HEADS * A_V_DIM
B_Q = B_HEADS * B_LAT
B_KV = B_LAT
I_Q = IDX_HEADS * IDX_DIM
I_K = IDX_DIM
I_W = IDX_HEADS
OFF_AQ, OFF_AK, OFF_AV = 0, A_Q, A_Q + A_K
OFF_BQ = OFF_AV + A_V
OFF_KV = OFF_BQ + B_Q
OFF_IQ = OFF_KV + B_KV
OFF_IK = OFF_IQ + I_Q
OFF_IW = OFF_IK + I_K
IN_COLS = OFF_IW + I_W
PEER_HEADS, PEER_NKEYS, PEER_QDIM, PEER_TOPK = 8, 128, 128, 16
EPS = 1e-6
NEG = -1e30
INT_MIN = -(2 ** 31)

SUBLANES = 8
LANES = 128
VMEM_LIMIT = 56 * 1024 * 1024

PROJ_TM = 256
A_TQ = 512
B_TQ = 256
SELECT_TM = 512
PEER_TM = 128
PEER_TB = 16
PEER_NSEL = PEER_HEADS * PEER_TOPK
PEER_PLAN = ((8192, 8192, (1, "outproj")), (8192, 8192, (3, "diffattn")), (8192, 4864, None), (0, 0, None))
HALF_ROWS = 4


def _split_bf16(x):
    hi = x.astype(jnp.bfloat16)
    lo = (x - hi.astype(jnp.float32)).astype(jnp.bfloat16)
    return hi, lo


def _dot(a, b):
    return jnp.dot(a, b, preferred_element_type=jnp.float32)


def _dot_nt(a, b):
    return lax.dot_general(a, b, (((1,), (1,)), ((), ())), preferred_element_type=jnp.float32)


def _cparams(sem):
    return pltpu.CompilerParams(dimension_semantics=sem, vmem_limit_bytes=VMEM_LIMIT)


def _adaln_kernel(c_ref, w_ref, b_ref, o_ref):
    cf = c_ref[...]
    a = cf * (1.0 / (1.0 + jnp.exp(-cf)))
    a_hi, a_lo = _split_bf16(a)
    w_hi, w_lo = _split_bf16(w_ref[...])
    o_ref[...] = _dot(a_hi, w_hi) + _dot(a_hi, w_lo) + _dot(a_lo, w_hi) + b_ref[...]


def _adaln(c, w, b):
    bsz, d = c.shape
    n = w.shape[1]
    tn = 1024
    return pl.pallas_call(
        _adaln_kernel,
        grid=(n // tn,),
        in_specs=[pl.BlockSpec((bsz, d), lambda j: (0, 0)),
                  pl.BlockSpec((d, tn), lambda j: (0, j)),
                  pl.BlockSpec((1, tn), lambda j: (0, j))],
        out_specs=pl.BlockSpec((bsz, tn), lambda j: (0, j)),
        out_shape=jax.ShapeDtypeStruct((bsz, n), jnp.float32),
        compiler_params=_cparams(("arbitrary",)),
        name="adaln",
    )(c, w, b.reshape(1, n))


def _rms_rows(x, g):
    return x * lax.rsqrt(jnp.mean(x * x, axis=-1, keepdims=True) + EPS) * g


def _group_norm_block(p, gmat, gain, n):
    hi, lo = _split_bf16(p * p)
    ss = _dot(hi, gmat) + _dot(lo, gmat)
    return p * lax.rsqrt(ss * (1.0 / n) + EPS) * gain


def _inproj_kernel(x_ref, sh_ref, sc_ref, g_ref, w_ref, gq_ref, gk_ref, gbq_ref, gkv_ref,
                   qa_ref, ka_ref, va_ref, qb_ref, kv_ref, qi_ref, ki_ref, wi_ref):
    h = _rms_rows(x_ref[...], g_ref[...]) * (1.0 + sc_ref[0]) + sh_ref[0]
    hb = h.astype(jnp.bfloat16)
    row = lax.broadcasted_iota(jnp.int32, (LANES, LANES), 0)
    col = lax.broadcasted_iota(jnp.int32, (LANES, LANES), 1)
    g64 = jnp.where((row // A_QK_DIM) == (col // A_QK_DIM), 1.0, 0.0).astype(jnp.bfloat16)
    g128 = jnp.ones((LANES, LANES), jnp.bfloat16)

    def proj(off, width):
        return _dot(hb, w_ref[:, off:off + width])

    for blk in range(A_Q // LANES):
        sl = slice(blk * LANES, (blk + 1) * LANES)
        p = proj(OFF_AQ + blk * LANES, LANES)
        qa_ref[:, sl] = (_group_norm_block(p, g64, gq_ref[...], A_QK_DIM) * (A_QK_DIM ** -0.5)).astype(qa_ref.dtype)
        p = proj(OFF_AK + blk * LANES, LANES)
        ka_ref[:, sl] = _group_norm_block(p, g64, gk_ref[...], A_QK_DIM).astype(ka_ref.dtype)
    va_ref[...] = proj(OFF_AV, A_V).astype(va_ref.dtype)
    for blk in range(B_HEADS):
        sl = slice(blk * LANES, (blk + 1) * LANES)
        p = proj(OFF_BQ + blk * LANES, LANES)
        qb_ref[:, sl] = _group_norm_block(p, g128, gbq_ref[...], B_LAT).astype(qb_ref.dtype)
    p = proj(OFF_KV, B_KV)
    kv_ref[...] = _group_norm_block(p, g128, gkv_ref[...], B_LAT).astype(kv_ref.dtype)
    qi_ref[...] = proj(OFF_IQ, I_Q).astype(qi_ref.dtype)
    tail = proj(OFF_IK, 2 * LANES)
    ki_ref[...] = tail[:, :LANES].astype(ki_ref.dtype)
    wi_ref[...] = tail[:, I_K:I_K + LANES]


def _inproj(x2, sh1, sc1, g1n, w_in, a_qk_gain, b_q_gain, b_kv_gain, batch):
    t, d = x2.shape
    per_b = (t // batch) // PROJ_TM
    wpad = OFF_IK + 2 * LANES
    w = jnp.zeros((d, wpad), jnp.bfloat16).at[:, :IN_COLS].set(w_in.astype(jnp.bfloat16))
    gq = jnp.tile(a_qk_gain[0], 2).reshape(1, LANES)
    gk = jnp.tile(a_qk_gain[1], 2).reshape(1, LANES)
    tok = lambda width: pl.BlockSpec((PROJ_TM, width), lambda i: (i, 0))
    vec = lambda width: pl.BlockSpec((1, width), lambda i: (0, 0))
    per_batch = pl.BlockSpec((1, 1, d), lambda i: (i // per_b, 0, 0))
    bf = jnp.bfloat16
    outs = [(A_Q, bf), (A_K, bf), (A_V, bf), (B_Q, bf), (B_KV, bf), (I_Q, bf), (LANES, bf), (LANES, jnp.float32)]
    return pl.pallas_call(
        _inproj_kernel,
        grid=(t // PROJ_TM,),
        in_specs=[tok(d), per_batch, per_batch, vec(d),
                  pl.BlockSpec((d, wpad), lambda i: (0, 0)),
                  vec(LANES), vec(LANES), vec(LANES), vec(LANES)],
        out_specs=[tok(wd) for wd, _ in outs],
        out_shape=[jax.ShapeDtypeStruct((t, wd), dt) for wd, dt in outs],
        compiler_params=_cparams(("arbitrary",)),
        name="inproj",
    )(x2, sh1.reshape(batch, 1, d), sc1.reshape(batch, 1, d), g1n.reshape(1, d), w,
      gq, gk, b_q_gain.reshape(1, LANES), b_kv_gain.reshape(1, LANES))


def _positions(q0, tq, s_len):
    qpos = q0 + lax.broadcasted_iota(jnp.int32, (tq, s_len), 0)
    kpos = lax.broadcasted_iota(jnp.int32, (tq, s_len), 1)
    cend = (qpos // CHUNK + 1) * CHUNK
    return qpos, kpos, cend


def _softmax_rows(s):
    m = jnp.max(s, axis=-1, keepdims=True)
    p = jnp.exp(s - m)
    return p, jnp.sum(p, axis=-1, keepdims=True)


def _diffattn_kernel(q_ref, k_ref, v_ref, lam_ref, gain_ref, o_ref, *, lam_init, q_base):
    tq, s_len = q_ref.shape[0], k_ref.shape[0]
    q0 = q_base + pl.program_id(1) * tq
    qpos, kpos, cend = _positions(q0, tq, s_len)
    dist = jnp.abs(qpos - kpos).astype(jnp.float32)
    negmask = jnp.where(kpos < cend, 0.0, NEG)
    lf = lam_ref[...]
    lam = (jnp.exp(jnp.sum(lf[0:1] * lf[1:2], axis=-1, keepdims=True))
           - jnp.exp(jnp.sum(lf[2:3] * lf[3:4], axis=-1, keepdims=True)) + lam_init)
    for h in range(A_HEADS):
        slope = 2.0 ** (-8.0 * (h + 1) / A_HEADS)
        bias = negmask - slope * dist
        v = v_ref[:, h * A_V_DIM:(h + 1) * A_V_DIM]
        outs = []
        for m in range(2):
            c0 = (h * 2 + m) * A_QK_DIM
            s = _dot_nt(q_ref[:, c0:c0 + A_QK_DIM], k_ref[:, c0:c0 + A_QK_DIM]) + bias
            p, l = _softmax_rows(s)
            outs.append(_dot(p.astype(jnp.bfloat16), v) / l)
        o = outs[0] - lam * outs[1]
        o = _rms_rows(o, gain_ref[...]) * (1.0 - lam_init)
        o_ref[:, h * A_V_DIM:(h + 1) * A_V_DIM] = o.astype(o_ref.dtype)


def _causal_groups(s_len, tq):
    span = min(tq, s_len)
    tiles = span // tq
    return [(g * tiles, tiles, (g + 1) * span) for g in range(s_len // span)]


def _qblk(tq, first_tile, width):
    return pl.BlockSpec((None, tq, width), lambda b, i: (b, first_tile + i, 0))


def _kblk(klen, width):
    return pl.BlockSpec((None, klen, width), lambda b, i: (b, 0, 0))


def _oblk(tq, width):
    return pl.BlockSpec((None, tq, width), lambda b, i: (b, i, 0))


def _diffattn(qa, ka, va, a_lambda, a_sub_gain, batch, lam_init):
    t = qa.shape[0]
    s_len = t // batch
    qa3, ka3, va3 = (a.reshape(batch, s_len, a.shape[1]) for a in (qa, ka, va))
    outs = []
    for first, tiles, klen in _causal_groups(s_len, A_TQ):
        outs.append(pl.pallas_call(
            functools.partial(_diffattn_kernel, lam_init=lam_init, q_base=first * A_TQ),
            grid=(batch, tiles),
            in_specs=[_qblk(A_TQ, first, A_Q), _kblk(klen, A_K), _kblk(klen, A_V),
                      pl.BlockSpec((4, A_QK_DIM), lambda b, i: (0, 0)),
                      pl.BlockSpec((1, A_V_DIM), lambda b, i: (0, 0))],
            out_specs=_oblk(A_TQ, A_V),
            out_shape=jax.ShapeDtypeStruct((batch, tiles * A_TQ, A_V), jnp.bfloat16),
            compiler_params=_cparams(("arbitrary", "arbitrary")),
            name="diffattn",
        )(qa3, ka3, va3, a_lambda, a_sub_gain.reshape(1, A_V_DIM)))
    return jnp.concatenate(outs, axis=1).reshape(t, A_V)


def _count(mask):
    return jnp.sum(jnp.where(mask, 1.0, 0.0), axis=-1, keepdims=True)


def _topk_mask(score, allowed, kpos, topk, s_len):
    bits = pltpu.bitcast(score + 0.0, jnp.int32)
    key = jnp.where(bits < 0, bits ^ jnp.int32(0x7FFFFFFF), bits)
    key = jnp.where(allowed, key, jnp.int32(INT_MIN))
    kf = float(topk)
    thr = jnp.where(_count(key >= 0) >= kf, jnp.int32(0), jnp.int32(INT_MIN))

    def value_bit(i, thr):
        cand = thr | (jnp.int32(1) << (30 - i))
        return jnp.where(_count(key >= cand) >= kf, cand, thr)

    thr = lax.fori_loop(0, 31, value_bit, thr)
    above = key > thr
    tie = key == thr
    need = kf - _count(above)
    nbits = (s_len - 1).bit_length()

    def pos_bit(i, pos):
        cand = pos | (jnp.int32(1) << (nbits - 1 - i))
        return jnp.where(_count(tie & (kpos < cand)) < need, cand, pos)

    pos = lax.fori_loop(0, nbits, pos_bit, jnp.zeros_like(thr))
    return allowed & (above | (tie & (kpos <= pos)))


def _dsa_kernel(qi_ref, wi_ref, ki_ref, qb_ref, kv_ref, wuv_ref, o_ref, *, topk, q_base):
    tq, s_len = qb_ref.shape[0], kv_ref.shape[0]
    q0 = q_base + pl.program_id(1) * tq
    qpos, kpos, cend = _positions(q0, tq, s_len)
    allowed = kpos < cend
    ki = ki_ref[...]
    wi = wi_ref[...]
    zpad = jnp.zeros((tq, LANES - IDX_DIM), jnp.bfloat16)
    score = jnp.zeros((tq, s_len), jnp.float32)
    for h in range(IDX_HEADS):
        qh = jnp.concatenate([qi_ref[:, h * IDX_DIM:(h + 1) * IDX_DIM], zpad], axis=-1)
        score = score + jnp.maximum(_dot_nt(qh, ki), 0.0) * wi[:, h:h + 1]
    score = score * ((IDX_DIM ** -0.5) * (IDX_HEADS ** -0.5))
    sel = _topk_mask(score, allowed, kpos, topk, s_len)

    dist = jnp.abs(qpos - kpos).astype(jnp.float32)
    negmask = jnp.where(sel, 0.0, NEG)
    kv = kv_ref[...]
    for h in range(B_HEADS):
        slope = 2.0 ** (-8.0 * (h + 1) / B_HEADS)
        s = _dot_nt(qb_ref[:, h * B_LAT:(h + 1) * B_LAT], kv) * (B_LAT ** -0.5) + (negmask - slope * dist)
        p, l = _softmax_rows(s)
        o_lat = _dot(p.astype(jnp.bfloat16), kv) / l
        o = _dot(o_lat.astype(jnp.bfloat16), wuv_ref[h])
        o_ref[:, h * B_V_DIM:(h + 1) * B_V_DIM] = o.astype(o_ref.dtype)


def _dsa(qi, wi, ki, qb, kvb, w_uv, batch):
    t = qb.shape[0]
    s_len = t // batch
    topk = min(DSA_TOPK_MAX, s_len // 4)
    width = B_HEADS * B_V_DIM
    qi3, wi3, ki3, qb3, kv3 = (a.reshape(batch, s_len, a.shape[1]) for a in (qi, wi, ki, qb, kvb))
    wuv = w_uv.astype(jnp.bfloat16)
    outs = []
    for first, tiles, klen in _causal_groups(s_len, B_TQ):
        outs.append(pl.pallas_call(
            functools.partial(_dsa_kernel, topk=topk, q_base=first * B_TQ),
            grid=(batch, tiles),
            in_specs=[_qblk(B_TQ, first, I_Q), _qblk(B_TQ, first, LANES), _kblk(klen, LANES),
                      _qblk(B_TQ, first, B_Q), _kblk(klen, B_KV),
                      pl.BlockSpec((B_HEADS, B_LAT, B_V_DIM), lambda b, i: (0, 0, 0))],
            out_specs=_oblk(B_TQ, width),
            out_shape=jax.ShapeDtypeStruct((batch, tiles * B_TQ, width), jnp.bfloat16),
            compiler_params=_cparams(("arbitrary", "arbitrary")),
            name="dsa",
        )(qi3, wi3, ki3, qb3, kv3, wuv))
    return jnp.concatenate(outs, axis=1).reshape(t, width)


def _outproj_kernel(x_ref, oa_ref, ob_ref, wo_ref, g1_ref, sh_ref, sc_ref, gn_ref, wq_ref,
                    x1_ref, h2_ref, q_ref):
    na = oa_ref.shape[1]
    y = _dot(oa_ref[...], wo_ref[:na, :]) + _dot(ob_ref[...], wo_ref[na:, :])
    x1 = x_ref[...] + g1_ref[0] * y
    x1_ref[...] = x1
    h2 = _rms_rows(x1, gn_ref[...]) * (1.0 + sc_ref[0]) + sh_ref[0]
    h2_ref[...] = h2
    q_ref[...] = _dot(h2.astype(jnp.bfloat16), wq_ref[...]).astype(q_ref.dtype)


def _outproj(x2, oa, ob, w_out, g1, sh2, sc2, g2n, peer_wq, batch):
    t, d = x2.shape
    per_b = (t // batch) // PROJ_TM
    nq = peer_wq.shape[1]
    tok = lambda width: pl.BlockSpec((PROJ_TM, width), lambda i: (i, 0))
    per_batch = pl.BlockSpec((1, 1, d), lambda i: (i // per_b, 0, 0))
    full = lambda a, b: pl.BlockSpec((a, b), lambda i: (0, 0))
    return pl.pallas_call(
        _outproj_kernel,
        grid=(t // PROJ_TM,),
        in_specs=[tok(d), tok(oa.shape[1]), tok(ob.shape[1]), full(w_out.shape[0], d),
                  per_batch, per_batch, per_batch, full(1, d), full(d, nq)],
        out_specs=[tok(d), tok(d), tok(nq)],
        out_shape=[jax.ShapeDtypeStruct((t, d), jnp.float32),
                   jax.ShapeDtypeStruct((t, d), jnp.float32),
                   jax.ShapeDtypeStruct((t, nq), jnp.bfloat16)],
        compiler_params=_cparams(("arbitrary",)),
        name="outproj",
    )(x2, oa, ob, w_out.astype(jnp.bfloat16), g1.reshape(batch, 1, d), sh2.reshape(batch, 1, d),
      sc2.reshape(batch, 1, d), g2n.reshape(1, d), peer_wq.astype(jnp.bfloat16))


def _extract_topk(vals, payload, k):
    n, tm = vals.shape
    rows = lax.broadcasted_iota(jnp.int32, (n, tm), 0).astype(jnp.float32)
    top_v, top_p = [], []
    for _ in range(k):
        m = jnp.max(vals, axis=0, keepdims=True)
        pos = jnp.min(jnp.where(vals == m, rows, float(n)), axis=0, keepdims=True)
        hit = rows == pos
        top_v.append(m)
        top_p.append(pos if payload is None else jnp.max(jnp.where(hit, payload, -1.0), axis=0, keepdims=True))
        vals = jnp.where(hit, -jnp.inf, vals)
    return jnp.concatenate(top_v, axis=0), jnp.concatenate(top_p, axis=0)


def _stair_pairs():
    return [(a, b) for a in range(PEER_TOPK) for b in range(PEER_TOPK) if (a + 1) * (b + 1) <= PEER_TOPK]


N_STAIR = len(_stair_pairs())
N_STAIR_PAD = -(-N_STAIR // SUBLANES) * SUBLANES


def _copy_rows(sel, x):
    x0 = x.astype(jnp.bfloat16)
    r1 = x - x0.astype(jnp.float32)
    x1 = r1.astype(jnp.bfloat16)
    x2 = (r1 - x1.astype(jnp.float32)).astype(jnp.bfloat16)
    return _dot(sel, x0) + _dot(sel, x1) + _dot(sel, x2)


def _peer_select_kernel(q_ref, sk_ref, sela_ref, selb_ref, e_ref, g_ref):
    tm = q_ref.shape[0]
    sela, selb = sela_ref[...], selb_ref[...]
    pad_row = lax.broadcasted_iota(jnp.int32, (N_STAIR_PAD, tm), 0) >= N_STAIR
    e_rows, g_rows = [], []
    for h in range(PEER_HEADS):
        sub = _dot_nt(sk_ref[...], q_ref[:, h * PEER_QDIM:(h + 1) * PEER_QDIM])
        sv0, si0 = _extract_topk(sub[:PEER_NKEYS], None, PEER_TOPK)
        sv1, si1 = _extract_topk(sub[PEER_NKEYS:], None, PEER_TOPK)
        cand = jnp.where(pad_row, -jnp.inf, _copy_rows(sela, sv0) + _copy_rows(selb, sv1))
        ids = (_dot(sela, si0.astype(jnp.bfloat16)) * float(PEER_NKEYS)
               + _dot(selb, si1.astype(jnp.bfloat16))) * float(HALF_ROWS)
        top_s, top_e = _extract_topk(cand, ids, PEER_TOPK)
        p = jnp.exp(top_s - top_s[0:1])
        e_rows.append(top_e)
        g_rows.append(p / jnp.sum(p, axis=0, keepdims=True))
    e_t = jnp.concatenate(e_rows, axis=0)
    g_t = jnp.concatenate(g_rows, axis=0)
    e_ref[...] = e_t.T.astype(jnp.int32)
    g_ref[...] = g_t.T


def _peer_select(q, sub_keys):
    t, nq = q.shape
    half = PEER_QDIM // 2
    sk = jnp.zeros((2 * PEER_NKEYS, PEER_QDIM), jnp.bfloat16)
    sk = sk.at[:PEER_NKEYS, :half].set(sub_keys[0].astype(jnp.bfloat16))
    sk = sk.at[PEER_NKEYS:, half:].set(sub_keys[1].astype(jnp.bfloat16))
    sel = np.zeros((2, N_STAIR_PAD, PEER_TOPK), np.float32)
    for r, (a, b) in enumerate(_stair_pairs()):
        sel[0, r, a] = 1.0
        sel[1, r, b] = 1.0
    sel = jnp.asarray(sel, jnp.bfloat16)
    tok = lambda width: pl.BlockSpec((SELECT_TM, width), lambda i: (i, 0))
    full = lambda a, b: pl.BlockSpec((a, b), lambda i: (0, 0))
    return pl.pallas_call(
        _peer_select_kernel,
        grid=(t // SELECT_TM,),
        in_specs=[tok(nq), full(2 * PEER_NKEYS, PEER_QDIM),
                  full(N_STAIR_PAD, PEER_TOPK), full(N_STAIR_PAD, PEER_TOPK)],
        out_specs=[tok(PEER_NSEL), tok(PEER_NSEL)],
        out_shape=[jax.ShapeDtypeStruct((t, PEER_NSEL), jnp.int32),
                   jax.ShapeDtypeStruct((t, PEER_NSEL), jnp.float32)],
        compiler_params=_cparams(("arbitrary",)),
        name="peer_select",
    )(q, sk, sel[0], sel[1])


def _gather_row(tab_ref, row):
    w = tab_ref[pl.ds(pl.multiple_of(row, HALF_ROWS), HALF_ROWS), :]
    hi = pltpu.bitcast(w & jnp.uint32(0xFFFF0000), jnp.float32)
    lo = pltpu.bitcast(w << 16, jnp.float32)
    return hi, lo


def _peer_u_kernel(idx_ref, h_ref, gate_ref, tab_ref, coef_ref, abuf_ref):
    ones = jnp.ones((2 * LANES, LANES), jnp.bfloat16)
    lane = lax.broadcasted_iota(jnp.int32, (PEER_NSEL, PEER_TM), 1)
    rows_per_tok = PEER_NSEL * HALF_ROWS

    def batch(b, a_t):
        def token(tb, c2):
            t = b * PEER_TB + tb
            x8 = h_ref[pl.ds(t, 1), :].reshape(SUBLANES, LANES)
            xh, xl = x8[:HALF_ROWS], x8[HALF_ROWS:]
            off = pl.multiple_of(tb * rows_per_tok, rows_per_tok)
            for j in range(PEER_NSEL):
                hi, lo = _gather_row(tab_ref, idx_ref[0, t, j])
                abuf_ref[pl.ds(off + j * HALF_ROWS, HALF_ROWS), :] = hi * xh + lo * xl
            return c2

        lax.fori_loop(0, PEER_TB, token, 0)
        parts = []
        for tb in range(PEER_TB):
            ab = abuf_ref[pl.ds(tb * rows_per_tok, PEER_NSEL, stride=HALF_ROWS), :]
            for s in range(1, HALF_ROWS):
                ab = ab + abuf_ref[pl.ds(tb * rows_per_tok + s, PEER_NSEL, stride=HALF_ROWS), :]
            parts.append(ab)
        ab_hi, ab_lo = _split_bf16(jnp.concatenate(parts, axis=0))
        r = _dot(jnp.concatenate([ab_hi, ab_lo], axis=1), ones)
        for tb in range(PEER_TB):
            a_t = jnp.where(lane == b * PEER_TB + tb, r[tb * PEER_NSEL:(tb + 1) * PEER_NSEL], a_t)
        return a_t

    a_t = lax.fori_loop(0, PEER_TM // PEER_TB, batch, jnp.zeros((PEER_NSEL, PEER_TM), jnp.float32))
    a = a_t.T
    coef_ref[...] = 0.5 * a * (1.0 + lax.erf(a * (2.0 ** -0.5))) * gate_ref[...]


def _peer_v_kernel(idx_ref, coef_ref, x_ref, g2_ref, tab_ref, out_ref, cb_ref):
    g2 = g2_ref[...]
    ones = jnp.ones((2 * LANES, LANES), jnp.bfloat16)
    eye = (lax.broadcasted_iota(jnp.int32, (PEER_NSEL, LANES), 0)
           == lax.broadcasted_iota(jnp.int32, (PEER_NSEL, LANES), 1))
    nacc = 2

    def batch(b, carry):
        for tb in range(PEER_TB):
            row = coef_ref[pl.ds(b * PEER_TB + tb, 1), :]
            d_hi, d_lo = _split_bf16(jnp.where(eye, jnp.broadcast_to(row, (PEER_NSEL, LANES)), 0.0))
            cb_ref[pl.ds(tb * PEER_NSEL, PEER_NSEL), :] = _dot(jnp.concatenate([d_hi, d_lo], axis=1), ones)

        def token(tb, c2):
            t = b * PEER_TB + tb
            zero = jnp.zeros((HALF_ROWS, LANES), jnp.float32)
            acc_h = [zero] * nacc
            acc_l = [zero] * nacc
            cbase = pl.multiple_of(tb * PEER_NSEL, PEER_NSEL)
            for j in range(PEER_NSEL):
                hi, lo = _gather_row(tab_ref, idx_ref[0, t, j])
                c = jnp.broadcast_to(cb_ref[pl.ds(cbase + j, 1), :], (HALF_ROWS, LANES))
                acc_h[j % nacc] = acc_h[j % nacc] + c * hi
                acc_l[j % nacc] = acc_l[j % nacc] + c * lo
            y8 = jnp.concatenate([acc_h[0] + acc_h[1], acc_l[0] + acc_l[1]], axis=0)
            out_ref[pl.ds(t, 1), :] = x_ref[pl.ds(t, 1), :] + g2 * y8.reshape(1, SUBLANES * LANES)
            return c2

        lax.fori_loop(0, PEER_TB, token, 0)
        return carry

    lax.fori_loop(0, PEER_TM // PEER_TB, batch, 0)


def _sc_peer_u(expert_ids, h2, tab, first, count):
    info = plsc.get_sparse_core_info()
    lanes, workers = info.num_lanes, info.num_cores * info.num_subcores
    half = tab.shape[1]
    per_w = count // workers
    nhalf = PEER_NSEL // 2
    assert per_w * workers == count and per_w >= 2 and nhalf % lanes == 0 and half % lanes == 0
    rows_blk = 4
    kunroll = 2
    mesh = plsc.VectorSubcoreMesh(core_axis_name="c", subcore_axis_name="s")

    @functools.partial(
        pl.kernel, mesh=mesh, name="peer_u_sc",
        compiler_params=pltpu.CompilerParams(needs_layout_passes=False),
        out_type=jax.ShapeDtypeStruct((count, PEER_NSEL), jnp.float32),
        scratch_types=[pltpu.VMEM((2, PEER_NSEL), jnp.int32),
                       pltpu.VMEM((nhalf, half), jnp.uint32), pltpu.VMEM((nhalf, half), jnp.uint32),
                       pltpu.VMEM((2, 2 * half), jnp.float32), pltpu.VMEM((2, PEER_NSEL), jnp.float32),
                       pltpu.SemaphoreType.DMA, pltpu.SemaphoreType.DMA, pltpu.SemaphoreType.DMA,
                       pltpu.SemaphoreType.DMA, pltpu.SemaphoreType.DMA((2,))])
    def run(idx_hbm, h_hbm, tab_hbm, out_hbm, idx_v, rows_a, rows_b, x_v, a_v, sem_a, sem_b, sem_i, sem_x, sem_o):
        wid = lax.axis_index("s") * info.num_cores + lax.axis_index("c")
        base = first + wid * per_w
        lane = lax.iota(jnp.int32, lanes)
        mask_hi = jnp.full((lanes,), 0xFFFF0000, jnp.uint32)

        def gather(p, part, rows, sem):
            return pltpu.make_async_copy(tab_hbm.at[idx_v.at[p, pl.ds(part * nhalf, nhalf)]], rows, sem)

        def load_idx(t, p):
            return pltpu.make_async_copy(idx_hbm.at[t], idx_v.at[p], sem_i)

        def load_x(t, p):
            return pltpu.make_async_copy(h_hbm.at[t], x_v.at[p], sem_x)

        def store_a(t, p):
            return pltpu.make_async_copy(a_v.at[p], out_hbm.at[t - first], sem_o.at[p])

        def compute(rows, p, part):
            def group(g, c2):
                vec = jnp.zeros((lanes,), jnp.float32)
                for rb in range(lanes // rows_blk):
                    def kslice(kk, accs):
                        accs = list(accs)
                        for ku in range(kunroll):
                            off = pl.multiple_of((kk * kunroll + ku) * lanes, lanes)
                            xh = x_v[p, pl.ds(off, lanes)]
                            xl = x_v[p, pl.ds(half + off, lanes)]
                            for r in range(rows_blk):
                                w = rows[g * lanes + rb * rows_blk + r, pl.ds(off, lanes)]
                                hi = plsc.bitcast(w & mask_hi, jnp.float32)
                                lo = plsc.bitcast(w << 16, jnp.float32)
                                accs[r] = accs[r] + hi * xh + lo * xl
                        return tuple(accs)

                    zero = jnp.zeros((lanes,), jnp.float32)
                    accs = lax.fori_loop(0, half // lanes // kunroll, kslice, (zero,) * rows_blk)
                    for r in range(rows_blk):
                        vec = jnp.where(lane == rb * rows_blk + r, jnp.sum(accs[r]), vec)
                a_v[p, pl.ds(pl.multiple_of(part * nhalf + g * lanes, lanes), lanes)] = vec
                return c2

            lax.fori_loop(0, nhalf // lanes, group, 0)

        load_idx(base, 0).start()
        load_x(base, 0).start()
        load_idx(base, 0).wait()
        gather(0, 0, rows_a, sem_a).start()
        gather(0, 1, rows_b, sem_b).start()
        load_x(base, 0).wait()

        def token(i, carry):
            t = base + i
            p = lax.rem(i, 2)
            q = 1 - p
            tn = jnp.minimum(t + 1, base + per_w - 1)
            load_idx(tn, q).start()
            load_x(tn, q).start()

            @pl.when(i >= 2)
            def _():
                store_a(t, p).wait()

            gather(p, 0, rows_a, sem_a).wait()
            compute(rows_a, p, 0)
            load_idx(tn, q).wait()
            gather(q, 0, rows_a, sem_a).start()
            gather(p, 1, rows_b, sem_b).wait()
            compute(rows_b, p, 1)
            gather(q, 1, rows_b, sem_b).start()
            load_x(tn, q).wait()
            store_a(t, p).start()
            return carry

        lax.fori_loop(0, per_w, token, 0)
        gather(0, 0, rows_a, sem_a).wait()
        gather(0, 1, rows_b, sem_b).wait()
        store_a(base, 0).wait()
        store_a(base, 1).wait()

    return run(expert_ids, h2, tab)


def _sc_peer_v(expert_ids, coef, x1, g2, tab, first, count, per_batch):
    info = plsc.get_sparse_core_info()
    lanes, workers = info.num_lanes, info.num_cores * info.num_subcores
    half = tab.shape[1]
    d = 2 * half
    per_w = count // workers
    nhalf = PEER_NSEL // 2
    sblk = 16
    runroll = 1
    assert per_w * workers == count and per_w >= 2 and nhalf % runroll == 0 and half % (lanes * sblk) == 0
    mesh = plsc.VectorSubcoreMesh(core_axis_name="c", subcore_axis_name="s")
    dma = pltpu.SemaphoreType.DMA

    @functools.partial(
        pl.kernel, mesh=mesh, name="peer_v_sc",
        compiler_params=pltpu.CompilerParams(needs_layout_passes=False),
        out_type=jax.ShapeDtypeStruct((count, d), jnp.float32),
        scratch_types=[pltpu.VMEM((2, PEER_NSEL), jnp.int32), pltpu.VMEM((2 * PEER_NSEL,), jnp.float32),
                       pltpu.VMEM((nhalf, half), jnp.uint32), pltpu.VMEM((nhalf, half), jnp.uint32),
                       pltpu.VMEM((2, d), jnp.float32), pltpu.VMEM((2, d), jnp.float32),
                       pltpu.VMEM((d,), jnp.float32), pltpu.VMEM((2, d), jnp.float32),
                       dma, dma, dma, dma, dma, dma, dma((2,))])
    def run(idx_hbm, coef_hbm, x_hbm, g_hbm, tab_hbm, out_hbm,
            idx_v, coef_v, rows_a, rows_b, x_v, g_v, y_v, o_v, sem_a, sem_b, sem_i, sem_c, sem_x, sem_g, sem_o):
        wid = lax.axis_index("s") * info.num_cores + lax.axis_index("c")
        base = first + wid * per_w
        mask_hi = jnp.full((lanes,), 0xFFFF0000, jnp.uint32)

        def gather(p, part, rows, sem):
            return pltpu.make_async_copy(tab_hbm.at[idx_v.at[p, pl.ds(part * nhalf, nhalf)]], rows, sem)

        def loads(t, p):
            cslot = coef_v.at[pl.ds(pl.multiple_of(p * PEER_NSEL, PEER_NSEL), PEER_NSEL)]
            return (pltpu.make_async_copy(idx_hbm.at[t], idx_v.at[p], sem_i),
                    pltpu.make_async_copy(coef_hbm.at[t], cslot, sem_c),
                    pltpu.make_async_copy(x_hbm.at[t], x_v.at[p], sem_x),
                    pltpu.make_async_copy(g_hbm.at[t // per_batch], g_v.at[p], sem_g))

        def store_o(t, p):
            return pltpu.make_async_copy(o_v.at[p], out_hbm.at[t - first], sem_o.at[p])

        def compute(rows, p, part):
            cbase = p * PEER_NSEL + part * nhalf
            for sb in range(half // lanes // sblk):
                offs = [(sb * sblk + s) * lanes for s in range(sblk)]
                if part == 0:
                    init = tuple(jnp.zeros((lanes,), jnp.float32) for _ in range(2 * sblk))
                else:
                    init = tuple([y_v[pl.ds(o, lanes)] for o in offs] + [y_v[pl.ds(half + o, lanes)] for o in offs])

                def rowloop(rr, accs):
                    accs = list(accs)
                    for ru in range(runroll):
                        r = rr * runroll + ru
                        c = plsc.load_gather(coef_v, [jnp.full((lanes,), cbase + r, jnp.int32)])
                        for s in range(sblk):
                            w = rows[r, pl.ds(offs[s], lanes)]
                            hi = plsc.bitcast(w & mask_hi, jnp.float32)
                            lo = plsc.bitcast(w << 16, jnp.float32)
                            accs[s] = accs[s] + c * hi
                            accs[sblk + s] = accs[sblk + s] + c * lo
                    return tuple(accs)

                accs = lax.fori_loop(0, nhalf // runroll, rowloop, init)
                for s in range(sblk):
                    for hl, o in ((0, offs[s]), (1, half + offs[s])):
                        if part == 0:
                            y_v[pl.ds(o, lanes)] = accs[hl * sblk + s]
                        else:
                            o_v[p, pl.ds(o, lanes)] = (x_v[p, pl.ds(o, lanes)]
                                                       + g_v[p, pl.ds(o, lanes)] * accs[hl * sblk + s])

        head = loads(base, 0)
        for c in head:
            c.start()
        head[0].wait()
        gather(0, 0, rows_a, sem_a).start()
        gather(0, 1, rows_b, sem_b).start()
        for c in head[1:]:
            c.wait()

        def token(i, carry):
            t = base + i
            p = lax.rem(i, 2)
            q = 1 - p
            tn = jnp.minimum(t + 1, base + per_w - 1)
            nxt = loads(tn, q)
            for c in nxt:
                c.start()

            @pl.when(i >= 2)
            def _():
                store_o(t, p).wait()

            gather(p, 0, rows_a, sem_a).wait()
            compute(rows_a, p, 0)
            nxt[0].wait()
            gather(q, 0, rows_a, sem_a).start()
            gather(p, 1, rows_b, sem_b).wait()
            compute(rows_b, p, 1)
            gather(q, 1, rows_b, sem_b).start()
            for c in nxt[1:]:
                c.wait()
            store_o(t, p).start()
            return carry

        lax.fori_loop(0, per_w, token, 0)
        gather(0, 0, rows_a, sem_a).wait()
        gather(0, 1, rows_b, sem_b).wait()
        store_o(base, 0).wait()
        store_o(base, 1).wait()

    return run(expert_ids, coef, x1, g2, tab)


def _gelu_gate_kernel(a_ref, g_ref, o_ref):
    a = a_ref[...]
    o_ref[...] = 0.5 * a * (1.0 + lax.erf(a * (2.0 ** -0.5))) * g_ref[...]


def _gelu_gate(a, gates, first):
    count = a.shape[0]
    return pl.pallas_call(
        _gelu_gate_kernel,
        grid=(count // PEER_TM,),
        in_specs=[pl.BlockSpec((PEER_TM, PEER_NSEL), lambda i: (i, 0)),
                  pl.BlockSpec((PEER_TM, PEER_NSEL), lambda i: (first // PEER_TM + i, 0))],
        out_specs=pl.BlockSpec((PEER_TM, PEER_NSEL), lambda i: (i, 0)),
        out_shape=jax.ShapeDtypeStruct((count, PEER_NSEL), jnp.float32),
        compiler_params=_cparams(("arbitrary",)),
        name="gelu_gate",
    )(a, gates)


def _pack_table(tab):
    n, d = tab.shape
    bits = lax.bitcast_convert_type(tab.astype(jnp.bfloat16), jnp.uint16).astype(jnp.uint32)
    packed = (bits[:, : d // 2] << 16) | bits[:, d // 2:]
    return packed.reshape(n * HALF_ROWS, LANES)


def _peer_specs(t, d):
    smem_blk = pl.BlockSpec((1, PEER_TM, PEER_NSEL), lambda i: (i, 0, 0), memory_space=pltpu.SMEM)
    tok_blk = pl.BlockSpec((PEER_TM, d), lambda i: (i, 0))
    sel_blk = pl.BlockSpec((PEER_TM, PEER_NSEL), lambda i: (i, 0))
    return smem_blk, tok_blk, sel_blk, pl.BlockSpec(memory_space=pltpu.VMEM)


def _after(x, prev):
    return x if prev is None else lax.optimization_barrier((x, prev))[0]


def _peer_u_phase(h2, experts, gates, u_packed, n_sc):
    t, d = h2.shape
    assert d == SUBLANES * LANES and d // 2 == HALF_ROWS * LANES and PEER_TM % PEER_TB == 0
    smem_blk, tok_blk, sel_blk, tab_spec = _peer_specs(t, d)
    parts = []
    n_tc = t - n_sc
    if n_tc:
        parts.append(pl.pallas_call(
            _peer_u_kernel,
            grid=(n_tc // PEER_TM,),
            in_specs=[smem_blk, tok_blk, sel_blk, tab_spec],
            out_specs=sel_blk,
            out_shape=jax.ShapeDtypeStruct((n_tc, PEER_NSEL), jnp.float32),
            scratch_shapes=[pltpu.VMEM((PEER_TB * PEER_NSEL * HALF_ROWS, LANES), jnp.float32)],
            compiler_params=_cparams(("arbitrary",)),
            name="peer_u",
        )(experts.reshape(t // PEER_TM, PEER_TM, PEER_NSEL), h2, gates, u_packed))
    a_sc = None
    if n_sc:
        a_sc = _sc_peer_u(experts // HALF_ROWS, h2, u_packed.reshape(-1, d // 2), n_tc, n_sc)
    return (parts[0] if parts else None), a_sc


def _finish_coef(coef_tc, a_sc, gates):
    parts = [] if coef_tc is None else [coef_tc]
    if a_sc is not None:
        parts.append(_gelu_gate(a_sc, gates, gates.shape[0] - a_sc.shape[0]))
    return parts[0] if len(parts) == 1 else jnp.concatenate(parts, axis=0)


def _peer_v_phase(coef, x1, g2, experts, v_packed, batch, n_sc):
    t, d = x1.shape
    per_b = t // PEER_TM // batch
    smem_blk, tok_blk, sel_blk, tab_spec = _peer_specs(t, d)
    parts = []
    n_tc = t - n_sc
    if n_tc:
        parts.append(pl.pallas_call(
            _peer_v_kernel,
            grid=(n_tc // PEER_TM,),
            in_specs=[smem_blk, sel_blk, tok_blk,
                      pl.BlockSpec((None, 1, d), lambda i: (i // per_b, 0, 0)),
                      tab_spec],
            out_specs=tok_blk,
            out_shape=jax.ShapeDtypeStruct((n_tc, d), jnp.float32),
            scratch_shapes=[pltpu.VMEM((PEER_TB * PEER_NSEL, LANES), jnp.float32)],
            compiler_params=_cparams(("arbitrary",)),
            name="peer_v",
        )(experts.reshape(t // PEER_TM, PEER_TM, PEER_NSEL), coef, x1, g2.reshape(batch, 1, d), v_packed))
    if n_sc:
        parts.append(_sc_peer_v(experts // HALF_ROWS, coef, x1, g2, v_packed.reshape(-1, d // 2),
                                n_tc, n_sc, t // batch))
    return parts


def kernel(x, c, ada_w, ada_b, norm1_g, norm2_g, w_in, a_qk_gain, a_lambda, a_sub_gain, b_q_gain, b_kv_gain, b_w_uv, w_out, peer_wq, peer_subkeys, peer_u, peer_v):
    b, s, d = x.shape
    t = b * s
    x2 = x.reshape(t, d)
    nchunk = len(PEER_PLAN)
    bc = b // nchunk
    tc = bc * s
    for l in range(ada_w.shape[0]):
        mod = _adaln(c, ada_w[l], ada_b[l])
        u_packed, v_packed = _pack_table(peer_u[l]), _pack_table(peer_v[l])
        lam_init = 0.8 - 0.6 * math.exp(-0.3 * l)
        outs = [None] * nchunk
        pending = {}
        prev_experts = None

        def issue_v(k, anchor):
            coef_tc, a_sc, gates, args = pending.pop(k)
            if a_sc is not None:
                anchor, a_sc = lax.optimization_barrier((anchor, a_sc))
            outs[k] = _peer_v_phase(_finish_coef(coef_tc, a_sc, gates), *args)
            return anchor

        def issue_planned(at, anchor):
            for k in [k for k in pending if PEER_PLAN[k][2] == at]:
                anchor = issue_v(k, anchor)
            return anchor

        for ck, (n_sc_u, n_sc_v, _) in enumerate(PEER_PLAN):
            xc = _after(x2[ck * tc:(ck + 1) * tc], prev_experts)
            sh1, sc1, g1, sh2, sc2, g2 = [mod[ck * bc:(ck + 1) * bc, i * d:(i + 1) * d] for i in range(6)]
            qa, ka, va, qb, kvb, qi, ki, wi = _inproj(xc, sh1, sc1, norm1_g[l], w_in[l], a_qk_gain[l],
                                                      b_q_gain[l], b_kv_gain[l], bc)
            oa = _diffattn(qa, ka, va, a_lambda[l], a_sub_gain[l], bc, lam_init)
            oa = issue_planned((ck, "diffattn"), oa)
            ob = _dsa(qi, wi, ki, qb, kvb, b_w_uv[l], bc)
            x1, h2, q = _outproj(xc, oa, ob, w_out[l], g1, sh2, sc2, norm2_g[l], peer_wq[l], bc)
            q = issue_planned((ck, "outproj"), q)
            experts, gates = _peer_select(q, peer_subkeys[l])
            prev_experts = experts
            coef_tc, a_sc = _peer_u_phase(h2, experts, gates, u_packed, n_sc_u)
            pending[ck] = (coef_tc, a_sc, gates, (x1, g2, experts, v_packed, bc, n_sc_v))
        anchor = pending[nchunk - 1][0]
        for k in sorted(pending):
            anchor = issue_v(k, anchor)
        x2 = jnp.concatenate([p for o in outs for p in o], axis=0)
    return x2.reshape(b, s, d)
```
